```python
import jax, jax.numpy as jnp
from jax import lax
import numpy as np

D_MODEL = 2048
BATCH = 4
SEQ = 2048
DEPTH = 1
DEC_BATCH = 128
DEC_SEQ = 4
PAST_LEN = 16384
PAGE_SIZE = 128

RET_HEADS = 4
RET_DK = 256
RET_DV = 512
RET_CHUNK = 128
ROPE_BASE = 10000.0
RET_QK = RET_HEADS * RET_DK
RET_V = RET_HEADS * RET_DV
RWKV_HEADS = 16
RWKV_N = 64
RWKV_W = RWKV_HEADS * RWKV_N
DECAY_LORA = 64
AAA_LORA = 64
GATE_LORA = 160
SHIFT_W = 3 * RWKV_W + DECAY_LORA + AAA_LORA + GATE_LORA
D_FF = 5632
NORM_EPS = 1e-6
GN_EPS_RET = 1e-6
GN_EPS_RWKV = 64e-5
PROJ_W = 2 * RET_QK + 2 * RET_V + SHIFT_W + 2 * D_MODEL
Z_SPLITS = (RET_QK, 2 * RET_QK, 2 * RET_QK + RET_V, 2 * RET_QK + 2 * RET_V,
            2 * RET_QK + 2 * RET_V + SHIFT_W, 2 * RET_QK + 2 * RET_V + SHIFT_W + D_MODEL)
S_SPLITS = (RWKV_W, 2 * RWKV_W, 3 * RWKV_W, 3 * RWKV_W + DECAY_LORA, 3 * RWKV_W + DECAY_LORA + AAA_LORA)

kernel_name = "retention_rwkv7_gated_macaron_step"


def rms_norm(x, g):
    xf = x.astype(jnp.float32)
    y = xf * lax.rsqrt(jnp.mean(xf * xf, axis=-1, keepdims=True) + NORM_EPS)
    return (y * g.astype(jnp.float32)).astype(x.dtype)


def swiglu(x, w_gu, w_down):
    gate, up = jnp.split(x @ w_gu, 2, axis=-1)
    return (jax.nn.silu(gate) * up) @ w_down


def rotary(x, pos):
    half = x.shape[-1] // 2
    inv = 1.0 / (ROPE_BASE ** jnp.linspace(0.0, 1.0, half, dtype=jnp.float32))
    ang = pos[:, None] * inv[None, :]
    cos = jnp.cos(ang)[None, :, None, :]
    sin = jnp.sin(ang)[None, :, None, :]
    x1 = x[..., 0::2]
    x2 = x[..., 1::2]
    return jnp.stack([x1 * cos - x2 * sin, x1 * sin + x2 * cos], axis=-1).reshape(x.shape)


def retention_log_decay():
    return jnp.log(1.0 - 2.0 ** (-5.0 - jnp.arange(RET_HEADS, dtype=jnp.float32)))


def retention_chunk(S, q, k, v, log_g):
    C = q.shape[1]
    idx = jnp.arange(C, dtype=jnp.float32)
    rel = idx[:, None] - idx[None, :]
    mask = jnp.where(rel[None] >= 0, jnp.exp(log_g[:, None, None] * jnp.maximum(rel, 0.0)[None]), 0.0)
    scores = jnp.einsum('bihd,bjhd->bhij', q, k) * mask[None]
    inner = jnp.einsum('bhij,bjhe->bihe', scores, v)
    q_dec = jnp.exp(log_g[None, :] * (idx[:, None] + 1.0))
    cross = jnp.einsum('bihd,bhde->bihe', q, S) * q_dec[None, :, :, None]
    k_dec = jnp.exp(log_g[None, :] * (C - 1.0 - idx[:, None]))
    S_new = (jnp.exp(log_g * C)[None, :, None, None] * S
             + jnp.einsum('bjhd,bjhe->bhde', k * k_dec[None, :, :, None], v))
    return S_new, inner + cross


def retention(q, k, v, S0):
    B, T, H, _ = q.shape
    C = RET_CHUNK if T % RET_CHUNK == 0 else T
    n = T // C
    log_g = retention_log_decay()

    def to_chunks(t):
        return jnp.moveaxis(t.astype(jnp.float32).reshape(B, n, C, H, t.shape[-1]), 1, 0)

    def step(S, inp):
        qc, kc, vc = inp
        return retention_chunk(S, qc, kc, vc, log_g)

    S, o = lax.scan(step, S0.astype(jnp.float32), (to_chunks(q), to_chunks(k), to_chunks(v)))
    return S, jnp.moveaxis(o, 0, 1).reshape(B, T, H, v.shape[-1])


def retention_branch(zq, zk, zv, zg, pos, S0):
    B, T, _ = zq.shape
    q = rotary(zq.reshape(B, T, RET_HEADS, RET_DK).astype(jnp.float32), pos)
    k = rotary(zk.reshape(B, T, RET_HEADS, RET_DK).astype(jnp.float32), pos) * (RET_DK ** -0.5)
    v = zv.reshape(B, T, RET_HEADS, RET_DV)
    S, o = retention(q, k, v, S0)
    o = o * lax.rsqrt(jnp.mean(o * o, axis=-1, keepdims=True) + GN_EPS_RET)
    o = o.reshape(B, T, RET_V) * jax.nn.silu(zg.astype(jnp.float32))
    return S, o.astype(zq.dtype)


def rwkv7_recurrence(S0, r, w, k, v, a, b):
    def step(S, inp):
        rt, wt, kt, vt, at, bt = inp
        sa = jnp.einsum('bhvk,bhk->bhv', S, at)
        S = S * wt[:, :, None, :] + sa[..., None] * bt[:, :, None, :] + vt[..., None] * kt[:, :, None, :]
        return S, jnp.einsum('bhvk,bhk->bhv', S, rt)

    xs = tuple(jnp.moveaxis(t.astype(jnp.float32), 1, 0) for t in (r, w, k, v, a, b))
    S, y = lax.scan(step, S0.astype(jnp.float32), xs)
    return S, jnp.moveaxis(y, 0, 1)


def rwkv7_branch(zs, shift0, S0, p):
    B, T, _ = zs.shape
    f32 = jnp.float32
    zs_prev = jnp.concatenate([shift0[:, None, :].astype(zs.dtype), zs[:, :-1]], axis=1)
    xs = zs + (zs_prev - zs) * p['rwkv_mu']
    xr, xk, xv, xw, xa, xg = jnp.split(xs, S_SPLITS, axis=-1)

    def heads(t):
        return t.reshape(B, T, RWKV_HEADS, RWKV_N)

    w_log = -jax.nn.softplus(-(p['rwkv_w0'] + jnp.tanh(xw) @ p['rwkv_w_up']).astype(f32)) - 0.5
    decay = jnp.exp(-jnp.exp(w_log))
    a_rate = jax.nn.sigmoid((p['rwkv_a0'] + xa @ p['rwkv_a_up']).astype(f32))
    g = (jax.nn.sigmoid(xg) @ p['rwkv_g_up']).astype(f32)
    k = xk.astype(f32)
    kk = heads(k * p['rwkv_k_k'].astype(f32))
    kk = kk * lax.rsqrt(jnp.maximum(jnp.sum(kk * kk, axis=-1, keepdims=True), 1e-24))
    k = heads(k * (1.0 + (a_rate - 1.0) * p['rwkv_k_a'].astype(f32)))
    r = heads(xr.astype(f32))
    v = heads(xv.astype(f32))
    S, y = rwkv7_recurrence(S0, r, heads(decay), k, v, -kk, kk * heads(a_rate))
    mean = jnp.mean(y, axis=-1, keepdims=True)
    var = jnp.mean(jnp.square(y - mean), axis=-1, keepdims=True)
    y = ((y - mean) * lax.rsqrt(var + GN_EPS_RWKV)).reshape(B, T, RWKV_W)
    y = y * p['rwkv_ln_w'].astype(f32) + p['rwkv_ln_b'].astype(f32)
    bonus = jnp.sum(r * k * p['rwkv_r_k'].astype(f32), axis=-1, keepdims=True) * v
    y = (y + bonus.reshape(B, T, RWKV_W)) * g
    return S, y.astype(zs.dtype), zs[:, -1]


def decoder_layer(x, pos, ret_S0, rwkv_S0, shift0, p):
    x = x + 0.5 * swiglu(rms_norm(x, p['norm_ffn1']), p['ffn1_w_gu'], p['ffn1_w_down'])
    h = rms_norm(x, p['norm_mix'])
    z = h @ p['w_in']
    zq, zk, zv, zg, zs, zgr, zgw = jnp.split(z, Z_SPLITS, axis=-1)
    ret_S, o_ret = retention_branch(zq, zk, zv, zg, pos, ret_S0)
    rwkv_S, o_rwkv, shift = rwkv7_branch(zs, shift0, rwkv_S0, p)
    merged = (jax.nn.sigmoid(zgr) * (o_ret @ p['w_ret_o'])
              + jax.nn.sigmoid(zgw) * (o_rwkv @ p['w_rwkv_o']))
    x = x + merged @ p['w_out']
    x = x + 0.5 * swiglu(rms_norm(x, p['norm_ffn2']), p['ffn2_w_gu'], p['ffn2_w_down'])
    return x, ret_S, rwkv_S, shift


def run_trunk(x, pos, ret0, rwkv0, shift0, layer_params, norm_final):
    ret_out, rwkv_out, shift_out = [], [], []
    for l in range(DEPTH):
        p = {name: arr[l] for name, arr in layer_params.items()}
        x, rS, wS, sh = decoder_layer(x, pos, ret0[l], rwkv0[l], shift0[l], p)
        ret_out.append(rS)
        rwkv_out.append(wS)
        shift_out.append(sh)
    return rms_norm(x, norm_final), jnp.stack(ret_out), jnp.stack(rwkv_out), jnp.stack(shift_out)


def setup_inputs(seed: int = 0) -> dict:
    key = jax.random.key(seed)
    ks = jax.random.split(key, 32)
    f32 = jnp.float32
    L = DEPTH
    D = D_MODEL

    def nrm(k, shape, s):
        return jax.random.normal(k, shape, f32) * s

    return {
        "x_prompt": nrm(ks[0], (BATCH, SEQ, D), 1.0),
        "x_sample": nrm(ks[1], (DEC_BATCH, DEC_SEQ, D), 1.0),
        "state_ret": nrm(ks[2], (L, DEC_BATCH, RET_HEADS, RET_DK, RET_DV), 0.3),
        "state_rwkv": nrm(ks[3], (L, DEC_BATCH, RWKV_HEADS, RWKV_N, RWKV_N), 0.3),
        "state_shift": nrm(ks[4], (L, DEC_BATCH, SHIFT_W), 1.0),
        "norm_ffn1": 1.0 + nrm(ks[5], (L, D), 0.05),
        "ffn1_w_gu": nrm(ks[6], (L, D, 2 * D_FF), D ** -0.5),
        "ffn1_w_down": nrm(ks[7], (L, D_FF, D), D_FF ** -0.5),
        "norm_mix": 1.0 + nrm(ks[8], (L, D), 0.05),
        "w_in": nrm(ks[9], (L, D, PROJ_W), D ** -0.5),
        "w_ret_o": nrm(ks[10], (L, RET_V, D), RET_V ** -0.5),
        "rwkv_mu": jax.random.uniform(ks[11], (L, SHIFT_W), f32),
        "rwkv_w0": jax.random.uniform(ks[12], (L, RWKV_W), f32, -4.0, 0.0),
        "rwkv_w_up": nrm(ks[13], (L, DECAY_LORA, RWKV_W), 0.5 * DECAY_LORA ** -0.5),
        "rwkv_a0": nrm(ks[14], (L, RWKV_W), 0.5),
        "rwkv_a_up": nrm(ks[15], (L, AAA_LORA, RWKV_W), AAA_LORA ** -0.5),
        "rwkv_g_up": nrm(ks[16], (L, GATE_LORA, RWKV_W), GATE_LORA ** -0.5),
        "rwkv_k_k": 0.85 + nrm(ks[17], (L, RWKV_W), 0.05),
        "rwkv_k_a": 1.0 + nrm(ks[18], (L, RWKV_W), 0.05),
        "rwkv_r_k": nrm(ks[19], (L, RWKV_HEADS, RWKV_N), 0.1),
        "rwkv_ln_w": 1.0 + nrm(ks[20], (L, RWKV_W), 0.05),
        "rwkv_ln_b": nrm(ks[21], (L, RWKV_W), 0.01),
        "w_rwkv_o": nrm(ks[22], (L, RWKV_W, D), RWKV_W ** -0.5),
        "w_out": nrm(ks[23], (L, D, D), D ** -0.5),
        "norm_ffn2": 1.0 + nrm(ks[24], (L, D), 0.05),
        "ffn2_w_gu": nrm(ks[25], (L, D, 2 * D_FF), D ** -0.5),
        "ffn2_w_down": nrm(ks[26], (L, D_FF, D), D_FF ** -0.5),
        "norm_final": 1.0 + nrm(ks[27], (D,), 0.05),
    }


def reference(x_prompt, x_sample, state_ret, state_rwkv, state_shift,
              norm_ffn1, ffn1_w_gu, ffn1_w_down, norm_mix, w_in, w_ret_o,
              rwkv_mu, rwkv_w0, rwkv_w_up, rwkv_a0, rwkv_a_up, rwkv_g_up,
              rwkv_k_k, rwkv_k_a, rwkv_r_k, rwkv_ln_w, rwkv_ln_b, w_rwkv_o, w_out,
              norm_ffn2, ffn2_w_gu, ffn2_w_down, norm_final):
    layer_params = {
        'norm_ffn1': norm_ffn1, 'ffn1_w_gu': ffn1_w_gu, 'ffn1_w_down': ffn1_w_down,
        'norm_mix': norm_mix, 'w_in': w_in, 'w_ret_o': w_ret_o,
        'rwkv_mu': rwkv_mu, 'rwkv_w0': rwkv_w0, 'rwkv_w_up': rwkv_w_up,
        'rwkv_a0': rwkv_a0, 'rwkv_a_up': rwkv_a_up, 'rwkv_g_up': rwkv_g_up,
        'rwkv_k_k': rwkv_k_k, 'rwkv_k_a': rwkv_k_a, 'rwkv_r_k': rwkv_r_k,
        'rwkv_ln_w': rwkv_ln_w, 'rwkv_ln_b': rwkv_ln_b, 'w_rwkv_o': w_rwkv_o,
        'w_out': w_out, 'norm_ffn2': norm_ffn2, 'ffn2_w_gu': ffn2_w_gu, 'ffn2_w_down': ffn2_w_down,
    }
    Bp, Tp, _ = x_prompt.shape
    Bs, Ts, _ = x_sample.shape
    pos_p = jnp.arange(Tp, dtype=jnp.float32)
    ret0_p = jnp.zeros((DEPTH, Bp, RET_HEADS, RET_DK, RET_DV), jnp.float32)
    rwkv0_p = jnp.zeros((DEPTH, Bp, RWKV_HEADS, RWKV_N, RWKV_N), jnp.float32)
    shift0_p = jnp.zeros((DEPTH, Bp, SHIFT_W), x_prompt.dtype)
    y_prompt, ret_p, rwkv_p, shift_p = run_trunk(x_prompt, pos_p, ret0_p, rwkv0_p, shift0_p,
                                                 layer_params, norm_final)
    pos_s = PAST_LEN + jnp.arange(Ts, dtype=jnp.float32)
    y_sample, ret_s, rwkv_s, shift_s = run_trunk(x_sample, pos_s, state_ret, state_rwkv, state_shift,
                                                 layer_params, norm_final)
    return (y_prompt, y_sample, ret_p, rwkv_p, shift_p, ret_s, rwkv_s, shift_s)
```

```python
import functools

import jax
import jax.numpy as jnp
import numpy as np
from jax import lax
from jax.experimental import pallas as pl
from jax.experimental.pallas import tpu as pltpu

F32 = jnp.float32
BF16 = jnp.bfloat16

D_MODEL = 2048
PAST_LEN = 16384
RET_HEADS = 4
RET_DK = 256
RET_DV = 512
RET_CHUNK = 128
ROPE_BASE = 10000.0
RET_QK = RET_HEADS * RET_DK
RET_V = RET_HEADS * RET_DV
RWKV_HEADS = 16
RWKV_N = 64
RWKV_W = RWKV_HEADS * RWKV_N
DECAY_LORA = 64
AAA_LORA = 64
GATE_LORA = 160
LORA_W = DECAY_LORA + AAA_LORA + GATE_LORA
LORA_PAD = 384
SHIFT_W = 3 * RWKV_W + LORA_W
D_FF = 5632
NORM_EPS = 1e-6
GN_EPS_RET = 1e-6
GN_EPS_RWKV = 64e-5

ZQ, ZK, ZV, ZG, ZS, ZGR, ZGW = 0, 1024, 2048, 4096, 6144, 9216, 11264
Z_MAIN_W = 13312

LANES = 128
RWKV_PAIRS = RWKV_W // LANES
RWKV_CHUNK = 64

VMEM_LIMIT = 56 * 1024 * 1024

TM = 512
TF = 512
TN_PROJ = 1024
TN_OUT = 512


def _rms(x, g, eps):
    return x * lax.rsqrt(jnp.mean(x * x, axis=-1, keepdims=True) + eps) * g


def _dot(a, b):
    return jnp.dot(a.astype(BF16), b.astype(BF16), preferred_element_type=F32)


def _dot_nt(a, b):
    return lax.dot_general(a.astype(BF16), b.astype(BF16), (((1,), (1,)), ((), ())),
                           preferred_element_type=F32)


def _dot_tn(a, b):
    return lax.dot_general(a.astype(BF16), b.astype(BF16), (((0,), (0,)), ((), ())),
                           preferred_element_type=F32)


def _cparams(sem):
    return pltpu.CompilerParams(dimension_semantics=sem, vmem_limit_bytes=VMEM_LIMIT)


def _ffn_kernel(x_ref, g_ref, wg_ref, wu_ref, wd_ref, gf_ref, o_ref, h_ref, acc_ref, *, final_norm):
    j = pl.program_id(1)

    @pl.when(j == 0)
    def _():
        h_ref[...] = _rms(x_ref[...], g_ref[...], NORM_EPS).astype(BF16)
        acc_ref[...] = jnp.zeros_like(acc_ref)

    h = h_ref[...]
    gate = jnp.dot(h, wg_ref[...], preferred_element_type=F32)
    up = jnp.dot(h, wu_ref[...], preferred_element_type=F32)
    act = (gate * jax.nn.sigmoid(gate)) * up
    acc_ref[...] += jnp.dot(act.astype(BF16), wd_ref[...], preferred_element_type=F32)

    @pl.when(j == pl.num_programs(1) - 1)
    def _():
        y = x_ref[...] + 0.5 * acc_ref[...]
        if final_norm:
            y = _rms(y, gf_ref[...], NORM_EPS)
        o_ref[...] = y


def _ffn(x, g, w_gu, w_down, g_final, final_norm):
    n = x.shape[0]
    nj = D_FF // TF
    return pl.pallas_call(
        functools.partial(_ffn_kernel, final_norm=final_norm),
        grid=(n // TM, nj),
        in_specs=[
            pl.BlockSpec((TM, D_MODEL), lambda i, j: (i, 0)),
            pl.BlockSpec((1, D_MODEL), lambda i, j: (0, 0)),
            pl.BlockSpec((D_MODEL, TF), lambda i, j: (0, j)),
            pl.BlockSpec((D_MODEL, TF), lambda i, j: (0, j + D_FF // TF)),
            pl.BlockSpec((TF, D_MODEL), lambda i, j: (j, 0)),
            pl.BlockSpec((1, D_MODEL), lambda i, j: (0, 0)),
        ],
        out_specs=pl.BlockSpec((TM, D_MODEL), lambda i, j: (i, 0)),
        out_shape=jax.ShapeDtypeStruct((n, D_MODEL), F32),
        scratch_shapes=[pltpu.VMEM((TM, D_MODEL), BF16), pltpu.VMEM((TM, D_MODEL), F32)],
        compiler_params=_cparams(("parallel", "arbitrary")),
        name="ffn",
    )(x, g, w_gu, w_gu, w_down, g_final)


def _proj_kernel(x_ref, g_ref, w_ref, o_ref, h_ref):
    @pl.when(pl.program_id(1) == 0)
    def _():
        h_ref[...] = _rms(x_ref[...], g_ref[...], NORM_EPS).astype(BF16)

    o_ref[...] = jnp.dot(h_ref[...], w_ref[...], preferred_element_type=F32)


def _proj(x, g, w, tn):
    n = x.shape[0]
    nout = w.shape[1]
    return pl.pallas_call(
        _proj_kernel,
        grid=(n // TM, nout // tn),
        in_specs=[
            pl.BlockSpec((TM, D_MODEL), lambda i, j: (i, 0)),
            pl.BlockSpec((1, D_MODEL), lambda i, j: (0, 0)),
            pl.BlockSpec((D_MODEL, tn), lambda i, j: (0, j)),
        ],
        out_specs=pl.BlockSpec((TM, tn), lambda i, j: (i, j)),
        out_shape=jax.ShapeDtypeStruct((n, nout), F32),
        scratch_shapes=[pltpu.VMEM((TM, D_MODEL), BF16)],
        compiler_params=_cparams(("parallel", "arbitrary")),
        name="proj",
    )(x, g, w)


def _pair_swap(x):
    n = x.shape[-1]
    lane = lax.broadcasted_iota(jnp.int32, x.shape, x.ndim - 1)
    prev = pltpu.roll(x, 1, x.ndim - 1)
    nxt = pltpu.roll(x, n - 1, x.ndim - 1)
    return jnp.where((lane & 1) == 1, prev, nxt)


def _rotary(x, cos, sin_signed):
    return x * cos + _pair_swap(x) * sin_signed


def _ret_finish(o, zg):
    o = o * lax.rsqrt(jnp.mean(o * o, axis=-1, keepdims=True) + GN_EPS_RET)
    return o * (zg * jax.nn.sigmoid(zg))


def _ret_prompt_kernel(lg_ref, q_ref, k_ref, v_ref, g_ref, cos_ref, sin_ref,
                       o_ref, s_out_ref, s_ref):
    c = pl.program_id(2)
    lg = lg_ref[pl.program_id(1)]
    C = RET_CHUNK

    @pl.when(c == 0)
    def _():
        s_ref[...] = jnp.zeros_like(s_ref)

    cos = cos_ref[...]
    sin = sin_ref[...]
    q = _rotary(q_ref[...], cos, sin)
    k = _rotary(k_ref[...], cos, sin) * (RET_DK ** -0.5)
    v = v_ref[...]

    ti = lax.broadcasted_iota(jnp.int32, (C, C), 0)
    tj = lax.broadcasted_iota(jnp.int32, (C, C), 1)
    rel = (ti - tj).astype(F32)
    mask = jnp.where(rel >= 0, jnp.exp(lg * jnp.maximum(rel, 0.0)), 0.0)
    idx = lax.broadcasted_iota(jnp.int32, (C, 1), 0).astype(F32)
    q_dec = jnp.exp(lg * (idx + 1.0))
    k_dec = jnp.exp(lg * (C - 1.0 - idx))

    s = s_ref[...]
    scores = _dot_nt(q, k) * mask
    inner = _dot(scores, v)
    cross = _dot(q, s) * q_dec
    s_new = jnp.exp(lg * C) * s + _dot_tn(k * k_dec, v)
    s_ref[...] = s_new
    o_ref[...] = _ret_finish(inner + cross, g_ref[...]).astype(BF16)

    @pl.when(c == pl.num_programs(2) - 1)
    def _():
        s_out_ref[0, 0] = s_new


def _ret_prompt(z, lg, cos, sin, batch, seq):
    nc = seq // RET_CHUNK
    C = RET_CHUNK
    return pl.pallas_call(
        _ret_prompt_kernel,
        grid=(batch, RET_HEADS, nc),
        in_specs=[
            pl.BlockSpec(memory_space=pltpu.SMEM),
            pl.BlockSpec((C, RET_DK), lambda b, h, c: (b * nc + c, ZQ // RET_DK + h)),
            pl.BlockSpec((C, RET_DK), lambda b, h, c: (b * nc + c, ZK // RET_DK + h)),
            pl.BlockSpec((C, RET_DV), lambda b, h, c: (b * nc + c, ZV // RET_DV + h)),
            pl.BlockSpec((C, RET_DV), lambda b, h, c: (b * nc + c, ZG // RET_DV + h)),
            pl.BlockSpec((C, RET_DK), lambda b, h, c: (c, 0)),
            pl.BlockSpec((C, RET_DK), lambda b, h, c: (c, 0)),
        ],
        out_specs=[
            pl.BlockSpec((C, RET_DV), lambda b, h, c: (b * nc + c, h)),
            pl.BlockSpec((1, 1, RET_DK, RET_DV), lambda b, h, c: (b, h, 0, 0)),
        ],
        out_shape=[
            jax.ShapeDtypeStruct((batch * seq, RET_V), BF16),
            jax.ShapeDtypeStruct((batch, RET_HEADS, RET_DK, RET_DV), F32),
        ],
        scratch_shapes=[pltpu.VMEM((RET_DK, RET_DV), F32)],
        compiler_params=_cparams(("parallel", "parallel", "arbitrary")),
        name="ret_prompt",
    )(lg, z, z, z, z, cos, sin)


def _ret_sample_kernel(lg_ref, q_ref, k_ref, v_ref, g_ref, cos_ref, sin_ref, s0_ref,
                       o_ref, s_out_ref, *, seq):
    lg = lg_ref[pl.program_id(1)]
    R = 8
    nb = R // seq
    cos = cos_ref[...]
    sin = sin_ref[...]
    q = _rotary(q_ref[...], cos, sin)
    k = _rotary(k_ref[...], cos, sin) * (RET_DK ** -0.5)
    v = v_ref[...]

    ri = lax.broadcasted_iota(jnp.int32, (R, R), 0)
    rj = lax.broadcasted_iota(jnp.int32, (R, R), 1)
    rel = ((ri % seq) - (rj % seq)).astype(F32)
    ok = ((ri // seq) == (rj // seq)) & (rel >= 0)
    mask = jnp.where(ok, jnp.exp(lg * jnp.maximum(rel, 0.0)), 0.0)
    row = lax.broadcasted_iota(jnp.int32, (R, 1), 0)
    t = (row % seq).astype(F32)
    q_dec = jnp.exp(lg * (t + 1.0))
    k_dec = jnp.exp(lg * (seq - 1.0 - t))
    kd = k * k_dec

    scores = _dot_nt(q, k) * mask
    inner = _dot(scores, v)
    cross = jnp.zeros_like(inner)
    for b in range(nb):
        s = s0_ref[b, 0]
        mine = (row // seq) == b
        cross = jnp.where(mine, _dot(q, s), cross)
        s_out_ref[b, 0] = jnp.exp(lg * seq) * s + _dot_tn(jnp.where(mine, kd, 0.0), v)
    o_ref[...] = _ret_finish(inner + cross * q_dec, g_ref[...]).astype(BF16)


def _ret_sample(z, lg, cos, sin, s0, row0, batch, seq):
    R = 8
    nb = R // seq
    rb0 = row0 // R
    return pl.pallas_call(
        functools.partial(_ret_sample_kernel, seq=seq),
        grid=(batch // nb, RET_HEADS),
        in_specs=[
            pl.BlockSpec(memory_space=pltpu.SMEM),
            pl.BlockSpec((R, RET_DK), lambda i, h: (rb0 + i, ZQ // RET_DK + h)),
            pl.BlockSpec((R, RET_DK), lambda i, h: (rb0 + i, ZK // RET_DK + h)),
            pl.BlockSpec((R, RET_DV), lambda i, h: (rb0 + i, ZV // RET_DV + h)),
            pl.BlockSpec((R, RET_DV), lambda i, h: (rb0 + i, ZG // RET_DV + h)),
            pl.BlockSpec((R, RET_DK), lambda i, h: (0, 0)),
            pl.BlockSpec((R, RET_DK), lambda i, h: (0, 0)),
            pl.BlockSpec((nb, 1, RET_DK, RET_DV), lambda i, h: (i, h, 0, 0)),
        ],
        out_specs=[
            pl.BlockSpec((R, RET_DV), lambda i, h: (i, h)),
            pl.BlockSpec((nb, 1, RET_DK, RET_DV), lambda i, h: (i, h, 0, 0)),
        ],
        out_shape=[
            jax.ShapeDtypeStruct((batch * seq, RET_V), BF16),
            jax.ShapeDtypeStruct((batch, RET_HEADS, RET_DK, RET_DV), F32),
        ],
        compiler_params=_cparams(("parallel", "parallel")),
        name="ret_sample",
    )(lg, z, z, z, z, cos, sin, s0)


def _rope_tables(pos):
    half = RET_DK // 2
    inv = 1.0 / (ROPE_BASE ** jnp.linspace(0.0, 1.0, half, dtype=F32))
    ang = pos[:, None] * inv[None, :]
    cos = jnp.repeat(jnp.cos(ang), 2, axis=-1)
    sin = jnp.sin(ang)
    sin_signed = jnp.stack([-sin, sin], axis=-1).reshape(pos.shape[0], RET_DK)
    return cos, sin_signed


def _segsum(x, seg_ref):
    rows = x.shape[0]
    xs = jnp.concatenate([x[:, p * LANES:(p + 1) * LANES] for p in range(RWKV_PAIRS)], axis=0)
    ss = jnp.dot(xs, seg_ref[...], preferred_element_type=F32, precision=lax.Precision.HIGHEST)
    return jnp.concatenate([ss[p * rows:(p + 1) * rows] for p in range(RWKV_PAIRS)], axis=1)


def _rwkv_kernel(zr_ref, zl_ref, shr_ref, shl_ref, s0_ref,
                 mur_ref, mul_ref, w0_ref, wup_ref, a0_ref, aup_ref, gup_ref,
                 kk_ref, ka_ref, rk_ref, lnw_ref, lnb_ref, seg_ref,
                 o_ref, s_out_ref, shor_ref, shol_ref,
                 prevr_ref, prevl_ref, s_ref, *, valid):
    C = RWKV_CHUNK
    W = RWKV_W
    c = pl.program_id(1)
    last = pl.num_programs(1) - 1

    @pl.when(c == 0)
    def _():
        prevr_ref[...] = shr_ref[0]
        prevl_ref[...] = shl_ref[0]
        s_ref[...] = s0_ref[0]

    zr = zr_ref[...]
    zl = zl_ref[...]
    row = lax.broadcasted_iota(jnp.int32, (C, 1), 0)

    def mix(z, prev_ref, mu_ref):
        z_prev = jnp.where(row == 0, prev_ref[...], pltpu.roll(z, 1, 0))
        return z + (z_prev - z) * mu_ref[...]

    xs = mix(zr, prevr_ref, mur_ref)
    xl = mix(zl, prevl_ref, mul_ref)
    prevr_ref[...] = zr[C - 1:C, :]
    prevl_ref[...] = zl[C - 1:C, :]

    r = xs[:, 0:W]
    k = xs[:, W:2 * W]
    v = xs[:, 2 * W:3 * W]
    x_wa = xl[:, 0:LANES]
    x_g = xl[:, LANES:LORA_PAD]

    y_w = w0_ref[...] + _dot(jnp.tanh(x_wa), wup_ref[...])
    w_log = -(jnp.maximum(-y_w, 0.0) + jnp.log1p(jnp.exp(-jnp.abs(y_w)))) - 0.5
    lw = -jnp.exp(w_log)
    a_rate = jax.nn.sigmoid(a0_ref[...] + _dot(x_wa, aup_ref[...]))
    g = _dot(jax.nn.sigmoid(x_g), gup_ref[...])

    kk = k * kk_ref[...]
    kk = kk * lax.rsqrt(jnp.maximum(_segsum(kk * kk, seg_ref), 1e-24))
    k = k * (1.0 + (a_rate - 1.0) * ka_ref[...])
    if valid < C:
        ok = row < valid
        lw = jnp.where(ok, lw, 0.0)
        kk = jnp.where(ok, kk, 0.0)
        k = jnp.where(ok, k, 0.0)
    a = -kk
    b = kk * a_rate

    ti = lax.broadcasted_iota(jnp.int32, (C, C), 0)
    tj = lax.broadcasted_iota(jnp.int32, (C, C), 1)
    tri = (tj <= ti).astype(F32)
    cum = jnp.dot(tri, lw, preferred_element_type=F32, precision=lax.Precision.HIGHEST)
    cum_last = cum[C - 1:C, :]
    e_neg = jnp.exp(-cum)
    a_t = a * jnp.exp(cum - lw)
    b_t = b * e_neg
    k_t = k * e_neg
    r_t = r * jnp.exp(cum)
    e_tail = jnp.exp(cum_last - cum)
    b_g = b * e_tail
    k_g = k * e_tail
    g_all = jnp.exp(cum_last)

    lane = lax.broadcasted_iota(jnp.int32, (1, LANES), 1)
    lo = lane < RWKV_N

    def split(x):
        return jnp.concatenate([jnp.where(lo, x, 0.0), jnp.where(lo, 0.0, x)], axis=0)

    row4 = lax.broadcasted_iota(jnp.int32, (C, 4 * C), 0)
    col4 = lax.broadcasted_iota(jnp.int32, (C, 4 * C), 1) % C
    strict = col4 < row4
    incl = col4 <= row4
    row2 = lax.broadcasted_iota(jnp.int32, (C, 2 * C), 0)
    col2 = lax.broadcasted_iota(jnp.int32, (C, 2 * C), 1)
    eye2 = ((col2 % C) == row2).astype(F32)
    left = col2 < C
    vi = lax.broadcasted_iota(jnp.int32, (LANES, LANES), 0)
    ki = lax.broadcasted_iota(jnp.int32, (LANES, LANES), 1)
    diag_blocks = (vi < RWKV_N) == (ki < RWKV_N)

    ys = []
    for p in range(RWKV_PAIRS):
        sl = slice(p * LANES, (p + 1) * LANES)
        s = s_ref[p]
        v_p = v[:, sl]
        v2 = split(v_p)
        gm = _dot_nt(jnp.concatenate([a_t[:, sl], r_t[:, sl]], axis=0),
                     jnp.concatenate([split(b_t[:, sl]), split(k_t[:, sl])], axis=0))
        ga = jnp.where(strict, gm[0:C], 0.0)
        gr = jnp.where(incl, gm[C:2 * C], 0.0)
        l_sbs = ga[:, 0:2 * C]
        pw = jnp.concatenate([jnp.where(left, l_sbs, 0.0), jnp.where(left, 0.0, l_sbs)], axis=0)
        t = eye2 + l_sbs
        steps = C.bit_length() - 2
        for _ in range(steps):
            pw = _dot(pw, pw)
            t = t + _dot(t, pw)
        w_rhs = _dot_nt(a_t[:, sl], s) + _dot(ga[:, 2 * C:4 * C], v2)
        u = _dot(t, split(w_rhs))
        ys.append(_dot_nt(r_t[:, sl], s) + _dot(gr, jnp.concatenate([split(u), v2], axis=0)))
        upd = _dot_tn(jnp.concatenate([u, v_p], axis=0),
                      jnp.concatenate([b_g[:, sl], k_g[:, sl]], axis=0))
        s_ref[p] = s * g_all[:, sl] + jnp.where(diag_blocks, upd, 0.0)

    y = jnp.concatenate(ys, axis=1)
    mean = _segsum(y, seg_ref) * (1.0 / RWKV_N)
    d = y - mean
    var = _segsum(d * d, seg_ref) * (1.0 / RWKV_N)
    yn = d * lax.rsqrt(var + GN_EPS_RWKV) * lnw_ref[...] + lnb_ref[...]
    bonus = _segsum(r * k * rk_ref[...], seg_ref) * v
    o_ref[...] = ((yn + bonus) * g).astype(BF16)

    @pl.when(c == last)
    def _():
        s_out_ref[0] = s_ref[...]
        shor_ref[0] = zr[valid - 1:valid, :]
        shol_ref[0] = zl[valid - 1:valid, :]


def _rwkv(zr, zr_col, zl, sh_r, sh_l, s0, params, batch, nc, valid):
    C = RWKV_CHUNK
    W = RWKV_W
    full = lambda shape: pl.BlockSpec(shape, lambda b, c: tuple(0 for _ in shape))
    in_specs = [
        pl.BlockSpec((C, 3 * W), lambda b, c: (b * nc + c, zr_col)),
        pl.BlockSpec((C, LORA_PAD), lambda b, c: (b * nc + c, 0)),
        pl.BlockSpec((1, 1, 3 * W), lambda b, c: (b, 0, 0)),
        pl.BlockSpec((1, 1, LORA_PAD), lambda b, c: (b, 0, 0)),
        pl.BlockSpec((1, RWKV_PAIRS, LANES, LANES), lambda b, c: (b, 0, 0, 0)),
        full((1, 3 * W)), full((1, LORA_PAD)),
        full((1, W)), full((LANES, W)), full((1, W)), full((LANES, W)), full((LORA_PAD - LANES, W)),
        full((1, W)), full((1, W)), full((1, W)), full((1, W)), full((1, W)),
        full((LANES, LANES)),
    ]
    out_specs = [
        pl.BlockSpec((C, W), lambda b, c: (b * nc + c, 0)),
        pl.BlockSpec((1, RWKV_PAIRS, LANES, LANES), lambda b, c: (b, 0, 0, 0)),
        pl.BlockSpec((1, 1, 3 * W), lambda b, c: (b, 0, 0)),
        pl.BlockSpec((1, 1, LORA_PAD), lambda b, c: (b, 0, 0)),
    ]
    out_shape = [
        jax.ShapeDtypeStruct((batch * nc * C, W), BF16),
        jax.ShapeDtypeStruct((batch, RWKV_PAIRS, LANES, LANES), F32),
        jax.ShapeDtypeStruct((batch, 1, 3 * W), F32),
        jax.ShapeDtypeStruct((batch, 1, LORA_PAD), F32),
    ]
    return pl.pallas_call(
        functools.partial(_rwkv_kernel, valid=valid),
        grid=(batch, nc),
        in_specs=in_specs,
        out_specs=out_specs,
        out_shape=out_shape,
        scratch_shapes=[pltpu.VMEM((1, 3 * W), F32), pltpu.VMEM((1, LORA_PAD), F32),
                        pltpu.VMEM((RWKV_PAIRS, LANES, LANES), F32)],
        compiler_params=_cparams(("parallel", "arbitrary")),
        name="rwkv",
    )(zr, zl, sh_r, sh_l, s0, *params)


def _pair_blockdiag(s):
    b = s.shape[0]
    s = s.reshape(b, RWKV_PAIRS, 2, RWKV_N, RWKV_N)
    z = jnp.zeros_like(s[:, :, 0])
    top = jnp.concatenate([s[:, :, 0], z], axis=-1)
    bot = jnp.concatenate([z, s[:, :, 1]], axis=-1)
    return jnp.concatenate([top, bot], axis=-2)


def _pair_unblock(s):
    b = s.shape[0]
    lo = s[:, :, :RWKV_N, :RWKV_N]
    hi = s[:, :, RWKV_N:, RWKV_N:]
    return jnp.stack([lo, hi], axis=2).reshape(b, RWKV_HEADS, RWKV_N, RWKV_N)


def _merge_kernel(oret_ref, orwkv_ref, gr_ref, gw_ref, wr_ref, ww_ref, o_ref):
    a = jnp.dot(oret_ref[...], wr_ref[...], preferred_element_type=F32)
    b = jnp.dot(orwkv_ref[...], ww_ref[...], preferred_element_type=F32)
    o_ref[...] = (jax.nn.sigmoid(gr_ref[...]) * a + jax.nn.sigmoid(gw_ref[...]) * b).astype(BF16)


def _merge(o_ret, o_rwkv, z, w_ret_o, w_rwkv_o):
    n = o_ret.shape[0]
    tn = TN_OUT
    return pl.pallas_call(
        _merge_kernel,
        grid=(n // TM, D_MODEL // tn),
        in_specs=[
            pl.BlockSpec((TM, RET_V), lambda i, j: (i, 0)),
            pl.BlockSpec((TM, RWKV_W), lambda i, j: (i, 0)),
            pl.BlockSpec((TM, tn), lambda i, j: (i, ZGR // tn + j)),
            pl.BlockSpec((TM, tn), lambda i, j: (i, ZGW // tn + j)),
            pl.BlockSpec((RET_V, tn), lambda i, j: (0, j)),
            pl.BlockSpec((RWKV_W, tn), lambda i, j: (0, j)),
        ],
        out_specs=pl.BlockSpec((TM, tn), lambda i, j: (i, j)),
        out_shape=jax.ShapeDtypeStruct((n, D_MODEL), BF16),
        compiler_params=_cparams(("parallel", "arbitrary")),
        name="merge",
    )(o_ret, o_rwkv, z, z, w_ret_o, w_rwkv_o)


def _outproj_kernel(x_ref, m_ref, w_ref, o_ref):
    o_ref[...] = x_ref[...] + jnp.dot(m_ref[...], w_ref[...], preferred_element_type=F32)


def _outproj(x, merged, w_out):
    n = x.shape[0]
    tn = TN_OUT
    return pl.pallas_call(
        _outproj_kernel,
        grid=(n // TM, D_MODEL // tn),
        in_specs=[
            pl.BlockSpec((TM, tn), lambda i, j: (i, j)),
            pl.BlockSpec((TM, D_MODEL), lambda i, j: (i, 0)),
            pl.BlockSpec((D_MODEL, tn), lambda i, j: (0, j)),
        ],
        out_specs=pl.BlockSpec((TM, tn), lambda i, j: (i, j)),
        out_shape=jax.ShapeDtypeStruct((n, D_MODEL), F32),
        compiler_params=_cparams(("parallel", "arbitrary")),
        name="outproj",
    )(x, merged, w_out)


def kernel(x_prompt, x_sample, state_ret, state_rwkv, state_shift, norm_ffn1, ffn1_w_gu, ffn1_w_down, norm_mix, w_in, w_ret_o, rwkv_mu, rwkv_w0, rwkv_w_up, rwkv_a0, rwkv_a_up, rwkv_g_up, rwkv_k_k, rwkv_k_a, rwkv_r_k, rwkv_ln_w, rwkv_ln_b, w_rwkv_o, w_out, norm_ffn2, ffn2_w_gu, ffn2_w_down, norm_final):
    bp, tp, d = x_prompt.shape
    bs, ts, _ = x_sample.shape
    n_p = bp * tp
    n_s = bs * ts
    W = RWKV_W
    C = RWKV_CHUNK
    row = lambda a: a.reshape(1, -1)

    x = jnp.concatenate([x_prompt.reshape(n_p, d), x_sample.reshape(n_s, d)], axis=0)

    x1 = _ffn(x, row(norm_ffn1[0]), ffn1_w_gu[0].astype(BF16), ffn1_w_down[0].astype(BF16),
              row(norm_final), final_norm=False)

    lora0 = ZS + 3 * W
    w_main = jnp.concatenate([w_in[0][:, :lora0], w_in[0][:, lora0 + LORA_W:]], axis=1).astype(BF16)
    w_lora = jnp.pad(w_in[0][:, lora0:lora0 + LORA_W], ((0, 0), (0, LORA_PAD - LORA_W))).astype(BF16)
    z = _proj(x1, row(norm_mix[0]), w_main, TN_PROJ)
    zl = _proj(x1, row(norm_mix[0]), w_lora, LORA_PAD)

    lg = jnp.log(1.0 - 2.0 ** (-5.0 - jnp.arange(RET_HEADS, dtype=F32)))
    cos_p, sin_p = _rope_tables(jnp.arange(tp, dtype=F32))
    cos_s, sin_s = _rope_tables(PAST_LEN + jnp.arange(ts, dtype=F32))
    rep = 8 // ts
    oret_p, ret_p = _ret_prompt(z, lg, cos_p, sin_p, bp, tp)
    oret_s, ret_s = _ret_sample(z, lg, jnp.tile(cos_s, (rep, 1)), jnp.tile(sin_s, (rep, 1)),
                                state_ret[0], n_p, bs, ts)
    o_ret = jnp.concatenate([oret_p, oret_s], axis=0)

    mu = rwkv_mu[0]
    pad_l = lambda a: jnp.pad(a, [(0, 0)] * (a.ndim - 1) + [(0, LORA_PAD - LORA_W)])
    seg = (jnp.arange(LANES)[:, None] // RWKV_N == jnp.arange(LANES)[None, :] // RWKV_N).astype(F32)
    zero64 = jnp.zeros((DECAY_LORA, W), F32)
    params = (
        row(mu[:3 * W]), pad_l(row(mu[3 * W:])),
        row(rwkv_w0[0]), jnp.concatenate([rwkv_w_up[0], zero64], axis=0).astype(BF16),
        row(rwkv_a0[0]), jnp.concatenate([zero64, rwkv_a_up[0]], axis=0).astype(BF16),
        jnp.pad(rwkv_g_up[0], ((0, LORA_PAD - LANES - GATE_LORA), (0, 0))).astype(BF16),
        row(rwkv_k_k[0]), row(rwkv_k_a[0]), row(rwkv_r_k[0]), row(rwkv_ln_w[0]), row(rwkv_ln_b[0]),
        seg,
    )
    orw_p, rwkv_p, shr_p, shl_p = _rwkv(
        z, ZS // (3 * W), zl, jnp.zeros((bp, 1, 3 * W), F32), jnp.zeros((bp, 1, LORA_PAD), F32),
        jnp.zeros((bp, RWKV_PAIRS, LANES, LANES), F32), params, bp, tp // C, C)
    padc = lambda a: jnp.pad(a.reshape(bs, ts, -1), ((0, 0), (0, C - ts), (0, 0))).reshape(bs * C, -1)
    sh_s = state_shift[0]
    orw_s, rwkv_s, shr_s, shl_s = _rwkv(
        padc(z[n_p:, ZS:ZS + 3 * W]), 0, padc(zl[n_p:]),
        sh_s[:, None, :3 * W], pad_l(sh_s[:, None, 3 * W:]),
        _pair_blockdiag(state_rwkv[0]), params, bs, 1, ts)
    o_rwkv = jnp.concatenate([orw_p, orw_s.reshape(bs, C, W)[:, :ts].reshape(n_s, W)], axis=0)

    merged = _merge(o_ret, o_rwkv, z, w_ret_o[0].astype(BF16), w_rwkv_o[0].astype(BF16))
    x2 = _outproj(x1, merged, w_out[0].astype(BF16))
    y = _ffn(x2, row(norm_ffn2[0]), ffn2_w_gu[0].astype(BF16), ffn2_w_down[0].astype(BF16),
             row(norm_final), final_norm=True)

    shift_p = jnp.concatenate([shr_p[:, 0], shl_p[:, 0, :LORA_W]], axis=-1)[None]
    shift_s = jnp.concatenate([shr_s[:, 0], shl_s[:, 0, :LORA_W]], axis=-1)[None]
    return (y[:n_p].reshape(bp, tp, d), y[n_p:].reshape(bs, ts, d),
            ret_p[None], _pair_unblock(rwkv_p)[None], shift_p,
            ret_s[None], _pair_unblock(rwkv_s)[None], shift_s)
```

```python
import functools

import jax
import jax.numpy as jnp
from jax import lax
from jax.experimental import pallas as pl
from jax.experimental.pallas import tpu as pltpu

F32 = jnp.float32
BF16 = jnp.bfloat16

D_MODEL = 2048
PAST_LEN = 16384
RET_HEADS = 4
RET_DK = 256
RET_DV = 512
RET_CHUNK = 128
ROPE_BASE = 10000.0
RET_QK = RET_HEADS * RET_DK
RET_V = RET_HEADS * RET_DV
RWKV_HEADS = 16
RWKV_N = 64
RWKV_W = RWKV_HEADS * RWKV_N
DECAY_LORA = 64
AAA_LORA = 64
GATE_LORA = 160
LORA_W = DECAY_LORA + AAA_LORA + GATE_LORA
LORA_PAD = 384
D_FF = 5632
NORM_EPS = 1e-6
GN_EPS_RET = 1e-6
GN_EPS_RWKV = 64e-5

ZQ, ZK, ZV, ZG, ZS, ZGR, ZGW = 0, 1024, 2048, 4096, 6144, 9216, 11264

LANES = 128
RWKV_PAIRS = RWKV_W // LANES
RWKV_CHUNK = 64

VMEM_LIMIT = 56 * 1024 * 1024

TM = 512
TF = 512
TN_PROJ = 1024
TN_OUT = 512


def _rms(x, g, eps):
    return x * lax.rsqrt(jnp.mean(x * x, axis=-1, keepdims=True) + eps) * g


def _dot(a, b):
    return jnp.dot(a.astype(BF16), b.astype(BF16), preferred_element_type=F32)


def _dot_nt(a, b):
    return lax.dot_general(a.astype(BF16), b.astype(BF16), (((1,), (1,)), ((), ())),
                           preferred_element_type=F32)


def _dot_tn(a, b):
    return lax.dot_general(a.astype(BF16), b.astype(BF16), (((0,), (0,)), ((), ())),
                           preferred_element_type=F32)


def _bf16_terms(x, n):
    terms = []
    for _ in range(n):
        t = x.astype(BF16)
        terms.append(t)
        x = x - t.astype(F32)
    return terms


def _cparams(sem):
    return pltpu.CompilerParams(dimension_semantics=sem, vmem_limit_bytes=VMEM_LIMIT)


def _ffn_kernel(*refs, final_norm, tiles_a, two_in, two_out):
    xs = refs[:2] if two_in else refs[:1]
    g_ref, wg_ref, wu_ref, wd_ref, gf_ref = refs[len(xs):len(xs) + 5]
    outs = refs[len(xs) + 5:-2]
    h_ref, acc_ref = refs[-2:]
    i = pl.program_id(0)
    j = pl.program_id(1)

    def x_tile():
        if two_in:
            return jnp.where(i < tiles_a, xs[0][...], xs[1][...])
        return xs[0][...]

    @pl.when(j == 0)
    def _():
        h_ref[...] = _rms(x_tile(), g_ref[...], NORM_EPS).astype(BF16)
        acc_ref[...] = jnp.zeros_like(acc_ref)

    h = h_ref[...]
    gate = jnp.dot(h, wg_ref[...], preferred_element_type=F32)
    up = jnp.dot(h, wu_ref[...], preferred_element_type=F32)
    act = (gate * jax.nn.sigmoid(gate)) * up
    acc_ref[...] += jnp.dot(act.astype(BF16), wd_ref[...], preferred_element_type=F32)

    @pl.when(j == pl.num_programs(1) - 1)
    def _():
        y = x_tile() + 0.5 * acc_ref[...]
        if final_norm:
            y = _rms(y, gf_ref[...], NORM_EPS)
        if two_out:
            @pl.when(i < tiles_a)
            def _():
                outs[0][...] = y

            @pl.when(i >= tiles_a)
            def _():
                outs[1][...] = y
        else:
            outs[0][...] = y


def _ffn(xs, g, w_gu, w_down, g_final, final_norm, out_rows):
    two_in = len(xs) == 2
    two_out = len(out_rows) == 2
    n = sum(x.shape[0] for x in xs)
    tiles_a = (xs[0].shape[0] if two_in else out_rows[0]) // TM
    nj = D_FF // TF
    first = lambda i, j: (jnp.minimum(i, tiles_a - 1), 0)
    second = lambda i, j: (jnp.maximum(i - tiles_a, 0), 0)
    whole = lambda i, j: (i, 0)
    tile = lambda m: pl.BlockSpec((TM, D_MODEL), m)
    return pl.pallas_call(
        functools.partial(_ffn_kernel, final_norm=final_norm, tiles_a=tiles_a, two_in=two_in, two_out=two_out),
        grid=(n // TM, nj),
        in_specs=([tile(first), tile(second)] if two_in else [tile(whole)]) + [
            pl.BlockSpec((1, D_MODEL), lambda i, j: (0, 0)),
            pl.BlockSpec((D_MODEL, TF), lambda i, j: (0, j)),
            pl.BlockSpec((D_MODEL, TF), lambda i, j: (0, j + D_FF // TF)),
            pl.BlockSpec((TF, D_MODEL), lambda i, j: (j, 0)),
            pl.BlockSpec((1, D_MODEL), lambda i, j: (0, 0)),
        ],
        out_specs=[tile(first), tile(second)] if two_out else [tile(whole)],
        out_shape=[jax.ShapeDtypeStruct((r, D_MODEL), F32) for r in out_rows],
        scratch_shapes=[pltpu.VMEM((TM, D_MODEL), BF16), pltpu.VMEM((TM, D_MODEL), F32)],
        compiler_params=_cparams(("arbitrary", "arbitrary")),
        name="ffn",
    )(*xs, g, w_gu, w_gu, w_down, g_final)


def _proj_kernel(x_ref, g_ref, w_ref, o_ref, h_ref):
    @pl.when(pl.program_id(1) == 0)
    def _():
        h_ref[...] = _rms(x_ref[...], g_ref[...], NORM_EPS).astype(BF16)

    o_ref[...] = jnp.dot(h_ref[...], w_ref[...], preferred_element_type=F32)


def _proj(x, g, w, tn):
    n = x.shape[0]
    nout = w.shape[1]
    return pl.pallas_call(
        _proj_kernel,
        grid=(n // TM, nout // tn),
        in_specs=[
            pl.BlockSpec((TM, D_MODEL), lambda i, j: (i, 0)),
            pl.BlockSpec((1, D_MODEL), lambda i, j: (0, 0)),
            pl.BlockSpec((D_MODEL, tn), lambda i, j: (0, j)),
        ],
        out_specs=pl.BlockSpec((TM, tn), lambda i, j: (i, j)),
        out_shape=jax.ShapeDtypeStruct((n, nout), F32),
        scratch_shapes=[pltpu.VMEM((TM, D_MODEL), BF16)],
        compiler_params=_cparams(("parallel", "arbitrary")),
        name="proj",
    )(x, g, w)


def _pair_swap(x):
    n = x.shape[-1]
    lane = lax.broadcasted_iota(jnp.int32, x.shape, x.ndim - 1)
    prev = pltpu.roll(x, 1, x.ndim - 1)
    nxt = pltpu.roll(x, n - 1, x.ndim - 1)
    return jnp.where((lane & 1) == 1, prev, nxt)


def _rotary(x, cos, sin_signed):
    return x * cos + _pair_swap(x) * sin_signed


def _ret_finish(o, zg):
    o = o * lax.rsqrt(jnp.mean(o * o, axis=-1, keepdims=True) + GN_EPS_RET)
    return o * (zg * jax.nn.sigmoid(zg))


def _ret_prompt_kernel(lg_ref, q_ref, k_ref, v_ref, g_ref, cos_ref, sin_ref,
                       o_ref, s_out_ref, s_ref):
    c = pl.program_id(2)
    lg = lg_ref[pl.program_id(1)]
    C = RET_CHUNK

    @pl.when(c == 0)
    def _():
        s_ref[...] = jnp.zeros_like(s_ref)

    cos = cos_ref[...]
    sin = sin_ref[...]
    q = _rotary(q_ref[...], cos, sin)
    k = _rotary(k_ref[...], cos, sin) * (RET_DK ** -0.5)
    v = v_ref[...]

    ti = lax.broadcasted_iota(jnp.int32, (C, C), 0)
    tj = lax.broadcasted_iota(jnp.int32, (C, C), 1)
    rel = (ti - tj).astype(F32)
    mask = jnp.where(rel >= 0, jnp.exp(lg * jnp.maximum(rel, 0.0)), 0.0)
    idx = lax.broadcasted_iota(jnp.int32, (C, 1), 0).astype(F32)
    q_dec = jnp.exp(lg * (idx + 1.0))
    k_dec = jnp.exp(lg * (C - 1.0 - idx))

    s = s_ref[...]
    scores = _dot_nt(q, k) * mask
    inner = _dot(scores, v)
    cross = _dot(q, s) * q_dec
    s_new = jnp.exp(lg * C) * s + _dot_tn(k * k_dec, v)
    s_ref[...] = s_new
    o_ref[...] = _ret_finish(inner + cross, g_ref[...]).astype(BF16)

    @pl.when(c == pl.num_programs(2) - 1)
    def _():
        s_out_ref[0, 0] = s_new


def _ret_prompt(z, lg, cos, sin, batch, seq):
    nc = seq // RET_CHUNK
    C = RET_CHUNK
    return pl.pallas_call(
        _ret_prompt_kernel,
        grid=(batch, RET_HEADS, nc),
        in_specs=[
            pl.BlockSpec(memory_space=pltpu.SMEM),
            pl.BlockSpec((C, RET_DK), lambda b, h, c: (b * nc + c, ZQ // RET_DK + h)),
            pl.BlockSpec((C, RET_DK), lambda b, h, c: (b * nc + c, ZK // RET_DK + h)),
            pl.BlockSpec((C, RET_DV), lambda b, h, c: (b * nc + c, ZV // RET_DV + h)),
            pl.BlockSpec((C, RET_DV), lambda b, h, c: (b * nc + c, ZG // RET_DV + h)),
            pl.BlockSpec((C, RET_DK), lambda b, h, c: (c, 0)),
            pl.BlockSpec((C, RET_DK), lambda b, h, c: (c, 0)),
        ],
        out_specs=[
            pl.BlockSpec((C, RET_DV), lambda b, h, c: (b * nc + c, h)),
            pl.BlockSpec((1, 1, RET_DK, RET_DV), lambda b, h, c: (b, h, 0, 0)),
        ],
        out_shape=[
            jax.ShapeDtypeStruct((z.shape[0], RET_V), BF16),
            jax.ShapeDtypeStruct((batch, RET_HEADS, RET_DK, RET_DV), F32),
        ],
        scratch_shapes=[pltpu.VMEM((RET_DK, RET_DV), F32)],
        compiler_params=_cparams(("parallel", "parallel", "arbitrary")),
        name="ret_prompt",
    )(lg, z, z, z, z, cos, sin)


RET_SAMPLE_ROWS = 16


def _ret_sample_kernel(lg_ref, q_ref, k_ref, v_ref, g_ref, cos_ref, sin_ref, s0_ref, o_prev_ref,
                       o_ref, s_out_ref, *, seq):
    del o_prev_ref
    R = RET_SAMPLE_ROWS
    nb = R // seq
    cos = cos_ref[...]
    sin = sin_ref[...]
    ri = lax.broadcasted_iota(jnp.int32, (R, R), 0)
    rj = lax.broadcasted_iota(jnp.int32, (R, R), 1)
    rel = ((ri % seq) - (rj % seq)).astype(F32)
    ok = ((ri // seq) == (rj // seq)) & (rel >= 0)
    row = lax.broadcasted_iota(jnp.int32, (R, 1), 0)
    t = (row % seq).astype(F32)

    for h in range(RET_HEADS):
        lg = lg_ref[h]
        q = _rotary(q_ref[:, h * RET_DK:(h + 1) * RET_DK], cos, sin)
        k = _rotary(k_ref[:, h * RET_DK:(h + 1) * RET_DK], cos, sin) * (RET_DK ** -0.5)
        v = v_ref[:, h * RET_DV:(h + 1) * RET_DV]
        mask = jnp.where(ok, jnp.exp(lg * jnp.maximum(rel, 0.0)), 0.0)
        q_dec = jnp.exp(lg * (t + 1.0))
        kd = k * jnp.exp(lg * (seq - 1.0 - t))
        inner = _dot(_dot_nt(q, k) * mask, v)
        cross = jnp.zeros_like(inner)
        for b in range(nb):
            s = s0_ref[b, h]
            mine = (row // seq) == b
            cross = jnp.where(mine, _dot(q, s), cross)
            s_out_ref[b, h] = jnp.exp(lg * seq) * s + _dot_tn(jnp.where(mine, kd, 0.0), v)
        o_ref[:, h * RET_DV:(h + 1) * RET_DV] = _ret_finish(
            inner + cross * q_dec, g_ref[:, h * RET_DV:(h + 1) * RET_DV]).astype(BF16)


def _ret_sample(z, lg, cos, sin, s0, o_prev, row0, batch, seq):
    R = RET_SAMPLE_ROWS
    nb = R // seq
    rb0 = row0 // R
    return pl.pallas_call(
        functools.partial(_ret_sample_kernel, seq=seq),
        grid=(batch // nb,),
        in_specs=[
            pl.BlockSpec(memory_space=pltpu.SMEM),
            pl.BlockSpec((R, RET_QK), lambda i: (rb0 + i, ZQ // RET_QK)),
            pl.BlockSpec((R, RET_QK), lambda i: (rb0 + i, ZK // RET_QK)),
            pl.BlockSpec((R, RET_V), lambda i: (rb0 + i, ZV // RET_V)),
            pl.BlockSpec((R, RET_V), lambda i: (rb0 + i, ZG // RET_V)),
            pl.BlockSpec((R, RET_DK), lambda i: (0, 0)),
            pl.BlockSpec((R, RET_DK), lambda i: (0, 0)),
            pl.BlockSpec((nb, RET_HEADS, RET_DK, RET_DV), lambda i: (i, 0, 0, 0)),
            pl.BlockSpec(memory_space=pl.ANY),
        ],
        out_specs=[
            pl.BlockSpec((R, RET_V), lambda i: (rb0 + i, 0)),
            pl.BlockSpec((nb, RET_HEADS, RET_DK, RET_DV), lambda i: (i, 0, 0, 0)),
        ],
        out_shape=[
            jax.ShapeDtypeStruct(o_prev.shape, BF16),
            jax.ShapeDtypeStruct((batch, RET_HEADS, RET_DK, RET_DV), F32),
        ],
        input_output_aliases={8: 0},
        compiler_params=_cparams(("parallel",)),
        name="ret_sample",
    )(lg, z, z, z, z, cos, sin, s0, o_prev)


def _rope_tables(pos):
    half = RET_DK // 2
    inv = 1.0 / (ROPE_BASE ** jnp.linspace(0.0, 1.0, half, dtype=F32))
    ang = pos[:, None] * inv[None, :]
    cos = jnp.repeat(jnp.cos(ang), 2, axis=-1)
    sin = jnp.sin(ang)
    sin_signed = jnp.stack([-sin, sin], axis=-1).reshape(pos.shape[0], RET_DK)
    return cos, sin_signed


def _segsum(x, seg_ref):
    rows = x.shape[0]
    n = rows * RWKV_PAIRS
    xs = jnp.concatenate([x[:, p * LANES:(p + 1) * LANES] for p in range(RWKV_PAIRS)], axis=0)
    ss = jnp.dot(jnp.concatenate(_bf16_terms(xs, 2), axis=0), seg_ref[...], preferred_element_type=F32)
    ss = ss[0:n] + ss[n:2 * n]
    return jnp.concatenate([ss[p * rows:(p + 1) * rows] for p in range(RWKV_PAIRS)], axis=1)


def _rwkv_chunk(zr, zl, prev_r, prev_l, prm, n_seq, read_state, update_state):
    (mur_ref, mul_ref, w0_ref, wup_ref, a0_ref, aup_ref, gup_ref,
     kk_ref, ka_ref, rk_ref, lnw_ref, lnb_ref, seg_ref) = prm
    C = RWKV_CHUNK
    W = RWKV_W
    TS = C // n_seq
    row = lax.broadcasted_iota(jnp.int32, (C, 1), 0)
    first = (row % TS) == 0

    def mix(z, prev, mu_ref):
        z_prev = jnp.where(first, prev, pltpu.roll(z, 1, 0))
        return z + (z_prev - z) * mu_ref[...]

    xs = mix(zr, prev_r, mur_ref)
    xl = mix(zl, prev_l, mul_ref)
    r = xs[:, 0:W]
    k = xs[:, W:2 * W]
    v = xs[:, 2 * W:3 * W]
    x_wa = xl[:, 0:LANES]
    x_g = xl[:, LANES:LORA_PAD]

    y_w = w0_ref[...] + _dot(jnp.tanh(x_wa), wup_ref[...])
    w_log = -(jnp.maximum(-y_w, 0.0) + jnp.log1p(jnp.exp(-jnp.abs(y_w)))) - 0.5
    lw = -jnp.exp(w_log)
    a_rate = jax.nn.sigmoid(a0_ref[...] + _dot(x_wa, aup_ref[...]))
    g = _dot(jax.nn.sigmoid(x_g), gup_ref[...])

    kk = k * kk_ref[...]
    k = k * (1.0 + (a_rate - 1.0) * ka_ref[...])
    sums = _segsum(jnp.concatenate([kk * kk, r * k * rk_ref[...]], axis=0), seg_ref)
    kk = kk * lax.rsqrt(jnp.maximum(sums[0:C], 1e-24))
    bonus_rk = sums[C:2 * C]
    a = -kk
    b = kk * a_rate

    ti = lax.broadcasted_iota(jnp.int32, (2 * C, C), 0)
    tj = lax.broadcasted_iota(jnp.int32, (2 * C, C), 1)
    same = ((ti % C) // TS) == (tj // TS)
    sel = (same & ((ti >= C) | (tj <= ti))).astype(BF16)
    sums_lw = sum(jnp.dot(sel, t, preferred_element_type=F32) for t in _bf16_terms(lw, 3))
    cum = sums_lw[0:C]
    tot = sums_lw[C:2 * C]
    e_neg = jnp.exp(-cum)
    a_t = a * jnp.exp(cum - lw)
    b_t = b * e_neg
    k_t = k * e_neg
    r_t = r * jnp.exp(cum)
    e_tail = jnp.exp(tot - cum)
    b_g = b * e_tail
    k_g = k * e_tail
    dec = jnp.exp(tot)

    lane = lax.broadcasted_iota(jnp.int32, (1, LANES), 1)
    lo = lane < RWKV_N

    def split(x):
        return jnp.concatenate([jnp.where(lo, x, 0.0), jnp.where(lo, 0.0, x)], axis=0)

    row4 = lax.broadcasted_iota(jnp.int32, (C, 4 * C), 0)
    col4 = lax.broadcasted_iota(jnp.int32, (C, 4 * C), 1) % C
    same4 = (row4 // TS) == (col4 // TS)
    strict = same4 & (col4 < row4)
    incl = same4 & (col4 <= row4)
    row2 = lax.broadcasted_iota(jnp.int32, (C, 2 * C), 0)
    col2 = lax.broadcasted_iota(jnp.int32, (C, 2 * C), 1)
    eye2 = ((col2 % C) == row2).astype(F32)
    left = col2 < C

    sls = [slice(p * LANES, (p + 1) * LANES) for p in range(RWKV_PAIRS)]
    v2s = [split(v[:, sl]) for sl in sls]
    ars = [jnp.concatenate([a_t[:, sl], r_t[:, sl]], axis=0) for sl in sls]
    gms = [_dot_nt(ar, jnp.concatenate([split(b_t[:, sl]), split(k_t[:, sl])], axis=0))
           for ar, sl in zip(ars, sls)]
    gas = [jnp.where(strict, gm[0:C], 0.0) for gm in gms]
    grs = [jnp.where(incl, gm[C:2 * C], 0.0) for gm in gms]
    pws = [jnp.concatenate([jnp.where(left, ga[:, 0:2 * C], 0.0), jnp.where(left, 0.0, ga[:, 0:2 * C])], axis=0)
           for ga in gas]
    ts = [eye2 + ga[:, 0:2 * C] for ga in gas]
    for _ in range(TS.bit_length() - 2):
        pws = [_dot(pw, pw) for pw in pws]
        ts = [t + _dot(t, pw) for t, pw in zip(ts, pws)]
    wis = [_dot(ga[:, 2 * C:4 * C], v2) for ga, v2 in zip(gas, v2s)]
    wys = read_state(ars)
    us = [_dot(t, split(wi + wy[0:C])) for t, wi, wy in zip(ts, wis, wys)]
    ys = [wy[C:2 * C] + _dot(gr, jnp.concatenate([split(u), v2], axis=0))
          for wy, gr, u, v2 in zip(wys, grs, us, v2s)]
    update_state([jnp.concatenate([u, v[:, sl]], axis=0) for u, sl in zip(us, sls)],
                 [jnp.concatenate([b_g[:, sl], k_g[:, sl]], axis=0) for sl in sls], dec)

    y = jnp.concatenate(ys, axis=1)
    mean = _segsum(y, seg_ref) * (1.0 / RWKV_N)
    d = y - mean
    var = _segsum(d * d, seg_ref) * (1.0 / RWKV_N)
    yn = d * lax.rsqrt(var + GN_EPS_RWKV) * lnw_ref[...] + lnb_ref[...]
    return ((yn + bonus_rk * v) * g).astype(BF16)


def _rwkv_prompt_kernel(zr_ref, zl_ref, *rest):
    prm = rest[:13]
    o_ref, s_out_ref, prevr_ref, prevl_ref, s_ref = rest[13:]
    C = RWKV_CHUNK
    c = pl.program_id(1)

    @pl.when(c == 0)
    def _():
        prevr_ref[...] = jnp.zeros_like(prevr_ref)
        prevl_ref[...] = jnp.zeros_like(prevl_ref)
        s_ref[...] = jnp.zeros_like(s_ref)

    zr = zr_ref[...]
    zl = zl_ref[...]
    vi = lax.broadcasted_iota(jnp.int32, (LANES, LANES), 0)
    ki = lax.broadcasted_iota(jnp.int32, (LANES, LANES), 1)
    diag_blocks = (vi < RWKV_N) == (ki < RWKV_N)

    def read_state(ars):
        return [_dot_nt(ar, s_ref[p]) for p, ar in enumerate(ars)]

    def update_state(uvs, bks, dec):
        upds = [_dot_tn(uv, bk) for uv, bk in zip(uvs, bks)]
        for p, upd in enumerate(upds):
            s_ref[p] = s_ref[p] * dec[0:1, p * LANES:(p + 1) * LANES] + jnp.where(diag_blocks, upd, 0.0)

    o_ref[...] = _rwkv_chunk(zr, zl, prevr_ref[...], prevl_ref[...], prm, 1, read_state, update_state)
    prevr_ref[...] = zr[C - 1:C, :]
    prevl_ref[...] = zl[C - 1:C, :]

    @pl.when(c == pl.num_programs(1) - 1)
    def _():
        s_out_ref[0] = s_ref[...]


def _rwkv_sample_kernel(zr_ref, zl_ref, shr_ref, shl_ref, s0_ref, *rest, n_seq):
    prm = rest[:13]
    o_ref, s_out_ref = rest[14:]
    C = RWKV_CHUNK
    N = RWKV_N
    TS = C // n_seq
    tokseq = (lax.broadcasted_iota(jnp.int32, (1, LANES), 1) % C) // TS

    def pick(x, j):
        return jnp.where(tokseq == j, x, 0.0)

    def stacked(h):
        return s0_ref[:, h].reshape(n_seq * N, N)

    def read_state(ars):
        res = [[_dot_nt(stacked(2 * p + e), ar[:, e * N:(e + 1) * N]) for e in range(2)]
               for p, ar in enumerate(ars)]
        out = []
        for rp in res:
            halves = []
            for re in rp:
                acc = pick(re[0:N], 0)
                for j in range(1, n_seq):
                    acc = acc + pick(re[j * N:(j + 1) * N], j)
                halves.append(acc)
            out.append(jnp.concatenate(halves, axis=0).T)
        return out

    def update_state(uvs, bks, dec):
        uvts = [uv.T for uv in uvs]
        upds = [[_dot(jnp.concatenate([pick(uvt[e * N:(e + 1) * N], j) for j in range(n_seq)], axis=0),
                      bk[:, e * N:(e + 1) * N]) for e in range(2)]
                for uvt, bk in zip(uvts, bks)]
        for p, up in enumerate(upds):
            for e, upd in enumerate(up):
                h = 2 * p + e
                dec_h = jnp.concatenate(
                    [jnp.broadcast_to(dec[j * TS:j * TS + 1, h * N:(h + 1) * N], (N, N)) for j in range(n_seq)],
                    axis=0)
                s_out_ref[:, h] = (stacked(h) * dec_h + upd).reshape(n_seq, N, N)

    o_ref[...] = _rwkv_chunk(zr_ref[...], zl_ref[...], shr_ref[...], shl_ref[...], prm, n_seq,
                             read_state, update_state)


def _rwkv_param_specs():
    W = RWKV_W
    shapes = [(1, 3 * W), (1, LORA_PAD), (1, W), (LANES, W), (1, W), (LANES, W), (LORA_PAD - LANES, W),
              (1, W), (1, W), (1, W), (1, W), (1, W), (LANES, LANES)]
    return [pl.BlockSpec(s, lambda *_: (0, 0)) for s in shapes]


def _rwkv_prompt(z, zl, params, batch, seq):
    C = RWKV_CHUNK
    W = RWKV_W
    nc = seq // C
    return pl.pallas_call(
        _rwkv_prompt_kernel,
        grid=(batch, nc),
        in_specs=[
            pl.BlockSpec((C, 3 * W), lambda b, c: (b * nc + c, ZS // (3 * W))),
            pl.BlockSpec((C, LORA_PAD), lambda b, c: (b * nc + c, 0)),
        ] + _rwkv_param_specs(),
        out_specs=[
            pl.BlockSpec((C, W), lambda b, c: (b * nc + c, 0)),
            pl.BlockSpec((1, RWKV_PAIRS, LANES, LANES), lambda b, c: (b, 0, 0, 0)),
        ],
        out_shape=[
            jax.ShapeDtypeStruct((z.shape[0], W), BF16),
            jax.ShapeDtypeStruct((batch, RWKV_PAIRS, LANES, LANES), F32),
        ],
        scratch_shapes=[pltpu.VMEM((1, 3 * W), F32), pltpu.VMEM((1, LORA_PAD), F32),
                        pltpu.VMEM((RWKV_PAIRS, LANES, LANES), F32)],
        compiler_params=_cparams(("parallel", "arbitrary")),
        name="rwkv_prompt",
    )(z, zl, *params)


def _rwkv_sample(z, zl, sh_r, sh_l, s0, params, o_prev, row0, batch, seq):
    C = RWKV_CHUNK
    W = RWKV_W
    n_seq = C // seq
    rb0 = row0 // C
    n_in = 5 + len(params)
    return pl.pallas_call(
        functools.partial(_rwkv_sample_kernel, n_seq=n_seq),
        grid=(batch // n_seq,),
        in_specs=[
            pl.BlockSpec((C, 3 * W), lambda i: (rb0 + i, ZS // (3 * W))),
            pl.BlockSpec((C, LORA_PAD), lambda i: (rb0 + i, 0)),
            pl.BlockSpec((C, 3 * W), lambda i: (i, 0)),
            pl.BlockSpec((C, LORA_PAD), lambda i: (i, 0)),
            pl.BlockSpec((n_seq, RWKV_HEADS, RWKV_N, RWKV_N), lambda i: (i, 0, 0, 0)),
        ] + _rwkv_param_specs() + [pl.BlockSpec(memory_space=pl.ANY)],
        out_specs=[
            pl.BlockSpec((C, W), lambda i: (rb0 + i, 0)),
            pl.BlockSpec((n_seq, RWKV_HEADS, RWKV_N, RWKV_N), lambda i: (i, 0, 0, 0)),
        ],
        out_shape=[
            jax.ShapeDtypeStruct(o_prev.shape, BF16),
            jax.ShapeDtypeStruct((batch, RWKV_HEADS, RWKV_N, RWKV_N), F32),
        ],
        input_output_aliases={n_in: 0},
        compiler_params=_cparams(("parallel",)),
        name="rwkv_sample",
    )(z, zl, sh_r, sh_l, s0, *params, o_prev)


def _pair_unblock(s):
    b = s.shape[0]
    lo = s[:, :, :RWKV_N, :RWKV_N]
    hi = s[:, :, RWKV_N:, RWKV_N:]
    return jnp.stack([lo, hi], axis=2).reshape(b, RWKV_HEADS, RWKV_N, RWKV_N)


def _merge_kernel(oret_ref, orwkv_ref, gr_ref, gw_ref, wr_ref, ww_ref, o_ref):
    a = jnp.dot(oret_ref[...], wr_ref[...], preferred_element_type=F32)
    b = jnp.dot(orwkv_ref[...], ww_ref[...], preferred_element_type=F32)
    o_ref[...] = (jax.nn.sigmoid(gr_ref[...]) * a + jax.nn.sigmoid(gw_ref[...]) * b).astype(BF16)


def _merge(o_ret, o_rwkv, z, w_ret_o, w_rwkv_o):
    n = o_ret.shape[0]
    tn = TN_OUT
    return pl.pallas_call(
        _merge_kernel,
        grid=(n // TM, D_MODEL // tn),
        in_specs=[
            pl.BlockSpec((TM, RET_V), lambda i, j: (i, 0)),
            pl.BlockSpec((TM, RWKV_W), lambda i, j: (i, 0)),
            pl.BlockSpec((TM, tn), lambda i, j: (i, ZGR // tn + j)),
            pl.BlockSpec((TM, tn), lambda i, j: (i, ZGW // tn + j)),
            pl.BlockSpec((RET_V, tn), lambda i, j: (0, j)),
            pl.BlockSpec((RWKV_W, tn), lambda i, j: (0, j)),
        ],
        out_specs=pl.BlockSpec((TM, tn), lambda i, j: (i, j)),
        out_shape=jax.ShapeDtypeStruct((n, D_MODEL), BF16),
        compiler_params=_cparams(("parallel", "arbitrary")),
        name="merge",
    )(o_ret, o_rwkv, z, z, w_ret_o, w_rwkv_o)


def _outproj_kernel(x_ref, m_ref, w_ref, o_ref):
    o_ref[...] = x_ref[...] + jnp.dot(m_ref[...], w_ref[...], preferred_element_type=F32)


def _outproj(x, merged, w_out):
    n = x.shape[0]
    tn = TN_OUT
    return pl.pallas_call(
        _outproj_kernel,
        grid=(n // TM, D_MODEL // tn),
        in_specs=[
            pl.BlockSpec((TM, tn), lambda i, j: (i, j)),
            pl.BlockSpec((TM, D_MODEL), lambda i, j: (i, 0)),
            pl.BlockSpec((D_MODEL, tn), lambda i, j: (0, j)),
        ],
        out_specs=pl.BlockSpec((TM, tn), lambda i, j: (i, j)),
        out_shape=jax.ShapeDtypeStruct((n, D_MODEL), F32),
        compiler_params=_cparams(("parallel", "arbitrary")),
        name="outproj",
    )(x, merged, w_out)


def kernel(x_prompt, x_sample, state_ret, state_rwkv, state_shift, norm_ffn1, ffn1_w_gu, ffn1_w_down, norm_mix, w_in, w_ret_o, rwkv_mu, rwkv_w0, rwkv_w_up, rwkv_a0, rwkv_a_up, rwkv_g_up, rwkv_k_k, rwkv_k_a, rwkv_r_k, rwkv_ln_w, rwkv_ln_b, w_rwkv_o, w_out, norm_ffn2, ffn2_w_gu, ffn2_w_down, norm_final):
    bp, tp, d = x_prompt.shape
    bs, ts, _ = x_sample.shape
    n_p = bp * tp
    n_s = bs * ts
    W = RWKV_W
    row = lambda a: a.reshape(1, -1)

    (x1,) = _ffn((x_prompt.reshape(n_p, d), x_sample.reshape(n_s, d)), row(norm_ffn1[0]),
                 ffn1_w_gu[0].astype(BF16), ffn1_w_down[0].astype(BF16), row(norm_final),
                 final_norm=False, out_rows=(n_p + n_s,))

    lora0 = ZS + 3 * W
    w_main = jnp.concatenate([w_in[0][:, :lora0], w_in[0][:, lora0 + LORA_W:]], axis=1).astype(BF16)
    w_lora = jnp.pad(w_in[0][:, lora0:lora0 + LORA_W], ((0, 0), (0, LORA_PAD - LORA_W))).astype(BF16)
    z = _proj(x1, row(norm_mix[0]), w_main, TN_PROJ)
    zl = _proj(x1, row(norm_mix[0]), w_lora, LORA_PAD)

    lg = jnp.log(1.0 - 2.0 ** (-5.0 - jnp.arange(RET_HEADS, dtype=F32)))
    cos_p, sin_p = _rope_tables(jnp.arange(tp, dtype=F32))
    cos_s, sin_s = _rope_tables(PAST_LEN + jnp.arange(ts, dtype=F32))
    rep = RET_SAMPLE_ROWS // ts
    o_ret, ret_p = _ret_prompt(z, lg, cos_p, sin_p, bp, tp)
    o_ret, ret_s = _ret_sample(z, lg, jnp.tile(cos_s, (rep, 1)), jnp.tile(sin_s, (rep, 1)),
                               state_ret[0], o_ret, n_p, bs, ts)

    mu = rwkv_mu[0]
    pad_l = lambda a: jnp.pad(a, [(0, 0)] * (a.ndim - 1) + [(0, LORA_PAD - LORA_W)])
    seg = (jnp.arange(LANES)[:, None] // RWKV_N == jnp.arange(LANES)[None, :] // RWKV_N).astype(BF16)
    zero64 = jnp.zeros((DECAY_LORA, W), F32)
    params = (
        row(mu[:3 * W]), pad_l(row(mu[3 * W:])),
        row(rwkv_w0[0]), jnp.concatenate([rwkv_w_up[0], zero64], axis=0).astype(BF16),
        row(rwkv_a0[0]), jnp.concatenate([zero64, rwkv_a_up[0]], axis=0).astype(BF16),
        jnp.pad(rwkv_g_up[0], ((0, LORA_PAD - LANES - GATE_LORA), (0, 0))).astype(BF16),
        row(rwkv_k_k[0]), row(rwkv_k_a[0]), row(rwkv_r_k[0]), row(rwkv_ln_w[0]), row(rwkv_ln_b[0]),
        seg,
    )
    o_rwkv, rwkv_p = _rwkv_prompt(z, zl, params, bp, tp)
    sh_s = state_shift[0]
    first_rows = lambda a: jnp.pad(a[:, None, :], ((0, 0), (0, ts - 1), (0, 0))).reshape(n_s, -1)
    o_rwkv, rwkv_s = _rwkv_sample(z, zl, first_rows(sh_s[:, :3 * W]), first_rows(pad_l(sh_s[:, 3 * W:])),
                                  state_rwkv[0], params, o_rwkv, n_p, bs, ts)

    merged = _merge(o_ret, o_rwkv, z, w_ret_o[0].astype(BF16), w_rwkv_o[0].astype(BF16))
    x2 = _outproj(x1, merged, w_out[0].astype(BF16))
    y_p, y_s = _ffn((x2,), row(norm_ffn2[0]), ffn2_w_gu[0].astype(BF16), ffn2_w_down[0].astype(BF16),
                    row(norm_final), final_norm=True, out_rows=(n_p, n_s))

    def last_rows(lo, hi, t):
        zr_last = z[lo:hi].reshape(-1, t, z.shape[1])[:, -1, ZS:ZS + 3 * W]
        zl_last = zl[lo:hi].reshape(-1, t, LORA_PAD)[:, -1, :LORA_W]
        return jnp.concatenate([zr_last, zl_last], axis=-1)[None]

    return (y_p.reshape(bp, tp, d), y_s.reshape(bs, ts, d),
            ret_p[None], _pair_unblock(rwkv_p)[None], last_rows(0, n_p, tp),
            ret_s[None], rwkv_s[None], last_rows(n_p, n_p + n_s, ts))
```

```python
import functools

import jax
import jax.numpy as jnp
from jax import lax
from jax.experimental import pallas as pl
from jax.experimental.pallas import tpu as pltpu

F32 = jnp.float32
BF16 = jnp.bfloat16

D_MODEL = 2048
PAST_LEN = 16384
RET_HEADS = 4
RET_DK = 256
RET_DV = 512
RET_CHUNK = 128
ROPE_BASE = 10000.0
RET_QK = RET_HEADS * RET_DK
RET_V = RET_HEADS * RET_DV
RWKV_HEADS = 16
RWKV_N = 64
RWKV_W = RWKV_HEADS * RWKV_N
DECAY_LORA = 64
AAA_LORA = 64
GATE_LORA = 160
LORA_W = DECAY_LORA + AAA_LORA + GATE_LORA
LORA_PAD = 384
D_FF = 5632
NORM_EPS = 1e-6
GN_EPS_RET = 1e-6
GN_EPS_RWKV = 64e-5

ZQ, ZK, ZV, ZG, ZS = 0, 1024, 2048, 4096, 6144
Z_MAIN_W = 9216
ZGR, ZGW = 0, 2048

LANES = 128
RWKV_PAIRS = RWKV_W // LANES
RWKV_CHUNK = 64
RWKV_PROMPT_GROUP = 2
RWKV_PROMPT_LEAD = 4

VMEM_LIMIT = 56 * 1024 * 1024

TM = 512
TF = 512
TN_PROJ = 1024
TN_OUT = 512


def _rms(x, g, eps):
    return x * lax.rsqrt(jnp.mean(x * x, axis=-1, keepdims=True) + eps) * g


def _dot(a, b):
    return jnp.dot(a.astype(BF16), b.astype(BF16), preferred_element_type=F32)


def _dot_nt(a, b):
    return lax.dot_general(a.astype(BF16), b.astype(BF16), (((1,), (1,)), ((), ())),
                           preferred_element_type=F32)


def _dot_tn(a, b):
    return lax.dot_general(a.astype(BF16), b.astype(BF16), (((0,), (0,)), ((), ())),
                           preferred_element_type=F32)


def _bf16_terms(x, n):
    terms = []
    for _ in range(n):
        t = x.astype(BF16)
        terms.append(t)
        x = x - t.astype(F32)
    return terms


def _cparams(sem):
    return pltpu.CompilerParams(dimension_semantics=sem, vmem_limit_bytes=VMEM_LIMIT)


def _ffn_kernel(*refs, final_norm, tiles_a, two_in, two_out):
    xs = refs[:2] if two_in else refs[:1]
    g_ref, wg_ref, wu_ref, wd_ref, gf_ref = refs[len(xs):len(xs) + 5]
    outs = refs[len(xs) + 5:-2]
    h_ref, acc_ref = refs[-2:]
    i = pl.program_id(0)
    j = pl.program_id(1)

    def x_tile():
        if two_in:
            return jnp.where(i < tiles_a, xs[0][...], xs[1][...])
        return xs[0][...]

    @pl.when(j == 0)
    def _():
        h_ref[...] = _rms(x_tile(), g_ref[...], NORM_EPS).astype(BF16)
        acc_ref[...] = jnp.zeros_like(acc_ref)

    h = h_ref[...]
    gate = jnp.dot(h, wg_ref[...], preferred_element_type=F32)
    up = jnp.dot(h, wu_ref[...], preferred_element_type=F32)
    act = (gate * jax.nn.sigmoid(gate)) * up
    acc_ref[...] += jnp.dot(act.astype(BF16), wd_ref[...], preferred_element_type=F32)

    @pl.when(j == pl.num_programs(1) - 1)
    def _():
        y = x_tile() + 0.5 * acc_ref[...]
        if final_norm:
            y = _rms(y, gf_ref[...], NORM_EPS)
        if two_out:
            @pl.when(i < tiles_a)
            def _():
                outs[0][...] = y

            @pl.when(i >= tiles_a)
            def _():
                outs[1][...] = y
        else:
            outs[0][...] = y


def _ffn(xs, g, w_gu, w_down, g_final, final_norm, out_rows):
    two_in = len(xs) == 2
    two_out = len(out_rows) == 2
    n = sum(x.shape[0] for x in xs)
    tiles_a = (xs[0].shape[0] if two_in else out_rows[0]) // TM
    nj = D_FF // TF
    first = lambda i, j: (jnp.minimum(i, tiles_a - 1), 0)
    second = lambda i, j: (jnp.maximum(i - tiles_a, 0), 0)
    whole = lambda i, j: (i, 0)
    tile = lambda m: pl.BlockSpec((TM, D_MODEL), m)
    return pl.pallas_call(
        functools.partial(_ffn_kernel, final_norm=final_norm, tiles_a=tiles_a, two_in=two_in, two_out=two_out),
        grid=(n // TM, nj),
        in_specs=([tile(first), tile(second)] if two_in else [tile(whole)]) + [
            pl.BlockSpec((1, D_MODEL), lambda i, j: (0, 0)),
            pl.BlockSpec((D_MODEL, TF), lambda i, j: (0, j)),
            pl.BlockSpec((D_MODEL, TF), lambda i, j: (0, j + D_FF // TF)),
            pl.BlockSpec((TF, D_MODEL), lambda i, j: (j, 0)),
            pl.BlockSpec((1, D_MODEL), lambda i, j: (0, 0)),
        ],
        out_specs=[tile(first), tile(second)] if two_out else [tile(whole)],
        out_shape=[jax.ShapeDtypeStruct((r, D_MODEL), F32) for r in out_rows],
        scratch_shapes=[pltpu.VMEM((TM, D_MODEL), BF16), pltpu.VMEM((TM, D_MODEL), F32)],
        compiler_params=_cparams(("arbitrary", "arbitrary")),
        name="ffn",
    )(*xs, g, w_gu, w_gu, w_down, g_final)


def _proj_kernel(x_ref, g_ref, w_ref, o_ref, h_ref):
    @pl.when(pl.program_id(1) == 0)
    def _():
        h_ref[...] = _rms(x_ref[...], g_ref[...], NORM_EPS).astype(BF16)

    o_ref[...] = jnp.dot(h_ref[...], w_ref[...], preferred_element_type=F32)


def _proj(x, g, w, tn, nout):
    n = x.shape[0]
    return pl.pallas_call(
        _proj_kernel,
        grid=(n // TM, nout // tn),
        in_specs=[
            pl.BlockSpec((TM, D_MODEL), lambda i, j: (i, 0)),
            pl.BlockSpec((1, D_MODEL), lambda i, j: (0, 0)),
            pl.BlockSpec((D_MODEL, tn), lambda i, j: (0, j)),
        ],
        out_specs=pl.BlockSpec((TM, tn), lambda i, j: (i, j)),
        out_shape=jax.ShapeDtypeStruct((n, nout), F32),
        scratch_shapes=[pltpu.VMEM((TM, D_MODEL), BF16)],
        compiler_params=_cparams(("parallel", "arbitrary")),
        name="proj",
    )(x, g, w)


def _pair_swap(x):
    n = x.shape[-1]
    lane = lax.broadcasted_iota(jnp.int32, x.shape, x.ndim - 1)
    prev = pltpu.roll(x, 1, x.ndim - 1)
    nxt = pltpu.roll(x, n - 1, x.ndim - 1)
    return jnp.where((lane & 1) == 1, prev, nxt)


def _rotary(x, cos, sin_signed):
    return x * cos + _pair_swap(x) * sin_signed


def _ret_finish(o, zg):
    o = o * lax.rsqrt(jnp.mean(o * o, axis=-1, keepdims=True) + GN_EPS_RET)
    return o * (zg * jax.nn.sigmoid(zg))


def _ret_prompt_kernel(lg_ref, q_ref, k_ref, v_ref, g_ref, cos_ref, sin_ref,
                       o_ref, s_out_ref, s_ref):
    c = pl.program_id(1)
    C = RET_CHUNK
    H = range(RET_HEADS)

    @pl.when(c == 0)
    def _():
        s_ref[...] = jnp.zeros_like(s_ref)

    cos = cos_ref[...]
    sin = sin_ref[...]
    ti = lax.broadcasted_iota(jnp.int32, (C, C), 0)
    tj = lax.broadcasted_iota(jnp.int32, (C, C), 1)
    rel = (ti - tj).astype(F32)
    idx = lax.broadcasted_iota(jnp.int32, (C, 1), 0).astype(F32)
    lgs = [lg_ref[h] for h in H]

    qs = [_rotary(q_ref[:, h * RET_DK:(h + 1) * RET_DK], cos, sin) for h in H]
    ks = [_rotary(k_ref[:, h * RET_DK:(h + 1) * RET_DK], cos, sin) * (RET_DK ** -0.5) for h in H]
    vs = [v_ref[:, h * RET_DV:(h + 1) * RET_DV] for h in H]
    ss = [s_ref[h] for h in H]
    scores = [_dot_nt(q, k) * jnp.where(rel >= 0, jnp.exp(lg * jnp.maximum(rel, 0.0)), 0.0)
              for q, k, lg in zip(qs, ks, lgs)]
    crosses = [_dot(q, s) * jnp.exp(lg * (idx + 1.0)) for q, s, lg in zip(qs, ss, lgs)]
    inners = [_dot(sc, v) for sc, v in zip(scores, vs)]
    s_news = [jnp.exp(lg * C) * s + _dot_tn(k * jnp.exp(lg * (C - 1.0 - idx)), v)
              for lg, s, k, v in zip(lgs, ss, ks, vs)]
    for h in H:
        s_ref[h] = s_news[h]
        o_ref[:, h * RET_DV:(h + 1) * RET_DV] = _ret_finish(
            inners[h] + crosses[h], g_ref[:, h * RET_DV:(h + 1) * RET_DV]).astype(BF16)

    @pl.when(c == pl.num_programs(1) - 1)
    def _():
        for h in H:
            s_out_ref[0, h] = s_news[h]


def _ret_prompt(z, lg, cos, sin, batch, seq):
    nc = seq // RET_CHUNK
    C = RET_CHUNK
    return pl.pallas_call(
        _ret_prompt_kernel,
        grid=(batch, nc),
        in_specs=[
            pl.BlockSpec(memory_space=pltpu.SMEM),
            pl.BlockSpec((C, RET_QK), lambda b, c: (b * nc + c, ZQ // RET_QK)),
            pl.BlockSpec((C, RET_QK), lambda b, c: (b * nc + c, ZK // RET_QK)),
            pl.BlockSpec((C, RET_V), lambda b, c: (b * nc + c, ZV // RET_V)),
            pl.BlockSpec((C, RET_V), lambda b, c: (b * nc + c, ZG // RET_V)),
            pl.BlockSpec((C, RET_DK), lambda b, c: (c, 0)),
            pl.BlockSpec((C, RET_DK), lambda b, c: (c, 0)),
        ],
        out_specs=[
            pl.BlockSpec((C, RET_V), lambda b, c: (b * nc + c, 0)),
            pl.BlockSpec((1, RET_HEADS, RET_DK, RET_DV), lambda b, c: (b, 0, 0, 0)),
        ],
        out_shape=[
            jax.ShapeDtypeStruct((z.shape[0], RET_V), BF16),
            jax.ShapeDtypeStruct((batch, RET_HEADS, RET_DK, RET_DV), F32),
        ],
        scratch_shapes=[pltpu.VMEM((RET_HEADS, RET_DK, RET_DV), F32)],
        compiler_params=_cparams(("parallel", "arbitrary")),
        name="ret_prompt",
    )(lg, z, z, z, z, cos, sin)


RET_SAMPLE_ROWS = 16


def _ret_sample_kernel(lg_ref, q_ref, k_ref, v_ref, g_ref, cos_ref, sin_ref, s0_ref, o_prev_ref,
                       o_ref, s_out_ref, *, seq):
    del o_prev_ref
    R = RET_SAMPLE_ROWS
    nb = R // seq
    cos = cos_ref[...]
    sin = sin_ref[...]
    ri = lax.broadcasted_iota(jnp.int32, (R, R), 0)
    rj = lax.broadcasted_iota(jnp.int32, (R, R), 1)
    rel = ((ri % seq) - (rj % seq)).astype(F32)
    ok = ((ri // seq) == (rj // seq)) & (rel >= 0)
    row = lax.broadcasted_iota(jnp.int32, (R, 1), 0)
    t = (row % seq).astype(F32)

    for h in range(RET_HEADS):
        lg = lg_ref[h]
        q = _rotary(q_ref[:, h * RET_DK:(h + 1) * RET_DK], cos, sin)
        k = _rotary(k_ref[:, h * RET_DK:(h + 1) * RET_DK], cos, sin) * (RET_DK ** -0.5)
        v = v_ref[:, h * RET_DV:(h + 1) * RET_DV]
        mask = jnp.where(ok, jnp.exp(lg * jnp.maximum(rel, 0.0)), 0.0)
        q_dec = jnp.exp(lg * (t + 1.0))
        kd = k * jnp.exp(lg * (seq - 1.0 - t))
        inner = _dot(_dot_nt(q, k) * mask, v)
        cross = jnp.zeros_like(inner)
        for b in range(nb):
            s = s0_ref[b, h]
            mine = (row // seq) == b
            cross = jnp.where(mine, _dot(q, s), cross)
            s_out_ref[b, h] = jnp.exp(lg * seq) * s + _dot_tn(jnp.where(mine, kd, 0.0), v)
        o_ref[:, h * RET_DV:(h + 1) * RET_DV] = _ret_finish(
            inner + cross * q_dec, g_ref[:, h * RET_DV:(h + 1) * RET_DV]).astype(BF16)


def _ret_sample(z, lg, cos, sin, s0, o_prev, row0, batch, seq):
    R = RET_SAMPLE_ROWS
    nb = R // seq
    rb0 = row0 // R
    return pl.pallas_call(
        functools.partial(_ret_sample_kernel, seq=seq),
        grid=(batch // nb,),
        in_specs=[
            pl.BlockSpec(memory_space=pltpu.SMEM),
            pl.BlockSpec((R, RET_QK), lambda i: (rb0 + i, ZQ // RET_QK)),
            pl.BlockSpec((R, RET_QK), lambda i: (rb0 + i, ZK // RET_QK)),
            pl.BlockSpec((R, RET_V), lambda i: (rb0 + i, ZV // RET_V)),
            pl.BlockSpec((R, RET_V), lambda i: (rb0 + i, ZG // RET_V)),
            pl.BlockSpec((R, RET_DK), lambda i: (0, 0)),
            pl.BlockSpec((R, RET_DK), lambda i: (0, 0)),
            pl.BlockSpec((nb, RET_HEADS, RET_DK, RET_DV), lambda i: (i, 0, 0, 0)),
            pl.BlockSpec(memory_space=pl.ANY),
        ],
        out_specs=[
            pl.BlockSpec((R, RET_V), lambda i: (rb0 + i, 0)),
            pl.BlockSpec((nb, RET_HEADS, RET_DK, RET_DV), lambda i: (i, 0, 0, 0)),
        ],
        out_shape=[
            jax.ShapeDtypeStruct(o_prev.shape, BF16),
            jax.ShapeDtypeStruct((batch, RET_HEADS, RET_DK, RET_DV), F32),
        ],
        input_output_aliases={8: 0},
        compiler_params=_cparams(("parallel",)),
        name="ret_sample",
    )(lg, z, z, z, z, cos, sin, s0, o_prev)


def _rope_tables(pos):
    half = RET_DK // 2
    inv = 1.0 / (ROPE_BASE ** jnp.linspace(0.0, 1.0, half, dtype=F32))
    ang = pos[:, None] * inv[None, :]
    cos = jnp.repeat(jnp.cos(ang), 2, axis=-1)
    sin = jnp.sin(ang)
    sin_signed = jnp.stack([-sin, sin], axis=-1).reshape(pos.shape[0], RET_DK)
    return cos, sin_signed


def _segsum(x, seg_ref):
    rows = x.shape[0]
    n = rows * RWKV_PAIRS
    xs = jnp.concatenate([x[:, p * LANES:(p + 1) * LANES] for p in range(RWKV_PAIRS)], axis=0)
    ss = jnp.dot(jnp.concatenate(_bf16_terms(xs, 2), axis=0), seg_ref[...], preferred_element_type=F32)
    ss = ss[0:n] + ss[n:2 * n]
    return jnp.concatenate([ss[p * rows:(p + 1) * rows] for p in range(RWKV_PAIRS)], axis=1)


def _run_staged(gens, lead):
    out = [None] * len(gens)
    live = [True] * len(gens)

    def step(i):
        try:
            next(gens[i])
        except StopIteration as stop:
            out[i] = stop.value
            live[i] = False

    for _ in range(lead):
        if live[0]:
            step(0)
    while any(live):
        for i in range(len(gens)):
            if live[i]:
                step(i)
    return out


def _rwkv_chunk(zr, zl, prev_r, prev_l, prm, n_seq, read_state=None, update_state=None, fold_state=None):
    (mur_ref, mul_ref, w0_ref, wup_ref, a0_ref, aup_ref, gup_ref,
     kk_ref, ka_ref, rk_ref, lnw_ref, lnb_ref, seg_ref) = prm
    C = RWKV_CHUNK
    W = RWKV_W
    R = zr.shape[0]
    TS = C // n_seq
    row = lax.broadcasted_iota(jnp.int32, (R, 1), 0)
    first = (row % TS) == 0

    def mix(z, prev, mu_ref):
        z_prev = jnp.where(first, prev, pltpu.roll(z, 1, 0))
        return z + (z_prev - z) * mu_ref[...]

    xs = mix(zr, prev_r, mur_ref)
    xl = mix(zl, prev_l, mul_ref)
    r = xs[:, 0:W]
    k = xs[:, W:2 * W]
    v = xs[:, 2 * W:3 * W]
    x_wa = xl[:, 0:LANES]
    x_g = xl[:, LANES:LORA_PAD]

    y_w = w0_ref[...] + _dot(jnp.tanh(x_wa), wup_ref[...])
    w_log = -(jnp.maximum(-y_w, 0.0) + jnp.log1p(jnp.exp(-jnp.abs(y_w)))) - 0.5
    lw = -jnp.exp(w_log)
    a_rate = jax.nn.sigmoid(a0_ref[...] + _dot(x_wa, aup_ref[...]))
    g = _dot(jax.nn.sigmoid(x_g), gup_ref[...])
    yield

    kk = k * kk_ref[...]
    k = k * (1.0 + (a_rate - 1.0) * ka_ref[...])
    sums = _segsum(jnp.concatenate([kk * kk, r * k * rk_ref[...]], axis=0), seg_ref)
    kk = kk * lax.rsqrt(jnp.maximum(sums[0:R], 1e-24))
    bonus_rk = sums[R:2 * R]
    a = -kk
    b = kk * a_rate
    yield

    ti = lax.broadcasted_iota(jnp.int32, (2 * R, R), 0)
    tj = lax.broadcasted_iota(jnp.int32, (2 * R, R), 1)
    same = ((ti % R) // TS) == (tj // TS)
    sel = (same & ((ti >= R) | (tj <= ti))).astype(BF16)
    sums_lw = sum(jnp.dot(sel, t, preferred_element_type=F32) for t in _bf16_terms(lw, 3))
    cum = sums_lw[0:R]
    tot = sums_lw[R:2 * R]
    e_neg = jnp.exp(-cum)
    a_t = a * jnp.exp(cum - lw)
    b_t = b * e_neg
    k_t = k * e_neg
    r_t = r * jnp.exp(cum)
    e_tail = jnp.exp(tot - cum)
    b_g = b * e_tail
    k_g = k * e_tail
    dec = jnp.exp(tot)
    yield

    lane = lax.broadcasted_iota(jnp.int32, (1, LANES), 1)
    lo = lane < RWKV_N

    def split(x):
        return jnp.concatenate([jnp.where(lo, x, 0.0), jnp.where(lo, 0.0, x)], axis=0)

    row4 = lax.broadcasted_iota(jnp.int32, (C, 4 * C), 0)
    col4 = lax.broadcasted_iota(jnp.int32, (C, 4 * C), 1) % C
    same4 = (row4 // TS) == (col4 // TS)
    strict = same4 & (col4 < row4)
    incl = same4 & (col4 <= row4)
    row2 = lax.broadcasted_iota(jnp.int32, (C, 2 * C), 0)
    col2 = lax.broadcasted_iota(jnp.int32, (C, 2 * C), 1)
    eye2 = ((col2 % C) == row2).astype(F32)
    left = col2 < C

    tiles = [(slice(i * C, (i + 1) * C), slice(p * LANES, (p + 1) * LANES))
             for i in range(R // C) for p in range(RWKV_PAIRS)]
    v2s = [split(v[rs, sl]) for rs, sl in tiles]
    ars = [jnp.concatenate([a_t[rs, sl], r_t[rs, sl]], axis=0) for rs, sl in tiles]
    if fold_state is None:
        wys = read_state(ars)
    gms = [_dot_nt(ar, jnp.concatenate([split(b_t[rs, sl]), split(k_t[rs, sl])], axis=0))
           for ar, (rs, sl) in zip(ars, tiles)]
    gas = [jnp.where(strict, gm[0:C], 0.0) for gm in gms]
    grs = [jnp.where(incl, gm[C:2 * C], 0.0) for gm in gms]
    def blockdiag(x):
        return jnp.concatenate([jnp.where(left, x, 0.0), jnp.where(left, 0.0, x)], axis=0)

    ts = [eye2 + ga[:, 0:2 * C] for ga in gas]
    n_sq = TS.bit_length() - 2
    if n_sq >= 1:
        pws = [_dot(ga[:, 0:2 * C], blockdiag(ga[:, 0:2 * C])) for ga in gas]
        yield
        for _ in range(n_sq - 1):
            tps = [_dot(jnp.concatenate([t, pw], axis=0), blockdiag(pw)) for t, pw in zip(ts, pws)]
            ts = [t + tp[0:C] for t, tp in zip(ts, tps)]
            pws = [tp[C:2 * C] for tp in tps]
            yield
        ts = [t + _dot(t, blockdiag(pw)) for t, pw in zip(ts, pws)]
    yield
    wis = [_dot(ga[:, 2 * C:4 * C], v2) for ga, v2 in zip(gas, v2s)]
    bks = [jnp.concatenate([b_g[rs, sl], k_g[rs, sl]], axis=0) for rs, sl in tiles]
    if fold_state is None:
        us = [_dot(t, split(wi + wy[0:C])) for t, wi, wy in zip(ts, wis, wys)]
        ys = [wy[C:2 * C] + _dot(gr, jnp.concatenate([split(u), v2], axis=0))
              for wy, gr, u, v2 in zip(wys, grs, us, v2s)]
        update_state([jnp.concatenate([u, v[rs, sl]], axis=0) for u, (rs, sl) in zip(us, tiles)], bks, dec)
    else:
        zc = jnp.zeros((C, LANES), F32)
        aus = [_dot(t, jnp.concatenate([split(a_t[rs, sl]), split(wi)], axis=1))
               for t, wi, (rs, sl) in zip(ts, wis, tiles)]
        yield
        rys = [_dot(gr, jnp.concatenate(
            [jnp.concatenate([split(au[:, 0:LANES]), split(au[:, LANES:2 * LANES])], axis=1),
             jnp.concatenate([jnp.concatenate([zc, zc], axis=0), v2], axis=1)], axis=0))
            for gr, au, v2 in zip(grs, aus, v2s)]
        mns = [_dot_tn(bk, jnp.concatenate([au, jnp.concatenate([zc, v[rs, sl]], axis=1)], axis=0))
               for bk, au, (rs, sl) in zip(bks, aus, tiles)]
        yield
        ys = fold_state([r_t[rs, sl] + ry[:, 0:LANES] for ry, (rs, sl) in zip(rys, tiles)],
                        [ry[:, LANES:2 * LANES] for ry in rys], mns, dec)

    yield
    y = jnp.concatenate([jnp.concatenate(ys[i * RWKV_PAIRS:(i + 1) * RWKV_PAIRS], axis=1)
                         for i in range(R // C)], axis=0)
    mean = _segsum(y, seg_ref) * (1.0 / RWKV_N)
    d = y - mean
    var = _segsum(d * d, seg_ref) * (1.0 / RWKV_N)
    yn = d * lax.rsqrt(var + GN_EPS_RWKV) * lnw_ref[...] + lnb_ref[...]
    return ((yn + bonus_rk * v) * g).astype(BF16)


def _rwkv_prompt_kernel(*refs, batch):
    zr_refs = refs[:batch]
    zl_refs = refs[batch:2 * batch]
    prm = refs[2 * batch:2 * batch + 13]
    o_ref, s_out_ref, prevr_ref, prevl_ref, h_ref = refs[2 * batch + 13:]
    C = RWKV_CHUNK
    c = pl.program_id(0)

    @pl.when(c == 0)
    def _():
        prevr_ref[...] = jnp.zeros_like(prevr_ref)
        prevl_ref[...] = jnp.zeros_like(prevl_ref)
        h_ref[...] = jnp.zeros_like(h_ref)

    ki = lax.broadcasted_iota(jnp.int32, (LANES, LANES), 0)
    vi = lax.broadcasted_iota(jnp.int32, (LANES, LANES), 1)
    diag_blocks = (ki < RWKV_N) == (vi < RWKV_N)

    def group(i0):
        seqs = range(i0, i0 + RWKV_PROMPT_GROUP)
        tiles = [(i, p) for i in seqs for p in range(RWKV_PAIRS)]
        zr = jnp.concatenate([zr_refs[i][...] for i in seqs], axis=0)
        zl = jnp.concatenate([zl_refs[i][...] for i in seqs], axis=0)
        rows = lambda ref: jnp.concatenate(
            [jnp.broadcast_to(ref[i:i + 1, :], (C, ref.shape[1])) for i in seqs], axis=0)

        def fold_state(r_hats, y_inds, mns, dec):
            dec_cols = [jnp.broadcast_to(dec[(i - i0) * C:(i - i0) * C + 1, p * LANES:(p + 1) * LANES],
                                         (LANES, LANES)).T for i, p in tiles]
            hs = [h_ref[i, p] for i, p in tiles]
            res = [_dot(jnp.concatenate([rh, jnp.where(diag_blocks, mn[:, 0:LANES], 0.0)], axis=0), h)
                   for rh, mn, h in zip(r_hats, mns, hs)]
            for (i, p), h, dc, rs, mn in zip(tiles, hs, dec_cols, res, mns):
                h_ref[i, p] = h * dc + rs[C:C + LANES] + jnp.where(diag_blocks, mn[:, LANES:2 * LANES], 0.0)
            return [rs[0:C] + yi for rs, yi in zip(res, y_inds)]

        y = yield from _rwkv_chunk(zr, zl, rows(prevr_ref), rows(prevl_ref), prm, 1, fold_state=fold_state)
        for n, i in enumerate(seqs):
            o_ref[i] = y[n * C:(n + 1) * C]
            prevr_ref[i:i + 1, :] = zr[(n + 1) * C - 1:(n + 1) * C, :]
            prevl_ref[i:i + 1, :] = zl[(n + 1) * C - 1:(n + 1) * C, :]

    _run_staged([group(i0) for i0 in range(0, batch, RWKV_PROMPT_GROUP)], RWKV_PROMPT_LEAD)

    @pl.when(c == pl.num_programs(0) - 1)
    def _():
        for i in range(batch):
            for p in range(RWKV_PAIRS):
                s_out_ref[i, p] = h_ref[i, p].T


def _rwkv_sample_kernel(zr_ref, zl_ref, shr_ref, shl_ref, s0_ref, *rest, n_seq):
    prm = rest[:13]
    o_ref, s_out_ref = rest[13:]
    C = RWKV_CHUNK
    N = RWKV_N
    TS = C // n_seq
    tokseq = (lax.broadcasted_iota(jnp.int32, (1, LANES), 1) % C) // TS

    def pick(x, j):
        return jnp.where(tokseq == j, x, 0.0)

    def stacked(h):
        return s0_ref[:, h].reshape(n_seq * N, N)

    def read_state(ars):
        res = [[_dot_nt(stacked(2 * p + e), ar[:, e * N:(e + 1) * N]) for e in range(2)]
               for p, ar in enumerate(ars)]
        out = []
        for rp in res:
            halves = []
            for re in rp:
                acc = pick(re[0:N], 0)
                for j in range(1, n_seq):
                    acc = acc + pick(re[j * N:(j + 1) * N], j)
                halves.append(acc)
            out.append(jnp.concatenate(halves, axis=0).T)
        return out

    def update_state(uvs, bks, dec):
        uvts = [uv.T for uv in uvs]
        upds = [[_dot(jnp.concatenate([pick(uvt[e * N:(e + 1) * N], j) for j in range(n_seq)], axis=0),
                      bk[:, e * N:(e + 1) * N]) for e in range(2)]
                for uvt, bk in zip(uvts, bks)]
        for p, up in enumerate(upds):
            for e, upd in enumerate(up):
                h = 2 * p + e
                dec_h = jnp.concatenate(
                    [jnp.broadcast_to(dec[j * TS:j * TS + 1, h * N:(h + 1) * N], (N, N)) for j in range(n_seq)],
                    axis=0)
                s_out_ref[:, h] = (stacked(h) * dec_h + upd).reshape(n_seq, N, N)

    (o_ref[...],) = _run_staged([_rwkv_chunk(zr_ref[...], zl_ref[...], shr_ref[...], shl_ref[...], prm, n_seq,
                                             read_state, update_state)], 0)


def _rwkv_param_specs():
    W = RWKV_W
    shapes = [(1, 3 * W), (1, LORA_PAD), (1, W), (LANES, W), (1, W), (LANES, W), (LORA_PAD - LANES, W),
              (1, W), (1, W), (1, W), (1, W), (1, W), (LANES, LANES)]
    return [pl.BlockSpec(s, lambda *_: (0, 0)) for s in shapes]


def _rwkv_prompt(z, zl, params, batch, seq):
    C = RWKV_CHUNK
    W = RWKV_W
    nc = seq // C
    zr_spec = lambda i: pl.BlockSpec((C, 3 * W), lambda c: (i * nc + c, ZS // (3 * W)))
    zl_spec = lambda i: pl.BlockSpec((C, LORA_PAD), lambda c: (i * nc + c, 0))
    return pl.pallas_call(
        functools.partial(_rwkv_prompt_kernel, batch=batch),
        grid=(nc,),
        in_specs=[zr_spec(i) for i in range(batch)] + [zl_spec(i) for i in range(batch)]
        + _rwkv_param_specs(),
        out_specs=[
            pl.BlockSpec((batch, C, W), lambda c: (0, c, 0)),
            pl.BlockSpec((batch, RWKV_PAIRS, LANES, LANES), lambda c: (0, 0, 0, 0)),
        ],
        out_shape=[
            jax.ShapeDtypeStruct((batch, seq, W), BF16),
            jax.ShapeDtypeStruct((batch, RWKV_PAIRS, LANES, LANES), F32),
        ],
        scratch_shapes=[pltpu.VMEM((batch, 3 * W), F32), pltpu.VMEM((batch, LORA_PAD), F32),
                        pltpu.VMEM((batch, RWKV_PAIRS, LANES, LANES), F32)],
        compiler_params=_cparams(("arbitrary",)),
        name="rwkv_prompt",
    )(*([z] * batch), *([zl] * batch), *params)


def _rwkv_sample(z, zl, sh_r, sh_l, s0, params, row0, batch, seq):
    C = RWKV_CHUNK
    W = RWKV_W
    n_seq = C // seq
    rb0 = row0 // C
    return pl.pallas_call(
        functools.partial(_rwkv_sample_kernel, n_seq=n_seq),
        grid=(batch // n_seq,),
        in_specs=[
            pl.BlockSpec((C, 3 * W), lambda i: (rb0 + i, ZS // (3 * W))),
            pl.BlockSpec((C, LORA_PAD), lambda i: (rb0 + i, 0)),
            pl.BlockSpec((C, 3 * W), lambda i: (i, 0)),
            pl.BlockSpec((C, LORA_PAD), lambda i: (i, 0)),
            pl.BlockSpec((n_seq, RWKV_HEADS, RWKV_N, RWKV_N), lambda i: (i, 0, 0, 0)),
        ] + _rwkv_param_specs(),
        out_specs=[
            pl.BlockSpec((C, W), lambda i: (i, 0)),
            pl.BlockSpec((n_seq, RWKV_HEADS, RWKV_N, RWKV_N), lambda i: (i, 0, 0, 0)),
        ],
        out_shape=[
            jax.ShapeDtypeStruct((batch * seq, W), BF16),
            jax.ShapeDtypeStruct((batch, RWKV_HEADS, RWKV_N, RWKV_N), F32),
        ],
        compiler_params=_cparams(("parallel",)),
        name="rwkv_sample",
    )(z, zl, sh_r, sh_l, s0, *params)


def _pair_unblock(s):
    b = s.shape[0]
    lo = s[:, :, :RWKV_N, :RWKV_N]
    hi = s[:, :, RWKV_N:, RWKV_N:]
    return jnp.stack([lo, hi], axis=2).reshape(b, RWKV_HEADS, RWKV_N, RWKV_N)


def _merge_kernel(oret_ref, orwkv_ref, gr_ref, gw_ref, wr_ref, ww_ref, o_ref):
    a = jnp.dot(oret_ref[...], wr_ref[...], preferred_element_type=F32)
    b = jnp.dot(orwkv_ref[...], ww_ref[...], preferred_element_type=F32)
    o_ref[...] = (jax.nn.sigmoid(gr_ref[...]) * a + jax.nn.sigmoid(gw_ref[...]) * b).astype(BF16)


def _merge(o_ret, o_rwkv, z, w_ret_o, w_rwkv_o):
    n = o_ret.shape[0]
    tn = TN_OUT
    return pl.pallas_call(
        _merge_kernel,
        grid=(n // TM, D_MODEL // tn),
        in_specs=[
            pl.BlockSpec((TM, RET_V), lambda i, j: (i, 0)),
            pl.BlockSpec((TM, RWKV_W), lambda i, j: (i, 0)),
            pl.BlockSpec((TM, tn), lambda i, j: (i, ZGR // tn + j)),
            pl.BlockSpec((TM, tn), lambda i, j: (i, ZGW // tn + j)),
            pl.BlockSpec((RET_V, tn), lambda i, j: (0, j)),
            pl.BlockSpec((RWKV_W, tn), lambda i, j: (0, j)),
        ],
        out_specs=pl.BlockSpec((TM, tn), lambda i, j: (i, j)),
        out_shape=jax.ShapeDtypeStruct((n, D_MODEL), BF16),
        compiler_params=_cparams(("parallel", "arbitrary")),
        name="merge",
    )(o_ret, o_rwkv, z, z, w_ret_o, w_rwkv_o)


def _outproj_kernel(x_ref, m_ref, w_ref, o_ref):
    o_ref[...] = x_ref[...] + jnp.dot(m_ref[...], w_ref[...], preferred_element_type=F32)


def _outproj(x, merged, w_out):
    n = x.shape[0]
    tn = TN_OUT
    return pl.pallas_call(
        _outproj_kernel,
        grid=(n // TM, D_MODEL // tn),
        in_specs=[
            pl.BlockSpec((TM, tn), lambda i, j: (i, j)),
            pl.BlockSpec((TM, D_MODEL), lambda i, j: (i, 0)),
            pl.BlockSpec((D_MODEL, tn), lambda i, j: (0, j)),
        ],
        out_specs=pl.BlockSpec((TM, tn), lambda i, j: (i, j)),
        out_shape=jax.ShapeDtypeStruct((n, D_MODEL), F32),
        compiler_params=_cparams(("parallel", "arbitrary")),
        name="outproj",
    )(x, merged, w_out)


def kernel(x_prompt, x_sample, state_ret, state_rwkv, state_shift, norm_ffn1, ffn1_w_gu, ffn1_w_down, norm_mix, w_in, w_ret_o, rwkv_mu, rwkv_w0, rwkv_w_up, rwkv_a0, rwkv_a_up, rwkv_g_up, rwkv_k_k, rwkv_k_a, rwkv_r_k, rwkv_ln_w, rwkv_ln_b, w_rwkv_o, w_out, norm_ffn2, ffn2_w_gu, ffn2_w_down, norm_final):
    bp, tp, d = x_prompt.shape
    bs, ts, _ = x_sample.shape
    n_p = bp * tp
    n_s = bs * ts
    W = RWKV_W
    row = lambda a: a.reshape(1, -1)

    (x1,) = _ffn((x_prompt.reshape(n_p, d), x_sample.reshape(n_s, d)), row(norm_ffn1[0]),
                 ffn1_w_gu[0].astype(BF16), ffn1_w_down[0].astype(BF16), row(norm_final),
                 final_norm=False, out_rows=(n_p + n_s,))

    w_in_bf = w_in[0].astype(BF16)
    w_lora = jnp.pad(w_in_bf[:, Z_MAIN_W:Z_MAIN_W + LORA_W], ((0, 0), (0, LORA_PAD - LORA_W)))
    z = _proj(x1, row(norm_mix[0]), w_in_bf, TN_PROJ, Z_MAIN_W)
    zl = _proj(x1, row(norm_mix[0]), w_lora, LORA_PAD, LORA_PAD)
    zg = _proj(x1, row(norm_mix[0]), w_in_bf[:, Z_MAIN_W + LORA_W:], TN_PROJ, 2 * D_MODEL)

    lg = jnp.log(1.0 - 2.0 ** (-5.0 - jnp.arange(RET_HEADS, dtype=F32)))
    cos_p, sin_p = _rope_tables(jnp.arange(tp, dtype=F32))
    cos_s, sin_s = _rope_tables(PAST_LEN + jnp.arange(ts, dtype=F32))
    rep = RET_SAMPLE_ROWS // ts
    o_ret, ret_p = _ret_prompt(z, lg, cos_p, sin_p, bp, tp)
    o_ret, ret_s = _ret_sample(z, lg, jnp.tile(cos_s, (rep, 1)), jnp.tile(sin_s, (rep, 1)),
                               state_ret[0], o_ret, n_p, bs, ts)

    mu = rwkv_mu[0]
    pad_l = lambda a: jnp.pad(a, [(0, 0)] * (a.ndim - 1) + [(0, LORA_PAD - LORA_W)])
    seg = (jnp.arange(LANES)[:, None] // RWKV_N == jnp.arange(LANES)[None, :] // RWKV_N).astype(BF16)
    zero64 = jnp.zeros((DECAY_LORA, W), F32)
    params = (
        row(mu[:3 * W]), pad_l(row(mu[3 * W:])),
        row(rwkv_w0[0]), jnp.concatenate([rwkv_w_up[0], zero64], axis=0).astype(BF16),
        row(rwkv_a0[0]), jnp.concatenate([zero64, rwkv_a_up[0]], axis=0).astype(BF16),
        jnp.pad(rwkv_g_up[0], ((0, LORA_PAD - LANES - GATE_LORA), (0, 0))).astype(BF16),
        row(rwkv_k_k[0]), row(rwkv_k_a[0]), row(rwkv_r_k[0]), row(rwkv_ln_w[0]), row(rwkv_ln_b[0]),
        seg,
    )
    orw_p, rwkv_p = _rwkv_prompt(z, zl, params, bp, tp)
    sh_s = state_shift[0]
    first_rows = lambda a: jnp.pad(a[:, None, :], ((0, 0), (0, ts - 1), (0, 0))).reshape(n_s, -1)
    orw_s, rwkv_s = _rwkv_sample(z, zl, first_rows(sh_s[:, :3 * W]), first_rows(pad_l(sh_s[:, 3 * W:])),
                                 state_rwkv[0], params, n_p, bs, ts)
    o_rwkv = jnp.concatenate([orw_p.reshape(n_p, W), orw_s], axis=0)

    merged = _merge(o_ret, o_rwkv, zg, w_ret_o[0].astype(BF16), w_rwkv_o[0].astype(BF16))
    x2 = _outproj(x1, merged, w_out[0].astype(BF16))
    y_p, y_s = _ffn((x2,), row(norm_ffn2[0]), ffn2_w_gu[0].astype(BF16), ffn2_w_down[0].astype(BF16),
                    row(norm_final), final_norm=True, out_rows=(n_p, n_s))

    def last_rows(lo, hi, t):
        zr_last = lax.slice(z, (lo + t - 1, ZS), (hi, ZS + 3 * W), (t, 1))
        zl_last = lax.slice(zl, (lo + t - 1, 0), (hi, LORA_W), (t, 1))
        return jnp.concatenate([zr_last, zl_last], axis=-1)[None]

    return (y_p.reshape(bp, tp, d), y_s.reshape(bs, ts, d),
            ret_p[None], _pair_unblock(rwkv_p)[None], last_rows(0, n_p, tp),
            ret_s[None], rwkv_s[None], last_rows(n_p, n_p + n_s, ts))
```

```python
import functools

import jax
import jax.numpy as jnp
from jax import lax
from jax.experimental import pallas as pl
from jax.experimental.pallas import tpu as pltpu

F32 = jnp.float32
BF16 = jnp.bfloat16

D_MODEL = 2048
PAST_LEN = 16384
RET_HEADS = 4
RET_DK = 256
RET_DV = 512
RET_CHUNK = 128
ROPE_BASE = 10000.0
RET_QK = RET_HEADS * RET_DK
RET_V = RET_HEADS * RET_DV
RWKV_HEADS = 16
RWKV_N = 64
RWKV_W = RWKV_HEADS * RWKV_N
DECAY_LORA = 64
AAA_LORA = 64
GATE_LORA = 160
LORA_W = DECAY_LORA + AAA_LORA + GATE_LORA
LORA_PAD = 384
D_FF = 5632
NORM_EPS = 1e-6
GN_EPS_RET = 1e-6
GN_EPS_RWKV = 64e-5

ZQ, ZK, ZV, ZG, ZS = 0, 1024, 2048, 4096, 6144
Z_MAIN_W = 9216
ZGR, ZGW = 0, 2048

LANES = 128
RWKV_PAIRS = RWKV_W // LANES
RWKV_CHUNK = 64
RWKV_PROMPT_GROUP = 2
RWKV_PROMPT_LEAD = 4

VMEM_LIMIT = 56 * 1024 * 1024

TM = 512
TF = 512
TN_PROJ = 1024
TM_MIX = 256


def _rms(x, g, eps):
    return x * lax.rsqrt(jnp.mean(x * x, axis=-1, keepdims=True) + eps) * g


def _dot(a, b):
    return jnp.dot(a.astype(BF16), b.astype(BF16), preferred_element_type=F32)


def _dot_nt(a, b):
    return lax.dot_general(a.astype(BF16), b.astype(BF16), (((1,), (1,)), ((), ())),
                           preferred_element_type=F32)


def _dot_tn(a, b):
    return lax.dot_general(a.astype(BF16), b.astype(BF16), (((0,), (0,)), ((), ())),
                           preferred_element_type=F32)


def _bf16_terms(x, n):
    terms = []
    for _ in range(n):
        t = x.astype(BF16)
        terms.append(t)
        x = x - t.astype(F32)
    return terms


def _cparams(sem):
    return pltpu.CompilerParams(dimension_semantics=sem, vmem_limit_bytes=VMEM_LIMIT)


def _ffn_kernel(*refs, final_norm, emit_h, tiles_a, two_in, two_out):
    xs = refs[:2] if two_in else refs[:1]
    g_ref, wg_ref, wu_ref, wd_ref, gf_ref = refs[len(xs):len(xs) + 5]
    outs = refs[len(xs) + 5:-2]
    if emit_h:
        outs, hn_ref = outs[:-1], outs[-1]
    h_ref, acc_ref = refs[-2:]
    i = pl.program_id(0)
    j = pl.program_id(1)

    def x_tile():
        if two_in:
            return jnp.where(i < tiles_a, xs[0][...], xs[1][...])
        return xs[0][...]

    @pl.when(j == 0)
    def _():
        h_ref[...] = _rms(x_tile(), g_ref[...], NORM_EPS).astype(BF16)
        acc_ref[...] = jnp.zeros_like(acc_ref)

    h = h_ref[...]
    gate = jnp.dot(h, wg_ref[...], preferred_element_type=F32)
    up = jnp.dot(h, wu_ref[...], preferred_element_type=F32)
    act = (gate * jax.nn.sigmoid(gate)) * up
    acc_ref[...] += jnp.dot(act.astype(BF16), wd_ref[...], preferred_element_type=F32)

    @pl.when(j == pl.num_programs(1) - 1)
    def _():
        y = x_tile() + 0.5 * acc_ref[...]
        if final_norm:
            y = _rms(y, gf_ref[...], NORM_EPS)
        if emit_h:
            hn_ref[...] = _rms(y, gf_ref[...], NORM_EPS).astype(BF16)
        if two_out:
            @pl.when(i < tiles_a)
            def _():
                outs[0][...] = y

            @pl.when(i >= tiles_a)
            def _():
                outs[1][...] = y
        else:
            outs[0][...] = y


def _ffn(xs, g, w_gu, w_down, g_after, final_norm, emit_h, out_rows):
    two_in = len(xs) == 2
    two_out = len(out_rows) == 2
    n = sum(x.shape[0] for x in xs)
    tiles_a = (xs[0].shape[0] if two_in else out_rows[0]) // TM
    nj = D_FF // TF
    first = lambda i, j: (jnp.minimum(i, tiles_a - 1), 0)
    second = lambda i, j: (jnp.maximum(i - tiles_a, 0), 0)
    whole = lambda i, j: (i, 0)
    tile = lambda m: pl.BlockSpec((TM, D_MODEL), m)
    return pl.pallas_call(
        functools.partial(_ffn_kernel, final_norm=final_norm, emit_h=emit_h, tiles_a=tiles_a,
                          two_in=two_in, two_out=two_out),
        grid=(n // TM, nj),
        in_specs=([tile(first), tile(second)] if two_in else [tile(whole)]) + [
            pl.BlockSpec((1, D_MODEL), lambda i, j: (0, 0)),
            pl.BlockSpec((D_MODEL, TF), lambda i, j: (0, j)),
            pl.BlockSpec((D_MODEL, TF), lambda i, j: (0, j + D_FF // TF)),
            pl.BlockSpec((TF, D_MODEL), lambda i, j: (j, 0)),
            pl.BlockSpec((1, D_MODEL), lambda i, j: (0, 0)),
        ],
        out_specs=([tile(first), tile(second)] if two_out else [tile(whole)]) + ([tile(whole)] if emit_h else []),
        out_shape=[jax.ShapeDtypeStruct((r, D_MODEL), F32) for r in out_rows]
        + ([jax.ShapeDtypeStruct((n, D_MODEL), BF16)] if emit_h else []),
        scratch_shapes=[pltpu.VMEM((TM, D_MODEL), BF16), pltpu.VMEM((TM, D_MODEL), F32)],
        compiler_params=_cparams(("arbitrary", "arbitrary")),
        name="ffn",
    )(*xs, g, w_gu, w_gu, w_down, g_after)


def _proj_kernel(h_ref, w_ref, o_ref):
    o_ref[...] = jnp.dot(h_ref[...], w_ref[...], preferred_element_type=F32)


def _proj(h, w, tn, nout):
    n = h.shape[0]
    return pl.pallas_call(
        _proj_kernel,
        grid=(nout // tn, n // TM),
        in_specs=[
            pl.BlockSpec((TM, D_MODEL), lambda j, i: (i, 0)),
            pl.BlockSpec((D_MODEL, tn), lambda j, i: (0, j)),
        ],
        out_specs=pl.BlockSpec((TM, tn), lambda j, i: (i, j)),
        out_shape=jax.ShapeDtypeStruct((n, nout), F32),
        compiler_params=_cparams(("parallel", "parallel")),
        name="proj",
    )(h, w)


def _pair_swap(x):
    n = x.shape[-1]
    lane = lax.broadcasted_iota(jnp.int32, x.shape, x.ndim - 1)
    prev = pltpu.roll(x, 1, x.ndim - 1)
    nxt = pltpu.roll(x, n - 1, x.ndim - 1)
    return jnp.where((lane & 1) == 1, prev, nxt)


def _rotary(x, cos, sin_signed):
    return x * cos + _pair_swap(x) * sin_signed


def _ret_finish(o, zg):
    o = o * lax.rsqrt(jnp.mean(o * o, axis=-1, keepdims=True) + GN_EPS_RET)
    return o * (zg * jax.nn.sigmoid(zg))


def _ret_prompt_kernel(lg_ref, q_ref, k_ref, v_ref, g_ref, cos_ref, sin_ref,
                       o_ref, s_out_ref, s_ref):
    c = pl.program_id(1)
    C = RET_CHUNK
    H = range(RET_HEADS)

    @pl.when(c == 0)
    def _():
        s_ref[...] = jnp.zeros_like(s_ref)

    cos = cos_ref[...]
    sin = sin_ref[...]
    ti = lax.broadcasted_iota(jnp.int32, (C, C), 0)
    tj = lax.broadcasted_iota(jnp.int32, (C, C), 1)
    rel = (ti - tj).astype(F32)
    idx = lax.broadcasted_iota(jnp.int32, (C, 1), 0).astype(F32)
    lgs = [lg_ref[h] for h in H]

    qs = [_rotary(q_ref[:, h * RET_DK:(h + 1) * RET_DK], cos, sin) for h in H]
    ks = [_rotary(k_ref[:, h * RET_DK:(h + 1) * RET_DK], cos, sin) * (RET_DK ** -0.5) for h in H]
    vs = [v_ref[:, h * RET_DV:(h + 1) * RET_DV] for h in H]
    ss = [s_ref[h] for h in H]
    scores = [_dot_nt(q, k) * jnp.where(rel >= 0, jnp.exp(lg * jnp.maximum(rel, 0.0)), 0.0)
              for q, k, lg in zip(qs, ks, lgs)]
    crosses = [_dot(q, s) * jnp.exp(lg * (idx + 1.0)) for q, s, lg in zip(qs, ss, lgs)]
    inners = [_dot(sc, v) for sc, v in zip(scores, vs)]
    s_news = [jnp.exp(lg * C) * s + _dot_tn(k * jnp.exp(lg * (C - 1.0 - idx)), v)
              for lg, s, k, v in zip(lgs, ss, ks, vs)]
    for h in H:
        s_ref[h] = s_news[h]
        o_ref[:, h * RET_DV:(h + 1) * RET_DV] = _ret_finish(
            inners[h] + crosses[h], g_ref[:, h * RET_DV:(h + 1) * RET_DV]).astype(BF16)

    @pl.when(c == pl.num_programs(1) - 1)
    def _():
        for h in H:
            s_out_ref[0, h] = s_news[h]


def _ret_prompt(z, lg, cos, sin, batch, seq):
    nc = seq // RET_CHUNK
    C = RET_CHUNK
    return pl.pallas_call(
        _ret_prompt_kernel,
        grid=(batch, nc),
        in_specs=[
            pl.BlockSpec(memory_space=pltpu.SMEM),
            pl.BlockSpec((C, RET_QK), lambda b, c: (b * nc + c, ZQ // RET_QK)),
            pl.BlockSpec((C, RET_QK), lambda b, c: (b * nc + c, ZK // RET_QK)),
            pl.BlockSpec((C, RET_V), lambda b, c: (b * nc + c, ZV // RET_V)),
            pl.BlockSpec((C, RET_V), lambda b, c: (b * nc + c, ZG // RET_V)),
            pl.BlockSpec((C, RET_DK), lambda b, c: (c, 0)),
            pl.BlockSpec((C, RET_DK), lambda b, c: (c, 0)),
        ],
        out_specs=[
            pl.BlockSpec((C, RET_V), lambda b, c: (b * nc + c, 0)),
            pl.BlockSpec((1, RET_HEADS, RET_DK, RET_DV), lambda b, c: (b, 0, 0, 0)),
        ],
        out_shape=[
            jax.ShapeDtypeStruct((z.shape[0], RET_V), BF16),
            jax.ShapeDtypeStruct((batch, RET_HEADS, RET_DK, RET_DV), F32),
        ],
        scratch_shapes=[pltpu.VMEM((RET_HEADS, RET_DK, RET_DV), F32)],
        compiler_params=_cparams(("parallel", "arbitrary")),
        name="ret_prompt",
    )(lg, z, z, z, z, cos, sin)


RET_SAMPLE_ROWS = 16


def _ret_sample_kernel(lg_ref, q_ref, k_ref, v_ref, g_ref, cos_ref, sin_ref, s0_ref, o_prev_ref,
                       o_ref, s_out_ref, *, seq):
    del o_prev_ref
    R = RET_SAMPLE_ROWS
    nb = R // seq
    cos = cos_ref[...]
    sin = sin_ref[...]
    ri = lax.broadcasted_iota(jnp.int32, (R, R), 0)
    rj = lax.broadcasted_iota(jnp.int32, (R, R), 1)
    rel = ((ri % seq) - (rj % seq)).astype(F32)
    ok = ((ri // seq) == (rj // seq)) & (rel >= 0)
    row = lax.broadcasted_iota(jnp.int32, (R, 1), 0)
    t = (row % seq).astype(F32)

    for h in range(RET_HEADS):
        lg = lg_ref[h]
        q = _rotary(q_ref[:, h * RET_DK:(h + 1) * RET_DK], cos, sin)
        k = _rotary(k_ref[:, h * RET_DK:(h + 1) * RET_DK], cos, sin) * (RET_DK ** -0.5)
        v = v_ref[:, h * RET_DV:(h + 1) * RET_DV]
        mask = jnp.where(ok, jnp.exp(lg * jnp.maximum(rel, 0.0)), 0.0)
        q_dec = jnp.exp(lg * (t + 1.0))
        kd = k * jnp.exp(lg * (seq - 1.0 - t))
        inner = _dot(_dot_nt(q, k) * mask, v)
        cross = jnp.zeros_like(inner)
        for b in range(nb):
            s = s0_ref[b, h]
            mine = (row // seq) == b
            cross = jnp.where(mine, _dot(q, s), cross)
            s_out_ref[b, h] = jnp.exp(lg * seq) * s + _dot_tn(jnp.where(mine, kd, 0.0), v)
        o_ref[:, h * RET_DV:(h + 1) * RET_DV] = _ret_finish(
            inner + cross * q_dec, g_ref[:, h * RET_DV:(h + 1) * RET_DV]).astype(BF16)


def _ret_sample(z, lg, cos, sin, s0, o_prev, row0, batch, seq):
    R = RET_SAMPLE_ROWS
    nb = R // seq
    rb0 = row0 // R
    return pl.pallas_call(
        functools.partial(_ret_sample_kernel, seq=seq),
        grid=(batch // nb,),
        in_specs=[
            pl.BlockSpec(memory_space=pltpu.SMEM),
            pl.BlockSpec((R, RET_QK), lambda i: (rb0 + i, ZQ // RET_QK)),
            pl.BlockSpec((R, RET_QK), lambda i: (rb0 + i, ZK // RET_QK)),
            pl.BlockSpec((R, RET_V), lambda i: (rb0 + i, ZV // RET_V)),
            pl.BlockSpec((R, RET_V), lambda i: (rb0 + i, ZG // RET_V)),
            pl.BlockSpec((R, RET_DK), lambda i: (0, 0)),
            pl.BlockSpec((R, RET_DK), lambda i: (0, 0)),
            pl.BlockSpec((nb, RET_HEADS, RET_DK, RET_DV), lambda i: (i, 0, 0, 0)),
            pl.BlockSpec(memory_space=pl.ANY),
        ],
        out_specs=[
            pl.BlockSpec((R, RET_V), lambda i: (rb0 + i, 0)),
            pl.BlockSpec((nb, RET_HEADS, RET_DK, RET_DV), lambda i: (i, 0, 0, 0)),
        ],
        out_shape=[
            jax.ShapeDtypeStruct(o_prev.shape, BF16),
            jax.ShapeDtypeStruct((batch, RET_HEADS, RET_DK, RET_DV), F32),
        ],
        input_output_aliases={8: 0},
        compiler_params=_cparams(("parallel",)),
        name="ret_sample",
    )(lg, z, z, z, z, cos, sin, s0, o_prev)


def _rope_tables(pos):
    half = RET_DK // 2
    inv = 1.0 / (ROPE_BASE ** jnp.linspace(0.0, 1.0, half, dtype=F32))
    ang = pos[:, None] * inv[None, :]
    cos = jnp.repeat(jnp.cos(ang), 2, axis=-1)
    sin = jnp.sin(ang)
    sin_signed = jnp.stack([-sin, sin], axis=-1).reshape(pos.shape[0], RET_DK)
    return cos, sin_signed


def _segsum(x, seg_ref):
    rows = x.shape[0]
    n = rows * RWKV_PAIRS
    xs = jnp.concatenate([x[:, p * LANES:(p + 1) * LANES] for p in range(RWKV_PAIRS)], axis=0)
    ss = jnp.dot(jnp.concatenate(_bf16_terms(xs, 2), axis=0), seg_ref[...], preferred_element_type=F32)
    ss = ss[0:n] + ss[n:2 * n]
    return jnp.concatenate([ss[p * rows:(p + 1) * rows] for p in range(RWKV_PAIRS)], axis=1)


def _run_staged(gens, lead):
    out = [None] * len(gens)
    live = [True] * len(gens)

    def step(i):
        try:
            next(gens[i])
        except StopIteration as stop:
            out[i] = stop.value
            live[i] = False

    for _ in range(lead):
        if live[0]:
            step(0)
    while any(live):
        for i in range(len(gens)):
            if live[i]:
                step(i)
    return out


def _rwkv_chunk(zr, zl, prev_r, prev_l, prm, n_seq, read_state=None, update_state=None, fold_state=None):
    (mur_ref, mul_ref, w0_ref, wup_ref, a0_ref, aup_ref, gup_ref,
     kk_ref, ka_ref, rk_ref, lnw_ref, lnb_ref, seg_ref) = prm
    C = RWKV_CHUNK
    W = RWKV_W
    R = zr.shape[0]
    TS = C // n_seq
    row = lax.broadcasted_iota(jnp.int32, (R, 1), 0)
    first = (row % TS) == 0

    def mix(z, prev, mu_ref):
        z_prev = jnp.where(first, prev, pltpu.roll(z, 1, 0))
        return z + (z_prev - z) * mu_ref[...]

    xs = mix(zr, prev_r, mur_ref)
    xl = mix(zl, prev_l, mul_ref)
    r = xs[:, 0:W]
    k = xs[:, W:2 * W]
    v = xs[:, 2 * W:3 * W]
    x_wa = xl[:, 0:LANES]
    x_g = xl[:, LANES:LORA_PAD]

    y_w = w0_ref[...] + _dot(jnp.tanh(x_wa), wup_ref[...])
    w_log = -(jnp.maximum(-y_w, 0.0) + jnp.log1p(jnp.exp(-jnp.abs(y_w)))) - 0.5
    lw = -jnp.exp(w_log)
    a_rate = jax.nn.sigmoid(a0_ref[...] + _dot(x_wa, aup_ref[...]))
    g = _dot(jax.nn.sigmoid(x_g), gup_ref[...])
    yield

    kk = k * kk_ref[...]
    k = k * (1.0 + (a_rate - 1.0) * ka_ref[...])
    sums = _segsum(jnp.concatenate([kk * kk, r * k * rk_ref[...]], axis=0), seg_ref)
    kk = kk * lax.rsqrt(jnp.maximum(sums[0:R], 1e-24))
    bonus_rk = sums[R:2 * R]
    a = -kk
    b = kk * a_rate
    yield

    ti = lax.broadcasted_iota(jnp.int32, (2 * R, R), 0)
    tj = lax.broadcasted_iota(jnp.int32, (2 * R, R), 1)
    same = ((ti % R) // TS) == (tj // TS)
    sel = (same & ((ti >= R) | (tj <= ti))).astype(BF16)
    sums_lw = sum(jnp.dot(sel, t, preferred_element_type=F32) for t in _bf16_terms(lw, 3))
    cum = sums_lw[0:R]
    tot = sums_lw[R:2 * R]
    e_neg = jnp.exp(-cum)
    a_t = a * jnp.exp(cum - lw)
    b_t = b * e_neg
    k_t = k * e_neg
    r_t = r * jnp.exp(cum)
    e_tail = jnp.exp(tot - cum)
    b_g = b * e_tail
    k_g = k * e_tail
    dec = jnp.exp(tot)
    yield

    lane = lax.broadcasted_iota(jnp.int32, (1, LANES), 1)
    lo = lane < RWKV_N

    def split(x):
        return jnp.concatenate([jnp.where(lo, x, 0.0), jnp.where(lo, 0.0, x)], axis=0)

    row4 = lax.broadcasted_iota(jnp.int32, (C, 4 * C), 0)
    col4 = lax.broadcasted_iota(jnp.int32, (C, 4 * C), 1) % C
    same4 = (row4 // TS) == (col4 // TS)
    strict = same4 & (col4 < row4)
    incl = same4 & (col4 <= row4)
    row2 = lax.broadcasted_iota(jnp.int32, (C, 2 * C), 0)
    col2 = lax.broadcasted_iota(jnp.int32, (C, 2 * C), 1)
    eye2 = ((col2 % C) == row2).astype(F32)
    left = col2 < C

    tiles = [(slice(i * C, (i + 1) * C), slice(p * LANES, (p + 1) * LANES))
             for i in range(R // C) for p in range(RWKV_PAIRS)]
    v2s = [split(v[rs, sl]) for rs, sl in tiles]
    ars = [jnp.concatenate([a_t[rs, sl], r_t[rs, sl]], axis=0) for rs, sl in tiles]
    if fold_state is None:
        wys = read_state(ars)
    gms = [_dot_nt(ar, jnp.concatenate([split(b_t[rs, sl]), split(k_t[rs, sl])], axis=0))
           for ar, (rs, sl) in zip(ars, tiles)]
    gas = [jnp.where(strict, gm[0:C], 0.0) for gm in gms]
    grs = [jnp.where(incl, gm[C:2 * C], 0.0) for gm in gms]
    def blockdiag(x):
        return jnp.concatenate([jnp.where(left, x, 0.0), jnp.where(left, 0.0, x)], axis=0)

    ts = [eye2 + ga[:, 0:2 * C] for ga in gas]
    n_sq = TS.bit_length() - 2
    if n_sq >= 1:
        pws = [_dot(ga[:, 0:2 * C], blockdiag(ga[:, 0:2 * C])) for ga in gas]
        yield
        for _ in range(n_sq - 1):
            tps = [_dot(jnp.concatenate([t, pw], axis=0), blockdiag(pw)) for t, pw in zip(ts, pws)]
            ts = [t + tp[0:C] for t, tp in zip(ts, tps)]
            pws = [tp[C:2 * C] for tp in tps]
            yield
        ts = [t + _dot(t, blockdiag(pw)) for t, pw in zip(ts, pws)]
    yield
    wis = [_dot(ga[:, 2 * C:4 * C], v2) for ga, v2 in zip(gas, v2s)]
    bks = [jnp.concatenate([b_g[rs, sl], k_g[rs, sl]], axis=0) for rs, sl in tiles]
    if fold_state is None:
        us = [_dot(t, split(wi + wy[0:C])) for t, wi, wy in zip(ts, wis, wys)]
        ys = [wy[C:2 * C] + _dot(gr, jnp.concatenate([split(u), v2], axis=0))
              for wy, gr, u, v2 in zip(wys, grs, us, v2s)]
        update_state([jnp.concatenate([u, v[rs, sl]], axis=0) for u, (rs, sl) in zip(us, tiles)], bks, dec)
    else:
        zc = jnp.zeros((C, LANES), F32)
        aus = [_dot(t, jnp.concatenate([split(a_t[rs, sl]), split(wi)], axis=1))
               for t, wi, (rs, sl) in zip(ts, wis, tiles)]
        yield
        rys = [_dot(gr, jnp.concatenate(
            [jnp.concatenate([split(au[:, 0:LANES]), split(au[:, LANES:2 * LANES])], axis=1),
             jnp.concatenate([jnp.concatenate([zc, zc], axis=0), v2], axis=1)], axis=0))
            for gr, au, v2 in zip(grs, aus, v2s)]
        mns = [_dot_tn(bk, jnp.concatenate([au, jnp.concatenate([zc, v[rs, sl]], axis=1)], axis=0))
               for bk, au, (rs, sl) in zip(bks, aus, tiles)]
        yield
        ys = fold_state([r_t[rs, sl] + ry[:, 0:LANES] for ry, (rs, sl) in zip(rys, tiles)],
                        [ry[:, LANES:2 * LANES] for ry in rys], mns, dec)

    yield
    y = jnp.concatenate([jnp.concatenate(ys[i * RWKV_PAIRS:(i + 1) * RWKV_PAIRS], axis=1)
                         for i in range(R // C)], axis=0)
    mean = _segsum(y, seg_ref) * (1.0 / RWKV_N)
    d = y - mean
    var = _segsum(d * d, seg_ref) * (1.0 / RWKV_N)
    yn = d * lax.rsqrt(var + GN_EPS_RWKV) * lnw_ref[...] + lnb_ref[...]
    return ((yn + bonus_rk * v) * g).astype(BF16)


def _rwkv_prompt_kernel(*refs, batch):
    zr_refs = refs[:batch]
    zl_refs = refs[batch:2 * batch]
    prm = refs[2 * batch:2 * batch + 13]
    o_ref, s_out_ref, prevr_ref, prevl_ref, h_ref = refs[2 * batch + 13:]
    C = RWKV_CHUNK
    c = pl.program_id(0)

    @pl.when(c == 0)
    def _():
        prevr_ref[...] = jnp.zeros_like(prevr_ref)
        prevl_ref[...] = jnp.zeros_like(prevl_ref)
        h_ref[...] = jnp.zeros_like(h_ref)

    ki = lax.broadcasted_iota(jnp.int32, (LANES, LANES), 0)
    vi = lax.broadcasted_iota(jnp.int32, (LANES, LANES), 1)
    diag_blocks = (ki < RWKV_N) == (vi < RWKV_N)

    def group(i0):
        seqs = range(i0, i0 + RWKV_PROMPT_GROUP)
        tiles = [(i, p) for i in seqs for p in range(RWKV_PAIRS)]
        zr = jnp.concatenate([zr_refs[i][...] for i in seqs], axis=0)
        zl = jnp.concatenate([zl_refs[i][...] for i in seqs], axis=0)
        rows = lambda ref: jnp.concatenate(
            [jnp.broadcast_to(ref[i:i + 1, :], (C, ref.shape[1])) for i in seqs], axis=0)

        def fold_state(r_hats, y_inds, mns, dec):
            dec_cols = [jnp.broadcast_to(dec[(i - i0) * C:(i - i0) * C + 1, p * LANES:(p + 1) * LANES],
                                         (LANES, LANES)).T for i, p in tiles]
            hs = [h_ref[i, p] for i, p in tiles]
            res = [_dot(jnp.concatenate([rh, jnp.where(diag_blocks, mn[:, 0:LANES], 0.0)], axis=0), h)
                   for rh, mn, h in zip(r_hats, mns, hs)]
            for (i, p), h, dc, rs, mn in zip(tiles, hs, dec_cols, res, mns):
                h_ref[i, p] = h * dc + rs[C:C + LANES] + jnp.where(diag_blocks, mn[:, LANES:2 * LANES], 0.0)
            return [rs[0:C] + yi for rs, yi in zip(res, y_inds)]

        y = yield from _rwkv_chunk(zr, zl, rows(prevr_ref), rows(prevl_ref), prm, 1, fold_state=fold_state)
        for n, i in enumerate(seqs):
            o_ref[i] = y[n * C:(n + 1) * C]
            prevr_ref[i:i + 1, :] = zr[(n + 1) * C - 1:(n + 1) * C, :]
            prevl_ref[i:i + 1, :] = zl[(n + 1) * C - 1:(n + 1) * C, :]

    _run_staged([group(i0) for i0 in range(0, batch, RWKV_PROMPT_GROUP)], RWKV_PROMPT_LEAD)

    @pl.when(c == pl.num_programs(0) - 1)
    def _():
        for i in range(batch):
            for p in range(RWKV_PAIRS):
                s_out_ref[i, p] = h_ref[i, p].T


def _rwkv_sample_kernel(zr_ref, zl_ref, shr_ref, shl_ref, s0_ref, *rest, n_seq):
    prm = rest[:13]
    o_ref, s_out_ref = rest[13:]
    C = RWKV_CHUNK
    N = RWKV_N
    TS = C // n_seq
    tokseq = (lax.broadcasted_iota(jnp.int32, (1, LANES), 1) % C) // TS

    def pick(x, j):
        return jnp.where(tokseq == j, x, 0.0)

    def stacked(h):
        return s0_ref[:, h].reshape(n_seq * N, N)

    def read_state(ars):
        res = [[_dot_nt(stacked(2 * p + e), ar[:, e * N:(e + 1) * N]) for e in range(2)]
               for p, ar in enumerate(ars)]
        out = []
        for rp in res:
            halves = []
            for re in rp:
                acc = pick(re[0:N], 0)
                for j in range(1, n_seq):
                    acc = acc + pick(re[j * N:(j + 1) * N], j)
                halves.append(acc)
            out.append(jnp.concatenate(halves, axis=0).T)
        return out

    def update_state(uvs, bks, dec):
        uvts = [uv.T for uv in uvs]
        upds = [[_dot(jnp.concatenate([pick(uvt[e * N:(e + 1) * N], j) for j in range(n_seq)], axis=0),
                      bk[:, e * N:(e + 1) * N]) for e in range(2)]
                for uvt, bk in zip(uvts, bks)]
        for p, up in enumerate(upds):
            for e, upd in enumerate(up):
                h = 2 * p + e
                dec_h = jnp.concatenate(
                    [jnp.broadcast_to(dec[j * TS:j * TS + 1, h * N:(h + 1) * N], (N, N)) for j in range(n_seq)],
                    axis=0)
                s_out_ref[:, h] = (stacked(h) * dec_h + upd).reshape(n_seq, N, N)

    (o_ref[...],) = _run_staged([_rwkv_chunk(zr_ref[...], zl_ref[...], shr_ref[...], shl_ref[...], prm, n_seq,
                                             read_state, update_state)], 0)


def _rwkv_param_specs():
    W = RWKV_W
    shapes = [(1, 3 * W), (1, LORA_PAD), (1, W), (LANES, W), (1, W), (LANES, W), (LORA_PAD - LANES, W),
              (1, W), (1, W), (1, W), (1, W), (1, W), (LANES, LANES)]
    return [pl.BlockSpec(s, lambda *_: (0, 0)) for s in shapes]


def _rwkv_prompt(z, zl, params, batch, seq):
    C = RWKV_CHUNK
    W = RWKV_W
    nc = seq // C
    zr_spec = lambda i: pl.BlockSpec((C, 3 * W), lambda c: (i * nc + c, ZS // (3 * W)))
    zl_spec = lambda i: pl.BlockSpec((C, LORA_PAD), lambda c: (i * nc + c, 0))
    return pl.pallas_call(
        functools.partial(_rwkv_prompt_kernel, batch=batch),
        grid=(nc,),
        in_specs=[zr_spec(i) for i in range(batch)] + [zl_spec(i) for i in range(batch)]
        + _rwkv_param_specs(),
        out_specs=[
            pl.BlockSpec((batch, C, W), lambda c: (0, c, 0)),
            pl.BlockSpec((batch, RWKV_PAIRS, LANES, LANES), lambda c: (0, 0, 0, 0)),
        ],
        out_shape=[
            jax.ShapeDtypeStruct((batch, seq, W), BF16),
            jax.ShapeDtypeStruct((batch, RWKV_PAIRS, LANES, LANES), F32),
        ],
        scratch_shapes=[pltpu.VMEM((batch, 3 * W), F32), pltpu.VMEM((batch, LORA_PAD), F32),
                        pltpu.VMEM((batch, RWKV_PAIRS, LANES, LANES), F32)],
        compiler_params=_cparams(("arbitrary",)),
        name="rwkv_prompt",
    )(*([z] * batch), *([zl] * batch), *params)


def _rwkv_sample(z, zl, sh_r, sh_l, s0, params, row0, batch, seq):
    C = RWKV_CHUNK
    W = RWKV_W
    n_seq = C // seq
    rb0 = row0 // C
    return pl.pallas_call(
        functools.partial(_rwkv_sample_kernel, n_seq=n_seq),
        grid=(batch // n_seq,),
        in_specs=[
            pl.BlockSpec((C, 3 * W), lambda i: (rb0 + i, ZS // (3 * W))),
            pl.BlockSpec((C, LORA_PAD), lambda i: (rb0 + i, 0)),
            pl.BlockSpec((C, 3 * W), lambda i: (i, 0)),
            pl.BlockSpec((C, LORA_PAD), lambda i: (i, 0)),
            pl.BlockSpec((n_seq, RWKV_HEADS, RWKV_N, RWKV_N), lambda i: (i, 0, 0, 0)),
        ] + _rwkv_param_specs(),
        out_specs=[
            pl.BlockSpec((C, W), lambda i: (i, 0)),
            pl.BlockSpec((n_seq, RWKV_HEADS, RWKV_N, RWKV_N), lambda i: (i, 0, 0, 0)),
        ],
        out_shape=[
            jax.ShapeDtypeStruct((batch * seq, W), BF16),
            jax.ShapeDtypeStruct((batch, RWKV_HEADS, RWKV_N, RWKV_N), F32),
        ],
        compiler_params=_cparams(("parallel",)),
        name="rwkv_sample",
    )(z, zl, sh_r, sh_l, s0, *params)


def _pair_unblock(s):
    b = s.shape[0]
    lo = s[:, :, :RWKV_N, :RWKV_N]
    hi = s[:, :, RWKV_N:, RWKV_N:]
    return jnp.stack([lo, hi], axis=2).reshape(b, RWKV_HEADS, RWKV_N, RWKV_N)


def _mix_out_kernel(oret_ref, orwkv_ref, gr_ref, gw_ref, x_ref, wr_ref, ww_ref, wo_ref, o_ref):
    a = jnp.dot(oret_ref[...], wr_ref[...], preferred_element_type=F32)
    b = jnp.dot(orwkv_ref[...], ww_ref[...], preferred_element_type=F32)
    merged = jax.nn.sigmoid(gr_ref[...]) * a + jax.nn.sigmoid(gw_ref[...]) * b
    o_ref[...] = x_ref[...] + jnp.dot(merged.astype(BF16), wo_ref[...], preferred_element_type=F32)


def _mix_out(o_ret, o_rwkv, zg, x, w_ret_o, w_rwkv_o, w_out):
    n = x.shape[0]
    tm = TM_MIX
    resident = lambda shape: pl.BlockSpec(shape, lambda i: (0, 0), pipeline_mode=pl.Buffered(1))
    return pl.pallas_call(
        _mix_out_kernel,
        grid=(n // tm,),
        in_specs=[
            pl.BlockSpec((tm, RET_V), lambda i: (i, 0)),
            pl.BlockSpec((tm, RWKV_W), lambda i: (i, 0)),
            pl.BlockSpec((tm, D_MODEL), lambda i: (i, ZGR // D_MODEL)),
            pl.BlockSpec((tm, D_MODEL), lambda i: (i, ZGW // D_MODEL)),
            pl.BlockSpec((tm, D_MODEL), lambda i: (i, 0)),
            resident((RET_V, D_MODEL)),
            resident((RWKV_W, D_MODEL)),
            resident((D_MODEL, D_MODEL)),
        ],
        out_specs=pl.BlockSpec((tm, D_MODEL), lambda i: (i, 0)),
        out_shape=jax.ShapeDtypeStruct((n, D_MODEL), F32),
        compiler_params=_cparams(("parallel",)),
        name="mix_out",
    )(o_ret, o_rwkv, zg, zg, x, w_ret_o, w_rwkv_o, w_out)


def kernel(x_prompt, x_sample, state_ret, state_rwkv, state_shift, norm_ffn1, ffn1_w_gu, ffn1_w_down, norm_mix, w_in, w_ret_o, rwkv_mu, rwkv_w0, rwkv_w_up, rwkv_a0, rwkv_a_up, rwkv_g_up, rwkv_k_k, rwkv_k_a, rwkv_r_k, rwkv_ln_w, rwkv_ln_b, w_rwkv_o, w_out, norm_ffn2, ffn2_w_gu, ffn2_w_down, norm_final):
    bp, tp, d = x_prompt.shape
    bs, ts, _ = x_sample.shape
    n_p = bp * tp
    n_s = bs * ts
    W = RWKV_W
    row = lambda a: a.reshape(1, -1)

    x1, h_mix = _ffn((x_prompt.reshape(n_p, d), x_sample.reshape(n_s, d)), row(norm_ffn1[0]),
                     ffn1_w_gu[0].astype(BF16), ffn1_w_down[0].astype(BF16), row(norm_mix[0]),
                     final_norm=False, emit_h=True, out_rows=(n_p + n_s,))

    w_in_bf = w_in[0].astype(BF16)
    w_lora = jnp.pad(w_in_bf[:, Z_MAIN_W:Z_MAIN_W + LORA_W], ((0, 0), (0, LORA_PAD - LORA_W)))
    z = _proj(h_mix, w_in_bf, TN_PROJ, Z_MAIN_W)
    zl = _proj(h_mix, w_lora, LORA_PAD, LORA_PAD)
    zg = _proj(h_mix, w_in_bf[:, Z_MAIN_W + LORA_W:], TN_PROJ, 2 * D_MODEL)

    lg = jnp.log(1.0 - 2.0 ** (-5.0 - jnp.arange(RET_HEADS, dtype=F32)))
    cos_p, sin_p = _rope_tables(jnp.arange(tp, dtype=F32))
    cos_s, sin_s = _rope_tables(PAST_LEN + jnp.arange(ts, dtype=F32))
    rep = RET_SAMPLE_ROWS // ts
    o_ret, ret_p = _ret_prompt(z, lg, cos_p, sin_p, bp, tp)
    o_ret, ret_s = _ret_sample(z, lg, jnp.tile(cos_s, (rep, 1)), jnp.tile(sin_s, (rep, 1)),
                               state_ret[0], o_ret, n_p, bs, ts)

    mu = rwkv_mu[0]
    pad_l = lambda a: jnp.pad(a, [(0, 0)] * (a.ndim - 1) + [(0, LORA_PAD - LORA_W)])
    seg = (jnp.arange(LANES)[:, None] // RWKV_N == jnp.arange(LANES)[None, :] // RWKV_N).astype(BF16)
    zero64 = jnp.zeros((DECAY_LORA, W), F32)
    params = (
        row(mu[:3 * W]), pad_l(row(mu[3 * W:])),
        row(rwkv_w0[0]), jnp.concatenate([rwkv_w_up[0], zero64], axis=0).astype(BF16),
        row(rwkv_a0[0]), jnp.concatenate([zero64, rwkv_a_up[0]], axis=0).astype(BF16),
        jnp.pad(rwkv_g_up[0], ((0, LORA_PAD - LANES - GATE_LORA), (0, 0))).astype(BF16),
        row(rwkv_k_k[0]), row(rwkv_k_a[0]), row(rwkv_r_k[0]), row(rwkv_ln_w[0]), row(rwkv_ln_b[0]),
        seg,
    )
    orw_p, rwkv_p = _rwkv_prompt(z, zl, params, bp, tp)
    sh_s = state_shift[0]
    first_rows = lambda a: jnp.pad(a[:, None, :], ((0, 0), (0, ts - 1), (0, 0))).reshape(n_s, -1)
    orw_s, rwkv_s = _rwkv_sample(z, zl, first_rows(sh_s[:, :3 * W]), first_rows(pad_l(sh_s[:, 3 * W:])),
                                 state_rwkv[0], params, n_p, bs, ts)
    o_rwkv = jnp.concatenate([orw_p.reshape(n_p, W), orw_s], axis=0)

    x2 = _mix_out(o_ret, o_rwkv, zg, x1, w_ret_o[0].astype(BF16), w_rwkv_o[0].astype(BF16),
                  w_out[0].astype(BF16))
    y_p, y_s = _ffn((x2,), row(norm_ffn2[0]), ffn2_w_gu[0].astype(BF16), ffn2_w_down[0].astype(BF16),
                    row(norm_final), final_norm=True, emit_h=False, out_rows=(n_p, n_s))

    def last_rows(lo, hi, t):
        zr_last = lax.slice(z, (lo + t - 1, ZS), (hi, ZS + 3 * W), (t, 1))
        zl_last = lax.slice(zl, (lo + t - 1, 0), (hi, LORA_W), (t, 1))
        return jnp.concatenate([zr_last, zl_last], axis=-1)[None]

    return (y_p.reshape(bp, tp, d), y_s.reshape(bs, ts, d),
            ret_p[None], _pair_unblock(rwkv_p)[None], last_rows(0, n_p, tp),
            ret_s[None], rwkv_s[None], last_rows(n_p, n_p + n_s, ts))
```

```python
import functools

import jax
import jax.numpy as jnp
from jax import lax
from jax.experimental import pallas as pl
from jax.experimental.pallas import tpu as pltpu

F32 = jnp.float32
BF16 = jnp.bfloat16

D_MODEL = 2048
PAST_LEN = 16384
RET_HEADS = 4
RET_DK = 256
RET_DV = 512
RET_CHUNK = 128
ROPE_BASE = 10000.0
RET_QK = RET_HEADS * RET_DK
RET_V = RET_HEADS * RET_DV
RWKV_HEADS = 16
RWKV_N = 64
RWKV_W = RWKV_HEADS * RWKV_N
DECAY_LORA = 64
AAA_LORA = 64
GATE_LORA = 160
LORA_W = DECAY_LORA + AAA_LORA + GATE_LORA
LORA_PAD = 384
D_FF = 5632
NORM_EPS = 1e-6
GN_EPS_RET = 1e-6
GN_EPS_RWKV = 64e-5

ZQ, ZK, ZV, ZG, ZS = 0, 1024, 2048, 4096, 6144
Z_MAIN_W = 9216
ZGR, ZGW = 0, 2048

LANES = 128
RWKV_PAIRS = RWKV_W // LANES
RWKV_CHUNK = 64
RWKV_PROMPT_GROUP = 2
RWKV_PROMPT_LEAD = 4

VMEM_LIMIT = 56 * 1024 * 1024

TM = 512
TF = 512
TN_PROJ = 1024
TM_MIX = 256


def _rms(x, g, eps):
    return x * lax.rsqrt(jnp.mean(x * x, axis=-1, keepdims=True) + eps) * g


def _dot(a, b):
    return jnp.dot(a.astype(BF16), b.astype(BF16), preferred_element_type=F32)


def _dot_nt(a, b):
    return lax.dot_general(a.astype(BF16), b.astype(BF16), (((1,), (1,)), ((), ())),
                           preferred_element_type=F32)


def _dot_tn(a, b):
    return lax.dot_general(a.astype(BF16), b.astype(BF16), (((0,), (0,)), ((), ())),
                           preferred_element_type=F32)


def _bf16_terms(x, n):
    terms = []
    for _ in range(n):
        t = x.astype(BF16)
        terms.append(t)
        x = x - t.astype(F32)
    return terms


def _cparams(sem):
    return pltpu.CompilerParams(dimension_semantics=sem, vmem_limit_bytes=VMEM_LIMIT)


def _ffn_kernel(*refs, final_norm, emit_h, h_in, tiles_a, two_in, two_out):
    if h_in:
        hin_ref, refs = refs[0], refs[1:]
    xs = refs[:2] if two_in else refs[:1]
    g_ref, wg_ref, wu_ref, wd_ref, gf_ref = refs[len(xs):len(xs) + 5]
    outs = refs[len(xs) + 5:-2]
    if emit_h:
        outs, hn_ref = outs[:-1], outs[-1]
    h_ref, acc_ref = refs[-2:]
    i = pl.program_id(0)
    j = pl.program_id(1)

    def x_tile():
        if two_in:
            return jnp.where(i < tiles_a, xs[0][...], xs[1][...])
        return xs[0][...]

    @pl.when(j == 0)
    def _():
        if not h_in:
            h_ref[...] = _rms(x_tile(), g_ref[...], NORM_EPS).astype(BF16)
        acc_ref[...] = jnp.zeros_like(acc_ref)

    h = hin_ref[...] if h_in else h_ref[...]
    gate = jnp.dot(h, wg_ref[...], preferred_element_type=F32)
    up = jnp.dot(h, wu_ref[...], preferred_element_type=F32)
    act = (gate * jax.nn.sigmoid(gate)) * up
    acc_ref[...] += jnp.dot(act.astype(BF16), wd_ref[...], preferred_element_type=F32)

    @pl.when(j == pl.num_programs(1) - 1)
    def _():
        y = x_tile() + 0.5 * acc_ref[...]
        if final_norm:
            y = _rms(y, gf_ref[...], NORM_EPS)
        if emit_h:
            hn_ref[...] = _rms(y, gf_ref[...], NORM_EPS).astype(BF16)
        if two_out:
            @pl.when(i < tiles_a)
            def _():
                outs[0][...] = y

            @pl.when(i >= tiles_a)
            def _():
                outs[1][...] = y
        else:
            outs[0][...] = y


def _ffn(xs, g, w_gu, w_down, g_after, final_norm, emit_h, out_rows, h=None):
    two_in = len(xs) == 2
    two_out = len(out_rows) == 2
    n = sum(x.shape[0] for x in xs)
    tiles_a = (xs[0].shape[0] if two_in else out_rows[0]) // TM
    nj = D_FF // TF
    first = lambda i, j: (jnp.minimum(i, tiles_a - 1), 0)
    second = lambda i, j: (jnp.maximum(i - tiles_a, 0), 0)
    whole = lambda i, j: (i, 0)
    tile = lambda m: pl.BlockSpec((TM, D_MODEL), m)
    h_in = h is not None
    return pl.pallas_call(
        functools.partial(_ffn_kernel, final_norm=final_norm, emit_h=emit_h, h_in=h_in, tiles_a=tiles_a,
                          two_in=two_in, two_out=two_out),
        grid=(n // TM, nj),
        in_specs=([tile(whole)] if h_in else []) + ([tile(first), tile(second)] if two_in else [tile(whole)]) + [
            pl.BlockSpec((1, D_MODEL), lambda i, j: (0, 0)),
            pl.BlockSpec((D_MODEL, TF), lambda i, j: (0, j)),
            pl.BlockSpec((D_MODEL, TF), lambda i, j: (0, j + D_FF // TF)),
            pl.BlockSpec((TF, D_MODEL), lambda i, j: (j, 0)),
            pl.BlockSpec((1, D_MODEL), lambda i, j: (0, 0)),
        ],
        out_specs=([tile(first), tile(second)] if two_out else [tile(whole)]) + ([tile(whole)] if emit_h else []),
        out_shape=[jax.ShapeDtypeStruct((r, D_MODEL), F32) for r in out_rows]
        + ([jax.ShapeDtypeStruct((n, D_MODEL), BF16)] if emit_h else []),
        scratch_shapes=[pltpu.VMEM((TM, D_MODEL), BF16), pltpu.VMEM((TM, D_MODEL), F32)],
        compiler_params=_cparams(("arbitrary", "arbitrary")),
        name="ffn",
    )(*((h,) if h_in else ()), *xs, g, w_gu, w_gu, w_down, g_after)


def _proj_kernel(h_ref, wt_ref, o_ref, w_ref):
    @pl.when(pl.program_id(1) == 0)
    def _():
        w_ref[...] = wt_ref[...].astype(BF16)

    o_ref[...] = _dot_nt(h_ref[...], w_ref[...])


def _proj(h, w_t, col0, tn, nout):
    n = h.shape[0]
    return pl.pallas_call(
        _proj_kernel,
        grid=(nout // tn, n // TM),
        in_specs=[
            pl.BlockSpec((TM, D_MODEL), lambda j, i: (i, 0)),
            pl.BlockSpec((pl.Element(tn), pl.Element(D_MODEL)),
                         lambda j, i: (pl.multiple_of(col0 + j * tn, 8), 0)),
        ],
        out_specs=pl.BlockSpec((TM, tn), lambda j, i: (i, j)),
        out_shape=jax.ShapeDtypeStruct((n, nout), F32),
        scratch_shapes=[pltpu.VMEM((tn, D_MODEL), BF16)],
        compiler_params=_cparams(("parallel", "arbitrary")),
        name="proj",
    )(h, w_t)


def _pair_swap(x):
    n = x.shape[-1]
    lane = lax.broadcasted_iota(jnp.int32, x.shape, x.ndim - 1)
    prev = pltpu.roll(x, 1, x.ndim - 1)
    nxt = pltpu.roll(x, n - 1, x.ndim - 1)
    return jnp.where((lane & 1) == 1, prev, nxt)


def _rotary(x, cos, sin_signed):
    return x * cos + _pair_swap(x) * sin_signed


def _ret_finish(o, zg):
    o = o * lax.rsqrt(jnp.mean(o * o, axis=-1, keepdims=True) + GN_EPS_RET)
    return o * (zg * jax.nn.sigmoid(zg))


def _ret_prompt_kernel(lg_ref, q_ref, k_ref, v_ref, g_ref, cos_ref, sin_ref,
                       o_ref, s_out_ref, s_ref):
    c = pl.program_id(1)
    C = RET_CHUNK
    H = range(RET_HEADS)

    @pl.when(c == 0)
    def _():
        s_ref[...] = jnp.zeros_like(s_ref)

    cos = cos_ref[...]
    sin = sin_ref[...]
    ti = lax.broadcasted_iota(jnp.int32, (C, C), 0)
    tj = lax.broadcasted_iota(jnp.int32, (C, C), 1)
    rel = (ti - tj).astype(F32)
    idx = lax.broadcasted_iota(jnp.int32, (C, 1), 0).astype(F32)
    lgs = [lg_ref[h] for h in H]

    qs = [_rotary(q_ref[:, h * RET_DK:(h + 1) * RET_DK], cos, sin) for h in H]
    ks = [_rotary(k_ref[:, h * RET_DK:(h + 1) * RET_DK], cos, sin) * (RET_DK ** -0.5) for h in H]
    vs = [v_ref[:, h * RET_DV:(h + 1) * RET_DV] for h in H]
    ss = [s_ref[h] for h in H]
    scores = [_dot_nt(q, k) * jnp.where(rel >= 0, jnp.exp(lg * jnp.maximum(rel, 0.0)), 0.0)
              for q, k, lg in zip(qs, ks, lgs)]
    crosses = [_dot(q, s) * jnp.exp(lg * (idx + 1.0)) for q, s, lg in zip(qs, ss, lgs)]
    inners = [_dot(sc, v) for sc, v in zip(scores, vs)]
    s_news = [jnp.exp(lg * C) * s + _dot_tn(k * jnp.exp(lg * (C - 1.0 - idx)), v)
              for lg, s, k, v in zip(lgs, ss, ks, vs)]
    for h in H:
        s_ref[h] = s_news[h]
        o_ref[:, h * RET_DV:(h + 1) * RET_DV] = _ret_finish(
            inners[h] + crosses[h], g_ref[:, h * RET_DV:(h + 1) * RET_DV]).astype(BF16)

    @pl.when(c == pl.num_programs(1) - 1)
    def _():
        for h in H:
            s_out_ref[0, h] = s_news[h]


def _ret_prompt(z, lg, cos, sin, batch, seq):
    nc = seq // RET_CHUNK
    C = RET_CHUNK
    return pl.pallas_call(
        _ret_prompt_kernel,
        grid=(batch, nc),
        in_specs=[
            pl.BlockSpec(memory_space=pltpu.SMEM),
            pl.BlockSpec((C, RET_QK), lambda b, c: (b * nc + c, ZQ // RET_QK)),
            pl.BlockSpec((C, RET_QK), lambda b, c: (b * nc + c, ZK // RET_QK)),
            pl.BlockSpec((C, RET_V), lambda b, c: (b * nc + c, ZV // RET_V)),
            pl.BlockSpec((C, RET_V), lambda b, c: (b * nc + c, ZG // RET_V)),
            pl.BlockSpec((C, RET_DK), lambda b, c: (c, 0)),
            pl.BlockSpec((C, RET_DK), lambda b, c: (c, 0)),
        ],
        out_specs=[
            pl.BlockSpec((C, RET_V), lambda b, c: (b * nc + c, 0)),
            pl.BlockSpec((1, RET_HEADS, RET_DK, RET_DV), lambda b, c: (b, 0, 0, 0)),
        ],
        out_shape=[
            jax.ShapeDtypeStruct((batch * seq, RET_V), BF16),
            jax.ShapeDtypeStruct((batch, RET_HEADS, RET_DK, RET_DV), F32),
        ],
        scratch_shapes=[pltpu.VMEM((RET_HEADS, RET_DK, RET_DV), F32)],
        compiler_params=_cparams(("parallel", "arbitrary")),
        name="ret_prompt",
    )(lg, z, z, z, z, cos, sin)


RET_SAMPLE_ROWS = 16


def _ret_sample_kernel(lg_ref, q_ref, k_ref, v_ref, g_ref, cos_ref, sin_ref, s0_ref,
                       o_ref, s_out_ref, *, seq):
    R = RET_SAMPLE_ROWS
    nb = R // seq
    cos = cos_ref[...]
    sin = sin_ref[...]
    ri = lax.broadcasted_iota(jnp.int32, (R, R), 0)
    rj = lax.broadcasted_iota(jnp.int32, (R, R), 1)
    rel = ((ri % seq) - (rj % seq)).astype(F32)
    ok = ((ri // seq) == (rj // seq)) & (rel >= 0)
    row = lax.broadcasted_iota(jnp.int32, (R, 1), 0)
    t = (row % seq).astype(F32)

    for h in range(RET_HEADS):
        lg = lg_ref[h]
        q = _rotary(q_ref[:, h * RET_DK:(h + 1) * RET_DK], cos, sin)
        k = _rotary(k_ref[:, h * RET_DK:(h + 1) * RET_DK], cos, sin) * (RET_DK ** -0.5)
        v = v_ref[:, h * RET_DV:(h + 1) * RET_DV]
        mask = jnp.where(ok, jnp.exp(lg * jnp.maximum(rel, 0.0)), 0.0)
        q_dec = jnp.exp(lg * (t + 1.0))
        kd = k * jnp.exp(lg * (seq - 1.0 - t))
        inner = _dot(_dot_nt(q, k) * mask, v)
        cross = jnp.zeros_like(inner)
        for b in range(nb):
            s = s0_ref[b, h]
            mine = (row // seq) == b
            cross = jnp.where(mine, _dot(q, s), cross)
            s_out_ref[b, h] = jnp.exp(lg * seq) * s + _dot_tn(jnp.where(mine, kd, 0.0), v)
        o_ref[:, h * RET_DV:(h + 1) * RET_DV] = _ret_finish(
            inner + cross * q_dec, g_ref[:, h * RET_DV:(h + 1) * RET_DV]).astype(BF16)


def _ret_sample(z, lg, cos, sin, s0, row0, batch, seq):
    R = RET_SAMPLE_ROWS
    nb = R // seq
    rb0 = row0 // R
    return pl.pallas_call(
        functools.partial(_ret_sample_kernel, seq=seq),
        grid=(batch // nb,),
        in_specs=[
            pl.BlockSpec(memory_space=pltpu.SMEM),
            pl.BlockSpec((R, RET_QK), lambda i: (rb0 + i, ZQ // RET_QK)),
            pl.BlockSpec((R, RET_QK), lambda i: (rb0 + i, ZK // RET_QK)),
            pl.BlockSpec((R, RET_V), lambda i: (rb0 + i, ZV // RET_V)),
            pl.BlockSpec((R, RET_V), lambda i: (rb0 + i, ZG // RET_V)),
            pl.BlockSpec((R, RET_DK), lambda i: (0, 0)),
            pl.BlockSpec((R, RET_DK), lambda i: (0, 0)),
            pl.BlockSpec((nb, RET_HEADS, RET_DK, RET_DV), lambda i: (i, 0, 0, 0)),
        ],
        out_specs=[
            pl.BlockSpec((R, RET_V), lambda i: (i, 0)),
            pl.BlockSpec((nb, RET_HEADS, RET_DK, RET_DV), lambda i: (i, 0, 0, 0)),
        ],
        out_shape=[
            jax.ShapeDtypeStruct((batch * seq, RET_V), BF16),
            jax.ShapeDtypeStruct((batch, RET_HEADS, RET_DK, RET_DV), F32),
        ],
        compiler_params=_cparams(("parallel",)),
        name="ret_sample",
    )(lg, z, z, z, z, cos, sin, s0)


def _rope_tables(pos):
    half = RET_DK // 2
    inv = 1.0 / (ROPE_BASE ** jnp.linspace(0.0, 1.0, half, dtype=F32))
    ang = pos[:, None] * inv[None, :]
    cos = jnp.repeat(jnp.cos(ang), 2, axis=-1)
    sin = jnp.sin(ang)
    sin_signed = jnp.stack([-sin, sin], axis=-1).reshape(pos.shape[0], RET_DK)
    return cos, sin_signed


def _segsum(x, seg_ref):
    rows = x.shape[0]
    n = rows * RWKV_PAIRS
    xs = jnp.concatenate([x[:, p * LANES:(p + 1) * LANES] for p in range(RWKV_PAIRS)], axis=0)
    ss = jnp.dot(jnp.concatenate(_bf16_terms(xs, 2), axis=0), seg_ref[...], preferred_element_type=F32)
    ss = ss[0:n] + ss[n:2 * n]
    return jnp.concatenate([ss[p * rows:(p + 1) * rows] for p in range(RWKV_PAIRS)], axis=1)


def _run_staged(gens, lead):
    out = [None] * len(gens)
    live = [True] * len(gens)

    def step(i):
        try:
            next(gens[i])
        except StopIteration as stop:
            out[i] = stop.value
            live[i] = False

    for _ in range(lead):
        if live[0]:
            step(0)
    while any(live):
        for i in range(len(gens)):
            if live[i]:
                step(i)
    return out


def _rwkv_chunk(zr, zl, prev_r, prev_l, prm, n_seq, read_state=None, update_state=None, fold_state=None):
    (mur_ref, mul_ref, w0_ref, wup_ref, a0_ref, aup_ref, gup_ref,
     kk_ref, ka_ref, rk_ref, lnw_ref, lnb_ref, seg_ref) = prm
    C = RWKV_CHUNK
    W = RWKV_W
    R = zr.shape[0]
    TS = C // n_seq
    row = lax.broadcasted_iota(jnp.int32, (R, 1), 0)
    first = (row % TS) == 0

    def mix(z, prev, mu_ref):
        z_prev = jnp.where(first, prev, pltpu.roll(z, 1, 0))
        return z + (z_prev - z) * mu_ref[...]

    xs = mix(zr, prev_r, mur_ref)
    xl = mix(zl, prev_l, mul_ref)
    r = xs[:, 0:W]
    k = xs[:, W:2 * W]
    v = xs[:, 2 * W:3 * W]
    x_wa = xl[:, 0:LANES]
    x_g = xl[:, LANES:LORA_PAD]

    y_w = w0_ref[...] + _dot(jnp.tanh(x_wa), wup_ref[...])
    w_log = -(jnp.maximum(-y_w, 0.0) + jnp.log1p(jnp.exp(-jnp.abs(y_w)))) - 0.5
    lw = -jnp.exp(w_log)
    a_rate = jax.nn.sigmoid(a0_ref[...] + _dot(x_wa, aup_ref[...]))
    g = _dot(jax.nn.sigmoid(x_g), gup_ref[...])
    yield

    kk = k * kk_ref[...]
    k = k * (1.0 + (a_rate - 1.0) * ka_ref[...])
    sums = _segsum(jnp.concatenate([kk * kk, r * k * rk_ref[...]], axis=0), seg_ref)
    kk = kk * lax.rsqrt(jnp.maximum(sums[0:R], 1e-24))
    bonus_rk = sums[R:2 * R]
    a = -kk
    b = kk * a_rate
    yield

    ti = lax.broadcasted_iota(jnp.int32, (2 * R, R), 0)
    tj = lax.broadcasted_iota(jnp.int32, (2 * R, R), 1)
    same = ((ti % R) // TS) == (tj // TS)
    sel = (same & ((ti >= R) | (tj <= ti))).astype(BF16)
    sums_lw = sum(jnp.dot(sel, t, preferred_element_type=F32) for t in _bf16_terms(lw, 3))
    cum = sums_lw[0:R]
    tot = sums_lw[R:2 * R]
    e_neg = jnp.exp(-cum)
    a_t = a * jnp.exp(cum - lw)
    b_t = b * e_neg
    k_t = k * e_neg
    r_t = r * jnp.exp(cum)
    e_tail = jnp.exp(tot - cum)
    b_g = b * e_tail
    k_g = k * e_tail
    dec = jnp.exp(tot)
    yield

    lane = lax.broadcasted_iota(jnp.int32, (1, LANES), 1)
    lo = lane < RWKV_N

    def split(x):
        return jnp.concatenate([jnp.where(lo, x, 0.0), jnp.where(lo, 0.0, x)], axis=0)

    row4 = lax.broadcasted_iota(jnp.int32, (C, 4 * C), 0)
    col4 = lax.broadcasted_iota(jnp.int32, (C, 4 * C), 1) % C
    same4 = (row4 // TS) == (col4 // TS)
    strict = same4 & (col4 < row4)
    incl = same4 & (col4 <= row4)
    row2 = lax.broadcasted_iota(jnp.int32, (C, 2 * C), 0)
    col2 = lax.broadcasted_iota(jnp.int32, (C, 2 * C), 1)
    eye2 = ((col2 % C) == row2).astype(F32)
    left = col2 < C

    tiles = [(slice(i * C, (i + 1) * C), slice(p * LANES, (p + 1) * LANES))
             for i in range(R // C) for p in range(RWKV_PAIRS)]
    v2s = [split(v[rs, sl]) for rs, sl in tiles]
    ars = [jnp.concatenate([a_t[rs, sl], r_t[rs, sl]], axis=0) for rs, sl in tiles]
    if fold_state is None:
        wys = read_state(ars)
    gms = [_dot_nt(ar, jnp.concatenate([split(b_t[rs, sl]), split(k_t[rs, sl])], axis=0))
           for ar, (rs, sl) in zip(ars, tiles)]
    gas = [jnp.where(strict, gm[0:C], 0.0) for gm in gms]
    grs = [jnp.where(incl, gm[C:2 * C], 0.0) for gm in gms]
    def blockdiag(x):
        return jnp.concatenate([jnp.where(left, x, 0.0), jnp.where(left, 0.0, x)], axis=0)

    ts = [eye2 + ga[:, 0:2 * C] for ga in gas]
    n_sq = TS.bit_length() - 2
    if n_sq >= 1:
        pws = [_dot(ga[:, 0:2 * C], blockdiag(ga[:, 0:2 * C])) for ga in gas]
        yield
        for _ in range(n_sq - 1):
            tps = [_dot(jnp.concatenate([t, pw], axis=0), blockdiag(pw)) for t, pw in zip(ts, pws)]
            ts = [t + tp[0:C] for t, tp in zip(ts, tps)]
            pws = [tp[C:2 * C] for tp in tps]
            yield
        ts = [t + _dot(t, blockdiag(pw)) for t, pw in zip(ts, pws)]
    yield
    wis = [_dot(ga[:, 2 * C:4 * C], v2) for ga, v2 in zip(gas, v2s)]
    bks = [jnp.concatenate([b_g[rs, sl], k_g[rs, sl]], axis=0) for rs, sl in tiles]
    if fold_state is None:
        us = [_dot(t, split(wi + wy[0:C])) for t, wi, wy in zip(ts, wis, wys)]
        ys = [wy[C:2 * C] + _dot(gr, jnp.concatenate([split(u), v2], axis=0))
              for wy, gr, u, v2 in zip(wys, grs, us, v2s)]
        update_state([jnp.concatenate([u, v[rs, sl]], axis=0) for u, (rs, sl) in zip(us, tiles)], bks, dec)
    else:
        zc = jnp.zeros((C, LANES), F32)
        aus = [_dot(t, jnp.concatenate([split(a_t[rs, sl]), split(wi)], axis=1))
               for t, wi, (rs, sl) in zip(ts, wis, tiles)]
        yield
        rys = [_dot(gr, jnp.concatenate(
            [jnp.concatenate([split(au[:, 0:LANES]), split(au[:, LANES:2 * LANES])], axis=1),
             jnp.concatenate([jnp.concatenate([zc, zc], axis=0), v2], axis=1)], axis=0))
            for gr, au, v2 in zip(grs, aus, v2s)]
        mns = [_dot_tn(bk, jnp.concatenate([au, jnp.concatenate([zc, v[rs, sl]], axis=1)], axis=0))
               for bk, au, (rs, sl) in zip(bks, aus, tiles)]
        yield
        ys = fold_state([r_t[rs, sl] + ry[:, 0:LANES] for ry, (rs, sl) in zip(rys, tiles)],
                        [ry[:, LANES:2 * LANES] for ry in rys], mns, dec)

    yield
    y = jnp.concatenate([jnp.concatenate(ys[i * RWKV_PAIRS:(i + 1) * RWKV_PAIRS], axis=1)
                         for i in range(R // C)], axis=0)
    mean = _segsum(y, seg_ref) * (1.0 / RWKV_N)
    d = y - mean
    var = _segsum(d * d, seg_ref) * (1.0 / RWKV_N)
    yn = d * lax.rsqrt(var + GN_EPS_RWKV) * lnw_ref[...] + lnb_ref[...]
    return ((yn + bonus_rk * v) * g).astype(BF16)


def _rwkv_prompt_kernel(*refs, batch):
    zr_refs = refs[:batch]
    zl_refs = refs[batch:2 * batch]
    prm = refs[2 * batch:2 * batch + 13]
    o_ref, s_out_ref, prevr_ref, prevl_ref, h_ref = refs[2 * batch + 13:]
    C = RWKV_CHUNK
    c = pl.program_id(0)

    @pl.when(c == 0)
    def _():
        prevr_ref[...] = jnp.zeros_like(prevr_ref)
        prevl_ref[...] = jnp.zeros_like(prevl_ref)
        h_ref[...] = jnp.zeros_like(h_ref)

    ki = lax.broadcasted_iota(jnp.int32, (LANES, LANES), 0)
    vi = lax.broadcasted_iota(jnp.int32, (LANES, LANES), 1)
    diag_blocks = (ki < RWKV_N) == (vi < RWKV_N)

    def group(i0):
        seqs = range(i0, i0 + RWKV_PROMPT_GROUP)
        tiles = [(i, p) for i in seqs for p in range(RWKV_PAIRS)]
        zr = jnp.concatenate([zr_refs[i][...] for i in seqs], axis=0)
        zl = jnp.concatenate([zl_refs[i][...] for i in seqs], axis=0)
        rows = lambda ref: jnp.concatenate(
            [jnp.broadcast_to(ref[i:i + 1, :], (C, ref.shape[1])) for i in seqs], axis=0)

        def fold_state(r_hats, y_inds, mns, dec):
            dec_cols = [jnp.broadcast_to(dec[(i - i0) * C:(i - i0) * C + 1, p * LANES:(p + 1) * LANES],
                                         (LANES, LANES)).T for i, p in tiles]
            hs = [h_ref[i, p] for i, p in tiles]
            res = [_dot(jnp.concatenate([rh, jnp.where(diag_blocks, mn[:, 0:LANES], 0.0)], axis=0), h)
                   for rh, mn, h in zip(r_hats, mns, hs)]
            for (i, p), h, dc, rs, mn in zip(tiles, hs, dec_cols, res, mns):
                h_ref[i, p] = h * dc + rs[C:C + LANES] + jnp.where(diag_blocks, mn[:, LANES:2 * LANES], 0.0)
            return [rs[0:C] + yi for rs, yi in zip(res, y_inds)]

        y = yield from _rwkv_chunk(zr, zl, rows(prevr_ref), rows(prevl_ref), prm, 1, fold_state=fold_state)
        for n, i in enumerate(seqs):
            o_ref[i] = y[n * C:(n + 1) * C]
            prevr_ref[i:i + 1, :] = zr[(n + 1) * C - 1:(n + 1) * C, :]
            prevl_ref[i:i + 1, :] = zl[(n + 1) * C - 1:(n + 1) * C, :]

    _run_staged([group(i0) for i0 in range(0, batch, RWKV_PROMPT_GROUP)], RWKV_PROMPT_LEAD)

    @pl.when(c == pl.num_programs(0) - 1)
    def _():
        for i in range(batch):
            for p in range(RWKV_PAIRS):
                s_out_ref[i, p] = h_ref[i, p].T


def _rwkv_sample_kernel(zr_ref, zl_ref, shr_ref, shl_ref, s0_ref, *rest, n_seq):
    prm = rest[:13]
    o_ref, s_out_ref = rest[13:]
    C = RWKV_CHUNK
    N = RWKV_N
    TS = C // n_seq
    tokseq = (lax.broadcasted_iota(jnp.int32, (1, LANES), 1) % C) // TS

    def pick(x, j):
        return jnp.where(tokseq == j, x, 0.0)

    def stacked(h):
        return s0_ref[:, h].reshape(n_seq * N, N)

    def read_state(ars):
        res = [[_dot_nt(stacked(2 * p + e), ar[:, e * N:(e + 1) * N]) for e in range(2)]
               for p, ar in enumerate(ars)]
        out = []
        for rp in res:
            halves = []
            for re in rp:
                acc = pick(re[0:N], 0)
                for j in range(1, n_seq):
                    acc = acc + pick(re[j * N:(j + 1) * N], j)
                halves.append(acc)
            out.append(jnp.concatenate(halves, axis=0).T)
        return out

    def update_state(uvs, bks, dec):
        uvts = [uv.T for uv in uvs]
        upds = [[_dot(jnp.concatenate([pick(uvt[e * N:(e + 1) * N], j) for j in range(n_seq)], axis=0),
                      bk[:, e * N:(e + 1) * N]) for e in range(2)]
                for uvt, bk in zip(uvts, bks)]
        for p, up in enumerate(upds):
            for e, upd in enumerate(up):
                h = 2 * p + e
                dec_h = jnp.concatenate(
                    [jnp.broadcast_to(dec[j * TS:j * TS + 1, h * N:(h + 1) * N], (N, N)) for j in range(n_seq)],
                    axis=0)
                s_out_ref[:, h] = (stacked(h) * dec_h + upd).reshape(n_seq, N, N)

    (o_ref[...],) = _run_staged([_rwkv_chunk(zr_ref[...], zl_ref[...], shr_ref[...], shl_ref[...], prm, n_seq,
                                             read_state, update_state)], 0)


def _rwkv_param_specs():
    W = RWKV_W
    shapes = [(1, 3 * W), (1, LORA_PAD), (1, W), (LANES, W), (1, W), (LANES, W), (LORA_PAD - LANES, W),
              (1, W), (1, W), (1, W), (1, W), (1, W), (LANES, LANES)]
    return [pl.BlockSpec(s, lambda *_: (0, 0)) for s in shapes]


def _rwkv_prompt(z, zl, params, batch, seq):
    C = RWKV_CHUNK
    W = RWKV_W
    nc = seq // C
    zr_spec = lambda i: pl.BlockSpec((C, 3 * W), lambda c: (i * nc + c, ZS // (3 * W)))
    zl_spec = lambda i: pl.BlockSpec((C, LORA_PAD), lambda c: (i * nc + c, 0))
    return pl.pallas_call(
        functools.partial(_rwkv_prompt_kernel, batch=batch),
        grid=(nc,),
        in_specs=[zr_spec(i) for i in range(batch)] + [zl_spec(i) for i in range(batch)]
        + _rwkv_param_specs(),
        out_specs=[
            pl.BlockSpec((batch, C, W), lambda c: (0, c, 0)),
            pl.BlockSpec((batch, RWKV_PAIRS, LANES, LANES), lambda c: (0, 0, 0, 0)),
        ],
        out_shape=[
            jax.ShapeDtypeStruct((batch, seq, W), BF16),
            jax.ShapeDtypeStruct((batch, RWKV_PAIRS, LANES, LANES), F32),
        ],
        scratch_shapes=[pltpu.VMEM((batch, 3 * W), F32), pltpu.VMEM((batch, LORA_PAD), F32),
                        pltpu.VMEM((batch, RWKV_PAIRS, LANES, LANES), F32)],
        compiler_params=_cparams(("arbitrary",)),
        name="rwkv_prompt",
    )(*([z] * batch), *([zl] * batch), *params)


def _rwkv_sample(z, zl, sh_r, sh_l, s0, params, row0, batch, seq):
    C = RWKV_CHUNK
    W = RWKV_W
    n_seq = C // seq
    rb0 = row0 // C
    return pl.pallas_call(
        functools.partial(_rwkv_sample_kernel, n_seq=n_seq),
        grid=(batch // n_seq,),
        in_specs=[
            pl.BlockSpec((C, 3 * W), lambda i: (rb0 + i, ZS // (3 * W))),
            pl.BlockSpec((C, LORA_PAD), lambda i: (rb0 + i, 0)),
            pl.BlockSpec((C, 3 * W), lambda i: (i, 0)),
            pl.BlockSpec((C, LORA_PAD), lambda i: (i, 0)),
            pl.BlockSpec((n_seq, RWKV_HEADS, RWKV_N, RWKV_N), lambda i: (i, 0, 0, 0)),
        ] + _rwkv_param_specs(),
        out_specs=[
            pl.BlockSpec((C, W), lambda i: (i, 0)),
            pl.BlockSpec((n_seq, RWKV_HEADS, RWKV_N, RWKV_N), lambda i: (i, 0, 0, 0)),
        ],
        out_shape=[
            jax.ShapeDtypeStruct((batch * seq, W), BF16),
            jax.ShapeDtypeStruct((batch, RWKV_HEADS, RWKV_N, RWKV_N), F32),
        ],
        compiler_params=_cparams(("parallel",)),
        name="rwkv_sample",
    )(z, zl, sh_r, sh_l, s0, *params)


def _pair_unblock(s):
    b = s.shape[0]
    lo = s[:, :, :RWKV_N, :RWKV_N]
    hi = s[:, :, RWKV_N:, RWKV_N:]
    return jnp.stack([lo, hi], axis=2).reshape(b, RWKV_HEADS, RWKV_N, RWKV_N)


def _mix_out_kernel(oret_a_ref, oret_b_ref, orwkv_a_ref, orwkv_b_ref, gr_ref, gw_ref, x_ref,
                    wr_ref, ww_ref, wo_ref, gn_ref, o_ref, hn_ref, *, tiles_a):
    first = pl.program_id(0) < tiles_a
    o_ret = jnp.where(first, oret_a_ref[...], oret_b_ref[...])
    o_rwkv = jnp.where(first, orwkv_a_ref[...], orwkv_b_ref[...])
    a = jnp.dot(o_ret, wr_ref[...], preferred_element_type=F32)
    b = jnp.dot(o_rwkv, ww_ref[...], preferred_element_type=F32)
    merged = jax.nn.sigmoid(gr_ref[...]) * a + jax.nn.sigmoid(gw_ref[...]) * b
    x2 = x_ref[...] + jnp.dot(merged.astype(BF16), wo_ref[...], preferred_element_type=F32)
    o_ref[...] = x2
    hn_ref[...] = _rms(x2, gn_ref[...], NORM_EPS).astype(BF16)


def _mix_out(o_rets, o_rwkvs, zg, x, w_ret_o, w_rwkv_o, w_out, g_next):
    n = x.shape[0]
    tm = TM_MIX
    tiles_a = o_rets[0].shape[0] // tm
    resident = lambda shape: pl.BlockSpec(shape, lambda i: (0, 0), pipeline_mode=pl.Buffered(1))
    first = lambda i: (jnp.minimum(i, tiles_a - 1), 0)
    second = lambda i: (jnp.maximum(i - tiles_a, 0), 0)
    return pl.pallas_call(
        functools.partial(_mix_out_kernel, tiles_a=tiles_a),
        grid=(n // tm,),
        in_specs=[
            pl.BlockSpec((tm, RET_V), first),
            pl.BlockSpec((tm, RET_V), second),
            pl.BlockSpec((tm, RWKV_W), first),
            pl.BlockSpec((tm, RWKV_W), second),
            pl.BlockSpec((tm, D_MODEL), lambda i: (i, ZGR // D_MODEL)),
            pl.BlockSpec((tm, D_MODEL), lambda i: (i, ZGW // D_MODEL)),
            pl.BlockSpec((tm, D_MODEL), lambda i: (i, 0)),
            resident((RET_V, D_MODEL)),
            resident((RWKV_W, D_MODEL)),
            resident((D_MODEL, D_MODEL)),
            pl.BlockSpec((1, D_MODEL), lambda i: (0, 0)),
        ],
        out_specs=[pl.BlockSpec((tm, D_MODEL), lambda i: (i, 0)), pl.BlockSpec((tm, D_MODEL), lambda i: (i, 0))],
        out_shape=[jax.ShapeDtypeStruct((n, D_MODEL), F32), jax.ShapeDtypeStruct((n, D_MODEL), BF16)],
        compiler_params=_cparams(("arbitrary",)),
        name="mix_out",
    )(*o_rets, *o_rwkvs, zg, zg, x, w_ret_o, w_rwkv_o, w_out, g_next)


def kernel(x_prompt, x_sample, state_ret, state_rwkv, state_shift, norm_ffn1, ffn1_w_gu, ffn1_w_down, norm_mix, w_in, w_ret_o, rwkv_mu, rwkv_w0, rwkv_w_up, rwkv_a0, rwkv_a_up, rwkv_g_up, rwkv_k_k, rwkv_k_a, rwkv_r_k, rwkv_ln_w, rwkv_ln_b, w_rwkv_o, w_out, norm_ffn2, ffn2_w_gu, ffn2_w_down, norm_final):
    bp, tp, d = x_prompt.shape
    bs, ts, _ = x_sample.shape
    n_p = bp * tp
    n_s = bs * ts
    W = RWKV_W
    row = lambda a: a.reshape(1, -1)

    x1, h_mix = _ffn((x_prompt.reshape(n_p, d), x_sample.reshape(n_s, d)), row(norm_ffn1[0]),
                     ffn1_w_gu[0].astype(BF16), ffn1_w_down[0].astype(BF16), row(norm_mix[0]),
                     final_norm=False, emit_h=True, out_rows=(n_p + n_s,))

    w_in_t = jnp.swapaxes(w_in[0], 0, 1)
    z = _proj(h_mix, w_in_t, 0, TN_PROJ, Z_MAIN_W)
    zl = _proj(h_mix, w_in_t, Z_MAIN_W, LORA_PAD, LORA_PAD)
    zg = _proj(h_mix, w_in_t, Z_MAIN_W + LORA_W, TN_PROJ, 2 * D_MODEL)

    lg = jnp.log(1.0 - 2.0 ** (-5.0 - jnp.arange(RET_HEADS, dtype=F32)))
    cos_p, sin_p = _rope_tables(jnp.arange(tp, dtype=F32))
    cos_s, sin_s = _rope_tables(PAST_LEN + jnp.arange(ts, dtype=F32))
    rep = RET_SAMPLE_ROWS // ts
    oret_p, ret_p = _ret_prompt(z, lg, cos_p, sin_p, bp, tp)
    oret_s, ret_s = _ret_sample(z, lg, jnp.tile(cos_s, (rep, 1)), jnp.tile(sin_s, (rep, 1)),
                                state_ret[0], n_p, bs, ts)

    mu = rwkv_mu[0]
    pad_l = lambda a: jnp.pad(a, [(0, 0)] * (a.ndim - 1) + [(0, LORA_PAD - LORA_W)])
    seg = (jnp.arange(LANES)[:, None] // RWKV_N == jnp.arange(LANES)[None, :] // RWKV_N).astype(BF16)
    zero64 = jnp.zeros((DECAY_LORA, W), F32)
    params = (
        row(mu[:3 * W]), pad_l(row(mu[3 * W:])),
        row(rwkv_w0[0]), jnp.concatenate([rwkv_w_up[0], zero64], axis=0).astype(BF16),
        row(rwkv_a0[0]), jnp.concatenate([zero64, rwkv_a_up[0]], axis=0).astype(BF16),
        jnp.pad(rwkv_g_up[0], ((0, LORA_PAD - LANES - GATE_LORA), (0, 0))).astype(BF16),
        row(rwkv_k_k[0]), row(rwkv_k_a[0]), row(rwkv_r_k[0]), row(rwkv_ln_w[0]), row(rwkv_ln_b[0]),
        seg,
    )
    orw_p, rwkv_p = _rwkv_prompt(z, zl, params, bp, tp)
    sh_s = state_shift[0]
    first_rows = lambda a: jnp.pad(a[:, None, :], ((0, 0), (0, ts - 1), (0, 0))).reshape(n_s, -1)
    orw_s, rwkv_s = _rwkv_sample(z, zl, first_rows(sh_s[:, :3 * W]), first_rows(pad_l(sh_s[:, 3 * W:])),
                                 state_rwkv[0], params, n_p, bs, ts)

    x2, h_ffn2 = _mix_out((oret_p, oret_s), (orw_p.reshape(n_p, W), orw_s), zg, x1,
                          w_ret_o[0].astype(BF16), w_rwkv_o[0].astype(BF16), w_out[0].astype(BF16),
                          row(norm_ffn2[0]))
    y_p, y_s = _ffn((x2,), row(norm_ffn2[0]), ffn2_w_gu[0].astype(BF16), ffn2_w_down[0].astype(BF16),
                    row(norm_final), final_norm=True, emit_h=False, out_rows=(n_p, n_s), h=h_ffn2)

    def last_rows(lo, hi, t):
        zr_last = lax.slice(z, (lo + t - 1, ZS), (hi, ZS + 3 * W), (t, 1))
        zl_last = lax.slice(zl, (lo + t - 1, 0), (hi, LORA_W), (t, 1))
        return jnp.concatenate([zr_last, zl_last], axis=-1)[None]

    return (y_p.reshape(bp, tp, d), y_s.reshape(bs, ts, d),
            ret_p[None], _pair_unblock(rwkv_p)[None], last_rows(0, n_p, tp),
            ret_s[None], rwkv_s[None], last_rows(n_p, n_p + n_s, ts))
```

```python
import functools

import jax
import jax.numpy as jnp
from jax import lax
from jax.experimental import pallas as pl
from jax.experimental.pallas import tpu as pltpu

F32 = jnp.float32
BF16 = jnp.bfloat16

D_MODEL = 2048
PAST_LEN = 16384
RET_HEADS = 4
RET_DK = 256
RET_DV = 512
RET_CHUNK = 128
ROPE_BASE = 10000.0
RET_QK = RET_HEADS * RET_DK
RET_V = RET_HEADS * RET_DV
RWKV_HEADS = 16
RWKV_N = 64
RWKV_W = RWKV_HEADS * RWKV_N
DECAY_LORA = 64
AAA_LORA = 64
GATE_LORA = 160
LORA_W = DECAY_LORA + AAA_LORA + GATE_LORA
LORA_PAD = 384
D_FF = 5632
NORM_EPS = 1e-6
GN_EPS_RET = 1e-6
GN_EPS_RWKV = 64e-5

ZQ, ZK, ZV, ZG, ZS = 0, 1024, 2048, 4096, 6144
Z_MAIN_W = 9216
ZGR, ZGW = 0, 2048

LANES = 128
RWKV_PAIRS = RWKV_W // LANES
RWKV_CHUNK = 64
RWKV_PROMPT_GROUP = 2
RWKV_PROMPT_LEAD = 4

VMEM_LIMIT = 56 * 1024 * 1024

TM = 512
TF = 512
TF_CAST = 256
TN_PROJ = 1024
TM_MIX = 256


def _rms(x, g, eps):
    return x * lax.rsqrt(jnp.mean(x * x, axis=-1, keepdims=True) + eps) * g


def _dot(a, b):
    return jnp.dot(a.astype(BF16), b.astype(BF16), preferred_element_type=F32)


def _dot_nt(a, b):
    return lax.dot_general(a.astype(BF16), b.astype(BF16), (((1,), (1,)), ((), ())),
                           preferred_element_type=F32)


def _dot_tn(a, b):
    return lax.dot_general(a.astype(BF16), b.astype(BF16), (((0,), (0,)), ((), ())),
                           preferred_element_type=F32)


def _bf16_terms(x, n):
    terms = []
    for _ in range(n):
        t = x.astype(BF16)
        terms.append(t)
        x = x - t.astype(F32)
    return terms


def _cparams(sem):
    return pltpu.CompilerParams(dimension_semantics=sem, vmem_limit_bytes=VMEM_LIMIT)


def _ffn_kernel(*refs, final_norm, emit_h, h_in, cast_w):
    if h_in:
        hin_ref, refs = refs[0], refs[1:]
    x_ref, g_ref, wg_ref, wu_ref, wd_ref, gf_ref = refs[:6]
    outs = list(refs[6:-2])
    h_ref, acc_ref = refs[-2:]
    o_ref = outs.pop(0)
    hn_ref = outs.pop(0) if emit_h else None
    j = pl.program_id(1)

    @pl.when(j == 0)
    def _():
        if not h_in:
            h_ref[...] = _rms(x_ref[...], g_ref[...], NORM_EPS).astype(BF16)
        acc_ref[...] = jnp.zeros_like(acc_ref)

    wg, wu, wd = wg_ref[...], wu_ref[...], wd_ref[...]
    if cast_w:
        wg, wu, wd = wg.astype(BF16), wu.astype(BF16), wd.astype(BF16)
        for w_out_ref, w in zip(outs, (wg, wu, wd)):
            w_out_ref[...] = w
    h = hin_ref[...] if h_in else h_ref[...]
    gate = jnp.dot(h, wg, preferred_element_type=F32)
    up = jnp.dot(h, wu, preferred_element_type=F32)
    act = (gate * jax.nn.sigmoid(gate)) * up
    acc_ref[...] += jnp.dot(act.astype(BF16), wd, preferred_element_type=F32)

    @pl.when(j == pl.num_programs(1) - 1)
    def _():
        y = x_ref[...] + 0.5 * acc_ref[...]
        if final_norm:
            y = _rms(y, gf_ref[...], NORM_EPS)
        if emit_h:
            hn_ref[...] = _rms(y, gf_ref[...], NORM_EPS).astype(BF16)
        o_ref[...] = y


def _ffn(x, row0, rows, g, weights, g_after, final_norm, emit_h, h=None):
    cast_w = len(weights) == 2
    tf = TF_CAST if cast_w else TF
    nj = D_FF // tf
    t0 = row0 // TM
    tile = pl.BlockSpec((TM, D_MODEL), lambda i, j: (t0 + i, 0))
    out_tile = pl.BlockSpec((TM, D_MODEL), lambda i, j: (i, 0))
    h_in = h is not None
    w_specs = [pl.BlockSpec((D_MODEL, tf), lambda i, j: (0, j)),
               pl.BlockSpec((D_MODEL, tf), (lambda i, j: (0, j + nj)) if cast_w else (lambda i, j: (0, j))),
               pl.BlockSpec((tf, D_MODEL), lambda i, j: (j, 0))]
    w_args = (weights[0], weights[0], weights[1]) if cast_w else weights
    out_specs = [out_tile] + ([out_tile] if emit_h else [])
    out_shape = [jax.ShapeDtypeStruct((rows, D_MODEL), F32)] + (
        [jax.ShapeDtypeStruct((rows, D_MODEL), BF16)] if emit_h else [])
    if cast_w:
        out_specs += [pl.BlockSpec((D_MODEL, tf), lambda i, j: (0, j)),
                      pl.BlockSpec((D_MODEL, tf), lambda i, j: (0, j)),
                      pl.BlockSpec((tf, D_MODEL), lambda i, j: (j, 0))]
        out_shape += [jax.ShapeDtypeStruct((D_MODEL, D_FF), BF16), jax.ShapeDtypeStruct((D_MODEL, D_FF), BF16),
                      jax.ShapeDtypeStruct((D_FF, D_MODEL), BF16)]
        assert rows == TM, "each weight tile is written exactly once"
    return pl.pallas_call(
        functools.partial(_ffn_kernel, final_norm=final_norm, emit_h=emit_h, h_in=h_in, cast_w=cast_w),
        grid=(rows // TM, nj),
        in_specs=([tile] if h_in else []) + [tile, pl.BlockSpec((1, D_MODEL), lambda i, j: (0, 0))] + w_specs
        + [pl.BlockSpec((1, D_MODEL), lambda i, j: (0, 0))],
        out_specs=out_specs,
        out_shape=out_shape,
        scratch_shapes=[pltpu.VMEM((TM, D_MODEL), BF16), pltpu.VMEM((TM, D_MODEL), F32)],
        compiler_params=_cparams(("parallel", "arbitrary")),
        name="ffn_cast" if cast_w else "ffn",
    )(*((h,) if h_in else ()), x, g, *w_args, g_after)


def _proj_kernel(ha_ref, hb_ref, wt_ref, o_ref, w_ref, *, tiles_a):
    i = pl.program_id(1)

    @pl.when(i == 0)
    def _():
        w_ref[...] = wt_ref[...].astype(BF16)

    o_ref[...] = _dot_nt(jnp.where(i < tiles_a, ha_ref[...], hb_ref[...]), w_ref[...])


def _proj(hs, w_t, col0, tn, nout):
    tiles_a = hs[0].shape[0] // TM
    n = hs[0].shape[0] + hs[1].shape[0]
    return pl.pallas_call(
        functools.partial(_proj_kernel, tiles_a=tiles_a),
        grid=(nout // tn, n // TM),
        in_specs=[
            pl.BlockSpec((TM, D_MODEL), lambda j, i: (jnp.minimum(i, tiles_a - 1), 0)),
            pl.BlockSpec((TM, D_MODEL), lambda j, i: (jnp.maximum(i - tiles_a, 0), 0)),
            pl.BlockSpec((pl.Element(tn), pl.Element(D_MODEL)),
                         lambda j, i: (pl.multiple_of(col0 + j * tn, 8), 0)),
        ],
        out_specs=pl.BlockSpec((TM, tn), lambda j, i: (i, j)),
        out_shape=jax.ShapeDtypeStruct((n, nout), F32),
        scratch_shapes=[pltpu.VMEM((tn, D_MODEL), BF16)],
        compiler_params=_cparams(("parallel", "arbitrary")),
        name="proj",
    )(*hs, w_t)


def _pair_swap(x):
    n = x.shape[-1]
    lane = lax.broadcasted_iota(jnp.int32, x.shape, x.ndim - 1)
    prev = pltpu.roll(x, 1, x.ndim - 1)
    nxt = pltpu.roll(x, n - 1, x.ndim - 1)
    return jnp.where((lane & 1) == 1, prev, nxt)


def _rotary(x, cos, sin_signed):
    return x * cos + _pair_swap(x) * sin_signed


def _ret_finish(o, zg):
    o = o * lax.rsqrt(jnp.mean(o * o, axis=-1, keepdims=True) + GN_EPS_RET)
    return o * (zg * jax.nn.sigmoid(zg))


def _ret_prompt_kernel(lg_ref, q_ref, k_ref, v_ref, g_ref, cos_ref, sin_ref,
                       o_ref, s_out_ref, s_ref):
    c = pl.program_id(1)
    C = RET_CHUNK
    H = range(RET_HEADS)

    @pl.when(c == 0)
    def _():
        s_ref[...] = jnp.zeros_like(s_ref)

    cos = cos_ref[...]
    sin = sin_ref[...]
    ti = lax.broadcasted_iota(jnp.int32, (C, C), 0)
    tj = lax.broadcasted_iota(jnp.int32, (C, C), 1)
    rel = (ti - tj).astype(F32)
    idx = lax.broadcasted_iota(jnp.int32, (C, 1), 0).astype(F32)
    lgs = [lg_ref[h] for h in H]

    qs = [_rotary(q_ref[:, h * RET_DK:(h + 1) * RET_DK], cos, sin) for h in H]
    ks = [_rotary(k_ref[:, h * RET_DK:(h + 1) * RET_DK], cos, sin) * (RET_DK ** -0.5) for h in H]
    vs = [v_ref[:, h * RET_DV:(h + 1) * RET_DV] for h in H]
    ss = [s_ref[h] for h in H]
    scores = [_dot_nt(q, k) * jnp.where(rel >= 0, jnp.exp(lg * jnp.maximum(rel, 0.0)), 0.0)
              for q, k, lg in zip(qs, ks, lgs)]
    crosses = [_dot(q, s) * jnp.exp(lg * (idx + 1.0)) for q, s, lg in zip(qs, ss, lgs)]
    inners = [_dot(sc, v) for sc, v in zip(scores, vs)]
    s_news = [jnp.exp(lg * C) * s + _dot_tn(k * jnp.exp(lg * (C - 1.0 - idx)), v)
              for lg, s, k, v in zip(lgs, ss, ks, vs)]
    for h in H:
        s_ref[h] = s_news[h]
        o_ref[:, h * RET_DV:(h + 1) * RET_DV] = _ret_finish(
            inners[h] + crosses[h], g_ref[:, h * RET_DV:(h + 1) * RET_DV]).astype(BF16)

    @pl.when(c == pl.num_programs(1) - 1)
    def _():
        for h in H:
            s_out_ref[0, h] = s_news[h]


def _ret_prompt(z, lg, cos, sin, batch, seq):
    nc = seq // RET_CHUNK
    C = RET_CHUNK
    return pl.pallas_call(
        _ret_prompt_kernel,
        grid=(batch, nc),
        in_specs=[
            pl.BlockSpec(memory_space=pltpu.SMEM),
            pl.BlockSpec((C, RET_QK), lambda b, c: (b * nc + c, ZQ // RET_QK)),
            pl.BlockSpec((C, RET_QK), lambda b, c: (b * nc + c, ZK // RET_QK)),
            pl.BlockSpec((C, RET_V), lambda b, c: (b * nc + c, ZV // RET_V)),
            pl.BlockSpec((C, RET_V), lambda b, c: (b * nc + c, ZG // RET_V)),
            pl.BlockSpec((C, RET_DK), lambda b, c: (c, 0)),
            pl.BlockSpec((C, RET_DK), lambda b, c: (c, 0)),
        ],
        out_specs=[
            pl.BlockSpec((C, RET_V), lambda b, c: (b * nc + c, 0)),
            pl.BlockSpec((1, RET_HEADS, RET_DK, RET_DV), lambda b, c: (b, 0, 0, 0)),
        ],
        out_shape=[
            jax.ShapeDtypeStruct((batch * seq, RET_V), BF16),
            jax.ShapeDtypeStruct((batch, RET_HEADS, RET_DK, RET_DV), F32),
        ],
        scratch_shapes=[pltpu.VMEM((RET_HEADS, RET_DK, RET_DV), F32)],
        compiler_params=_cparams(("parallel", "arbitrary")),
        name="ret_prompt",
    )(lg, z, z, z, z, cos, sin)


RET_SAMPLE_ROWS = 16


def _ret_sample_kernel(lg_ref, q_ref, k_ref, v_ref, g_ref, cos_ref, sin_ref, s0_ref,
                       o_ref, s_out_ref, *, seq):
    R = RET_SAMPLE_ROWS
    nb = R // seq
    cos = cos_ref[...]
    sin = sin_ref[...]
    ri = lax.broadcasted_iota(jnp.int32, (R, R), 0)
    rj = lax.broadcasted_iota(jnp.int32, (R, R), 1)
    rel = ((ri % seq) - (rj % seq)).astype(F32)
    ok = ((ri // seq) == (rj // seq)) & (rel >= 0)
    row = lax.broadcasted_iota(jnp.int32, (R, 1), 0)
    t = (row % seq).astype(F32)

    for h in range(RET_HEADS):
        lg = lg_ref[h]
        q = _rotary(q_ref[:, h * RET_DK:(h + 1) * RET_DK], cos, sin)
        k = _rotary(k_ref[:, h * RET_DK:(h + 1) * RET_DK], cos, sin) * (RET_DK ** -0.5)
        v = v_ref[:, h * RET_DV:(h + 1) * RET_DV]
        mask = jnp.where(ok, jnp.exp(lg * jnp.maximum(rel, 0.0)), 0.0)
        q_dec = jnp.exp(lg * (t + 1.0))
        kd = k * jnp.exp(lg * (seq - 1.0 - t))
        inner = _dot(_dot_nt(q, k) * mask, v)
        cross = jnp.zeros_like(inner)
        for b in range(nb):
            s = s0_ref[b, h]
            mine = (row // seq) == b
            cross = jnp.where(mine, _dot(q, s), cross)
            s_out_ref[b, h] = jnp.exp(lg * seq) * s + _dot_tn(jnp.where(mine, kd, 0.0), v)
        o_ref[:, h * RET_DV:(h + 1) * RET_DV] = _ret_finish(
            inner + cross * q_dec, g_ref[:, h * RET_DV:(h + 1) * RET_DV]).astype(BF16)


def _ret_sample(z, lg, cos, sin, s0, row0, batch, seq):
    R = RET_SAMPLE_ROWS
    nb = R // seq
    rb0 = row0 // R
    return pl.pallas_call(
        functools.partial(_ret_sample_kernel, seq=seq),
        grid=(batch // nb,),
        in_specs=[
            pl.BlockSpec(memory_space=pltpu.SMEM),
            pl.BlockSpec((R, RET_QK), lambda i: (rb0 + i, ZQ // RET_QK)),
            pl.BlockSpec((R, RET_QK), lambda i: (rb0 + i, ZK // RET_QK)),
            pl.BlockSpec((R, RET_V), lambda i: (rb0 + i, ZV // RET_V)),
            pl.BlockSpec((R, RET_V), lambda i: (rb0 + i, ZG // RET_V)),
            pl.BlockSpec((R, RET_DK), lambda i: (0, 0)),
            pl.BlockSpec((R, RET_DK), lambda i: (0, 0)),
            pl.BlockSpec((nb, RET_HEADS, RET_DK, RET_DV), lambda i: (i, 0, 0, 0)),
        ],
        out_specs=[
            pl.BlockSpec((R, RET_V), lambda i: (i, 0)),
            pl.BlockSpec((nb, RET_HEADS, RET_DK, RET_DV), lambda i: (i, 0, 0, 0)),
        ],
        out_shape=[
            jax.ShapeDtypeStruct((batch * seq, RET_V), BF16),
            jax.ShapeDtypeStruct((batch, RET_HEADS, RET_DK, RET_DV), F32),
        ],
        compiler_params=_cparams(("parallel",)),
        name="ret_sample",
    )(lg, z, z, z, z, cos, sin, s0)


def _rope_tables(pos):
    half = RET_DK // 2
    inv = 1.0 / (ROPE_BASE ** jnp.linspace(0.0, 1.0, half, dtype=F32))
    ang = pos[:, None] * inv[None, :]
    cos = jnp.repeat(jnp.cos(ang), 2, axis=-1)
    sin = jnp.sin(ang)
    sin_signed = jnp.stack([-sin, sin], axis=-1).reshape(pos.shape[0], RET_DK)
    return cos, sin_signed


def _segsum(x, seg_ref):
    rows = x.shape[0]
    n = rows * RWKV_PAIRS
    xs = jnp.concatenate([x[:, p * LANES:(p + 1) * LANES] for p in range(RWKV_PAIRS)], axis=0)
    ss = jnp.dot(jnp.concatenate(_bf16_terms(xs, 2), axis=0), seg_ref[...], preferred_element_type=F32)
    ss = ss[0:n] + ss[n:2 * n]
    return jnp.concatenate([ss[p * rows:(p + 1) * rows] for p in range(RWKV_PAIRS)], axis=1)


def _run_staged(gens, lead):
    out = [None] * len(gens)
    live = [True] * len(gens)

    def step(i):
        try:
            next(gens[i])
        except StopIteration as stop:
            out[i] = stop.value
            live[i] = False

    for _ in range(lead):
        if live[0]:
            step(0)
    while any(live):
        for i in range(len(gens)):
            if live[i]:
                step(i)
    return out


def _rwkv_chunk(zr, zl, prev_r, prev_l, prm, n_seq, read_state=None, update_state=None, fold_state=None):
    (mur_ref, mul_ref, w0_ref, wup_ref, a0_ref, aup_ref, gup_ref,
     kk_ref, ka_ref, rk_ref, lnw_ref, lnb_ref, seg_ref) = prm
    C = RWKV_CHUNK
    W = RWKV_W
    R = zr.shape[0]
    TS = C // n_seq
    row = lax.broadcasted_iota(jnp.int32, (R, 1), 0)
    first = (row % TS) == 0

    def mix(z, prev, mu_ref):
        z_prev = jnp.where(first, prev, pltpu.roll(z, 1, 0))
        return z + (z_prev - z) * mu_ref[...]

    xs = mix(zr, prev_r, mur_ref)
    xl = mix(zl, prev_l, mul_ref)
    r = xs[:, 0:W]
    k = xs[:, W:2 * W]
    v = xs[:, 2 * W:3 * W]
    x_wa = xl[:, 0:LANES]
    x_g = xl[:, LANES:LORA_PAD]

    y_w = w0_ref[...] + _dot(jnp.tanh(x_wa), wup_ref[...])
    w_log = -(jnp.maximum(-y_w, 0.0) + jnp.log1p(jnp.exp(-jnp.abs(y_w)))) - 0.5
    lw = -jnp.exp(w_log)
    a_rate = jax.nn.sigmoid(a0_ref[...] + _dot(x_wa, aup_ref[...]))
    g = _dot(jax.nn.sigmoid(x_g), gup_ref[...])
    yield

    kk = k * kk_ref[...]
    k = k * (1.0 + (a_rate - 1.0) * ka_ref[...])
    sums = _segsum(jnp.concatenate([kk * kk, r * k * rk_ref[...]], axis=0), seg_ref)
    kk = kk * lax.rsqrt(jnp.maximum(sums[0:R], 1e-24))
    bonus_rk = sums[R:2 * R]
    a = -kk
    b = kk * a_rate
    yield

    ti = lax.broadcasted_iota(jnp.int32, (2 * R, R), 0)
    tj = lax.broadcasted_iota(jnp.int32, (2 * R, R), 1)
    same = ((ti % R) // TS) == (tj // TS)
    sel = (same & ((ti >= R) | (tj <= ti))).astype(BF16)
    sums_lw = sum(jnp.dot(sel, t, preferred_element_type=F32) for t in _bf16_terms(lw, 3))
    cum = sums_lw[0:R]
    tot = sums_lw[R:2 * R]
    e_neg = jnp.exp(-cum)
    a_t = a * jnp.exp(cum - lw)
    b_t = b * e_neg
    k_t = k * e_neg
    r_t = r * jnp.exp(cum)
    e_tail = jnp.exp(tot - cum)
    b_g = b * e_tail
    k_g = k * e_tail
    dec = jnp.exp(tot)
    yield

    lane = lax.broadcasted_iota(jnp.int32, (1, LANES), 1)
    lo = lane < RWKV_N

    def split(x):
        return jnp.concatenate([jnp.where(lo, x, 0.0), jnp.where(lo, 0.0, x)], axis=0)

    row4 = lax.broadcasted_iota(jnp.int32, (C, 4 * C), 0)
    col4 = lax.broadcasted_iota(jnp.int32, (C, 4 * C), 1) % C
    same4 = (row4 // TS) == (col4 // TS)
    strict = same4 & (col4 < row4)
    incl = same4 & (col4 <= row4)
    row2 = lax.broadcasted_iota(jnp.int32, (C, 2 * C), 0)
    col2 = lax.broadcasted_iota(jnp.int32, (C, 2 * C), 1)
    eye2 = ((col2 % C) == row2).astype(F32)
    left = col2 < C

    tiles = [(slice(i * C, (i + 1) * C), slice(p * LANES, (p + 1) * LANES))
             for i in range(R // C) for p in range(RWKV_PAIRS)]
    v2s = [split(v[rs, sl]) for rs, sl in tiles]
    ars = [jnp.concatenate([a_t[rs, sl], r_t[rs, sl]], axis=0) for rs, sl in tiles]
    if fold_state is None:
        wys = read_state(ars)
    gms = [_dot_nt(ar, jnp.concatenate([split(b_t[rs, sl]), split(k_t[rs, sl])], axis=0))
           for ar, (rs, sl) in zip(ars, tiles)]
    gas = [jnp.where(strict, gm[0:C], 0.0) for gm in gms]
    grs = [jnp.where(incl, gm[C:2 * C], 0.0) for gm in gms]
    def blockdiag(x):
        return jnp.concatenate([jnp.where(left, x, 0.0), jnp.where(left, 0.0, x)], axis=0)

    ts = [eye2 + ga[:, 0:2 * C] for ga in gas]
    n_sq = TS.bit_length() - 2
    if n_sq >= 1:
        pws = [_dot(ga[:, 0:2 * C], blockdiag(ga[:, 0:2 * C])) for ga in gas]
        yield
        for _ in range(n_sq - 1):
            tps = [_dot(jnp.concatenate([t, pw], axis=0), blockdiag(pw)) for t, pw in zip(ts, pws)]
            ts = [t + tp[0:C] for t, tp in zip(ts, tps)]
            pws = [tp[C:2 * C] for tp in tps]
            yield
        ts = [t + _dot(t, blockdiag(pw)) for t, pw in zip(ts, pws)]
    yield
    wis = [_dot(ga[:, 2 * C:4 * C], v2) for ga, v2 in zip(gas, v2s)]
    bks = [jnp.concatenate([b_g[rs, sl], k_g[rs, sl]], axis=0) for rs, sl in tiles]
    if fold_state is None:
        us = [_dot(t, split(wi + wy[0:C])) for t, wi, wy in zip(ts, wis, wys)]
        ys = [wy[C:2 * C] + _dot(gr, jnp.concatenate([split(u), v2], axis=0))
              for wy, gr, u, v2 in zip(wys, grs, us, v2s)]
        update_state([jnp.concatenate([u, v[rs, sl]], axis=0) for u, (rs, sl) in zip(us, tiles)], bks, dec)
    else:
        zc = jnp.zeros((C, LANES), F32)
        aus = [_dot(t, jnp.concatenate([split(a_t[rs, sl]), split(wi)], axis=1))
               for t, wi, (rs, sl) in zip(ts, wis, tiles)]
        yield
        rys = [_dot(gr, jnp.concatenate(
            [jnp.concatenate([split(au[:, 0:LANES]), split(au[:, LANES:2 * LANES])], axis=1),
             jnp.concatenate([jnp.concatenate([zc, zc], axis=0), v2], axis=1)], axis=0))
            for gr, au, v2 in zip(grs, aus, v2s)]
        mns = [_dot_tn(bk, jnp.concatenate([au, jnp.concatenate([zc, v[rs, sl]], axis=1)], axis=0))
               for bk, au, (rs, sl) in zip(bks, aus, tiles)]
        yield
        ys = fold_state([r_t[rs, sl] + ry[:, 0:LANES] for ry, (rs, sl) in zip(rys, tiles)],
                        [ry[:, LANES:2 * LANES] for ry in rys], mns, dec)

    yield
    y = jnp.concatenate([jnp.concatenate(ys[i * RWKV_PAIRS:(i + 1) * RWKV_PAIRS], axis=1)
                         for i in range(R // C)], axis=0)
    mean = _segsum(y, seg_ref) * (1.0 / RWKV_N)
    d = y - mean
    var = _segsum(d * d, seg_ref) * (1.0 / RWKV_N)
    yn = d * lax.rsqrt(var + GN_EPS_RWKV) * lnw_ref[...] + lnb_ref[...]
    return ((yn + bonus_rk * v) * g).astype(BF16)


def _rwkv_prompt_kernel(*refs, batch):
    zr_refs = refs[:batch]
    zl_refs = refs[batch:2 * batch]
    prm = refs[2 * batch:2 * batch + 13]
    o_ref, s_out_ref, prevr_ref, prevl_ref, h_ref = refs[2 * batch + 13:]
    C = RWKV_CHUNK
    c = pl.program_id(0)

    @pl.when(c == 0)
    def _():
        prevr_ref[...] = jnp.zeros_like(prevr_ref)
        prevl_ref[...] = jnp.zeros_like(prevl_ref)
        h_ref[...] = jnp.zeros_like(h_ref)

    ki = lax.broadcasted_iota(jnp.int32, (LANES, LANES), 0)
    vi = lax.broadcasted_iota(jnp.int32, (LANES, LANES), 1)
    diag_blocks = (ki < RWKV_N) == (vi < RWKV_N)

    def group(i0):
        seqs = range(i0, i0 + RWKV_PROMPT_GROUP)
        tiles = [(i, p) for i in seqs for p in range(RWKV_PAIRS)]
        zr = jnp.concatenate([zr_refs[i][...] for i in seqs], axis=0)
        zl = jnp.concatenate([zl_refs[i][...] for i in seqs], axis=0)
        rows = lambda ref: jnp.concatenate(
            [jnp.broadcast_to(ref[i:i + 1, :], (C, ref.shape[1])) for i in seqs], axis=0)

        def fold_state(r_hats, y_inds, mns, dec):
            dec_cols = [jnp.broadcast_to(dec[(i - i0) * C:(i - i0) * C + 1, p * LANES:(p + 1) * LANES],
                                         (LANES, LANES)).T for i, p in tiles]
            hs = [h_ref[i, p] for i, p in tiles]
            res = [_dot(jnp.concatenate([rh, jnp.where(diag_blocks, mn[:, 0:LANES], 0.0)], axis=0), h)
                   for rh, mn, h in zip(r_hats, mns, hs)]
            for (i, p), h, dc, rs, mn in zip(tiles, hs, dec_cols, res, mns):
                h_ref[i, p] = h * dc + rs[C:C + LANES] + jnp.where(diag_blocks, mn[:, LANES:2 * LANES], 0.0)
            return [rs[0:C] + yi for rs, yi in zip(res, y_inds)]

        y = yield from _rwkv_chunk(zr, zl, rows(prevr_ref), rows(prevl_ref), prm, 1, fold_state=fold_state)
        for n, i in enumerate(seqs):
            o_ref[i] = y[n * C:(n + 1) * C]
            prevr_ref[i:i + 1, :] = zr[(n + 1) * C - 1:(n + 1) * C, :]
            prevl_ref[i:i + 1, :] = zl[(n + 1) * C - 1:(n + 1) * C, :]

    _run_staged([group(i0) for i0 in range(0, batch, RWKV_PROMPT_GROUP)], RWKV_PROMPT_LEAD)

    @pl.when(c == pl.num_programs(0) - 1)
    def _():
        for i in range(batch):
            for p in range(RWKV_PAIRS):
                s_out_ref[i, p] = h_ref[i, p].T


def _rwkv_sample_kernel(zr_ref, zl_ref, shr_ref, shl_ref, s0_ref, *rest, n_seq):
    prm = rest[:13]
    o_ref, s_out_ref = rest[13:]
    C = RWKV_CHUNK
    N = RWKV_N
    TS = C // n_seq
    tokseq = (lax.broadcasted_iota(jnp.int32, (1, LANES), 1) % C) // TS

    def pick(x, j):
        return jnp.where(tokseq == j, x, 0.0)

    def stacked(h):
        return s0_ref[:, h].reshape(n_seq * N, N)

    def read_state(ars):
        res = [[_dot_nt(stacked(2 * p + e), ar[:, e * N:(e + 1) * N]) for e in range(2)]
               for p, ar in enumerate(ars)]
        out = []
        for rp in res:
            halves = []
            for re in rp:
                acc = pick(re[0:N], 0)
                for j in range(1, n_seq):
                    acc = acc + pick(re[j * N:(j + 1) * N], j)
                halves.append(acc)
            out.append(jnp.concatenate(halves, axis=0).T)
        return out

    def update_state(uvs, bks, dec):
        uvts = [uv.T for uv in uvs]
        upds = [[_dot(jnp.concatenate([pick(uvt[e * N:(e + 1) * N], j) for j in range(n_seq)], axis=0),
                      bk[:, e * N:(e + 1) * N]) for e in range(2)]
                for uvt, bk in zip(uvts, bks)]
        for p, up in enumerate(upds):
            for e, upd in enumerate(up):
                h = 2 * p + e
                dec_h = jnp.concatenate(
                    [jnp.broadcast_to(dec[j * TS:j * TS + 1, h * N:(h + 1) * N], (N, N)) for j in range(n_seq)],
                    axis=0)
                s_out_ref[:, h] = (stacked(h) * dec_h + upd).reshape(n_seq, N, N)

    (o_ref[...],) = _run_staged([_rwkv_chunk(zr_ref[...], zl_ref[...], shr_ref[...], shl_ref[...], prm, n_seq,
                                             read_state, update_state)], 0)


def _rwkv_param_specs():
    W = RWKV_W
    shapes = [(1, 3 * W), (1, LORA_PAD), (1, W), (LANES, W), (1, W), (LANES, W), (LORA_PAD - LANES, W),
              (1, W), (1, W), (1, W), (1, W), (1, W), (LANES, LANES)]
    return [pl.BlockSpec(s, lambda *_: (0, 0)) for s in shapes]


def _rwkv_prompt(z, zl, params, batch, seq):
    C = RWKV_CHUNK
    W = RWKV_W
    nc = seq // C
    zr_spec = lambda i: pl.BlockSpec((C, 3 * W), lambda c: (i * nc + c, ZS // (3 * W)))
    zl_spec = lambda i: pl.BlockSpec((C, LORA_PAD), lambda c: (i * nc + c, 0))
    return pl.pallas_call(
        functools.partial(_rwkv_prompt_kernel, batch=batch),
        grid=(nc,),
        in_specs=[zr_spec(i) for i in range(batch)] + [zl_spec(i) for i in range(batch)]
        + _rwkv_param_specs(),
        out_specs=[
            pl.BlockSpec((batch, C, W), lambda c: (0, c, 0)),
            pl.BlockSpec((batch, RWKV_PAIRS, LANES, LANES), lambda c: (0, 0, 0, 0)),
        ],
        out_shape=[
            jax.ShapeDtypeStruct((batch, seq, W), BF16),
            jax.ShapeDtypeStruct((batch, RWKV_PAIRS, LANES, LANES), F32),
        ],
        scratch_shapes=[pltpu.VMEM((batch, 3 * W), F32), pltpu.VMEM((batch, LORA_PAD), F32),
                        pltpu.VMEM((batch, RWKV_PAIRS, LANES, LANES), F32)],
        compiler_params=_cparams(("arbitrary",)),
        name="rwkv_prompt",
    )(*([z] * batch), *([zl] * batch), *params)


def _rwkv_sample(z, zl, sh_r, sh_l, s0, params, row0, batch, seq):
    C = RWKV_CHUNK
    W = RWKV_W
    n_seq = C // seq
    rb0 = row0 // C
    return pl.pallas_call(
        functools.partial(_rwkv_sample_kernel, n_seq=n_seq),
        grid=(batch // n_seq,),
        in_specs=[
            pl.BlockSpec((C, 3 * W), lambda i: (rb0 + i, ZS // (3 * W))),
            pl.BlockSpec((C, LORA_PAD), lambda i: (rb0 + i, 0)),
            pl.BlockSpec((C, 3 * W), lambda i: (i, 0)),
            pl.BlockSpec((C, LORA_PAD), lambda i: (i, 0)),
            pl.BlockSpec((n_seq, RWKV_HEADS, RWKV_N, RWKV_N), lambda i: (i, 0, 0, 0)),
        ] + _rwkv_param_specs(),
        out_specs=[
            pl.BlockSpec((C, W), lambda i: (i, 0)),
            pl.BlockSpec((n_seq, RWKV_HEADS, RWKV_N, RWKV_N), lambda i: (i, 0, 0, 0)),
        ],
        out_shape=[
            jax.ShapeDtypeStruct((batch * seq, W), BF16),
            jax.ShapeDtypeStruct((batch, RWKV_HEADS, RWKV_N, RWKV_N), F32),
        ],
        compiler_params=_cparams(("parallel",)),
        name="rwkv_sample",
    )(z, zl, sh_r, sh_l, s0, *params)


def _pair_unblock(s):
    b = s.shape[0]
    lo = s[:, :, :RWKV_N, :RWKV_N]
    hi = s[:, :, RWKV_N:, RWKV_N:]
    return jnp.stack([lo, hi], axis=2).reshape(b, RWKV_HEADS, RWKV_N, RWKV_N)


def _mix_out_kernel(oret_a_ref, oret_b_ref, orwkv_a_ref, orwkv_b_ref, xa_ref, xb_ref, gr_ref, gw_ref,
                    wr_ref, ww_ref, wo_ref, gn_ref, o_ref, hn_ref, *, tiles_a):
    first = pl.program_id(0) < tiles_a
    o_ret = jnp.where(first, oret_a_ref[...], oret_b_ref[...])
    o_rwkv = jnp.where(first, orwkv_a_ref[...], orwkv_b_ref[...])
    a = jnp.dot(o_ret, wr_ref[...], preferred_element_type=F32)
    b = jnp.dot(o_rwkv, ww_ref[...], preferred_element_type=F32)
    merged = jax.nn.sigmoid(gr_ref[...]) * a + jax.nn.sigmoid(gw_ref[...]) * b
    x2 = (jnp.where(first, xa_ref[...], xb_ref[...])
          + jnp.dot(merged.astype(BF16), wo_ref[...], preferred_element_type=F32))
    o_ref[...] = x2
    hn_ref[...] = _rms(x2, gn_ref[...], NORM_EPS).astype(BF16)


def _mix_out(o_rets, o_rwkvs, xs, zg, w_ret_o, w_rwkv_o, w_out, g_next):
    n = xs[0].shape[0] + xs[1].shape[0]
    tm = TM_MIX
    tiles_a = o_rets[0].shape[0] // tm
    resident = lambda shape: pl.BlockSpec(shape, lambda i: (0, 0), pipeline_mode=pl.Buffered(1))
    first = lambda i: (jnp.minimum(i, tiles_a - 1), 0)
    second = lambda i: (jnp.maximum(i - tiles_a, 0), 0)
    return pl.pallas_call(
        functools.partial(_mix_out_kernel, tiles_a=tiles_a),
        grid=(n // tm,),
        in_specs=[
            pl.BlockSpec((tm, RET_V), first),
            pl.BlockSpec((tm, RET_V), second),
            pl.BlockSpec((tm, RWKV_W), first),
            pl.BlockSpec((tm, RWKV_W), second),
            pl.BlockSpec((tm, D_MODEL), first),
            pl.BlockSpec((tm, D_MODEL), second),
            pl.BlockSpec((tm, D_MODEL), lambda i: (i, ZGR // D_MODEL)),
            pl.BlockSpec((tm, D_MODEL), lambda i: (i, ZGW // D_MODEL)),
            resident((RET_V, D_MODEL)),
            resident((RWKV_W, D_MODEL)),
            resident((D_MODEL, D_MODEL)),
            pl.BlockSpec((1, D_MODEL), lambda i: (0, 0)),
        ],
        out_specs=[pl.BlockSpec((tm, D_MODEL), lambda i: (i, 0)), pl.BlockSpec((tm, D_MODEL), lambda i: (i, 0))],
        out_shape=[jax.ShapeDtypeStruct((n, D_MODEL), F32), jax.ShapeDtypeStruct((n, D_MODEL), BF16)],
        compiler_params=_cparams(("arbitrary",)),
        name="mix_out",
    )(*o_rets, *o_rwkvs, *xs, zg, zg, w_ret_o, w_rwkv_o, w_out, g_next)


def kernel(x_prompt, x_sample, state_ret, state_rwkv, state_shift, norm_ffn1, ffn1_w_gu, ffn1_w_down, norm_mix, w_in, w_ret_o, rwkv_mu, rwkv_w0, rwkv_w_up, rwkv_a0, rwkv_a_up, rwkv_g_up, rwkv_k_k, rwkv_k_a, rwkv_r_k, rwkv_ln_w, rwkv_ln_b, w_rwkv_o, w_out, norm_ffn2, ffn2_w_gu, ffn2_w_down, norm_final):
    bp, tp, d = x_prompt.shape
    bs, ts, _ = x_sample.shape
    n_p = bp * tp
    n_s = bs * ts
    W = RWKV_W
    row = lambda a: a.reshape(1, -1)

    x1_s, h_s, *w_ffn1 = _ffn(x_sample.reshape(n_s, d), 0, n_s, row(norm_ffn1[0]),
                              (ffn1_w_gu[0], ffn1_w_down[0]), row(norm_mix[0]), final_norm=False, emit_h=True)
    x1_p, h_p = _ffn(x_prompt.reshape(n_p, d), 0, n_p, row(norm_ffn1[0]), w_ffn1, row(norm_mix[0]),
                     final_norm=False, emit_h=True)

    w_in_t = jnp.swapaxes(w_in[0], 0, 1)
    z = _proj((h_p, h_s), w_in_t, 0, TN_PROJ, Z_MAIN_W)
    zl = _proj((h_p, h_s), w_in_t, Z_MAIN_W, LORA_PAD, LORA_PAD)
    zg = _proj((h_p, h_s), w_in_t, Z_MAIN_W + LORA_W, TN_PROJ, 2 * D_MODEL)

    lg = jnp.log(1.0 - 2.0 ** (-5.0 - jnp.arange(RET_HEADS, dtype=F32)))
    cos_p, sin_p = _rope_tables(jnp.arange(tp, dtype=F32))
    cos_s, sin_s = _rope_tables(PAST_LEN + jnp.arange(ts, dtype=F32))
    rep = RET_SAMPLE_ROWS // ts
    oret_p, ret_p = _ret_prompt(z, lg, cos_p, sin_p, bp, tp)
    oret_s, ret_s = _ret_sample(z, lg, jnp.tile(cos_s, (rep, 1)), jnp.tile(sin_s, (rep, 1)),
                                state_ret[0], n_p, bs, ts)

    mu = rwkv_mu[0]
    pad_l = lambda a: jnp.pad(a, [(0, 0)] * (a.ndim - 1) + [(0, LORA_PAD - LORA_W)])
    seg = (jnp.arange(LANES)[:, None] // RWKV_N == jnp.arange(LANES)[None, :] // RWKV_N).astype(BF16)
    zero64 = jnp.zeros((DECAY_LORA, W), F32)
    params = (
        row(mu[:3 * W]), pad_l(row(mu[3 * W:])),
        row(rwkv_w0[0]), jnp.concatenate([rwkv_w_up[0], zero64], axis=0).astype(BF16),
        row(rwkv_a0[0]), jnp.concatenate([zero64, rwkv_a_up[0]], axis=0).astype(BF16),
        jnp.pad(rwkv_g_up[0], ((0, LORA_PAD - LANES - GATE_LORA), (0, 0))).astype(BF16),
        row(rwkv_k_k[0]), row(rwkv_k_a[0]), row(rwkv_r_k[0]), row(rwkv_ln_w[0]), row(rwkv_ln_b[0]),
        seg,
    )
    orw_p, rwkv_p = _rwkv_prompt(z, zl, params, bp, tp)
    sh_s = state_shift[0]
    first_rows = lambda a: jnp.pad(a[:, None, :], ((0, 0), (0, ts - 1), (0, 0))).reshape(n_s, -1)
    orw_s, rwkv_s = _rwkv_sample(z, zl, first_rows(sh_s[:, :3 * W]), first_rows(pad_l(sh_s[:, 3 * W:])),
                                 state_rwkv[0], params, n_p, bs, ts)

    x2, h_ffn2 = _mix_out((oret_p, oret_s), (orw_p.reshape(n_p, W), orw_s), (x1_p, x1_s), zg,
                          w_ret_o[0].astype(BF16), w_rwkv_o[0].astype(BF16), w_out[0].astype(BF16),
                          row(norm_ffn2[0]))
    y_s, *w_ffn2 = _ffn(x2, n_p, n_s, row(norm_ffn2[0]), (ffn2_w_gu[0], ffn2_w_down[0]), row(norm_final),
                        final_norm=True, emit_h=False, h=h_ffn2)
    (y_p,) = _ffn(x2, 0, n_p, row(norm_ffn2[0]), w_ffn2, row(norm_final),
                  final_norm=True, emit_h=False, h=h_ffn2)

    def last_rows(firsts, t):
        rows = [f + t - 1 for f in firsts]
        zr_last = jnp.concatenate([z[r:r + 1, ZS:ZS + 3 * W] for r in rows], axis=0)
        zl_last = jnp.concatenate([zl[r:r + 1, :LORA_W] for r in rows], axis=0)
        return jnp.concatenate([zr_last, zl_last], axis=-1)[None]

    shift_s = jnp.concatenate([lax.slice(z, (n_p + ts - 1, ZS), (n_p + n_s, ZS + 3 * W), (ts, 1)),
                               lax.slice(zl, (n_p + ts - 1, 0), (n_p + n_s, LORA_W), (ts, 1))], axis=-1)[None]
    return (y_p.reshape(bp, tp, d), y_s.reshape(bs, ts, d),
            ret_p[None], _pair_unblock(rwkv_p)[None], last_rows(range(0, n_p, tp), tp),
            ret_s[None], rwkv_s[None], shift_s)
```

```python
import functools

import jax
import jax.numpy as jnp
from jax import lax
from jax.experimental import pallas as pl
from jax.experimental.pallas import tpu as pltpu

F32 = jnp.float32
BF16 = jnp.bfloat16

D_MODEL = 2048
PAST_LEN = 16384
RET_HEADS = 4
RET_DK = 256
RET_DV = 512
RET_CHUNK = 128
ROPE_BASE = 10000.0
RET_QK = RET_HEADS * RET_DK
RET_V = RET_HEADS * RET_DV
RWKV_HEADS = 16
RWKV_N = 64
RWKV_W = RWKV_HEADS * RWKV_N
DECAY_LORA = 64
AAA_LORA = 64
GATE_LORA = 160
LORA_W = DECAY_LORA + AAA_LORA + GATE_LORA
LORA_PAD = 384
D_FF = 5632
NORM_EPS = 1e-6
GN_EPS_RET = 1e-6
GN_EPS_RWKV = 64e-5

ZQ, ZK, ZV, ZG, ZS = 0, 1024, 2048, 4096, 6144
Z_MAIN_W = 9216
ZGR, ZGW = 0, 2048

LANES = 128
SUBLANES = 8
RWKV_PAIRS = RWKV_W // LANES
RWKV_CHUNK = 64
RWKV_PROMPT_GROUP = 2
RWKV_PROMPT_LEAD = 21

V7X_VMEM_BYTES = 64 * 1024 * 1024
VMEM_LIMIT = V7X_VMEM_BYTES - 8 * 1024 * 1024

TM = 512
TF = 512
TF_CAST = 256
TN_PROJ = 1024
TM_MIX = 256


def _rms(x, g, eps):
    return x * lax.rsqrt(jnp.mean(x * x, axis=-1, keepdims=True) + eps) * g


def _dot(a, b):
    return jnp.dot(a.astype(BF16), b.astype(BF16), preferred_element_type=F32)


def _dot_nt(a, b):
    return lax.dot_general(a.astype(BF16), b.astype(BF16), (((1,), (1,)), ((), ())),
                           preferred_element_type=F32)


def _dot_tn(a, b):
    return lax.dot_general(a.astype(BF16), b.astype(BF16), (((0,), (0,)), ((), ())),
                           preferred_element_type=F32)


def _bf16_terms(x, n):
    terms = []
    for _ in range(n):
        t = x.astype(BF16)
        terms.append(t)
        x = x - t.astype(F32)
    return terms


def _cparams(sem):
    return pltpu.CompilerParams(dimension_semantics=sem, vmem_limit_bytes=VMEM_LIMIT)


def _ffn_kernel(*refs, final_norm, emit_h, h_in, cast_w):
    if h_in:
        hin_ref, refs = refs[0], refs[1:]
    x_ref, g_ref, wg_ref, wu_ref, wd_ref, gf_ref = refs[:6]
    outs = list(refs[6:-2])
    h_ref, acc_ref = refs[-2:]
    o_ref = outs.pop(0)
    hn_ref = outs.pop(0) if emit_h else None
    j = pl.program_id(1)

    @pl.when(j == 0)
    def _():
        if not h_in:
            h_ref[...] = _rms(x_ref[...], g_ref[...], NORM_EPS).astype(BF16)
        acc_ref[...] = jnp.zeros_like(acc_ref)

    wg, wu, wd = wg_ref[...], wu_ref[...], wd_ref[...]
    if cast_w:
        wg, wu, wd = wg.astype(BF16), wu.astype(BF16), wd.astype(BF16)
        for w_out_ref, w in zip(outs, (wg, wu, wd)):
            w_out_ref[...] = w
    h = hin_ref[...] if h_in else h_ref[...]
    gate = jnp.dot(h, wg, preferred_element_type=F32)
    up = jnp.dot(h, wu, preferred_element_type=F32)
    act = (gate * jax.nn.sigmoid(gate)) * up
    acc_ref[...] += jnp.dot(act.astype(BF16), wd, preferred_element_type=F32)

    @pl.when(j == pl.num_programs(1) - 1)
    def _():
        y = x_ref[...] + 0.5 * acc_ref[...]
        if final_norm:
            y = _rms(y, gf_ref[...], NORM_EPS)
        if emit_h:
            hn_ref[...] = _rms(y, gf_ref[...], NORM_EPS).astype(BF16)
        o_ref[...] = y


def _ffn(x, row0, rows, g, weights, g_after, final_norm, emit_h, h=None):
    cast_w = len(weights) == 2
    tf = TF_CAST if cast_w else TF
    nj = D_FF // tf
    t0 = row0 // TM
    tile = pl.BlockSpec((TM, D_MODEL), lambda i, j: (t0 + i, 0))
    out_tile = pl.BlockSpec((TM, D_MODEL), lambda i, j: (i, 0))
    h_in = h is not None
    w_specs = [pl.BlockSpec((D_MODEL, tf), lambda i, j: (0, j)),
               pl.BlockSpec((D_MODEL, tf), (lambda i, j: (0, j + nj)) if cast_w else (lambda i, j: (0, j))),
               pl.BlockSpec((tf, D_MODEL), lambda i, j: (j, 0))]
    w_args = (weights[0], weights[0], weights[1]) if cast_w else weights
    out_specs = [out_tile] + ([out_tile] if emit_h else [])
    out_shape = [jax.ShapeDtypeStruct((rows, D_MODEL), F32)] + (
        [jax.ShapeDtypeStruct((rows, D_MODEL), BF16)] if emit_h else [])
    if cast_w:
        out_specs += [pl.BlockSpec((D_MODEL, tf), lambda i, j: (0, j)),
                      pl.BlockSpec((D_MODEL, tf), lambda i, j: (0, j)),
                      pl.BlockSpec((tf, D_MODEL), lambda i, j: (j, 0))]
        out_shape += [jax.ShapeDtypeStruct((D_MODEL, D_FF), BF16), jax.ShapeDtypeStruct((D_MODEL, D_FF), BF16),
                      jax.ShapeDtypeStruct((D_FF, D_MODEL), BF16)]
        assert rows == TM, "each weight tile is written exactly once"
    return pl.pallas_call(
        functools.partial(_ffn_kernel, final_norm=final_norm, emit_h=emit_h, h_in=h_in, cast_w=cast_w),
        grid=(rows // TM, nj),
        in_specs=([tile] if h_in else []) + [tile, pl.BlockSpec((1, D_MODEL), lambda i, j: (0, 0))] + w_specs
        + [pl.BlockSpec((1, D_MODEL), lambda i, j: (0, 0))],
        out_specs=out_specs,
        out_shape=out_shape,
        scratch_shapes=[pltpu.VMEM((TM, D_MODEL), BF16), pltpu.VMEM((TM, D_MODEL), F32)],
        compiler_params=_cparams(("parallel", "arbitrary")),
        name="ffn_cast" if cast_w else "ffn",
    )(*((h,) if h_in else ()), x, g, *w_args, g_after)


def _proj_kernel(ha_ref, hb_ref, wt_ref, o_ref, w_ref, *, tiles_a):
    i = pl.program_id(1)

    @pl.when(i == 0)
    def _():
        w_ref[...] = wt_ref[...].astype(BF16)

    o_ref[...] = _dot_nt(jnp.where(i < tiles_a, ha_ref[...], hb_ref[...]), w_ref[...])


def _proj(hs, w_t, col0, tn, nout):
    tiles_a = hs[0].shape[0] // TM
    n = hs[0].shape[0] + hs[1].shape[0]
    return pl.pallas_call(
        functools.partial(_proj_kernel, tiles_a=tiles_a),
        grid=(nout // tn, n // TM),
        in_specs=[
            pl.BlockSpec((TM, D_MODEL), lambda j, i: (jnp.minimum(i, tiles_a - 1), 0)),
            pl.BlockSpec((TM, D_MODEL), lambda j, i: (jnp.maximum(i - tiles_a, 0), 0)),
            pl.BlockSpec((pl.Element(tn), pl.Element(D_MODEL)),
                         lambda j, i: (pl.multiple_of(col0 + j * tn, SUBLANES), 0)),
        ],
        out_specs=pl.BlockSpec((TM, tn), lambda j, i: (i, j)),
        out_shape=jax.ShapeDtypeStruct((n, nout), F32),
        scratch_shapes=[pltpu.VMEM((tn, D_MODEL), BF16)],
        compiler_params=_cparams(("parallel", "arbitrary")),
        name="proj",
    )(*hs, w_t)


def _pair_swap(x):
    n = x.shape[-1]
    lane = lax.broadcasted_iota(jnp.int32, x.shape, x.ndim - 1)
    prev = pltpu.roll(x, 1, x.ndim - 1)
    nxt = pltpu.roll(x, n - 1, x.ndim - 1)
    return jnp.where((lane & 1) == 1, prev, nxt)


def _rotary(x, cos, sin_signed):
    return x * cos + _pair_swap(x) * sin_signed


def _ret_finish(o, zg):
    o = o * lax.rsqrt(jnp.mean(o * o, axis=-1, keepdims=True) + GN_EPS_RET)
    return o * (zg * jax.nn.sigmoid(zg))


RET_PROMPT_SEQS = 2


def _ret_prompt_kernel(lg_ref, *refs):
    nb = RET_PROMPT_SEQS
    q_refs, k_refs, v_refs, g_refs = (refs[i * nb:(i + 1) * nb] for i in range(4))
    cos_ref, sin_ref, o_ref, s_out_ref, s_ref = refs[4 * nb:]
    c = pl.program_id(1)
    C = RET_CHUNK

    @pl.when(c == 0)
    def _():
        s_ref[...] = jnp.zeros_like(s_ref)

    cos = cos_ref[...]
    sin = sin_ref[...]
    ti = lax.broadcasted_iota(jnp.int32, (C, C), 0)
    tj = lax.broadcasted_iota(jnp.int32, (C, C), 1)
    rel = (ti - tj).astype(F32)
    idx = lax.broadcasted_iota(jnp.int32, (C, 1), 0).astype(F32)
    H = range(RET_HEADS)
    lgs = [lg_ref[h] for h in H]
    finals = []

    def sequence(b):
        qs = [_rotary(q_refs[b][:, h * RET_DK:(h + 1) * RET_DK], cos, sin) for h in H]
        ks = [_rotary(k_refs[b][:, h * RET_DK:(h + 1) * RET_DK], cos, sin) * (RET_DK ** -0.5) for h in H]
        yield
        vs = [v_refs[b][:, h * RET_DV:(h + 1) * RET_DV] for h in H]
        ss = [s_ref[b, h] for h in H]
        scores = [_dot_nt(q, k) * jnp.where(rel >= 0, jnp.exp(lg * jnp.maximum(rel, 0.0)), 0.0)
                  for q, k, lg in zip(qs, ks, lgs)]
        crosses = [_dot(q, s) * jnp.exp(lg * (idx + 1.0)) for q, s, lg in zip(qs, ss, lgs)]
        inners = [_dot(sc, v) for sc, v in zip(scores, vs)]
        s_news = [jnp.exp(lg * C) * s + _dot_tn(k * jnp.exp(lg * (C - 1.0 - idx)), v)
                  for lg, s, k, v in zip(lgs, ss, ks, vs)]
        yield
        for h in H:
            s_ref[b, h] = s_news[h]
            o_ref[b, :, h * RET_DV:(h + 1) * RET_DV] = _ret_finish(
                inners[h] + crosses[h], g_refs[b][:, h * RET_DV:(h + 1) * RET_DV]).astype(BF16)
        finals.append((b, s_news))

    _run_staged([sequence(b) for b in range(nb)], 1)

    @pl.when(c == pl.num_programs(1) - 1)
    def _():
        for b, s_news in finals:
            for h in H:
                s_out_ref[b, h] = s_news[h]


def _ret_prompt(z, lg, cos, sin, batch, seq):
    nc = seq // RET_CHUNK
    C = RET_CHUNK
    nb = RET_PROMPT_SEQS
    cols = lambda width, col0: [pl.BlockSpec((C, width), lambda g, c, b=b: ((g * nb + b) * nc + c, col0 // width))
                                for b in range(nb)]
    return pl.pallas_call(
        _ret_prompt_kernel,
        grid=(batch // nb, nc),
        in_specs=[pl.BlockSpec(memory_space=pltpu.SMEM)]
        + cols(RET_QK, ZQ) + cols(RET_QK, ZK) + cols(RET_V, ZV) + cols(RET_V, ZG)
        + [pl.BlockSpec((C, RET_DK), lambda g, c: (c, 0)), pl.BlockSpec((C, RET_DK), lambda g, c: (c, 0))],
        out_specs=[
            pl.BlockSpec((nb, C, RET_V), lambda g, c: (g, c, 0)),
            pl.BlockSpec((nb, RET_HEADS, RET_DK, RET_DV), lambda g, c: (g, 0, 0, 0)),
        ],
        out_shape=[
            jax.ShapeDtypeStruct((batch, seq, RET_V), BF16),
            jax.ShapeDtypeStruct((batch, RET_HEADS, RET_DK, RET_DV), F32),
        ],
        scratch_shapes=[pltpu.VMEM((nb, RET_HEADS, RET_DK, RET_DV), F32)],
        compiler_params=_cparams(("parallel", "arbitrary")),
        name="ret_prompt",
    )(lg, *([z] * (4 * nb)), cos, sin)


RET_SAMPLE_ROWS = 16


def _ret_sample_kernel(lg_ref, q_ref, k_ref, v_ref, g_ref, cos_ref, sin_ref, s0_ref,
                       o_ref, s_out_ref, *, seq):
    R = RET_SAMPLE_ROWS
    nb = R // seq
    cos = cos_ref[...]
    sin = sin_ref[...]
    ri = lax.broadcasted_iota(jnp.int32, (R, R), 0)
    rj = lax.broadcasted_iota(jnp.int32, (R, R), 1)
    rel = ((ri % seq) - (rj % seq)).astype(F32)
    ok = ((ri // seq) == (rj // seq)) & (rel >= 0)
    row = lax.broadcasted_iota(jnp.int32, (R, 1), 0)
    t = (row % seq).astype(F32)

    for h in range(RET_HEADS):
        lg = lg_ref[h]
        q = _rotary(q_ref[:, h * RET_DK:(h + 1) * RET_DK], cos, sin)
        k = _rotary(k_ref[:, h * RET_DK:(h + 1) * RET_DK], cos, sin) * (RET_DK ** -0.5)
        v = v_ref[:, h * RET_DV:(h + 1) * RET_DV]
        mask = jnp.where(ok, jnp.exp(lg * jnp.maximum(rel, 0.0)), 0.0)
        q_dec = jnp.exp(lg * (t + 1.0))
        kd = k * jnp.exp(lg * (seq - 1.0 - t))
        inner = _dot(_dot_nt(q, k) * mask, v)
        cross = jnp.zeros_like(inner)
        for b in range(nb):
            s = s0_ref[b, h]
            mine = (row // seq) == b
            cross = jnp.where(mine, _dot(q, s), cross)
            s_out_ref[b, h] = jnp.exp(lg * seq) * s + _dot_tn(jnp.where(mine, kd, 0.0), v)
        o_ref[:, h * RET_DV:(h + 1) * RET_DV] = _ret_finish(
            inner + cross * q_dec, g_ref[:, h * RET_DV:(h + 1) * RET_DV]).astype(BF16)


def _ret_sample(z, lg, cos, sin, s0, row0, batch, seq):
    R = RET_SAMPLE_ROWS
    nb = R // seq
    rb0 = row0 // R
    return pl.pallas_call(
        functools.partial(_ret_sample_kernel, seq=seq),
        grid=(batch // nb,),
        in_specs=[
            pl.BlockSpec(memory_space=pltpu.SMEM),
            pl.BlockSpec((R, RET_QK), lambda i: (rb0 + i, ZQ // RET_QK)),
            pl.BlockSpec((R, RET_QK), lambda i: (rb0 + i, ZK // RET_QK)),
            pl.BlockSpec((R, RET_V), lambda i: (rb0 + i, ZV // RET_V)),
            pl.BlockSpec((R, RET_V), lambda i: (rb0 + i, ZG // RET_V)),
            pl.BlockSpec((R, RET_DK), lambda i: (0, 0)),
            pl.BlockSpec((R, RET_DK), lambda i: (0, 0)),
            pl.BlockSpec((nb, RET_HEADS, RET_DK, RET_DV), lambda i: (i, 0, 0, 0)),
        ],
        out_specs=[
            pl.BlockSpec((R, RET_V), lambda i: (i, 0)),
            pl.BlockSpec((nb, RET_HEADS, RET_DK, RET_DV), lambda i: (i, 0, 0, 0)),
        ],
        out_shape=[
            jax.ShapeDtypeStruct((batch * seq, RET_V), BF16),
            jax.ShapeDtypeStruct((batch, RET_HEADS, RET_DK, RET_DV), F32),
        ],
        compiler_params=_cparams(("parallel",)),
        name="ret_sample",
    )(lg, z, z, z, z, cos, sin, s0)


def _rope_tables(pos):
    half = RET_DK // 2
    inv = 1.0 / (ROPE_BASE ** jnp.linspace(0.0, 1.0, half, dtype=F32))
    ang = pos[:, None] * inv[None, :]
    cos = jnp.repeat(jnp.cos(ang), 2, axis=-1)
    sin = jnp.sin(ang)
    sin_signed = jnp.stack([-sin, sin], axis=-1).reshape(pos.shape[0], RET_DK)
    return cos, sin_signed


def _segsum(x, seg_ref):
    rows = x.shape[0]
    n = rows * RWKV_PAIRS
    xs = jnp.concatenate([x[:, p * LANES:(p + 1) * LANES] for p in range(RWKV_PAIRS)], axis=0)
    ss = jnp.dot(jnp.concatenate(_bf16_terms(xs, 2), axis=0), seg_ref[...], preferred_element_type=F32)
    ss = ss[0:n] + ss[n:2 * n]
    return jnp.concatenate([ss[p * rows:(p + 1) * rows] for p in range(RWKV_PAIRS)], axis=1)


def _run_staged(gens, lead):
    out = [None] * len(gens)
    live = [True] * len(gens)

    def step(i):
        try:
            next(gens[i])
        except StopIteration as stop:
            out[i] = stop.value
            live[i] = False

    for _ in range(lead):
        if live[0]:
            step(0)
    while any(live):
        for i in range(len(gens)):
            if live[i]:
                step(i)
    return out


def _each(fn, *lists):
    out = []
    for args in zip(*lists):
        out.append(fn(*args))
        yield
    return out


def _rwkv_chunk(zr, zl, prev_r, prev_l, prm, n_seq, **state_fns):
    front = yield from _rwkv_front(zr, zl, prev_r, prev_l, prm, n_seq)
    return (yield from _rwkv_back(*front, prm, n_seq, **state_fns))


def _rwkv_front(zr, zl, prev_r, prev_l, prm, n_seq):
    (mur_ref, mul_ref, w0_ref, wup_ref, a0_ref, aup_ref, gup_ref,
     kk_ref, ka_ref, rk_ref, lnw_ref, lnb_ref, seg_ref) = prm
    C = RWKV_CHUNK
    W = RWKV_W
    R = zr.shape[0]
    TS = C // n_seq
    row = lax.broadcasted_iota(jnp.int32, (R, 1), 0)
    first = (row % TS) == 0

    def mix(z, prev, mu_ref):
        z_prev = jnp.where(first, prev, pltpu.roll(z, 1, 0))
        return z + (z_prev - z) * mu_ref[...]

    xs_r = mix(zr[:, 0:W], prev_r[:, 0:W], mur_ref.at[:, 0:W])
    yield
    xs_k = mix(zr[:, W:2 * W], prev_r[:, W:2 * W], mur_ref.at[:, W:2 * W])
    yield
    xs_v = mix(zr[:, 2 * W:3 * W], prev_r[:, 2 * W:3 * W], mur_ref.at[:, 2 * W:3 * W])
    yield
    xl = mix(zl, prev_l, mul_ref)
    r, k, v = xs_r, xs_k, xs_v
    x_wa = xl[:, 0:LANES]
    x_g = xl[:, LANES:LORA_PAD]

    y_w = w0_ref[...] + _dot(jnp.tanh(x_wa), wup_ref[...])
    yield
    w_log = -(jnp.maximum(-y_w, 0.0) + jnp.log1p(jnp.exp(-jnp.abs(y_w)))) - 0.5
    lw = -jnp.exp(w_log)
    yield
    a_rate = jax.nn.sigmoid(a0_ref[...] + _dot(x_wa, aup_ref[...]))
    yield
    g = _dot(jax.nn.sigmoid(x_g), gup_ref[...])
    yield

    kk = k * kk_ref[...]
    k = k * (1.0 + (a_rate - 1.0) * ka_ref[...])
    yield
    sums = _segsum(jnp.concatenate([kk * kk, r * k * rk_ref[...]], axis=0), seg_ref)
    yield
    kk = kk * lax.rsqrt(jnp.maximum(sums[0:R], 1e-24))
    bonus_rk = sums[R:2 * R]
    a = -kk
    b = kk * a_rate
    yield

    ti = lax.broadcasted_iota(jnp.int32, (2 * R, R), 0)
    tj = lax.broadcasted_iota(jnp.int32, (2 * R, R), 1)
    same = ((ti % R) // TS) == (tj // TS)
    sel = (same & ((ti >= R) | (tj <= ti))).astype(BF16)
    sums_lw = sum(jnp.dot(sel, t, preferred_element_type=F32) for t in _bf16_terms(lw, 3))
    yield
    cum = sums_lw[0:R]
    tot = sums_lw[R:2 * R]
    e_neg = jnp.exp(-cum)
    yield
    a_t = (a * jnp.exp(cum - lw)).astype(BF16)
    yield
    b_t = (b * e_neg).astype(BF16)
    yield
    k_t = (k * e_neg).astype(BF16)
    yield
    r_t = r * jnp.exp(cum)
    yield
    e_tail = jnp.exp(tot - cum)
    yield
    b_g = (b * e_tail).astype(BF16)
    yield
    k_g = (k * e_tail).astype(BF16)
    dec = jnp.exp(tot)
    yield
    return (a_t, b_t, k_t, b_g, k_g), (r_t, v, dec, g, bonus_rk)


def _rwkv_back(mm_operands, f32_arrays, prm, n_seq, read_state=None, update_state=None, fold_state=None):
    a_t, b_t, k_t, b_g, k_g = mm_operands
    r_t, v, dec, g, bonus_rk = f32_arrays
    lnw_ref, lnb_ref, seg_ref = prm[10:13]
    C = RWKV_CHUNK
    R = r_t.shape[0]
    TS = C // n_seq

    lane = lax.broadcasted_iota(jnp.int32, (1, LANES), 1)
    lo = lane < RWKV_N

    def split(x):
        return jnp.concatenate([jnp.where(lo, x, 0.0), jnp.where(lo, 0.0, x)], axis=0)

    row4 = lax.broadcasted_iota(jnp.int32, (C, 4 * C), 0)
    col4 = lax.broadcasted_iota(jnp.int32, (C, 4 * C), 1) % C
    same4 = (row4 // TS) == (col4 // TS)
    strict = same4 & (col4 < row4)
    incl = same4 & (col4 <= row4)
    row2 = lax.broadcasted_iota(jnp.int32, (C, 2 * C), 0)
    col2 = lax.broadcasted_iota(jnp.int32, (C, 2 * C), 1)
    eye2 = ((col2 % C) == row2).astype(F32)
    left = col2 < C

    tiles = [(slice(i * C, (i + 1) * C), slice(p * LANES, (p + 1) * LANES))
             for i in range(R // C) for p in range(RWKV_PAIRS)]
    v2s = [split(v[rs, sl]) for rs, sl in tiles]
    ars = [jnp.concatenate([a_t[rs, sl], r_t[rs, sl]], axis=0) for rs, sl in tiles]
    if fold_state is None:
        wys = read_state(ars)
    gms = yield from _each(
        lambda ar, t: _dot_nt(ar, jnp.concatenate([split(b_t[t[0], t[1]]), split(k_t[t[0], t[1]])], axis=0)),
        ars, tiles)
    gas = [jnp.where(strict, gm[0:C], 0.0) for gm in gms]
    grs = [jnp.where(incl, gm[C:2 * C], 0.0) for gm in gms]

    def blockdiag(x):
        return jnp.concatenate([jnp.where(left, x, 0.0), jnp.where(left, 0.0, x)], axis=0)

    ts = [eye2 + ga[:, 0:2 * C] for ga in gas]
    n_sq = TS.bit_length() - 2
    if n_sq >= 1:
        pws = yield from _each(lambda ga: _dot(ga[:, 0:2 * C], blockdiag(ga[:, 0:2 * C])), gas)
        for _ in range(n_sq - 1):
            tps = yield from _each(lambda t, pw: _dot(jnp.concatenate([t, pw], axis=0), blockdiag(pw)), ts, pws)
            ts = [t + tp[0:C] for t, tp in zip(ts, tps)]
            pws = [tp[C:2 * C] for tp in tps]
        ts = yield from _each(lambda t, pw: t + _dot(t, blockdiag(pw)), ts, pws)
    wis = yield from _each(lambda ga, v2: _dot(ga[:, 2 * C:4 * C], v2), gas, v2s)
    bks = [jnp.concatenate([b_g[rs, sl], k_g[rs, sl]], axis=0) for rs, sl in tiles]
    if fold_state is None:
        us = [_dot(t, split(wi + wy[0:C])) for t, wi, wy in zip(ts, wis, wys)]
        ys = [wy[C:2 * C] + _dot(gr, jnp.concatenate([split(u), v2], axis=0))
              for wy, gr, u, v2 in zip(wys, grs, us, v2s)]
        update_state([jnp.concatenate([u, v[rs, sl]], axis=0) for u, (rs, sl) in zip(us, tiles)], bks, dec)
    else:
        zc = jnp.zeros((C, LANES), F32)
        aus = yield from _each(
            lambda t, wi, tl: _dot(t, jnp.concatenate([split(a_t[tl[0], tl[1]]), split(wi)], axis=1)),
            ts, wis, tiles)
        rys = yield from _each(
            lambda gr, au, v2: _dot(gr, jnp.concatenate(
                [jnp.concatenate([split(au[:, 0:LANES]), split(au[:, LANES:2 * LANES])], axis=1),
                 jnp.concatenate([jnp.concatenate([zc, zc], axis=0), v2], axis=1)], axis=0)),
            grs, aus, v2s)
        mns = yield from _each(
            lambda bk, au, tl: _dot_tn(bk, jnp.concatenate(
                [au, jnp.concatenate([zc, v[tl[0], tl[1]]], axis=1)], axis=0)),
            bks, aus, tiles)
        ys = fold_state([r_t[rs, sl] + ry[:, 0:LANES] for ry, (rs, sl) in zip(rys, tiles)],
                        [ry[:, LANES:2 * LANES] for ry in rys], mns, dec)

    yield
    y = jnp.concatenate([jnp.concatenate(ys[i * RWKV_PAIRS:(i + 1) * RWKV_PAIRS], axis=1)
                         for i in range(R // C)], axis=0)
    mean = _segsum(y, seg_ref) * (1.0 / RWKV_N)
    d = y - mean
    var = _segsum(d * d, seg_ref) * (1.0 / RWKV_N)
    yn = d * lax.rsqrt(var + GN_EPS_RWKV) * lnw_ref[...] + lnb_ref[...]
    return ((yn + bonus_rk * v) * g).astype(BF16)


def _rwkv_prompt_kernel(*refs, batch):
    zr_refs = refs[:batch]
    zl_refs = refs[batch:2 * batch]
    prm = refs[2 * batch:2 * batch + 13]
    o_ref, s_out_ref, prevr_ref, prevl_ref, h_ref = refs[2 * batch + 13:]
    C = RWKV_CHUNK
    c = pl.program_id(0)

    @pl.when(c == 0)
    def _():
        prevr_ref[...] = jnp.zeros_like(prevr_ref)
        prevl_ref[...] = jnp.zeros_like(prevl_ref)
        h_ref[...] = jnp.zeros_like(h_ref)

    ki = lax.broadcasted_iota(jnp.int32, (LANES, LANES), 0)
    vi = lax.broadcasted_iota(jnp.int32, (LANES, LANES), 1)
    diag_blocks = (ki < RWKV_N) == (vi < RWKV_N)

    def group(i0):
        seqs = range(i0, i0 + RWKV_PROMPT_GROUP)
        tiles = [(i, p) for i in seqs for p in range(RWKV_PAIRS)]
        zr = jnp.concatenate([zr_refs[i][...] for i in seqs], axis=0)
        zl = jnp.concatenate([zl_refs[i][...] for i in seqs], axis=0)
        rows = lambda ref: jnp.concatenate(
            [jnp.broadcast_to(ref[i:i + 1, :], (C, ref.shape[1])) for i in seqs], axis=0)

        def fold_state(r_hats, y_inds, mns, dec):
            dec_cols = [jnp.broadcast_to(dec[(i - i0) * C:(i - i0) * C + 1, p * LANES:(p + 1) * LANES],
                                         (LANES, LANES)).T for i, p in tiles]
            hs = [h_ref[i, p] for i, p in tiles]
            res = [_dot(jnp.concatenate([rh, jnp.where(diag_blocks, mn[:, 0:LANES], 0.0)], axis=0), h)
                   for rh, mn, h in zip(r_hats, mns, hs)]
            for (i, p), h, dc, rs, mn in zip(tiles, hs, dec_cols, res, mns):
                h_ref[i, p] = h * dc + rs[C:C + LANES] + jnp.where(diag_blocks, mn[:, LANES:2 * LANES], 0.0)
            return [rs[0:C] + yi for rs, yi in zip(res, y_inds)]

        y = yield from _rwkv_chunk(zr, zl, rows(prevr_ref), rows(prevl_ref), prm, 1, fold_state=fold_state)
        for n, i in enumerate(seqs):
            o_ref[i] = y[n * C:(n + 1) * C]
            prevr_ref[i:i + 1, :] = zr[(n + 1) * C - 1:(n + 1) * C, :]
            prevl_ref[i:i + 1, :] = zl[(n + 1) * C - 1:(n + 1) * C, :]

    _run_staged([group(i0) for i0 in range(0, batch, RWKV_PROMPT_GROUP)], RWKV_PROMPT_LEAD)

    @pl.when(c == pl.num_programs(0) - 1)
    def _():
        for i in range(batch):
            for p in range(RWKV_PAIRS):
                s_out_ref[i, p] = h_ref[i, p].T


def _rwkv_sample_kernel(zr_ref, zl_ref, shr_ref, shl_ref, s0_ref, *rest, n_seq):
    prm = rest[:13]
    o_ref, s_out_ref = rest[13:]
    C = RWKV_CHUNK
    N = RWKV_N
    TS = C // n_seq
    tokseq = (lax.broadcasted_iota(jnp.int32, (1, LANES), 1) % C) // TS

    def pick(x, j):
        return jnp.where(tokseq == j, x, 0.0)

    def stacked(h):
        return s0_ref[:, h].reshape(n_seq * N, N)

    def read_state(ars):
        res = [[_dot_nt(stacked(2 * p + e), ar[:, e * N:(e + 1) * N]) for e in range(2)]
               for p, ar in enumerate(ars)]
        out = []
        for rp in res:
            halves = []
            for re in rp:
                acc = pick(re[0:N], 0)
                for j in range(1, n_seq):
                    acc = acc + pick(re[j * N:(j + 1) * N], j)
                halves.append(acc)
            out.append(jnp.concatenate(halves, axis=0).T)
        return out

    def update_state(uvs, bks, dec):
        uvts = [uv.T for uv in uvs]
        upds = [[_dot(jnp.concatenate([pick(uvt[e * N:(e + 1) * N], j) for j in range(n_seq)], axis=0),
                      bk[:, e * N:(e + 1) * N]) for e in range(2)]
                for uvt, bk in zip(uvts, bks)]
        for p, up in enumerate(upds):
            for e, upd in enumerate(up):
                h = 2 * p + e
                dec_h = jnp.concatenate(
                    [jnp.broadcast_to(dec[j * TS:j * TS + 1, h * N:(h + 1) * N], (N, N)) for j in range(n_seq)],
                    axis=0)
                s_out_ref[:, h] = (stacked(h) * dec_h + upd).reshape(n_seq, N, N)

    (o_ref[...],) = _run_staged([_rwkv_chunk(zr_ref[...], zl_ref[...], shr_ref[...], shl_ref[...], prm, n_seq,
                                             read_state=read_state, update_state=update_state)], 0)


def _rwkv_param_specs():
    W = RWKV_W
    shapes = [(1, 3 * W), (1, LORA_PAD), (1, W), (LANES, W), (1, W), (LANES, W), (LORA_PAD - LANES, W),
              (1, W), (1, W), (1, W), (1, W), (1, W), (LANES, LANES)]
    return [pl.BlockSpec(s, lambda *_: (0, 0)) for s in shapes]


def _rwkv_prompt(z, zl, params, batch, seq):
    C = RWKV_CHUNK
    W = RWKV_W
    nc = seq // C
    zr_spec = lambda i: pl.BlockSpec((C, 3 * W), lambda c: (i * nc + c, ZS // (3 * W)))
    zl_spec = lambda i: pl.BlockSpec((C, LORA_PAD), lambda c: (i * nc + c, 0))
    return pl.pallas_call(
        functools.partial(_rwkv_prompt_kernel, batch=batch),
        grid=(nc,),
        in_specs=[zr_spec(i) for i in range(batch)] + [zl_spec(i) for i in range(batch)]
        + _rwkv_param_specs(),
        out_specs=[
            pl.BlockSpec((batch, C, W), lambda c: (0, c, 0)),
            pl.BlockSpec((batch, RWKV_PAIRS, LANES, LANES), lambda c: (0, 0, 0, 0)),
        ],
        out_shape=[
            jax.ShapeDtypeStruct((batch, seq, W), BF16),
            jax.ShapeDtypeStruct((batch, RWKV_PAIRS, LANES, LANES), F32),
        ],
        scratch_shapes=[pltpu.VMEM((batch, 3 * W), F32), pltpu.VMEM((batch, LORA_PAD), F32),
                        pltpu.VMEM((batch, RWKV_PAIRS, LANES, LANES), F32)],
        compiler_params=_cparams(("arbitrary",)),
        name="rwkv_prompt",
    )(*([z] * batch), *([zl] * batch), *params)


def _rwkv_sample(z, zl, sh_r, sh_l, s0, params, row0, batch, seq):
    C = RWKV_CHUNK
    W = RWKV_W
    n_seq = C // seq
    rb0 = row0 // C
    return pl.pallas_call(
        functools.partial(_rwkv_sample_kernel, n_seq=n_seq),
        grid=(batch // n_seq,),
        in_specs=[
            pl.BlockSpec((C, 3 * W), lambda i: (rb0 + i, ZS // (3 * W))),
            pl.BlockSpec((C, LORA_PAD), lambda i: (rb0 + i, 0)),
            pl.BlockSpec((C, 3 * W), lambda i: (i, 0)),
            pl.BlockSpec((C, LORA_PAD), lambda i: (i, 0)),
            pl.BlockSpec((n_seq, RWKV_HEADS, RWKV_N, RWKV_N), lambda i: (i, 0, 0, 0)),
        ] + _rwkv_param_specs(),
        out_specs=[
            pl.BlockSpec((C, W), lambda i: (i, 0)),
            pl.BlockSpec((n_seq, RWKV_HEADS, RWKV_N, RWKV_N), lambda i: (i, 0, 0, 0)),
        ],
        out_shape=[
            jax.ShapeDtypeStruct((batch * seq, W), BF16),
            jax.ShapeDtypeStruct((batch, RWKV_HEADS, RWKV_N, RWKV_N), F32),
        ],
        compiler_params=_cparams(("parallel",)),
        name="rwkv_sample",
    )(z, zl, sh_r, sh_l, s0, *params)


def _pair_unblock(s):
    b = s.shape[0]
    lo = s[:, :, :RWKV_N, :RWKV_N]
    hi = s[:, :, RWKV_N:, RWKV_N:]
    return jnp.stack([lo, hi], axis=2).reshape(b, RWKV_HEADS, RWKV_N, RWKV_N)


def _mix_out_kernel(oret_a_ref, oret_b_ref, orwkv_a_ref, orwkv_b_ref, xa_ref, xb_ref, gr_ref, gw_ref,
                    wr_ref, ww_ref, wo_ref, gn_ref, o_ref, hn_ref, *, tiles_a):
    first = pl.program_id(0) < tiles_a
    o_ret = jnp.where(first, oret_a_ref[...], oret_b_ref[...])
    o_rwkv = jnp.where(first, orwkv_a_ref[...], orwkv_b_ref[...])
    a = jnp.dot(o_ret, wr_ref[...], preferred_element_type=F32)
    b = jnp.dot(o_rwkv, ww_ref[...], preferred_element_type=F32)
    merged = jax.nn.sigmoid(gr_ref[...]) * a + jax.nn.sigmoid(gw_ref[...]) * b
    x2 = (jnp.where(first, xa_ref[...], xb_ref[...])
          + jnp.dot(merged.astype(BF16), wo_ref[...], preferred_element_type=F32))
    o_ref[...] = x2
    hn_ref[...] = _rms(x2, gn_ref[...], NORM_EPS).astype(BF16)


def _mix_out(o_rets, o_rwkvs, xs, zg, w_ret_o, w_rwkv_o, w_out, g_next):
    n = xs[0].shape[0] + xs[1].shape[0]
    tm = TM_MIX
    tiles_a = o_rets[0].shape[0] // tm
    resident = lambda shape: pl.BlockSpec(shape, lambda i: (0, 0), pipeline_mode=pl.Buffered(1))
    first = lambda i: (jnp.minimum(i, tiles_a - 1), 0)
    second = lambda i: (jnp.maximum(i - tiles_a, 0), 0)
    return pl.pallas_call(
        functools.partial(_mix_out_kernel, tiles_a=tiles_a),
        grid=(n // tm,),
        in_specs=[
            pl.BlockSpec((tm, RET_V), first),
            pl.BlockSpec((tm, RET_V), second),
            pl.BlockSpec((tm, RWKV_W), first),
            pl.BlockSpec((tm, RWKV_W), second),
            pl.BlockSpec((tm, D_MODEL), first),
            pl.BlockSpec((tm, D_MODEL), second),
            pl.BlockSpec((tm, D_MODEL), lambda i: (i, ZGR // D_MODEL)),
            pl.BlockSpec((tm, D_MODEL), lambda i: (i, ZGW // D_MODEL)),
            resident((RET_V, D_MODEL)),
            resident((RWKV_W, D_MODEL)),
            resident((D_MODEL, D_MODEL)),
            pl.BlockSpec((1, D_MODEL), lambda i: (0, 0)),
        ],
        out_specs=[pl.BlockSpec((tm, D_MODEL), lambda i: (i, 0)), pl.BlockSpec((tm, D_MODEL), lambda i: (i, 0))],
        out_shape=[jax.ShapeDtypeStruct((n, D_MODEL), F32), jax.ShapeDtypeStruct((n, D_MODEL), BF16)],
        compiler_params=_cparams(("arbitrary",)),
        name="mix_out",
    )(*o_rets, *o_rwkvs, *xs, zg, zg, w_ret_o, w_rwkv_o, w_out, g_next)


def kernel(x_prompt, x_sample, state_ret, state_rwkv, state_shift, norm_ffn1, ffn1_w_gu, ffn1_w_down, norm_mix, w_in, w_ret_o, rwkv_mu, rwkv_w0, rwkv_w_up, rwkv_a0, rwkv_a_up, rwkv_g_up, rwkv_k_k, rwkv_k_a, rwkv_r_k, rwkv_ln_w, rwkv_ln_b, w_rwkv_o, w_out, norm_ffn2, ffn2_w_gu, ffn2_w_down, norm_final):
    bp, tp, d = x_prompt.shape
    bs, ts, _ = x_sample.shape
    n_p = bp * tp
    n_s = bs * ts
    W = RWKV_W
    row = lambda a: a.reshape(1, -1)
    assert d == D_MODEL and n_p % TM == 0 and n_s == TM and n_p % TM_MIX == 0
    assert tp % RET_CHUNK == 0 and tp % RWKV_CHUNK == 0 and bp % RWKV_PROMPT_GROUP == 0 and bp % RET_PROMPT_SEQS == 0
    assert RWKV_CHUNK % ts == 0 and n_s % RWKV_CHUNK == 0 and RET_SAMPLE_ROWS % ts == 0 and n_s % RET_SAMPLE_ROWS == 0

    x1_s, h_s, *w_ffn1 = _ffn(x_sample.reshape(n_s, d), 0, n_s, row(norm_ffn1[0]),
                              (ffn1_w_gu[0], ffn1_w_down[0]), row(norm_mix[0]), final_norm=False, emit_h=True)
    x1_p, h_p = _ffn(x_prompt.reshape(n_p, d), 0, n_p, row(norm_ffn1[0]), w_ffn1, row(norm_mix[0]),
                     final_norm=False, emit_h=True)

    w_in_t = jnp.swapaxes(w_in[0], 0, 1)
    z = _proj((h_p, h_s), w_in_t, 0, TN_PROJ, Z_MAIN_W)
    zl = _proj((h_p, h_s), w_in_t, Z_MAIN_W, LORA_PAD, LORA_PAD)
    zg = _proj((h_p, h_s), w_in_t, Z_MAIN_W + LORA_W, TN_PROJ, 2 * D_MODEL)

    lg = jnp.log(1.0 - 2.0 ** (-5.0 - jnp.arange(RET_HEADS, dtype=F32)))
    cos_p, sin_p = _rope_tables(jnp.arange(tp, dtype=F32))
    cos_s, sin_s = _rope_tables(PAST_LEN + jnp.arange(ts, dtype=F32))
    rep = RET_SAMPLE_ROWS // ts
    oret_p, ret_p = _ret_prompt(z, lg, cos_p, sin_p, bp, tp)
    oret_s, ret_s = _ret_sample(z, lg, jnp.tile(cos_s, (rep, 1)), jnp.tile(sin_s, (rep, 1)),
                                state_ret[0], n_p, bs, ts)

    mu = rwkv_mu[0]
    pad_l = lambda a: jnp.pad(a, [(0, 0)] * (a.ndim - 1) + [(0, LORA_PAD - LORA_W)])
    seg = (jnp.arange(LANES)[:, None] // RWKV_N == jnp.arange(LANES)[None, :] // RWKV_N).astype(BF16)
    zero64 = jnp.zeros((DECAY_LORA, W), F32)
    params = (
        row(mu[:3 * W]), pad_l(row(mu[3 * W:])),
        row(rwkv_w0[0]), jnp.concatenate([rwkv_w_up[0], zero64], axis=0).astype(BF16),
        row(rwkv_a0[0]), jnp.concatenate([zero64, rwkv_a_up[0]], axis=0).astype(BF16),
        jnp.pad(rwkv_g_up[0], ((0, LORA_PAD - LANES - GATE_LORA), (0, 0))).astype(BF16),
        row(rwkv_k_k[0]), row(rwkv_k_a[0]), row(rwkv_r_k[0]), row(rwkv_ln_w[0]), row(rwkv_ln_b[0]),
        seg,
    )
    orw_p, rwkv_p = _rwkv_prompt(z, zl, params, bp, tp)
    sh_s = state_shift[0]
    first_rows = lambda a: jnp.pad(a[:, None, :], ((0, 0), (0, ts - 1), (0, 0))).reshape(n_s, -1)
    orw_s, rwkv_s = _rwkv_sample(z, zl, first_rows(sh_s[:, :3 * W]), first_rows(pad_l(sh_s[:, 3 * W:])),
                                 state_rwkv[0], params, n_p, bs, ts)

    x2, h_ffn2 = _mix_out((oret_p.reshape(n_p, RET_V), oret_s), (orw_p.reshape(n_p, W), orw_s), (x1_p, x1_s), zg,
                          w_ret_o[0].astype(BF16), w_rwkv_o[0].astype(BF16), w_out[0].astype(BF16),
                          row(norm_ffn2[0]))
    y_s, *w_ffn2 = _ffn(x2, n_p, n_s, row(norm_ffn2[0]), (ffn2_w_gu[0], ffn2_w_down[0]), row(norm_final),
                        final_norm=True, emit_h=False, h=h_ffn2)
    (y_p,) = _ffn(x2, 0, n_p, row(norm_ffn2[0]), w_ffn2, row(norm_final),
                  final_norm=True, emit_h=False, h=h_ffn2)

    def last_rows(firsts, t):
        rows = [f + t - 1 for f in firsts]
        zr_last = jnp.concatenate([z[r:r + 1, ZS:ZS + 3 * W] for r in rows], axis=0)
        zl_last = jnp.concatenate([zl[r:r + 1, :LORA_W] for r in rows], axis=0)
        return jnp.concatenate([zr_last, zl_last], axis=-1)[None]

    shift_s = jnp.concatenate([lax.slice(z, (n_p + ts - 1, ZS), (n_p + n_s, ZS + 3 * W), (ts, 1)),
                               lax.slice(zl, (n_p + ts - 1, 0), (n_p + n_s, LORA_W), (ts, 1))], axis=-1)[None]
    return (y_p.reshape(bp, tp, d), y_s.reshape(bs, ts, d),
            ret_p[None], _pair_unblock(rwkv_p)[None], last_rows(range(0, n_p, tp), tp),
            ret_s[None], rwkv_s[None], shift_s)
```

```python
import functools

import jax
import jax.numpy as jnp
from jax import lax
from jax.experimental import pallas as pl
from jax.experimental.pallas import tpu as pltpu

F32 = jnp.float32
BF16 = jnp.bfloat16

D_MODEL = 2048
PAST_LEN = 16384
RET_HEADS = 4
RET_DK = 256
RET_DV = 512
RET_CHUNK = 128
ROPE_BASE = 10000.0
RET_QK = RET_HEADS * RET_DK
RET_V = RET_HEADS * RET_DV
RWKV_HEADS = 16
RWKV_N = 64
RWKV_W = RWKV_HEADS * RWKV_N
DECAY_LORA = 64
AAA_LORA = 64
GATE_LORA = 160
LORA_W = DECAY_LORA + AAA_LORA + GATE_LORA
LORA_PAD = 384
D_FF = 5632
NORM_EPS = 1e-6
GN_EPS_RET = 1e-6
GN_EPS_RWKV = 64e-5

ZQ, ZK, ZV, ZG, ZS = 0, 1024, 2048, 4096, 6144
Z_MAIN_W = 9216
ZGR, ZGW = 0, 2048

LANES = 128
SUBLANES = 8
RWKV_PAIRS = RWKV_W // LANES
RWKV_CHUNK = 64
RWKV_PROMPT_GROUP = 2
RWKV_PROMPT_LEAD = 21

V7X_VMEM_BYTES = 64 * 1024 * 1024
VMEM_LIMIT = V7X_VMEM_BYTES - 8 * 1024 * 1024

TM = 512
TF = 512
TF_CAST = 256
TN_PROJ = 1024
TN_PROJ_MAIN = 1536
TM_MIX = 256


def _rms(x, g, eps):
    return x * lax.rsqrt(jnp.mean(x * x, axis=-1, keepdims=True) + eps) * g


def _dot(a, b):
    return jnp.dot(a.astype(BF16), b.astype(BF16), preferred_element_type=F32)


def _dot_nt(a, b):
    return lax.dot_general(a.astype(BF16), b.astype(BF16), (((1,), (1,)), ((), ())),
                           preferred_element_type=F32)


def _dot_tn(a, b):
    return lax.dot_general(a.astype(BF16), b.astype(BF16), (((0,), (0,)), ((), ())),
                           preferred_element_type=F32)


def _bf16_terms(x, n):
    terms = []
    for _ in range(n):
        t = x.astype(BF16)
        terms.append(t)
        x = x - t.astype(F32)
    return terms


def _cparams(sem):
    return pltpu.CompilerParams(dimension_semantics=sem, vmem_limit_bytes=VMEM_LIMIT)


def _ffn_kernel(*refs, final_norm, emit_h, h_in, cast_w):
    if h_in:
        hin_ref, refs = refs[0], refs[1:]
    x_ref, g_ref, wg_ref, wu_ref, wd_ref, gf_ref = refs[:6]
    outs = list(refs[6:-2])
    h_ref, acc_ref = refs[-2:]
    o_ref = outs.pop(0)
    hn_ref = outs.pop(0) if emit_h else None
    j = pl.program_id(1)

    @pl.when(j == 0)
    def _():
        if not h_in:
            h_ref[...] = _rms(x_ref[...], g_ref[...], NORM_EPS).astype(BF16)
        acc_ref[...] = jnp.zeros_like(acc_ref)

    wg, wu, wd = wg_ref[...], wu_ref[...], wd_ref[...]
    if cast_w:
        wg, wu, wd = wg.astype(BF16), wu.astype(BF16), wd.astype(BF16)
        for w_out_ref, w in zip(outs, (wg, wu, wd)):
            w_out_ref[...] = w
    h = hin_ref[...] if h_in else h_ref[...]
    half = wg.shape[1] // 2
    cols = (slice(0, half), slice(half, 2 * half))
    gus = [(jnp.dot(h, wg[:, c], preferred_element_type=F32), jnp.dot(h, wu[:, c], preferred_element_type=F32))
           for c in cols]
    downs = [jnp.dot(((g * jax.nn.sigmoid(g)) * u).astype(BF16), wd[c, :], preferred_element_type=F32)
             for (g, u), c in zip(gus, cols)]
    acc_ref[...] += downs[0] + downs[1]

    @pl.when(j == pl.num_programs(1) - 1)
    def _():
        y = x_ref[...] + 0.5 * acc_ref[...]
        if final_norm:
            y = _rms(y, gf_ref[...], NORM_EPS)
        if emit_h:
            hn_ref[...] = _rms(y, gf_ref[...], NORM_EPS).astype(BF16)
        o_ref[...] = y


def _ffn(x, row0, rows, g, weights, g_after, final_norm, emit_h, h=None):
    cast_w = len(weights) == 2
    tf = TF_CAST if cast_w else TF
    nj = D_FF // tf
    t0 = row0 // TM
    tile = pl.BlockSpec((TM, D_MODEL), lambda i, j: (t0 + i, 0))
    out_tile = pl.BlockSpec((TM, D_MODEL), lambda i, j: (i, 0))
    h_in = h is not None
    w_specs = [pl.BlockSpec((D_MODEL, tf), lambda i, j: (0, j)),
               pl.BlockSpec((D_MODEL, tf), (lambda i, j: (0, j + nj)) if cast_w else (lambda i, j: (0, j))),
               pl.BlockSpec((tf, D_MODEL), lambda i, j: (j, 0))]
    w_args = (weights[0], weights[0], weights[1]) if cast_w else weights
    out_specs = [out_tile] + ([out_tile] if emit_h else [])
    out_shape = [jax.ShapeDtypeStruct((rows, D_MODEL), F32)] + (
        [jax.ShapeDtypeStruct((rows, D_MODEL), BF16)] if emit_h else [])
    if cast_w:
        out_specs += [pl.BlockSpec((D_MODEL, tf), lambda i, j: (0, j)),
                      pl.BlockSpec((D_MODEL, tf), lambda i, j: (0, j)),
                      pl.BlockSpec((tf, D_MODEL), lambda i, j: (j, 0))]
        out_shape += [jax.ShapeDtypeStruct((D_MODEL, D_FF), BF16), jax.ShapeDtypeStruct((D_MODEL, D_FF), BF16),
                      jax.ShapeDtypeStruct((D_FF, D_MODEL), BF16)]
        assert rows == TM, "each weight tile is written exactly once"
    return pl.pallas_call(
        functools.partial(_ffn_kernel, final_norm=final_norm, emit_h=emit_h, h_in=h_in, cast_w=cast_w),
        grid=(rows // TM, nj),
        in_specs=([tile] if h_in else []) + [tile, pl.BlockSpec((1, D_MODEL), lambda i, j: (0, 0))] + w_specs
        + [pl.BlockSpec((1, D_MODEL), lambda i, j: (0, 0))],
        out_specs=out_specs,
        out_shape=out_shape,
        scratch_shapes=[pltpu.VMEM((TM, D_MODEL), BF16), pltpu.VMEM((TM, D_MODEL), F32)],
        compiler_params=_cparams(("parallel", "arbitrary")),
        name="ffn_cast" if cast_w else "ffn",
    )(*((h,) if h_in else ()), x, g, *w_args, g_after)


def _proj_kernel(ha_ref, hb_ref, wt_ref, o_ref, w_ref, *, tiles_a):
    i = pl.program_id(1)

    @pl.when(i == 0)
    def _():
        w_ref[...] = wt_ref[...].astype(BF16)

    o_ref[...] = _dot_nt(jnp.where(i < tiles_a, ha_ref[...], hb_ref[...]), w_ref[...])


def _proj(hs, w_t, col0, tn, nout):
    tiles_a = hs[0].shape[0] // TM
    n = hs[0].shape[0] + hs[1].shape[0]
    return pl.pallas_call(
        functools.partial(_proj_kernel, tiles_a=tiles_a),
        grid=(nout // tn, n // TM),
        in_specs=[
            pl.BlockSpec((TM, D_MODEL), lambda j, i: (jnp.minimum(i, tiles_a - 1), 0)),
            pl.BlockSpec((TM, D_MODEL), lambda j, i: (jnp.maximum(i - tiles_a, 0), 0)),
            pl.BlockSpec((pl.Element(tn), pl.Element(D_MODEL)),
                         lambda j, i: (pl.multiple_of(col0 + j * tn, SUBLANES), 0)),
        ],
        out_specs=pl.BlockSpec((TM, tn), lambda j, i: (i, j)),
        out_shape=jax.ShapeDtypeStruct((n, nout), F32),
        scratch_shapes=[pltpu.VMEM((tn, D_MODEL), BF16)],
        compiler_params=_cparams(("parallel", "arbitrary")),
        name="proj",
    )(*hs, w_t)


def _pair_swap(x):
    n = x.shape[-1]
    lane = lax.broadcasted_iota(jnp.int32, x.shape, x.ndim - 1)
    prev = pltpu.roll(x, 1, x.ndim - 1)
    nxt = pltpu.roll(x, n - 1, x.ndim - 1)
    return jnp.where((lane & 1) == 1, prev, nxt)


def _rotary(x, cos, sin_signed):
    return x * cos + _pair_swap(x) * sin_signed


def _ret_finish(o, zg):
    o = o * lax.rsqrt(jnp.mean(o * o, axis=-1, keepdims=True) + GN_EPS_RET)
    return o * (zg * jax.nn.sigmoid(zg))


RET_PROMPT_SEQS = 2


def _ret_prompt_kernel(lg_ref, *refs):
    nb = RET_PROMPT_SEQS
    q_refs, k_refs, v_refs, g_refs = (refs[i * nb:(i + 1) * nb] for i in range(4))
    cos_ref, sin_ref, o_ref, s_out_ref, s_ref = refs[4 * nb:]
    c = pl.program_id(1)
    C = RET_CHUNK

    @pl.when(c == 0)
    def _():
        s_ref[...] = jnp.zeros_like(s_ref)

    cos = cos_ref[...]
    sin = sin_ref[...]
    ti = lax.broadcasted_iota(jnp.int32, (C, C), 0)
    tj = lax.broadcasted_iota(jnp.int32, (C, C), 1)
    rel = (ti - tj).astype(F32)
    idx = lax.broadcasted_iota(jnp.int32, (C, 1), 0).astype(F32)
    H = range(RET_HEADS)
    lgs = [lg_ref[h] for h in H]
    finals = []

    def sequence(b):
        qs = [_rotary(q_refs[b][:, h * RET_DK:(h + 1) * RET_DK], cos, sin) for h in H]
        ks = [_rotary(k_refs[b][:, h * RET_DK:(h + 1) * RET_DK], cos, sin) * (RET_DK ** -0.5) for h in H]
        yield
        vs = [v_refs[b][:, h * RET_DV:(h + 1) * RET_DV] for h in H]
        ss = [s_ref[b, h] for h in H]
        scores = [_dot_nt(q, k) * jnp.where(rel >= 0, jnp.exp(lg * jnp.maximum(rel, 0.0)), 0.0)
                  for q, k, lg in zip(qs, ks, lgs)]
        crosses = [_dot(q, s) * jnp.exp(lg * (idx + 1.0)) for q, s, lg in zip(qs, ss, lgs)]
        inners = [_dot(sc, v) for sc, v in zip(scores, vs)]
        s_news = [jnp.exp(lg * C) * s + _dot_tn(k * jnp.exp(lg * (C - 1.0 - idx)), v)
                  for lg, s, k, v in zip(lgs, ss, ks, vs)]
        yield
        for h in H:
            s_ref[b, h] = s_news[h]
            o_ref[b, :, h * RET_DV:(h + 1) * RET_DV] = _ret_finish(
                inners[h] + crosses[h], g_refs[b][:, h * RET_DV:(h + 1) * RET_DV]).astype(BF16)
        finals.append((b, s_news))

    _run_staged([sequence(b) for b in range(nb)], 1)

    @pl.when(c == pl.num_programs(1) - 1)
    def _():
        for b, s_news in finals:
            for h in H:
                s_out_ref[b, h] = s_news[h]


def _ret_prompt(z, lg, cos, sin, batch, seq):
    nc = seq // RET_CHUNK
    C = RET_CHUNK
    nb = RET_PROMPT_SEQS
    cols = lambda width, col0: [pl.BlockSpec((C, width), lambda g, c, b=b: ((g * nb + b) * nc + c, col0 // width))
                                for b in range(nb)]
    return pl.pallas_call(
        _ret_prompt_kernel,
        grid=(batch // nb, nc),
        in_specs=[pl.BlockSpec(memory_space=pltpu.SMEM)]
        + cols(RET_QK, ZQ) + cols(RET_QK, ZK) + cols(RET_V, ZV) + cols(RET_V, ZG)
        + [pl.BlockSpec((C, RET_DK), lambda g, c: (c, 0)), pl.BlockSpec((C, RET_DK), lambda g, c: (c, 0))],
        out_specs=[
            pl.BlockSpec((nb, C, RET_V), lambda g, c: (g, c, 0)),
            pl.BlockSpec((nb, RET_HEADS, RET_DK, RET_DV), lambda g, c: (g, 0, 0, 0)),
        ],
        out_shape=[
            jax.ShapeDtypeStruct((batch, seq, RET_V), BF16),
            jax.ShapeDtypeStruct((batch, RET_HEADS, RET_DK, RET_DV), F32),
        ],
        scratch_shapes=[pltpu.VMEM((nb, RET_HEADS, RET_DK, RET_DV), F32)],
        compiler_params=_cparams(("parallel", "arbitrary")),
        name="ret_prompt",
    )(lg, *([z] * (4 * nb)), cos, sin)


RET_SAMPLE_ROWS = 16


def _ret_sample_kernel(lg_ref, q_ref, k_ref, v_ref, g_ref, cos_ref, sin_ref, s0_ref,
                       o_ref, s_out_ref, *, seq):
    R = RET_SAMPLE_ROWS
    nb = R // seq
    cos = cos_ref[...]
    sin = sin_ref[...]
    ri = lax.broadcasted_iota(jnp.int32, (R, R), 0)
    rj = lax.broadcasted_iota(jnp.int32, (R, R), 1)
    rel = ((ri % seq) - (rj % seq)).astype(F32)
    ok = ((ri // seq) == (rj // seq)) & (rel >= 0)
    row = lax.broadcasted_iota(jnp.int32, (R, 1), 0)
    t = (row % seq).astype(F32)

    for h in range(RET_HEADS):
        lg = lg_ref[h]
        q = _rotary(q_ref[:, h * RET_DK:(h + 1) * RET_DK], cos, sin)
        k = _rotary(k_ref[:, h * RET_DK:(h + 1) * RET_DK], cos, sin) * (RET_DK ** -0.5)
        v = v_ref[:, h * RET_DV:(h + 1) * RET_DV]
        mask = jnp.where(ok, jnp.exp(lg * jnp.maximum(rel, 0.0)), 0.0)
        q_dec = jnp.exp(lg * (t + 1.0))
        kd = k * jnp.exp(lg * (seq - 1.0 - t))
        inner = _dot(_dot_nt(q, k) * mask, v)
        cross = jnp.zeros_like(inner)
        for b in range(nb):
            s = s0_ref[b, h]
            mine = (row // seq) == b
            cross = jnp.where(mine, _dot(q, s), cross)
            s_out_ref[b, h] = jnp.exp(lg * seq) * s + _dot_tn(jnp.where(mine, kd, 0.0), v)
        o_ref[:, h * RET_DV:(h + 1) * RET_DV] = _ret_finish(
            inner + cross * q_dec, g_ref[:, h * RET_DV:(h + 1) * RET_DV]).astype(BF16)


def _ret_sample(z, lg, cos, sin, s0, row0, batch, seq):
    R = RET_SAMPLE_ROWS
    nb = R // seq
    rb0 = row0 // R
    return pl.pallas_call(
        functools.partial(_ret_sample_kernel, seq=seq),
        grid=(batch // nb,),
        in_specs=[
            pl.BlockSpec(memory_space=pltpu.SMEM),
            pl.BlockSpec((R, RET_QK), lambda i: (rb0 + i, ZQ // RET_QK)),
            pl.BlockSpec((R, RET_QK), lambda i: (rb0 + i, ZK // RET_QK)),
            pl.BlockSpec((R, RET_V), lambda i: (rb0 + i, ZV // RET_V)),
            pl.BlockSpec((R, RET_V), lambda i: (rb0 + i, ZG // RET_V)),
            pl.BlockSpec((R, RET_DK), lambda i: (0, 0)),
            pl.BlockSpec((R, RET_DK), lambda i: (0, 0)),
            pl.BlockSpec((nb, RET_HEADS, RET_DK, RET_DV), lambda i: (i, 0, 0, 0)),
        ],
        out_specs=[
            pl.BlockSpec((R, RET_V), lambda i: (i, 0)),
            pl.BlockSpec((nb, RET_HEADS, RET_DK, RET_DV), lambda i: (i, 0, 0, 0)),
        ],
        out_shape=[
            jax.ShapeDtypeStruct((batch * seq, RET_V), BF16),
            jax.ShapeDtypeStruct((batch, RET_HEADS, RET_DK, RET_DV), F32),
        ],
        compiler_params=_cparams(("parallel",)),
        name="ret_sample",
    )(lg, z, z, z, z, cos, sin, s0)


def _rope_tables(pos):
    half = RET_DK // 2
    inv = 1.0 / (ROPE_BASE ** jnp.linspace(0.0, 1.0, half, dtype=F32))
    ang = pos[:, None] * inv[None, :]
    cos = jnp.repeat(jnp.cos(ang), 2, axis=-1)
    sin = jnp.sin(ang)
    sin_signed = jnp.stack([-sin, sin], axis=-1).reshape(pos.shape[0], RET_DK)
    return cos, sin_signed


def _segsum(x, seg_ref):
    rows = x.shape[0]
    n = rows * RWKV_PAIRS
    xs = jnp.concatenate([x[:, p * LANES:(p + 1) * LANES] for p in range(RWKV_PAIRS)], axis=0)
    ss = jnp.dot(jnp.concatenate(_bf16_terms(xs, 2), axis=0), seg_ref[...], preferred_element_type=F32)
    ss = ss[0:n] + ss[n:2 * n]
    return jnp.concatenate([ss[p * rows:(p + 1) * rows] for p in range(RWKV_PAIRS)], axis=1)


def _run_staged(gens, lead):
    out = [None] * len(gens)
    live = [True] * len(gens)

    def step(i):
        try:
            next(gens[i])
        except StopIteration as stop:
            out[i] = stop.value
            live[i] = False

    for _ in range(lead):
        if live[0]:
            step(0)
    while any(live):
        for i in range(len(gens)):
            if live[i]:
                step(i)
    return out


def _each(fn, *lists):
    out = []
    for args in zip(*lists):
        out.append(fn(*args))
        yield
    return out


def _rwkv_chunk(zr, zl, prev_r, prev_l, prm, n_seq, **state_fns):
    front = yield from _rwkv_front(zr, zl, prev_r, prev_l, prm, n_seq)
    return (yield from _rwkv_back(*front, prm, n_seq, **state_fns))


def _rwkv_front(zr, zl, prev_r, prev_l, prm, n_seq):
    (mur_ref, mul_ref, w0_ref, wup_ref, a0_ref, aup_ref, gup_ref,
     kk_ref, ka_ref, rk_ref, lnw_ref, lnb_ref, seg_ref) = prm
    C = RWKV_CHUNK
    W = RWKV_W
    R = zr.shape[0]
    TS = C // n_seq
    row = lax.broadcasted_iota(jnp.int32, (R, 1), 0)
    first = (row % TS) == 0

    def mix(z, prev, mu_ref):
        z_prev = jnp.where(first, prev, pltpu.roll(z, 1, 0))
        return z + (z_prev - z) * mu_ref[...]

    xs_r = mix(zr[:, 0:W], prev_r[:, 0:W], mur_ref.at[:, 0:W])
    yield
    xs_k = mix(zr[:, W:2 * W], prev_r[:, W:2 * W], mur_ref.at[:, W:2 * W])
    yield
    xs_v = mix(zr[:, 2 * W:3 * W], prev_r[:, 2 * W:3 * W], mur_ref.at[:, 2 * W:3 * W])
    yield
    xl = mix(zl, prev_l, mul_ref)
    r, k, v = xs_r, xs_k, xs_v
    x_wa = xl[:, 0:LANES]
    x_g = xl[:, LANES:LORA_PAD]

    y_w = w0_ref[...] + _dot(jnp.tanh(x_wa), wup_ref[...])
    yield
    w_log = -(jnp.maximum(-y_w, 0.0) + jnp.log(1.0 + jnp.exp(-jnp.abs(y_w)))) - 0.5
    lw = -jnp.exp(w_log)
    yield
    a_rate = jax.nn.sigmoid(a0_ref[...] + _dot(x_wa, aup_ref[...]))
    yield
    g = _dot(jax.nn.sigmoid(x_g), gup_ref[...])
    yield

    kk = k * kk_ref[...]
    k = k * (1.0 + (a_rate - 1.0) * ka_ref[...])
    yield
    sums = _segsum(jnp.concatenate([kk * kk, r * k * rk_ref[...]], axis=0), seg_ref)
    yield
    kk = kk * lax.rsqrt(jnp.maximum(sums[0:R], 1e-24))
    bonus_rk = sums[R:2 * R]
    a = -kk
    b = kk * a_rate
    yield

    ti = lax.broadcasted_iota(jnp.int32, (2 * R, R), 0)
    tj = lax.broadcasted_iota(jnp.int32, (2 * R, R), 1)
    same = ((ti % R) // TS) == (tj // TS)
    sel = (same & ((ti >= R) | (tj <= ti))).astype(BF16)
    sums_lw = sum(jnp.dot(sel, t, preferred_element_type=F32) for t in _bf16_terms(lw, 3))
    yield
    cum = sums_lw[0:R]
    tot = sums_lw[R:2 * R]
    e_neg = jnp.exp(-cum)
    yield
    a_t = (a * jnp.exp(cum - lw)).astype(BF16)
    yield
    b_t = (b * e_neg).astype(BF16)
    yield
    k_t = (k * e_neg).astype(BF16)
    yield
    r_t = r * jnp.exp(cum)
    yield
    e_tail = jnp.exp(tot - cum)
    yield
    b_g = (b * e_tail).astype(BF16)
    yield
    k_g = (k * e_tail).astype(BF16)
    dec = jnp.exp(tot)
    yield
    return (a_t, b_t, k_t, b_g, k_g), (r_t, v, dec, g, bonus_rk)


def _rwkv_back(mm_operands, f32_arrays, prm, n_seq, read_state=None, update_state=None, fold_state=None):
    a_t, b_t, k_t, b_g, k_g = mm_operands
    r_t, v, dec, g, bonus_rk = f32_arrays
    lnw_ref, lnb_ref, seg_ref = prm[10:13]
    C = RWKV_CHUNK
    R = r_t.shape[0]
    TS = C // n_seq

    lane = lax.broadcasted_iota(jnp.int32, (1, LANES), 1)
    lo = lane < RWKV_N

    def split(x):
        return jnp.concatenate([jnp.where(lo, x, 0.0), jnp.where(lo, 0.0, x)], axis=0)

    row4 = lax.broadcasted_iota(jnp.int32, (C, 4 * C), 0)
    col4 = lax.broadcasted_iota(jnp.int32, (C, 4 * C), 1) % C
    same4 = (row4 // TS) == (col4 // TS)
    strict = same4 & (col4 < row4)
    incl = same4 & (col4 <= row4)
    row2 = lax.broadcasted_iota(jnp.int32, (C, 2 * C), 0)
    col2 = lax.broadcasted_iota(jnp.int32, (C, 2 * C), 1)
    eye2 = ((col2 % C) == row2).astype(F32)
    left = col2 < C

    tiles = [(slice(i * C, (i + 1) * C), slice(p * LANES, (p + 1) * LANES))
             for i in range(R // C) for p in range(RWKV_PAIRS)]
    v2s = [split(v[rs, sl]) for rs, sl in tiles]
    ars = [jnp.concatenate([a_t[rs, sl], r_t[rs, sl]], axis=0) for rs, sl in tiles]
    if fold_state is None:
        wys = read_state(ars)
    gms = yield from _each(
        lambda ar, t: _dot_nt(ar, jnp.concatenate([split(b_t[t[0], t[1]]), split(k_t[t[0], t[1]])], axis=0)),
        ars, tiles)
    gas = [jnp.where(strict, gm[0:C], 0.0) for gm in gms]
    grs = [jnp.where(incl, gm[C:2 * C], 0.0) for gm in gms]

    def blockdiag(x):
        return jnp.concatenate([jnp.where(left, x, 0.0), jnp.where(left, 0.0, x)], axis=0)

    ts = [eye2 + ga[:, 0:2 * C] for ga in gas]
    n_sq = TS.bit_length() - 2
    if n_sq >= 1:
        pws = yield from _each(lambda ga: _dot(ga[:, 0:2 * C], blockdiag(ga[:, 0:2 * C])), gas)
        for _ in range(n_sq - 1):
            tps = yield from _each(lambda t, pw: _dot(jnp.concatenate([t, pw], axis=0), blockdiag(pw)), ts, pws)
            ts = [t + tp[0:C] for t, tp in zip(ts, tps)]
            pws = [tp[C:2 * C] for tp in tps]
        ts = yield from _each(lambda t, pw: t + _dot(t, blockdiag(pw)), ts, pws)
    wis = yield from _each(lambda ga, v2: _dot(ga[:, 2 * C:4 * C], v2), gas, v2s)
    bks = [jnp.concatenate([b_g[rs, sl], k_g[rs, sl]], axis=0) for rs, sl in tiles]
    if fold_state is None:
        us = [_dot(t, split(wi + wy[0:C])) for t, wi, wy in zip(ts, wis, wys)]
        ys = [wy[C:2 * C] + _dot(gr, jnp.concatenate([split(u), v2], axis=0))
              for wy, gr, u, v2 in zip(wys, grs, us, v2s)]
        update_state([jnp.concatenate([u, v[rs, sl]], axis=0) for u, (rs, sl) in zip(us, tiles)], bks, dec)
    else:
        zc = jnp.zeros((C, LANES), F32)
        aus = yield from _each(
            lambda t, wi, tl: _dot(t, jnp.concatenate([split(a_t[tl[0], tl[1]]), split(wi)], axis=1)),
            ts, wis, tiles)
        rys = yield from _each(
            lambda gr, au, v2: _dot(gr, jnp.concatenate(
                [jnp.concatenate([split(au[:, 0:LANES]), split(au[:, LANES:2 * LANES])], axis=1),
                 jnp.concatenate([jnp.concatenate([zc, zc], axis=0), v2], axis=1)], axis=0)),
            grs, aus, v2s)
        mns = yield from _each(
            lambda bk, au, tl: _dot_tn(bk, jnp.concatenate(
                [au, jnp.concatenate([zc, v[tl[0], tl[1]]], axis=1)], axis=0)),
            bks, aus, tiles)
        ys = fold_state([r_t[rs, sl] + ry[:, 0:LANES] for ry, (rs, sl) in zip(rys, tiles)],
                        [ry[:, LANES:2 * LANES] for ry in rys], mns, dec)

    yield
    y = jnp.concatenate([jnp.concatenate(ys[i * RWKV_PAIRS:(i + 1) * RWKV_PAIRS], axis=1)
                         for i in range(R // C)], axis=0)
    mean = _segsum(y, seg_ref) * (1.0 / RWKV_N)
    d = y - mean
    var = _segsum(d * d, seg_ref) * (1.0 / RWKV_N)
    yn = d * lax.rsqrt(var + GN_EPS_RWKV) * lnw_ref[...] + lnb_ref[...]
    return ((yn + bonus_rk * v) * g).astype(BF16)


def _rwkv_prompt_kernel(*refs, batch):
    zr_refs = refs[:batch]
    zl_refs = refs[batch:2 * batch]
    prm = refs[2 * batch:2 * batch + 13]
    o_ref, s_out_ref, prevr_ref, prevl_ref, h_ref = refs[2 * batch + 13:]
    C = RWKV_CHUNK
    c = pl.program_id(0)

    @pl.when(c == 0)
    def _():
        prevr_ref[...] = jnp.zeros_like(prevr_ref)
        prevl_ref[...] = jnp.zeros_like(prevl_ref)
        h_ref[...] = jnp.zeros_like(h_ref)

    ki = lax.broadcasted_iota(jnp.int32, (LANES, LANES), 0)
    vi = lax.broadcasted_iota(jnp.int32, (LANES, LANES), 1)
    diag_blocks = (ki < RWKV_N) == (vi < RWKV_N)

    def group(i0):
        seqs = range(i0, i0 + RWKV_PROMPT_GROUP)
        tiles = [(i, p) for i in seqs for p in range(RWKV_PAIRS)]
        zr = jnp.concatenate([zr_refs[i][...] for i in seqs], axis=0)
        zl = jnp.concatenate([zl_refs[i][...] for i in seqs], axis=0)
        rows = lambda ref: jnp.concatenate(
            [jnp.broadcast_to(ref[i:i + 1, :], (C, ref.shape[1])) for i in seqs], axis=0)

        def fold_state(r_hats, y_inds, mns, dec):
            dec_cols = [jnp.broadcast_to(dec[(i - i0) * C:(i - i0) * C + 1, p * LANES:(p + 1) * LANES],
                                         (LANES, LANES)).T for i, p in tiles]
            hs = [h_ref[i, p] for i, p in tiles]
            res = [_dot(jnp.concatenate([rh, jnp.where(diag_blocks, mn[:, 0:LANES], 0.0)], axis=0), h)
                   for rh, mn, h in zip(r_hats, mns, hs)]
            for (i, p), h, dc, rs, mn in zip(tiles, hs, dec_cols, res, mns):
                h_ref[i, p] = h * dc + rs[C:C + LANES] + jnp.where(diag_blocks, mn[:, LANES:2 * LANES], 0.0)
            return [rs[0:C] + yi for rs, yi in zip(res, y_inds)]

        y = yield from _rwkv_chunk(zr, zl, rows(prevr_ref), rows(prevl_ref), prm, 1, fold_state=fold_state)
        for n, i in enumerate(seqs):
            o_ref[i] = y[n * C:(n + 1) * C]
            prevr_ref[i:i + 1, :] = zr[(n + 1) * C - 1:(n + 1) * C, :]
            prevl_ref[i:i + 1, :] = zl[(n + 1) * C - 1:(n + 1) * C, :]

    _run_staged([group(i0) for i0 in range(0, batch, RWKV_PROMPT_GROUP)], RWKV_PROMPT_LEAD)

    @pl.when(c == pl.num_programs(0) - 1)
    def _():
        for i in range(batch):
            for p in range(RWKV_PAIRS):
                s_out_ref[i, p] = h_ref[i, p].T


def _rwkv_sample_kernel(zr_ref, zl_ref, shr_ref, shl_ref, s0_ref, *rest, n_seq):
    prm = rest[:13]
    o_ref, s_out_ref = rest[13:]
    C = RWKV_CHUNK
    N = RWKV_N
    TS = C // n_seq
    tokseq = (lax.broadcasted_iota(jnp.int32, (1, LANES), 1) % C) // TS

    def pick(x, j):
        return jnp.where(tokseq == j, x, 0.0)

    def stacked(h):
        return s0_ref[:, h].reshape(n_seq * N, N)

    def read_state(ars):
        res = [[_dot_nt(stacked(2 * p + e), ar[:, e * N:(e + 1) * N]) for e in range(2)]
               for p, ar in enumerate(ars)]
        out = []
        for rp in res:
            halves = []
            for re in rp:
                acc = pick(re[0:N], 0)
                for j in range(1, n_seq):
                    acc = acc + pick(re[j * N:(j + 1) * N], j)
                halves.append(acc)
            out.append(jnp.concatenate(halves, axis=0).T)
        return out

    def update_state(uvs, bks, dec):
        uvts = [uv.T for uv in uvs]
        upds = [[_dot(jnp.concatenate([pick(uvt[e * N:(e + 1) * N], j) for j in range(n_seq)], axis=0),
                      bk[:, e * N:(e + 1) * N]) for e in range(2)]
                for uvt, bk in zip(uvts, bks)]
        for p, up in enumerate(upds):
            for e, upd in enumerate(up):
                h = 2 * p + e
                dec_h = jnp.concatenate(
                    [jnp.broadcast_to(dec[j * TS:j * TS + 1, h * N:(h + 1) * N], (N, N)) for j in range(n_seq)],
                    axis=0)
                s_out_ref[:, h] = (stacked(h) * dec_h + upd).reshape(n_seq, N, N)

    (o_ref[...],) = _run_staged([_rwkv_chunk(zr_ref[...], zl_ref[...], shr_ref[...], shl_ref[...], prm, n_seq,
                                             read_state=read_state, update_state=update_state)], 0)


def _rwkv_param_specs():
    W = RWKV_W
    shapes = [(1, 3 * W), (1, LORA_PAD), (1, W), (LANES, W), (1, W), (LANES, W), (LORA_PAD - LANES, W),
              (1, W), (1, W), (1, W), (1, W), (1, W), (LANES, LANES)]
    return [pl.BlockSpec(s, lambda *_: (0, 0)) for s in shapes]


def _rwkv_prompt(z, zl, params, batch, seq):
    C = RWKV_CHUNK
    W = RWKV_W
    nc = seq // C
    zr_spec = lambda i: pl.BlockSpec((C, 3 * W), lambda c: (i * nc + c, ZS // (3 * W)))
    zl_spec = lambda i: pl.BlockSpec((C, LORA_PAD), lambda c: (i * nc + c, 0))
    return pl.pallas_call(
        functools.partial(_rwkv_prompt_kernel, batch=batch),
        grid=(nc,),
        in_specs=[zr_spec(i) for i in range(batch)] + [zl_spec(i) for i in range(batch)]
        + _rwkv_param_specs(),
        out_specs=[
            pl.BlockSpec((batch, C, W), lambda c: (0, c, 0)),
            pl.BlockSpec((batch, RWKV_PAIRS, LANES, LANES), lambda c: (0, 0, 0, 0)),
        ],
        out_shape=[
            jax.ShapeDtypeStruct((batch, seq, W), BF16),
            jax.ShapeDtypeStruct((batch, RWKV_PAIRS, LANES, LANES), F32),
        ],
        scratch_shapes=[pltpu.VMEM((batch, 3 * W), F32), pltpu.VMEM((batch, LORA_PAD), F32),
                        pltpu.VMEM((batch, RWKV_PAIRS, LANES, LANES), F32)],
        compiler_params=_cparams(("arbitrary",)),
        name="rwkv_prompt",
    )(*([z] * batch), *([zl] * batch), *params)


def _rwkv_sample(z, zl, sh_r, sh_l, s0, params, row0, batch, seq):
    C = RWKV_CHUNK
    W = RWKV_W
    n_seq = C // seq
    rb0 = row0 // C
    return pl.pallas_call(
        functools.partial(_rwkv_sample_kernel, n_seq=n_seq),
        grid=(batch // n_seq,),
        in_specs=[
            pl.BlockSpec((C, 3 * W), lambda i: (rb0 + i, ZS // (3 * W))),
            pl.BlockSpec((C, LORA_PAD), lambda i: (rb0 + i, 0)),
            pl.BlockSpec((C, 3 * W), lambda i: (i, 0)),
            pl.BlockSpec((C, LORA_PAD), lambda i: (i, 0)),
            pl.BlockSpec((n_seq, RWKV_HEADS, RWKV_N, RWKV_N), lambda i: (i, 0, 0, 0)),
        ] + _rwkv_param_specs(),
        out_specs=[
            pl.BlockSpec((C, W), lambda i: (i, 0)),
            pl.BlockSpec((n_seq, RWKV_HEADS, RWKV_N, RWKV_N), lambda i: (i, 0, 0, 0)),
        ],
        out_shape=[
            jax.ShapeDtypeStruct((batch * seq, W), BF16),
            jax.ShapeDtypeStruct((batch, RWKV_HEADS, RWKV_N, RWKV_N), F32),
        ],
        compiler_params=_cparams(("parallel",)),
        name="rwkv_sample",
    )(z, zl, sh_r, sh_l, s0, *params)


def _pair_unblock(s):
    b = s.shape[0]
    lo = s[:, :, :RWKV_N, :RWKV_N]
    hi = s[:, :, RWKV_N:, RWKV_N:]
    return jnp.stack([lo, hi], axis=2).reshape(b, RWKV_HEADS, RWKV_N, RWKV_N)


def _mix_out_kernel(oret_a_ref, oret_b_ref, orwkv_a_ref, orwkv_b_ref, xa_ref, xb_ref, gr_ref, gw_ref,
                    wr_ref, ww_ref, wo_ref, gn_ref, o_ref, hn_ref, *, tiles_a):
    first = pl.program_id(0) < tiles_a
    o_ret = jnp.where(first, oret_a_ref[...], oret_b_ref[...])
    o_rwkv = jnp.where(first, orwkv_a_ref[...], orwkv_b_ref[...])
    a = jnp.dot(o_ret, wr_ref[...], preferred_element_type=F32)
    b = jnp.dot(o_rwkv, ww_ref[...], preferred_element_type=F32)
    merged = jax.nn.sigmoid(gr_ref[...]) * a + jax.nn.sigmoid(gw_ref[...]) * b
    x2 = (jnp.where(first, xa_ref[...], xb_ref[...])
          + jnp.dot(merged.astype(BF16), wo_ref[...], preferred_element_type=F32))
    o_ref[...] = x2
    hn_ref[...] = _rms(x2, gn_ref[...], NORM_EPS).astype(BF16)


def _mix_out(o_rets, o_rwkvs, xs, zg, w_ret_o, w_rwkv_o, w_out, g_next):
    n = xs[0].shape[0] + xs[1].shape[0]
    tm = TM_MIX
    tiles_a = o_rets[0].shape[0] // tm
    resident = lambda shape: pl.BlockSpec(shape, lambda i: (0, 0), pipeline_mode=pl.Buffered(1))
    first = lambda i: (jnp.minimum(i, tiles_a - 1), 0)
    second = lambda i: (jnp.maximum(i - tiles_a, 0), 0)
    return pl.pallas_call(
        functools.partial(_mix_out_kernel, tiles_a=tiles_a),
        grid=(n // tm,),
        in_specs=[
            pl.BlockSpec((tm, RET_V), first),
            pl.BlockSpec((tm, RET_V), second),
            pl.BlockSpec((tm, RWKV_W), first),
            pl.BlockSpec((tm, RWKV_W), second),
            pl.BlockSpec((tm, D_MODEL), first),
            pl.BlockSpec((tm, D_MODEL), second),
            pl.BlockSpec((tm, D_MODEL), lambda i: (i, ZGR // D_MODEL)),
            pl.BlockSpec((tm, D_MODEL), lambda i: (i, ZGW // D_MODEL)),
            resident((RET_V, D_MODEL)),
            resident((RWKV_W, D_MODEL)),
            resident((D_MODEL, D_MODEL)),
            pl.BlockSpec((1, D_MODEL), lambda i: (0, 0)),
        ],
        out_specs=[pl.BlockSpec((tm, D_MODEL), lambda i: (i, 0)), pl.BlockSpec((tm, D_MODEL), lambda i: (i, 0))],
        out_shape=[jax.ShapeDtypeStruct((n, D_MODEL), F32), jax.ShapeDtypeStruct((n, D_MODEL), BF16)],
        compiler_params=_cparams(("arbitrary",)),
        name="mix_out",
    )(*o_rets, *o_rwkvs, *xs, zg, zg, w_ret_o, w_rwkv_o, w_out, g_next)


def kernel(x_prompt, x_sample, state_ret, state_rwkv, state_shift, norm_ffn1, ffn1_w_gu, ffn1_w_down, norm_mix, w_in, w_ret_o, rwkv_mu, rwkv_w0, rwkv_w_up, rwkv_a0, rwkv_a_up, rwkv_g_up, rwkv_k_k, rwkv_k_a, rwkv_r_k, rwkv_ln_w, rwkv_ln_b, w_rwkv_o, w_out, norm_ffn2, ffn2_w_gu, ffn2_w_down, norm_final):
    bp, tp, d = x_prompt.shape
    bs, ts, _ = x_sample.shape
    n_p = bp * tp
    n_s = bs * ts
    W = RWKV_W
    row = lambda a: a.reshape(1, -1)
    assert d == D_MODEL and n_p % TM == 0 and n_s == TM and n_p % TM_MIX == 0
    assert tp % RET_CHUNK == 0 and tp % RWKV_CHUNK == 0 and bp % RWKV_PROMPT_GROUP == 0 and bp % RET_PROMPT_SEQS == 0
    assert RWKV_CHUNK % ts == 0 and n_s % RWKV_CHUNK == 0 and RET_SAMPLE_ROWS % ts == 0 and n_s % RET_SAMPLE_ROWS == 0

    x1_s, h_s, *w_ffn1 = _ffn(x_sample.reshape(n_s, d), 0, n_s, row(norm_ffn1[0]),
                              (ffn1_w_gu[0], ffn1_w_down[0]), row(norm_mix[0]), final_norm=False, emit_h=True)
    x1_p, h_p = _ffn(x_prompt.reshape(n_p, d), 0, n_p, row(norm_ffn1[0]), w_ffn1, row(norm_mix[0]),
                     final_norm=False, emit_h=True)

    w_in_t = jnp.swapaxes(w_in[0], 0, 1)
    z = _proj((h_p, h_s), w_in_t, 0, TN_PROJ_MAIN, Z_MAIN_W)
    zl = _proj((h_p, h_s), w_in_t, Z_MAIN_W, LORA_PAD, LORA_PAD)
    zg = _proj((h_p, h_s), w_in_t, Z_MAIN_W + LORA_W, TN_PROJ, 2 * D_MODEL)

    lg = jnp.log(1.0 - 2.0 ** (-5.0 - jnp.arange(RET_HEADS, dtype=F32)))
    cos_p, sin_p = _rope_tables(jnp.arange(tp, dtype=F32))
    cos_s, sin_s = _rope_tables(PAST_LEN + jnp.arange(ts, dtype=F32))
    rep = RET_SAMPLE_ROWS // ts
    oret_p, ret_p = _ret_prompt(z, lg, cos_p, sin_p, bp, tp)
    oret_s, ret_s = _ret_sample(z, lg, jnp.tile(cos_s, (rep, 1)), jnp.tile(sin_s, (rep, 1)),
                                state_ret[0], n_p, bs, ts)

    mu = rwkv_mu[0]
    pad_l = lambda a: jnp.pad(a, [(0, 0)] * (a.ndim - 1) + [(0, LORA_PAD - LORA_W)])
    seg = (jnp.arange(LANES)[:, None] // RWKV_N == jnp.arange(LANES)[None, :] // RWKV_N).astype(BF16)
    zero64 = jnp.zeros((DECAY_LORA, W), F32)
    params = (
        row(mu[:3 * W]), pad_l(row(mu[3 * W:])),
        row(rwkv_w0[0]), jnp.concatenate([rwkv_w_up[0], zero64], axis=0).astype(BF16),
        row(rwkv_a0[0]), jnp.concatenate([zero64, rwkv_a_up[0]], axis=0).astype(BF16),
        jnp.pad(rwkv_g_up[0], ((0, LORA_PAD - LANES - GATE_LORA), (0, 0))).astype(BF16),
        row(rwkv_k_k[0]), row(rwkv_k_a[0]), row(rwkv_r_k[0]), row(rwkv_ln_w[0]), row(rwkv_ln_b[0]),
        seg,
    )
    orw_p, rwkv_p = _rwkv_prompt(z, zl, params, bp, tp)
    sh_s = state_shift[0]
    first_rows = lambda a: jnp.pad(a[:, None, :], ((0, 0), (0, ts - 1), (0, 0))).reshape(n_s, -1)
    orw_s, rwkv_s = _rwkv_sample(z, zl, first_rows(sh_s[:, :3 * W]), first_rows(pad_l(sh_s[:, 3 * W:])),
                                 state_rwkv[0], params, n_p, bs, ts)

    x2, h_ffn2 = _mix_out((oret_p.reshape(n_p, RET_V), oret_s), (orw_p.reshape(n_p, W), orw_s), (x1_p, x1_s), zg,
                          w_ret_o[0].astype(BF16), w_rwkv_o[0].astype(BF16), w_out[0].astype(BF16),
                          row(norm_ffn2[0]))
    y_s, *w_ffn2 = _ffn(x2, n_p, n_s, row(norm_ffn2[0]), (ffn2_w_gu[0], ffn2_w_down[0]), row(norm_final),
                        final_norm=True, emit_h=False, h=h_ffn2)
    (y_p,) = _ffn(x2, 0, n_p, row(norm_ffn2[0]), w_ffn2, row(norm_final),
                  final_norm=True, emit_h=False, h=h_ffn2)

    def last_rows(firsts, t):
        rows = [f + t - 1 for f in firsts]
        zr_last = jnp.concatenate([z[r:r + 1, ZS:ZS + 3 * W] for r in rows], axis=0)
        zl_last = jnp.concatenate([zl[r:r + 1, :LORA_W] for r in rows], axis=0)
        return jnp.concatenate([zr_last, zl_last], axis=-1)[None]

    shift_s = jnp.concatenate([lax.slice(z, (n_p + ts - 1, ZS), (n_p + n_s, ZS + 3 * W), (ts, 1)),
                               lax.slice(zl, (n_p + ts - 1, 0), (n_p + n_s, LORA_W), (ts, 1))], axis=-1)[None]
    return (y_p.reshape(bp, tp, d), y_s.reshape(bs, ts, d),
            ret_p[None], _pair_unblock(rwkv_p)[None], last_rows(range(0, n_p, tp), tp),
            ret_s[None], rwkv_s[None], shift_s)
```

```python
import functools

import jax
import jax.numpy as jnp
from jax import lax
from jax.experimental import pallas as pl
from jax.experimental.pallas import tpu as pltpu

F32 = jnp.float32
BF16 = jnp.bfloat16

D_MODEL = 2048
PAST_LEN = 16384
RET_HEADS = 4
RET_DK = 256
RET_DV = 512
RET_CHUNK = 128
ROPE_BASE = 10000.0
RET_QK = RET_HEADS * RET_DK
RET_V = RET_HEADS * RET_DV
RWKV_HEADS = 16
RWKV_N = 64
RWKV_W = RWKV_HEADS * RWKV_N
DECAY_LORA = 64
AAA_LORA = 64
GATE_LORA = 160
LORA_W = DECAY_LORA + AAA_LORA + GATE_LORA
LORA_PAD = 384
D_FF = 5632
NORM_EPS = 1e-6
GN_EPS_RET = 1e-6
GN_EPS_RWKV = 64e-5

ZQ, ZK, ZV, ZG, ZS = 0, 1024, 2048, 4096, 6144
Z_MAIN_W = 9216
ZGR, ZGW = 0, 2048

LANES = 128
SUBLANES = 8
MXU_TILE = 256
RWKV_PAIRS = RWKV_W // LANES
RWKV_CHUNK = 64
RWKV_PROMPT_GROUP = 2
RWKV_PROMPT_LEAD = 21

V7X_VMEM_BYTES = 64 * 1024 * 1024
VMEM_LIMIT = V7X_VMEM_BYTES - 8 * 1024 * 1024

TM = 512
TF = 512
TF_CAST = 256
TN_PROJ = 1024
TN_PROJ_MAIN = 1536
TM_MIX = 256


def _rms(x, g, eps):
    return x * lax.rsqrt(jnp.mean(x * x, axis=-1, keepdims=True) + eps) * g


def _dot(a, b):
    return jnp.dot(a.astype(BF16), b.astype(BF16), preferred_element_type=F32)


def _dot_nt(a, b):
    return lax.dot_general(a.astype(BF16), b.astype(BF16), (((1,), (1,)), ((), ())),
                           preferred_element_type=F32)


def _dot_tn(a, b):
    return lax.dot_general(a.astype(BF16), b.astype(BF16), (((0,), (0,)), ((), ())),
                           preferred_element_type=F32)


def _bf16_terms(x, n):
    terms = []
    for _ in range(n):
        t = x.astype(BF16)
        terms.append(t)
        x = x - t.astype(F32)
    return terms


def _cparams(sem):
    return pltpu.CompilerParams(dimension_semantics=sem, vmem_limit_bytes=VMEM_LIMIT)


def _ffn_kernel(*refs, final_norm, emit_h, h_in, cast_w):
    if h_in:
        hin_ref, refs = refs[0], refs[1:]
    x_ref, g_ref, wg_ref, wu_ref, wd_ref, gf_ref = refs[:6]
    outs = list(refs[6:-2])
    h_ref, acc_ref = refs[-2:]
    o_ref = outs.pop(0)
    hn_ref = outs.pop(0) if emit_h else None
    j = pl.program_id(1)

    @pl.when(j == 0)
    def _():
        if not h_in:
            h_ref[...] = _rms(x_ref[...], g_ref[...], NORM_EPS).astype(BF16)
        acc_ref[...] = jnp.zeros_like(acc_ref)

    wg, wu, wd = wg_ref[...], wu_ref[...], wd_ref[...]
    if cast_w:
        wg, wu, wd = wg.astype(BF16), wu.astype(BF16), wd.astype(BF16)
        for w_out_ref, w in zip(outs, (wg, wu, wd)):
            w_out_ref[...] = w
    h = hin_ref[...] if h_in else h_ref[...]
    tf = wg.shape[1]
    half = tf // 2 if tf // 2 >= MXU_TILE else tf
    cols = tuple(slice(c, c + half) for c in range(0, tf, half))
    gus = [(jnp.dot(h, wg[:, c], preferred_element_type=F32), jnp.dot(h, wu[:, c], preferred_element_type=F32))
           for c in cols]
    downs = [jnp.dot(((g * jax.nn.sigmoid(g)) * u).astype(BF16), wd[c, :], preferred_element_type=F32)
             for (g, u), c in zip(gus, cols)]
    acc_ref[...] += sum(downs[1:], downs[0])

    @pl.when(j == pl.num_programs(1) - 1)
    def _():
        y = x_ref[...] + 0.5 * acc_ref[...]
        if final_norm:
            y = _rms(y, gf_ref[...], NORM_EPS)
        if emit_h:
            hn_ref[...] = _rms(y, gf_ref[...], NORM_EPS).astype(BF16)
        o_ref[...] = y


def _ffn(x, row0, rows, g, weights, g_after, final_norm, emit_h, h=None):
    cast_w = len(weights) == 2
    tf = TF_CAST if cast_w else TF
    nj = D_FF // tf
    t0 = row0 // TM
    tile = pl.BlockSpec((TM, D_MODEL), lambda i, j: (t0 + i, 0))
    out_tile = pl.BlockSpec((TM, D_MODEL), lambda i, j: (i, 0))
    h_in = h is not None
    w_specs = [pl.BlockSpec((D_MODEL, tf), lambda i, j: (0, j)),
               pl.BlockSpec((D_MODEL, tf), (lambda i, j: (0, j + nj)) if cast_w else (lambda i, j: (0, j))),
               pl.BlockSpec((tf, D_MODEL), lambda i, j: (j, 0))]
    w_args = (weights[0], weights[0], weights[1]) if cast_w else weights
    out_specs = [out_tile] + ([out_tile] if emit_h else [])
    out_shape = [jax.ShapeDtypeStruct((rows, D_MODEL), F32)] + (
        [jax.ShapeDtypeStruct((rows, D_MODEL), BF16)] if emit_h else [])
    if cast_w:
        out_specs += [pl.BlockSpec((D_MODEL, tf), lambda i, j: (0, j)),
                      pl.BlockSpec((D_MODEL, tf), lambda i, j: (0, j)),
                      pl.BlockSpec((tf, D_MODEL), lambda i, j: (j, 0))]
        out_shape += [jax.ShapeDtypeStruct((D_MODEL, D_FF), BF16), jax.ShapeDtypeStruct((D_MODEL, D_FF), BF16),
                      jax.ShapeDtypeStruct((D_FF, D_MODEL), BF16)]
        assert rows == TM, "each weight tile is written exactly once"
    return pl.pallas_call(
        functools.partial(_ffn_kernel, final_norm=final_norm, emit_h=emit_h, h_in=h_in, cast_w=cast_w),
        grid=(rows // TM, nj),
        in_specs=([tile] if h_in else []) + [tile, pl.BlockSpec((1, D_MODEL), lambda i, j: (0, 0))] + w_specs
        + [pl.BlockSpec((1, D_MODEL), lambda i, j: (0, 0))],
        out_specs=out_specs,
        out_shape=out_shape,
        scratch_shapes=[pltpu.VMEM((TM, D_MODEL), BF16), pltpu.VMEM((TM, D_MODEL), F32)],
        compiler_params=_cparams(("parallel", "arbitrary")),
        name="ffn_cast" if cast_w else "ffn",
    )(*((h,) if h_in else ()), x, g, *w_args, g_after)


def _proj_kernel(ha_ref, hb_ref, wt_ref, o_ref, w_ref, *, tiles_a):
    i = pl.program_id(1)

    @pl.when(i == 0)
    def _():
        w_ref[...] = wt_ref[...].astype(BF16)

    o_ref[...] = _dot_nt(jnp.where(i < tiles_a, ha_ref[...], hb_ref[...]), w_ref[...])


def _proj(hs, w_t, col0, tn, nout):
    tiles_a = hs[0].shape[0] // TM
    n = hs[0].shape[0] + hs[1].shape[0]
    return pl.pallas_call(
        functools.partial(_proj_kernel, tiles_a=tiles_a),
        grid=(nout // tn, n // TM),
        in_specs=[
            pl.BlockSpec((TM, D_MODEL), lambda j, i: (jnp.minimum(i, tiles_a - 1), 0)),
            pl.BlockSpec((TM, D_MODEL), lambda j, i: (jnp.maximum(i - tiles_a, 0), 0)),
            pl.BlockSpec((pl.Element(tn), pl.Element(D_MODEL)),
                         lambda j, i: (pl.multiple_of(col0 + j * tn, SUBLANES), 0)),
        ],
        out_specs=pl.BlockSpec((TM, tn), lambda j, i: (i, j)),
        out_shape=jax.ShapeDtypeStruct((n, nout), F32),
        scratch_shapes=[pltpu.VMEM((tn, D_MODEL), BF16)],
        compiler_params=_cparams(("parallel", "arbitrary")),
        name="proj",
    )(*hs, w_t)


def _pair_swap(x):
    n = x.shape[-1]
    lane = lax.broadcasted_iota(jnp.int32, x.shape, x.ndim - 1)
    prev = pltpu.roll(x, 1, x.ndim - 1)
    nxt = pltpu.roll(x, n - 1, x.ndim - 1)
    return jnp.where((lane & 1) == 1, prev, nxt)


def _rotary(x, cos, sin_signed):
    return x * cos + _pair_swap(x) * sin_signed


def _ret_finish(o, zg):
    o = o * lax.rsqrt(jnp.mean(o * o, axis=-1, keepdims=True) + GN_EPS_RET)
    return o * (zg * jax.nn.sigmoid(zg))


RET_PROMPT_SEQS = 2


def _ret_prompt_kernel(lg_ref, *refs):
    nb = RET_PROMPT_SEQS
    q_refs, k_refs, v_refs, g_refs = (refs[i * nb:(i + 1) * nb] for i in range(4))
    cos_ref, sin_ref, o_ref, s_out_ref, s_ref = refs[4 * nb:]
    c = pl.program_id(1)
    C = RET_CHUNK

    @pl.when(c == 0)
    def _():
        s_ref[...] = jnp.zeros_like(s_ref)

    cos = cos_ref[...]
    sin = sin_ref[...]
    ti = lax.broadcasted_iota(jnp.int32, (C, C), 0)
    tj = lax.broadcasted_iota(jnp.int32, (C, C), 1)
    rel = (ti - tj).astype(F32)
    idx = lax.broadcasted_iota(jnp.int32, (C, 1), 0).astype(F32)
    H = range(RET_HEADS)
    lgs = [lg_ref[h] for h in H]
    finals = []

    def sequence(b):
        qs = [_rotary(q_refs[b][:, h * RET_DK:(h + 1) * RET_DK], cos, sin) for h in H]
        ks = [_rotary(k_refs[b][:, h * RET_DK:(h + 1) * RET_DK], cos, sin) * (RET_DK ** -0.5) for h in H]
        yield
        vs = [v_refs[b][:, h * RET_DV:(h + 1) * RET_DV] for h in H]
        ss = [s_ref[b, h] for h in H]
        scores = [_dot_nt(q, k) * jnp.where(rel >= 0, jnp.exp(lg * jnp.maximum(rel, 0.0)), 0.0)
                  for q, k, lg in zip(qs, ks, lgs)]
        crosses = [_dot(q, s) * jnp.exp(lg * (idx + 1.0)) for q, s, lg in zip(qs, ss, lgs)]
        inners = [_dot(sc, v) for sc, v in zip(scores, vs)]
        s_news = [jnp.exp(lg * C) * s + _dot_tn(k * jnp.exp(lg * (C - 1.0 - idx)), v)
                  for lg, s, k, v in zip(lgs, ss, ks, vs)]
        yield
        for h in H:
            s_ref[b, h] = s_news[h]
            o_ref[b, :, h * RET_DV:(h + 1) * RET_DV] = _ret_finish(
                inners[h] + crosses[h], g_refs[b][:, h * RET_DV:(h + 1) * RET_DV]).astype(BF16)
        finals.append((b, s_news))

    _run_staged([sequence(b) for b in range(nb)], 1)

    @pl.when(c == pl.num_programs(1) - 1)
    def _():
        for b, s_news in finals:
            for h in H:
                s_out_ref[b, h] = s_news[h]


def _ret_prompt(z, lg, cos, sin, batch, seq):
    nc = seq // RET_CHUNK
    C = RET_CHUNK
    nb = RET_PROMPT_SEQS
    cols = lambda width, col0: [pl.BlockSpec((C, width), lambda g, c, b=b: ((g * nb + b) * nc + c, col0 // width))
                                for b in range(nb)]
    return pl.pallas_call(
        _ret_prompt_kernel,
        grid=(batch // nb, nc),
        in_specs=[pl.BlockSpec(memory_space=pltpu.SMEM)]
        + cols(RET_QK, ZQ) + cols(RET_QK, ZK) + cols(RET_V, ZV) + cols(RET_V, ZG)
        + [pl.BlockSpec((C, RET_DK), lambda g, c: (c, 0)), pl.BlockSpec((C, RET_DK), lambda g, c: (c, 0))],
        out_specs=[
            pl.BlockSpec((nb, C, RET_V), lambda g, c: (g, c, 0)),
            pl.BlockSpec((nb, RET_HEADS, RET_DK, RET_DV), lambda g, c: (g, 0, 0, 0)),
        ],
        out_shape=[
            jax.ShapeDtypeStruct((batch, seq, RET_V), BF16),
            jax.ShapeDtypeStruct((batch, RET_HEADS, RET_DK, RET_DV), F32),
        ],
        scratch_shapes=[pltpu.VMEM((nb, RET_HEADS, RET_DK, RET_DV), F32)],
        compiler_params=_cparams(("parallel", "arbitrary")),
        name="ret_prompt",
    )(lg, *([z] * (4 * nb)), cos, sin)


RET_SAMPLE_ROWS = 16


def _ret_sample_kernel(lg_ref, q_ref, k_ref, v_ref, g_ref, cos_ref, sin_ref, s0_ref,
                       o_ref, s_out_ref, *, seq):
    R = RET_SAMPLE_ROWS
    nb = R // seq
    cos = cos_ref[...]
    sin = sin_ref[...]
    ri = lax.broadcasted_iota(jnp.int32, (R, R), 0)
    rj = lax.broadcasted_iota(jnp.int32, (R, R), 1)
    rel = ((ri % seq) - (rj % seq)).astype(F32)
    ok = ((ri // seq) == (rj // seq)) & (rel >= 0)
    row = lax.broadcasted_iota(jnp.int32, (R, 1), 0)
    t = (row % seq).astype(F32)

    for h in range(RET_HEADS):
        lg = lg_ref[h]
        q = _rotary(q_ref[:, h * RET_DK:(h + 1) * RET_DK], cos, sin)
        k = _rotary(k_ref[:, h * RET_DK:(h + 1) * RET_DK], cos, sin) * (RET_DK ** -0.5)
        v = v_ref[:, h * RET_DV:(h + 1) * RET_DV]
        mask = jnp.where(ok, jnp.exp(lg * jnp.maximum(rel, 0.0)), 0.0)
        q_dec = jnp.exp(lg * (t + 1.0))
        kd = k * jnp.exp(lg * (seq - 1.0 - t))
        inner = _dot(_dot_nt(q, k) * mask, v)
        cross = jnp.zeros_like(inner)
        for b in range(nb):
            s = s0_ref[b, h]
            mine = (row // seq) == b
            cross = jnp.where(mine, _dot(q, s), cross)
            s_out_ref[b, h] = jnp.exp(lg * seq) * s + _dot_tn(jnp.where(mine, kd, 0.0), v)
        o_ref[:, h * RET_DV:(h + 1) * RET_DV] = _ret_finish(
            inner + cross * q_dec, g_ref[:, h * RET_DV:(h + 1) * RET_DV]).astype(BF16)


def _ret_sample(z, lg, cos, sin, s0, row0, batch, seq):
    R = RET_SAMPLE_ROWS
    nb = R // seq
    rb0 = row0 // R
    return pl.pallas_call(
        functools.partial(_ret_sample_kernel, seq=seq),
        grid=(batch // nb,),
        in_specs=[
            pl.BlockSpec(memory_space=pltpu.SMEM),
            pl.BlockSpec((R, RET_QK), lambda i: (rb0 + i, ZQ // RET_QK)),
            pl.BlockSpec((R, RET_QK), lambda i: (rb0 + i, ZK // RET_QK)),
            pl.BlockSpec((R, RET_V), lambda i: (rb0 + i, ZV // RET_V)),
            pl.BlockSpec((R, RET_V), lambda i: (rb0 + i, ZG // RET_V)),
            pl.BlockSpec((R, RET_DK), lambda i: (0, 0)),
            pl.BlockSpec((R, RET_DK), lambda i: (0, 0)),
            pl.BlockSpec((nb, RET_HEADS, RET_DK, RET_DV), lambda i: (i, 0, 0, 0)),
        ],
        out_specs=[
            pl.BlockSpec((R, RET_V), lambda i: (i, 0)),
            pl.BlockSpec((nb, RET_HEADS, RET_DK, RET_DV), lambda i: (i, 0, 0, 0)),
        ],
        out_shape=[
            jax.ShapeDtypeStruct((batch * seq, RET_V), BF16),
            jax.ShapeDtypeStruct((batch, RET_HEADS, RET_DK, RET_DV), F32),
        ],
        compiler_params=_cparams(("parallel",)),
        name="ret_sample",
    )(lg, z, z, z, z, cos, sin, s0)


def _rope_tables(pos):
    half = RET_DK // 2
    inv = 1.0 / (ROPE_BASE ** jnp.linspace(0.0, 1.0, half, dtype=F32))
    ang = pos[:, None] * inv[None, :]
    cos = jnp.repeat(jnp.cos(ang), 2, axis=-1)
    sin = jnp.sin(ang)
    sin_signed = jnp.stack([-sin, sin], axis=-1).reshape(pos.shape[0], RET_DK)
    return cos, sin_signed


def _segsum(x, seg_ref):
    rows = x.shape[0]
    n = rows * RWKV_PAIRS
    xs = jnp.concatenate([x[:, p * LANES:(p + 1) * LANES] for p in range(RWKV_PAIRS)], axis=0)
    ss = jnp.dot(jnp.concatenate(_bf16_terms(xs, 2), axis=0), seg_ref[...], preferred_element_type=F32)
    ss = ss[0:n] + ss[n:2 * n]
    return jnp.concatenate([ss[p * rows:(p + 1) * rows] for p in range(RWKV_PAIRS)], axis=1)


def _run_staged(gens, lead):
    out = [None] * len(gens)
    live = [True] * len(gens)

    def step(i):
        try:
            next(gens[i])
        except StopIteration as stop:
            out[i] = stop.value
            live[i] = False

    for _ in range(lead):
        if live[0]:
            step(0)
    while any(live):
        for i in range(len(gens)):
            if live[i]:
                step(i)
    return out


def _each(fn, *lists):
    out = []
    for args in zip(*lists):
        out.append(fn(*args))
        yield
    return out


def _rwkv_chunk(zr, zl, prev_r, prev_l, prm, n_seq, **state_fns):
    front = yield from _rwkv_front(zr, zl, prev_r, prev_l, prm, n_seq)
    return (yield from _rwkv_back(*front, prm, n_seq, **state_fns))


def _rwkv_front(zr, zl, prev_r, prev_l, prm, n_seq):
    (mur_ref, mul_ref, w0_ref, wup_ref, a0_ref, aup_ref, gup_ref,
     kk_ref, ka_ref, rk_ref, lnw_ref, lnb_ref, seg_ref) = prm
    C = RWKV_CHUNK
    W = RWKV_W
    R = zr.shape[0]
    TS = C // n_seq
    row = lax.broadcasted_iota(jnp.int32, (R, 1), 0)
    first = (row % TS) == 0

    def mix(z, prev, mu_ref):
        z_prev = jnp.where(first, prev, pltpu.roll(z, 1, 0))
        return z + (z_prev - z) * mu_ref[...]

    xs_r = mix(zr[:, 0:W], prev_r[:, 0:W], mur_ref.at[:, 0:W])
    yield
    xs_k = mix(zr[:, W:2 * W], prev_r[:, W:2 * W], mur_ref.at[:, W:2 * W])
    yield
    xs_v = mix(zr[:, 2 * W:3 * W], prev_r[:, 2 * W:3 * W], mur_ref.at[:, 2 * W:3 * W])
    yield
    xl = mix(zl, prev_l, mul_ref)
    r, k, v = xs_r, xs_k, xs_v
    x_wa = xl[:, 0:LANES]
    x_g = xl[:, LANES:LORA_PAD]

    y_w = w0_ref[...] + _dot(jnp.tanh(x_wa), wup_ref[...])
    yield
    w_log = -(jnp.maximum(-y_w, 0.0) + jnp.log(1.0 + jnp.exp(-jnp.abs(y_w)))) - 0.5
    lw = -jnp.exp(w_log)
    yield
    a_rate = jax.nn.sigmoid(a0_ref[...] + _dot(x_wa, aup_ref[...]))
    yield
    g = _dot(jax.nn.sigmoid(x_g), gup_ref[...])
    yield

    kk = k * kk_ref[...]
    k = k * (1.0 + (a_rate - 1.0) * ka_ref[...])
    yield
    sums = _segsum(jnp.concatenate([kk * kk, r * k * rk_ref[...]], axis=0), seg_ref)
    yield
    kk = kk * lax.rsqrt(jnp.maximum(sums[0:R], 1e-24))
    bonus_rk = sums[R:2 * R]
    a = -kk
    b = kk * a_rate
    yield

    ti = lax.broadcasted_iota(jnp.int32, (2 * R, R), 0)
    tj = lax.broadcasted_iota(jnp.int32, (2 * R, R), 1)
    same = ((ti % R) // TS) == (tj // TS)
    sel = (same & ((ti >= R) | (tj <= ti))).astype(BF16)
    sums_lw = sum(jnp.dot(sel, t, preferred_element_type=F32) for t in _bf16_terms(lw, 3))
    yield
    cum = sums_lw[0:R]
    tot = sums_lw[R:2 * R]
    e_neg = jnp.exp(-cum)
    yield
    a_t = (a * jnp.exp(cum - lw)).astype(BF16)
    yield
    b_t = (b * e_neg).astype(BF16)
    yield
    k_t = (k * e_neg).astype(BF16)
    yield
    r_t = r * jnp.exp(cum)
    yield
    e_tail = jnp.exp(tot - cum)
    yield
    b_g = (b * e_tail).astype(BF16)
    yield
    k_g = (k * e_tail).astype(BF16)
    dec = jnp.exp(tot)
    yield
    return (a_t, b_t, k_t, b_g, k_g), (r_t, v, dec, g, bonus_rk)


def _rwkv_back(mm_operands, f32_arrays, prm, n_seq, read_state=None, update_state=None, fold_state=None):
    a_t, b_t, k_t, b_g, k_g = mm_operands
    r_t, v, dec, g, bonus_rk = f32_arrays
    lnw_ref, lnb_ref, seg_ref = prm[10:13]
    C = RWKV_CHUNK
    R = r_t.shape[0]
    TS = C // n_seq

    lane = lax.broadcasted_iota(jnp.int32, (1, LANES), 1)
    lo = lane < RWKV_N

    def split(x):
        return jnp.concatenate([jnp.where(lo, x, 0.0), jnp.where(lo, 0.0, x)], axis=0)

    row4 = lax.broadcasted_iota(jnp.int32, (C, 4 * C), 0)
    col4 = lax.broadcasted_iota(jnp.int32, (C, 4 * C), 1) % C
    same4 = (row4 // TS) == (col4 // TS)
    strict = same4 & (col4 < row4)
    incl = same4 & (col4 <= row4)
    row2 = lax.broadcasted_iota(jnp.int32, (C, 2 * C), 0)
    col2 = lax.broadcasted_iota(jnp.int32, (C, 2 * C), 1)
    eye2 = ((col2 % C) == row2).astype(F32)
    left = col2 < C

    tiles = [(slice(i * C, (i + 1) * C), slice(p * LANES, (p + 1) * LANES))
             for i in range(R // C) for p in range(RWKV_PAIRS)]
    v2s = [split(v[rs, sl]) for rs, sl in tiles]
    ars = [jnp.concatenate([a_t[rs, sl], r_t[rs, sl]], axis=0) for rs, sl in tiles]
    if fold_state is None:
        wys = read_state(ars)
    gms = yield from _each(
        lambda ar, t: _dot_nt(ar, jnp.concatenate([split(b_t[t[0], t[1]]), split(k_t[t[0], t[1]])], axis=0)),
        ars, tiles)
    gas = [jnp.where(strict, gm[0:C], 0.0) for gm in gms]
    grs = [jnp.where(incl, gm[C:2 * C], 0.0) for gm in gms]

    def blockdiag(x):
        return jnp.concatenate([jnp.where(left, x, 0.0), jnp.where(left, 0.0, x)], axis=0)

    ts = [eye2 + ga[:, 0:2 * C] for ga in gas]
    n_sq = TS.bit_length() - 2
    if n_sq >= 1:
        pws = yield from _each(lambda ga: _dot(ga[:, 0:2 * C], blockdiag(ga[:, 0:2 * C])), gas)
        for _ in range(n_sq - 1):
            tps = yield from _each(lambda t, pw: _dot(jnp.concatenate([t, pw], axis=0), blockdiag(pw)), ts, pws)
            ts = [t + tp[0:C] for t, tp in zip(ts, tps)]
            pws = [tp[C:2 * C] for tp in tps]
        ts = yield from _each(lambda t, pw: t + _dot(t, blockdiag(pw)), ts, pws)
    wis = yield from _each(lambda ga, v2: _dot(ga[:, 2 * C:4 * C], v2), gas, v2s)
    bks = [jnp.concatenate([b_g[rs, sl], k_g[rs, sl]], axis=0) for rs, sl in tiles]
    if fold_state is None:
        us = [_dot(t, split(wi + wy[0:C])) for t, wi, wy in zip(ts, wis, wys)]
        ys = [wy[C:2 * C] + _dot(gr, jnp.concatenate([split(u), v2], axis=0))
              for wy, gr, u, v2 in zip(wys, grs, us, v2s)]
        update_state([jnp.concatenate([u, v[rs, sl]], axis=0) for u, (rs, sl) in zip(us, tiles)], bks, dec)
    else:
        zc = jnp.zeros((C, LANES), F32)
        aus = yield from _each(
            lambda t, wi, tl: _dot(t, jnp.concatenate([split(a_t[tl[0], tl[1]]), split(wi)], axis=1)),
            ts, wis, tiles)
        rys = yield from _each(
            lambda gr, au, v2: _dot(gr, jnp.concatenate(
                [jnp.concatenate([split(au[:, 0:LANES]), split(au[:, LANES:2 * LANES])], axis=1),
                 jnp.concatenate([jnp.concatenate([zc, zc], axis=0), v2], axis=1)], axis=0)),
            grs, aus, v2s)
        mns = yield from _each(
            lambda bk, au, tl: _dot_tn(bk, jnp.concatenate(
                [au, jnp.concatenate([zc, v[tl[0], tl[1]]], axis=1)], axis=0)),
            bks, aus, tiles)
        ys = fold_state([r_t[rs, sl] + ry[:, 0:LANES] for ry, (rs, sl) in zip(rys, tiles)],
                        [ry[:, LANES:2 * LANES] for ry in rys], mns, dec)

    yield
    y = jnp.concatenate([jnp.concatenate(ys[i * RWKV_PAIRS:(i + 1) * RWKV_PAIRS], axis=1)
                         for i in range(R // C)], axis=0)
    mean = _segsum(y, seg_ref) * (1.0 / RWKV_N)
    d = y - mean
    var = _segsum(d * d, seg_ref) * (1.0 / RWKV_N)
    yn = d * lax.rsqrt(var + GN_EPS_RWKV) * lnw_ref[...] + lnb_ref[...]
    return ((yn + bonus_rk * v) * g).astype(BF16)


def _rwkv_prompt_kernel(*refs, batch):
    zr_refs = refs[:batch]
    zl_refs = refs[batch:2 * batch]
    prm = refs[2 * batch:2 * batch + 13]
    o_ref, s_out_ref, prevr_ref, prevl_ref, h_ref = refs[2 * batch + 13:]
    C = RWKV_CHUNK
    c = pl.program_id(0)

    @pl.when(c == 0)
    def _():
        prevr_ref[...] = jnp.zeros_like(prevr_ref)
        prevl_ref[...] = jnp.zeros_like(prevl_ref)
        h_ref[...] = jnp.zeros_like(h_ref)

    ki = lax.broadcasted_iota(jnp.int32, (LANES, LANES), 0)
    vi = lax.broadcasted_iota(jnp.int32, (LANES, LANES), 1)
    diag_blocks = (ki < RWKV_N) == (vi < RWKV_N)

    def group(i0):
        seqs = range(i0, i0 + RWKV_PROMPT_GROUP)
        tiles = [(i, p) for i in seqs for p in range(RWKV_PAIRS)]
        zr = jnp.concatenate([zr_refs[i][...] for i in seqs], axis=0)
        zl = jnp.concatenate([zl_refs[i][...] for i in seqs], axis=0)
        rows = lambda ref: jnp.concatenate(
            [jnp.broadcast_to(ref[i:i + 1, :], (C, ref.shape[1])) for i in seqs], axis=0)

        def fold_state(r_hats, y_inds, mns, dec):
            dec_cols = [jnp.broadcast_to(dec[(i - i0) * C:(i - i0) * C + 1, p * LANES:(p + 1) * LANES],
                                         (LANES, LANES)).T for i, p in tiles]
            hs = [h_ref[i, p] for i, p in tiles]
            res = [_dot(jnp.concatenate([rh, jnp.where(diag_blocks, mn[:, 0:LANES], 0.0)], axis=0), h)
                   for rh, mn, h in zip(r_hats, mns, hs)]
            for (i, p), h, dc, rs, mn in zip(tiles, hs, dec_cols, res, mns):
                h_ref[i, p] = h * dc + rs[C:C + LANES] + jnp.where(diag_blocks, mn[:, LANES:2 * LANES], 0.0)
            return [rs[0:C] + yi for rs, yi in zip(res, y_inds)]

        y = yield from _rwkv_chunk(zr, zl, rows(prevr_ref), rows(prevl_ref), prm, 1, fold_state=fold_state)
        for n, i in enumerate(seqs):
            o_ref[i] = y[n * C:(n + 1) * C]
            prevr_ref[i:i + 1, :] = zr[(n + 1) * C - 1:(n + 1) * C, :]
            prevl_ref[i:i + 1, :] = zl[(n + 1) * C - 1:(n + 1) * C, :]

    _run_staged([group(i0) for i0 in range(0, batch, RWKV_PROMPT_GROUP)], RWKV_PROMPT_LEAD)

    @pl.when(c == pl.num_programs(0) - 1)
    def _():
        for i in range(batch):
            for p in range(RWKV_PAIRS):
                s_out_ref[i, p] = h_ref[i, p].T


def _rwkv_sample_kernel(zr_ref, zl_ref, shr_ref, shl_ref, s0_ref, *rest, n_seq):
    prm = rest[:13]
    o_ref, s_out_ref = rest[13:]
    C = RWKV_CHUNK
    N = RWKV_N
    TS = C // n_seq
    tokseq = (lax.broadcasted_iota(jnp.int32, (1, LANES), 1) % C) // TS

    def pick(x, j):
        return jnp.where(tokseq == j, x, 0.0)

    def stacked(h):
        return s0_ref[:, h].reshape(n_seq * N, N)

    def read_state(ars):
        res = [[_dot_nt(stacked(2 * p + e), ar[:, e * N:(e + 1) * N]) for e in range(2)]
               for p, ar in enumerate(ars)]
        out = []
        for rp in res:
            halves = []
            for re in rp:
                acc = pick(re[0:N], 0)
                for j in range(1, n_seq):
                    acc = acc + pick(re[j * N:(j + 1) * N], j)
                halves.append(acc)
            out.append(jnp.concatenate(halves, axis=0).T)
        return out

    def update_state(uvs, bks, dec):
        uvts = [uv.T for uv in uvs]
        upds = [[_dot(jnp.concatenate([pick(uvt[e * N:(e + 1) * N], j) for j in range(n_seq)], axis=0),
                      bk[:, e * N:(e + 1) * N]) for e in range(2)]
                for uvt, bk in zip(uvts, bks)]
        for p, up in enumerate(upds):
            for e, upd in enumerate(up):
                h = 2 * p + e
                dec_h = jnp.concatenate(
                    [jnp.broadcast_to(dec[j * TS:j * TS + 1, h * N:(h + 1) * N], (N, N)) for j in range(n_seq)],
                    axis=0)
                s_out_ref[:, h] = (stacked(h) * dec_h + upd).reshape(n_seq, N, N)

    (o_ref[...],) = _run_staged([_rwkv_chunk(zr_ref[...], zl_ref[...], shr_ref[...], shl_ref[...], prm, n_seq,
                                             read_state=read_state, update_state=update_state)], 0)


def _rwkv_param_specs():
    W = RWKV_W
    shapes = [(1, 3 * W), (1, LORA_PAD), (1, W), (LANES, W), (1, W), (LANES, W), (LORA_PAD - LANES, W),
              (1, W), (1, W), (1, W), (1, W), (1, W), (LANES, LANES)]
    return [pl.BlockSpec(s, lambda *_: (0, 0)) for s in shapes]


def _rwkv_prompt(z, zl, params, batch, seq):
    C = RWKV_CHUNK
    W = RWKV_W
    nc = seq // C
    zr_spec = lambda i: pl.BlockSpec((C, 3 * W), lambda c: (i * nc + c, ZS // (3 * W)))
    zl_spec = lambda i: pl.BlockSpec((C, LORA_PAD), lambda c: (i * nc + c, 0))
    return pl.pallas_call(
        functools.partial(_rwkv_prompt_kernel, batch=batch),
        grid=(nc,),
        in_specs=[zr_spec(i) for i in range(batch)] + [zl_spec(i) for i in range(batch)]
        + _rwkv_param_specs(),
        out_specs=[
            pl.BlockSpec((batch, C, W), lambda c: (0, c, 0)),
            pl.BlockSpec((batch, RWKV_PAIRS, LANES, LANES), lambda c: (0, 0, 0, 0)),
        ],
        out_shape=[
            jax.ShapeDtypeStruct((batch, seq, W), BF16),
            jax.ShapeDtypeStruct((batch, RWKV_PAIRS, LANES, LANES), F32),
        ],
        scratch_shapes=[pltpu.VMEM((batch, 3 * W), F32), pltpu.VMEM((batch, LORA_PAD), F32),
                        pltpu.VMEM((batch, RWKV_PAIRS, LANES, LANES), F32)],
        compiler_params=_cparams(("arbitrary",)),
        name="rwkv_prompt",
    )(*([z] * batch), *([zl] * batch), *params)


def _rwkv_sample(z, zl, sh_r, sh_l, s0, params, row0, batch, seq):
    C = RWKV_CHUNK
    W = RWKV_W
    n_seq = C // seq
    rb0 = row0 // C
    return pl.pallas_call(
        functools.partial(_rwkv_sample_kernel, n_seq=n_seq),
        grid=(batch // n_seq,),
        in_specs=[
            pl.BlockSpec((C, 3 * W), lambda i: (rb0 + i, ZS // (3 * W))),
            pl.BlockSpec((C, LORA_PAD), lambda i: (rb0 + i, 0)),
            pl.BlockSpec((C, 3 * W), lambda i: (i, 0)),
            pl.BlockSpec((C, LORA_PAD), lambda i: (i, 0)),
            pl.BlockSpec((n_seq, RWKV_HEADS, RWKV_N, RWKV_N), lambda i: (i, 0, 0, 0)),
        ] + _rwkv_param_specs(),
        out_specs=[
            pl.BlockSpec((C, W), lambda i: (i, 0)),
            pl.BlockSpec((n_seq, RWKV_HEADS, RWKV_N, RWKV_N), lambda i: (i, 0, 0, 0)),
        ],
        out_shape=[
            jax.ShapeDtypeStruct((batch * seq, W), BF16),
            jax.ShapeDtypeStruct((batch, RWKV_HEADS, RWKV_N, RWKV_N), F32),
        ],
        compiler_params=_cparams(("parallel",)),
        name="rwkv_sample",
    )(z, zl, sh_r, sh_l, s0, *params)


def _pair_unblock(s):
    b = s.shape[0]
    lo = s[:, :, :RWKV_N, :RWKV_N]
    hi = s[:, :, RWKV_N:, RWKV_N:]
    return jnp.stack([lo, hi], axis=2).reshape(b, RWKV_HEADS, RWKV_N, RWKV_N)


def _mix_out_kernel(oret_a_ref, oret_b_ref, orwkv_a_ref, orwkv_b_ref, xa_ref, xb_ref, gr_ref, gw_ref,
                    wr_ref, ww_ref, wo_ref, gn_ref, o_ref, hn_ref, *, tiles_a):
    first = pl.program_id(0) < tiles_a
    o_ret = jnp.where(first, oret_a_ref[...], oret_b_ref[...])
    o_rwkv = jnp.where(first, orwkv_a_ref[...], orwkv_b_ref[...])
    a = jnp.dot(o_ret, wr_ref[...], preferred_element_type=F32)
    b = jnp.dot(o_rwkv, ww_ref[...], preferred_element_type=F32)
    merged = jax.nn.sigmoid(gr_ref[...]) * a + jax.nn.sigmoid(gw_ref[...]) * b
    x2 = (jnp.where(first, xa_ref[...], xb_ref[...])
          + jnp.dot(merged.astype(BF16), wo_ref[...], preferred_element_type=F32))
    o_ref[...] = x2
    hn_ref[...] = _rms(x2, gn_ref[...], NORM_EPS).astype(BF16)


def _mix_out(o_rets, o_rwkvs, xs, zg, w_ret_o, w_rwkv_o, w_out, g_next):
    n = xs[0].shape[0] + xs[1].shape[0]
    tm = TM_MIX
    tiles_a = o_rets[0].shape[0] // tm
    resident = lambda shape: pl.BlockSpec(shape, lambda i: (0, 0), pipeline_mode=pl.Buffered(1))
    first = lambda i: (jnp.minimum(i, tiles_a - 1), 0)
    second = lambda i: (jnp.maximum(i - tiles_a, 0), 0)
    return pl.pallas_call(
        functools.partial(_mix_out_kernel, tiles_a=tiles_a),
        grid=(n // tm,),
        in_specs=[
            pl.BlockSpec((tm, RET_V), first),
            pl.BlockSpec((tm, RET_V), second),
            pl.BlockSpec((tm, RWKV_W), first),
            pl.BlockSpec((tm, RWKV_W), second),
            pl.BlockSpec((tm, D_MODEL), first),
            pl.BlockSpec((tm, D_MODEL), second),
            pl.BlockSpec((tm, D_MODEL), lambda i: (i, ZGR // D_MODEL)),
            pl.BlockSpec((tm, D_MODEL), lambda i: (i, ZGW // D_MODEL)),
            resident((RET_V, D_MODEL)),
            resident((RWKV_W, D_MODEL)),
            resident((D_MODEL, D_MODEL)),
            pl.BlockSpec((1, D_MODEL), lambda i: (0, 0)),
        ],
        out_specs=[pl.BlockSpec((tm, D_MODEL), lambda i: (i, 0)), pl.BlockSpec((tm, D_MODEL), lambda i: (i, 0))],
        out_shape=[jax.ShapeDtypeStruct((n, D_MODEL), F32), jax.ShapeDtypeStruct((n, D_MODEL), BF16)],
        compiler_params=_cparams(("arbitrary",)),
        name="mix_out",
    )(*o_rets, *o_rwkvs, *xs, zg, zg, w_ret_o, w_rwkv_o, w_out, g_next)


def kernel(x_prompt, x_sample, state_ret, state_rwkv, state_shift, norm_ffn1, ffn1_w_gu, ffn1_w_down, norm_mix, w_in, w_ret_o, rwkv_mu, rwkv_w0, rwkv_w_up, rwkv_a0, rwkv_a_up, rwkv_g_up, rwkv_k_k, rwkv_k_a, rwkv_r_k, rwkv_ln_w, rwkv_ln_b, w_rwkv_o, w_out, norm_ffn2, ffn2_w_gu, ffn2_w_down, norm_final):
    bp, tp, d = x_prompt.shape
    bs, ts, _ = x_sample.shape
    n_p = bp * tp
    n_s = bs * ts
    W = RWKV_W
    row = lambda a: a.reshape(1, -1)
    assert d == D_MODEL and n_p % TM == 0 and n_s == TM and n_p % TM_MIX == 0
    assert tp % RET_CHUNK == 0 and tp % RWKV_CHUNK == 0 and bp % RWKV_PROMPT_GROUP == 0 and bp % RET_PROMPT_SEQS == 0
    assert RWKV_CHUNK % ts == 0 and n_s % RWKV_CHUNK == 0 and RET_SAMPLE_ROWS % ts == 0 and n_s % RET_SAMPLE_ROWS == 0

    x1_s, h_s, *w_ffn1 = _ffn(x_sample.reshape(n_s, d), 0, n_s, row(norm_ffn1[0]),
                              (ffn1_w_gu[0], ffn1_w_down[0]), row(norm_mix[0]), final_norm=False, emit_h=True)
    x1_p, h_p = _ffn(x_prompt.reshape(n_p, d), 0, n_p, row(norm_ffn1[0]), w_ffn1, row(norm_mix[0]),
                     final_norm=False, emit_h=True)

    w_in_t = jnp.swapaxes(w_in[0], 0, 1)
    z = _proj((h_p, h_s), w_in_t, 0, TN_PROJ_MAIN, Z_MAIN_W)
    zl = _proj((h_p, h_s), w_in_t, Z_MAIN_W, LORA_PAD, LORA_PAD)
    zg = _proj((h_p, h_s), w_in_t, Z_MAIN_W + LORA_W, TN_PROJ, 2 * D_MODEL)

    lg = jnp.log(1.0 - 2.0 ** (-5.0 - jnp.arange(RET_HEADS, dtype=F32)))
    cos_p, sin_p = _rope_tables(jnp.arange(tp, dtype=F32))
    cos_s, sin_s = _rope_tables(PAST_LEN + jnp.arange(ts, dtype=F32))
    rep = RET_SAMPLE_ROWS // ts
    oret_p, ret_p = _ret_prompt(z, lg, cos_p, sin_p, bp, tp)
    oret_s, ret_s = _ret_sample(z, lg, jnp.tile(cos_s, (rep, 1)), jnp.tile(sin_s, (rep, 1)),
                                state_ret[0], n_p, bs, ts)

    mu = rwkv_mu[0]
    pad_l = lambda a: jnp.pad(a, [(0, 0)] * (a.ndim - 1) + [(0, LORA_PAD - LORA_W)])
    seg = (jnp.arange(LANES)[:, None] // RWKV_N == jnp.arange(LANES)[None, :] // RWKV_N).astype(BF16)
    zero64 = jnp.zeros((DECAY_LORA, W), F32)
    params = (
        row(mu[:3 * W]), pad_l(row(mu[3 * W:])),
        row(rwkv_w0[0]), jnp.concatenate([rwkv_w_up[0], zero64], axis=0).astype(BF16),
        row(rwkv_a0[0]), jnp.concatenate([zero64, rwkv_a_up[0]], axis=0).astype(BF16),
        jnp.pad(rwkv_g_up[0], ((0, LORA_PAD - LANES - GATE_LORA), (0, 0))).astype(BF16),
        row(rwkv_k_k[0]), row(rwkv_k_a[0]), row(rwkv_r_k[0]), row(rwkv_ln_w[0]), row(rwkv_ln_b[0]),
        seg,
    )
    orw_p, rwkv_p = _rwkv_prompt(z, zl, params, bp, tp)
    sh_s = state_shift[0]
    first_rows = lambda a: jnp.pad(a[:, None, :], ((0, 0), (0, ts - 1), (0, 0))).reshape(n_s, -1)
    orw_s, rwkv_s = _rwkv_sample(z, zl, first_rows(sh_s[:, :3 * W]), first_rows(pad_l(sh_s[:, 3 * W:])),
                                 state_rwkv[0], params, n_p, bs, ts)

    x2, h_ffn2 = _mix_out((oret_p.reshape(n_p, RET_V), oret_s), (orw_p.reshape(n_p, W), orw_s), (x1_p, x1_s), zg,
                          w_ret_o[0].astype(BF16), w_rwkv_o[0].astype(BF16), w_out[0].astype(BF16),
                          row(norm_ffn2[0]))
    y_s, *w_ffn2 = _ffn(x2, n_p, n_s, row(norm_ffn2[0]), (ffn2_w_gu[0], ffn2_w_down[0]), row(norm_final),
                        final_norm=True, emit_h=False, h=h_ffn2)
    (y_p,) = _ffn(x2, 0, n_p, row(norm_ffn2[0]), w_ffn2, row(norm_final),
                  final_norm=True, emit_h=False, h=h_ffn2)

    def last_rows(firsts, t):
        rows = [f + t - 1 for f in firsts]
        zr_last = jnp.concatenate([z[r:r + 1, ZS:ZS + 3 * W] for r in rows], axis=0)
        zl_last = jnp.concatenate([zl[r:r + 1, :LORA_W] for r in rows], axis=0)
        return jnp.concatenate([zr_last, zl_last], axis=-1)[None]

    shift_s = jnp.concatenate([lax.slice(z, (n_p + ts - 1, ZS), (n_p + n_s, ZS + 3 * W), (ts, 1)),
                               lax.slice(zl, (n_p + ts - 1, 0), (n_p + n_s, LORA_W), (ts, 1))], axis=-1)[None]
    return (y_p.reshape(bp, tp, d), y_s.reshape(bs, ts, d),
            ret_p[None], _pair_unblock(rwkv_p)[None], last_rows(range(0, n_p, tp), tp),
            ret_s[None], rwkv_s[None], shift_s)
```

```python
import functools

import jax
import jax.numpy as jnp
from jax import lax
from jax.experimental import pallas as pl
from jax.experimental.pallas import tpu as pltpu

F32 = jnp.float32
BF16 = jnp.bfloat16

D_MODEL = 2048
PAST_LEN = 16384
RET_HEADS = 4
RET_DK = 256
RET_DV = 512
RET_CHUNK = 128
ROPE_BASE = 10000.0
RET_QK = RET_HEADS * RET_DK
RET_V = RET_HEADS * RET_DV
RWKV_HEADS = 16
RWKV_N = 64
RWKV_W = RWKV_HEADS * RWKV_N
DECAY_LORA = 64
AAA_LORA = 64
GATE_LORA = 160
LORA_W = DECAY_LORA + AAA_LORA + GATE_LORA
LORA_PAD = 384
D_FF = 5632
NORM_EPS = 1e-6
GN_EPS_RET = 1e-6
GN_EPS_RWKV = 64e-5

ZQ, ZK, ZV, ZG, ZS = 0, 1024, 2048, 4096, 6144
Z_MAIN_W = 9216
ZGR, ZGW = 0, 2048

LANES = 128
SUBLANES = 8
MXU_TILE = 256
RWKV_PAIRS = RWKV_W // LANES
RWKV_CHUNK = 64
RWKV_PROMPT_GROUP = 2
RWKV_PROMPT_LEAD = 21

V7X_VMEM_BYTES = 64 * 1024 * 1024
VMEM_LIMIT = V7X_VMEM_BYTES - 8 * 1024 * 1024

TM = 512
TF = 512
TF_CAST = 256
TN_PROJ_MAIN = 1536
TN_PROJ_GATES = 2048
TM_MIX = 256


def _rms(x, g, eps):
    return x * lax.rsqrt(jnp.mean(x * x, axis=-1, keepdims=True) + eps) * g


def _dot(a, b):
    return jnp.dot(a.astype(BF16), b.astype(BF16), preferred_element_type=F32)


def _dot_nt(a, b):
    return lax.dot_general(a.astype(BF16), b.astype(BF16), (((1,), (1,)), ((), ())),
                           preferred_element_type=F32)


def _dot_tn(a, b):
    return lax.dot_general(a.astype(BF16), b.astype(BF16), (((0,), (0,)), ((), ())),
                           preferred_element_type=F32)


def _bf16_terms(x, n):
    terms = []
    for _ in range(n):
        t = x.astype(BF16)
        terms.append(t)
        x = x - t.astype(F32)
    return terms


def _cparams(sem):
    return pltpu.CompilerParams(dimension_semantics=sem, vmem_limit_bytes=VMEM_LIMIT)


def _ffn_kernel(*refs, final_norm, emit_h, h_in, cast_w):
    if h_in:
        hin_ref, refs = refs[0], refs[1:]
    x_ref, g_ref, wg_ref, wu_ref, wd_ref, gf_ref = refs[:6]
    outs = list(refs[6:-2])
    h_ref, acc_ref = refs[-2:]
    o_ref = outs.pop(0)
    hn_ref = outs.pop(0) if emit_h else None
    j = pl.program_id(1)

    @pl.when(j == 0)
    def _():
        if not h_in:
            h_ref[...] = _rms(x_ref[...], g_ref[...], NORM_EPS).astype(BF16)
        acc_ref[...] = jnp.zeros_like(acc_ref)

    wg, wu, wd = wg_ref[...], wu_ref[...], wd_ref[...]
    if cast_w:
        wg, wu, wd = wg.astype(BF16), wu.astype(BF16), wd.astype(BF16)
        for w_out_ref, w in zip(outs, (wg, wu, wd)):
            w_out_ref[...] = w
    h = hin_ref[...] if h_in else h_ref[...]
    tf = wg.shape[1]
    half = tf // 2 if tf // 2 >= MXU_TILE else tf
    cols = tuple(slice(c, c + half) for c in range(0, tf, half))
    gus = [(jnp.dot(h, wg[:, c], preferred_element_type=F32), jnp.dot(h, wu[:, c], preferred_element_type=F32))
           for c in cols]
    downs = [jnp.dot(((g * jax.nn.sigmoid(g)) * u).astype(BF16), wd[c, :], preferred_element_type=F32)
             for (g, u), c in zip(gus, cols)]
    acc_ref[...] += sum(downs[1:], downs[0])

    @pl.when(j == pl.num_programs(1) - 1)
    def _():
        y = x_ref[...] + 0.5 * acc_ref[...]
        if final_norm:
            y = _rms(y, gf_ref[...], NORM_EPS)
        if emit_h:
            hn_ref[...] = _rms(y, gf_ref[...], NORM_EPS).astype(BF16)
        o_ref[...] = y


def _ffn(x, row0, rows, g, weights, g_after, final_norm, emit_h, h=None):
    cast_w = len(weights) == 2
    tf = TF_CAST if cast_w else TF
    nj = D_FF // tf
    t0 = row0 // TM
    tile = pl.BlockSpec((TM, D_MODEL), lambda i, j: (t0 + i, 0))
    out_tile = pl.BlockSpec((TM, D_MODEL), lambda i, j: (i, 0))
    h_in = h is not None
    w_specs = [pl.BlockSpec((D_MODEL, tf), lambda i, j: (0, j)),
               pl.BlockSpec((D_MODEL, tf), (lambda i, j: (0, j + nj)) if cast_w else (lambda i, j: (0, j))),
               pl.BlockSpec((tf, D_MODEL), lambda i, j: (j, 0))]
    w_args = (weights[0], weights[0], weights[1]) if cast_w else weights
    out_specs = [out_tile] + ([out_tile] if emit_h else [])
    out_shape = [jax.ShapeDtypeStruct((rows, D_MODEL), F32)] + (
        [jax.ShapeDtypeStruct((rows, D_MODEL), BF16)] if emit_h else [])
    if cast_w:
        out_specs += [pl.BlockSpec((D_MODEL, tf), lambda i, j: (0, j)),
                      pl.BlockSpec((D_MODEL, tf), lambda i, j: (0, j)),
                      pl.BlockSpec((tf, D_MODEL), lambda i, j: (j, 0))]
        out_shape += [jax.ShapeDtypeStruct((D_MODEL, D_FF), BF16), jax.ShapeDtypeStruct((D_MODEL, D_FF), BF16),
                      jax.ShapeDtypeStruct((D_FF, D_MODEL), BF16)]
        assert rows == TM, "each weight tile is written exactly once"
    return pl.pallas_call(
        functools.partial(_ffn_kernel, final_norm=final_norm, emit_h=emit_h, h_in=h_in, cast_w=cast_w),
        grid=(rows // TM, nj),
        in_specs=([tile] if h_in else []) + [tile, pl.BlockSpec((1, D_MODEL), lambda i, j: (0, 0))] + w_specs
        + [pl.BlockSpec((1, D_MODEL), lambda i, j: (0, 0))],
        out_specs=out_specs,
        out_shape=out_shape,
        scratch_shapes=[pltpu.VMEM((TM, D_MODEL), BF16), pltpu.VMEM((TM, D_MODEL), F32)],
        compiler_params=_cparams(("parallel", "arbitrary")),
        name="ffn_cast" if cast_w else "ffn",
    )(*((h,) if h_in else ()), x, g, *w_args, g_after)


def _proj_kernel(ha_ref, hb_ref, wt_ref, o_ref, *scratch, tiles_a):
    i = pl.program_id(1)
    h = jnp.where(i < tiles_a, ha_ref[...], hb_ref[...])
    if scratch:
        (w_ref,) = scratch

        @pl.when(i == 0)
        def _():
            w_ref[...] = wt_ref[...].astype(BF16)

        o_ref[...] = _dot_nt(h, w_ref[...])
    else:
        o_ref[...] = _dot_nt(h, wt_ref[...])


def _proj(hs, w_t, col0, tn, nout):
    tiles_a = hs[0].shape[0] // TM
    n = hs[0].shape[0] + hs[1].shape[0]
    assert col0 % tn == 0 and nout % tn == 0
    return pl.pallas_call(
        functools.partial(_proj_kernel, tiles_a=tiles_a),
        grid=(nout // tn, n // TM),
        in_specs=[
            pl.BlockSpec((TM, D_MODEL), lambda j, i: (jnp.minimum(i, tiles_a - 1), 0)),
            pl.BlockSpec((TM, D_MODEL), lambda j, i: (jnp.maximum(i - tiles_a, 0), 0)),
            pl.BlockSpec((tn, D_MODEL), lambda j, i: (col0 // tn + j, 0)),
        ],
        out_specs=pl.BlockSpec((TM, tn), lambda j, i: (i, j)),
        out_shape=jax.ShapeDtypeStruct((n, nout), F32),
        scratch_shapes=[pltpu.VMEM((tn, D_MODEL), BF16)] if w_t.dtype == F32 else [],
        compiler_params=_cparams(("parallel", "arbitrary")),
        name="proj",
    )(*hs, w_t)


def _pair_swap(x):
    n = x.shape[-1]
    lane = lax.broadcasted_iota(jnp.int32, x.shape, x.ndim - 1)
    prev = pltpu.roll(x, 1, x.ndim - 1)
    nxt = pltpu.roll(x, n - 1, x.ndim - 1)
    return jnp.where((lane & 1) == 1, prev, nxt)


def _rotary(x, cos, sin_signed):
    return x * cos + _pair_swap(x) * sin_signed


def _ret_finish(o, zg):
    o = o * lax.rsqrt(jnp.mean(o * o, axis=-1, keepdims=True) + GN_EPS_RET)
    return o * (zg * jax.nn.sigmoid(zg))


RET_PROMPT_SEQS = 2


def _ret_prompt_kernel(lg_ref, *refs):
    nb = RET_PROMPT_SEQS
    q_refs, k_refs, v_refs, g_refs = (refs[i * nb:(i + 1) * nb] for i in range(4))
    cos_ref, sin_ref, o_ref, s_out_ref, s_ref = refs[4 * nb:]
    c = pl.program_id(1)
    C = RET_CHUNK

    @pl.when(c == 0)
    def _():
        s_ref[...] = jnp.zeros_like(s_ref)

    cos = cos_ref[...]
    sin = sin_ref[...]
    ti = lax.broadcasted_iota(jnp.int32, (C, C), 0)
    tj = lax.broadcasted_iota(jnp.int32, (C, C), 1)
    rel = (ti - tj).astype(F32)
    idx = lax.broadcasted_iota(jnp.int32, (C, 1), 0).astype(F32)
    H = range(RET_HEADS)
    lgs = [lg_ref[h] for h in H]
    finals = []

    def sequence(b):
        qs = [_rotary(q_refs[b][:, h * RET_DK:(h + 1) * RET_DK], cos, sin) for h in H]
        ks = [_rotary(k_refs[b][:, h * RET_DK:(h + 1) * RET_DK], cos, sin) * (RET_DK ** -0.5) for h in H]
        yield
        vs = [v_refs[b][:, h * RET_DV:(h + 1) * RET_DV] for h in H]
        ss = [s_ref[b, h] for h in H]
        scores = [_dot_nt(q, k) * jnp.where(rel >= 0, jnp.exp(lg * jnp.maximum(rel, 0.0)), 0.0)
                  for q, k, lg in zip(qs, ks, lgs)]
        crosses = [_dot(q, s) * jnp.exp(lg * (idx + 1.0)) for q, s, lg in zip(qs, ss, lgs)]
        inners = [_dot(sc, v) for sc, v in zip(scores, vs)]
        s_news = [jnp.exp(lg * C) * s + _dot_tn(k * jnp.exp(lg * (C - 1.0 - idx)), v)
                  for lg, s, k, v in zip(lgs, ss, ks, vs)]
        yield
        for h in H:
            s_ref[b, h] = s_news[h]
            o_ref[b, :, h * RET_DV:(h + 1) * RET_DV] = _ret_finish(
                inners[h] + crosses[h], g_refs[b][:, h * RET_DV:(h + 1) * RET_DV]).astype(BF16)
        finals.append((b, s_news))

    _run_staged([sequence(b) for b in range(nb)], 1)

    @pl.when(c == pl.num_programs(1) - 1)
    def _():
        for b, s_news in finals:
            for h in H:
                s_out_ref[b, h] = s_news[h]


def _ret_prompt(z, lg, cos, sin, batch, seq):
    nc = seq // RET_CHUNK
    C = RET_CHUNK
    nb = RET_PROMPT_SEQS
    cols = lambda width, col0: [pl.BlockSpec((C, width), lambda g, c, b=b: ((g * nb + b) * nc + c, col0 // width))
                                for b in range(nb)]
    return pl.pallas_call(
        _ret_prompt_kernel,
        grid=(batch // nb, nc),
        in_specs=[pl.BlockSpec(memory_space=pltpu.SMEM)]
        + cols(RET_QK, ZQ) + cols(RET_QK, ZK) + cols(RET_V, ZV) + cols(RET_V, ZG)
        + [pl.BlockSpec((C, RET_DK), lambda g, c: (c, 0)), pl.BlockSpec((C, RET_DK), lambda g, c: (c, 0))],
        out_specs=[
            pl.BlockSpec((nb, C, RET_V), lambda g, c: (g, c, 0)),
            pl.BlockSpec((nb, RET_HEADS, RET_DK, RET_DV), lambda g, c: (g, 0, 0, 0)),
        ],
        out_shape=[
            jax.ShapeDtypeStruct((batch, seq, RET_V), BF16),
            jax.ShapeDtypeStruct((batch, RET_HEADS, RET_DK, RET_DV), F32),
        ],
        scratch_shapes=[pltpu.VMEM((nb, RET_HEADS, RET_DK, RET_DV), F32)],
        compiler_params=_cparams(("parallel", "arbitrary")),
        name="ret_prompt",
    )(lg, *([z] * (4 * nb)), cos, sin)


RET_SAMPLE_ROWS = 16


def _ret_sample_kernel(lg_ref, q_ref, k_ref, v_ref, g_ref, cos_ref, sin_ref, s0_ref,
                       o_ref, s_out_ref, *, seq):
    R = RET_SAMPLE_ROWS
    nb = R // seq
    cos = cos_ref[...]
    sin = sin_ref[...]
    ri = lax.broadcasted_iota(jnp.int32, (R, R), 0)
    rj = lax.broadcasted_iota(jnp.int32, (R, R), 1)
    rel = ((ri % seq) - (rj % seq)).astype(F32)
    ok = ((ri // seq) == (rj // seq)) & (rel >= 0)
    row = lax.broadcasted_iota(jnp.int32, (R, 1), 0)
    t = (row % seq).astype(F32)

    for h in range(RET_HEADS):
        lg = lg_ref[h]
        q = _rotary(q_ref[:, h * RET_DK:(h + 1) * RET_DK], cos, sin)
        k = _rotary(k_ref[:, h * RET_DK:(h + 1) * RET_DK], cos, sin) * (RET_DK ** -0.5)
        v = v_ref[:, h * RET_DV:(h + 1) * RET_DV]
        mask = jnp.where(ok, jnp.exp(lg * jnp.maximum(rel, 0.0)), 0.0)
        q_dec = jnp.exp(lg * (t + 1.0))
        kd = k * jnp.exp(lg * (seq - 1.0 - t))
        inner = _dot(_dot_nt(q, k) * mask, v)
        cross = jnp.zeros_like(inner)
        for b in range(nb):
            s = s0_ref[b, h]
            mine = (row // seq) == b
            cross = jnp.where(mine, _dot(q, s), cross)
            s_out_ref[b, h] = jnp.exp(lg * seq) * s + _dot_tn(jnp.where(mine, kd, 0.0), v)
        o_ref[:, h * RET_DV:(h + 1) * RET_DV] = _ret_finish(
            inner + cross * q_dec, g_ref[:, h * RET_DV:(h + 1) * RET_DV]).astype(BF16)


def _ret_sample(z, lg, cos, sin, s0, row0, batch, seq):
    R = RET_SAMPLE_ROWS
    nb = R // seq
    rb0 = row0 // R
    return pl.pallas_call(
        functools.partial(_ret_sample_kernel, seq=seq),
        grid=(batch // nb,),
        in_specs=[
            pl.BlockSpec(memory_space=pltpu.SMEM),
            pl.BlockSpec((R, RET_QK), lambda i: (rb0 + i, ZQ // RET_QK)),
            pl.BlockSpec((R, RET_QK), lambda i: (rb0 + i, ZK // RET_QK)),
            pl.BlockSpec((R, RET_V), lambda i: (rb0 + i, ZV // RET_V)),
            pl.BlockSpec((R, RET_V), lambda i: (rb0 + i, ZG // RET_V)),
            pl.BlockSpec((R, RET_DK), lambda i: (0, 0)),
            pl.BlockSpec((R, RET_DK), lambda i: (0, 0)),
            pl.BlockSpec((nb, RET_HEADS, RET_DK, RET_DV), lambda i: (i, 0, 0, 0)),
        ],
        out_specs=[
            pl.BlockSpec((R, RET_V), lambda i: (i, 0)),
            pl.BlockSpec((nb, RET_HEADS, RET_DK, RET_DV), lambda i: (i, 0, 0, 0)),
        ],
        out_shape=[
            jax.ShapeDtypeStruct((batch * seq, RET_V), BF16),
            jax.ShapeDtypeStruct((batch, RET_HEADS, RET_DK, RET_DV), F32),
        ],
        compiler_params=_cparams(("parallel",)),
        name="ret_sample",
    )(lg, z, z, z, z, cos, sin, s0)


def _rope_tables(pos):
    half = RET_DK // 2
    inv = 1.0 / (ROPE_BASE ** jnp.linspace(0.0, 1.0, half, dtype=F32))
    ang = pos[:, None] * inv[None, :]
    cos = jnp.repeat(jnp.cos(ang), 2, axis=-1)
    sin = jnp.sin(ang)
    sin_signed = jnp.stack([-sin, sin], axis=-1).reshape(pos.shape[0], RET_DK)
    return cos, sin_signed


def _segsum(x, seg_ref):
    rows = x.shape[0]
    n = rows * RWKV_PAIRS
    xs = jnp.concatenate([x[:, p * LANES:(p + 1) * LANES] for p in range(RWKV_PAIRS)], axis=0)
    ss = jnp.dot(jnp.concatenate(_bf16_terms(xs, 2), axis=0), seg_ref[...], preferred_element_type=F32)
    ss = ss[0:n] + ss[n:2 * n]
    return jnp.concatenate([ss[p * rows:(p + 1) * rows] for p in range(RWKV_PAIRS)], axis=1)


def _run_staged(gens, lead):
    out = [None] * len(gens)
    live = [True] * len(gens)

    def step(i):
        try:
            next(gens[i])
        except StopIteration as stop:
            out[i] = stop.value
            live[i] = False

    for _ in range(lead):
        if live[0]:
            step(0)
    while any(live):
        for i in range(len(gens)):
            if live[i]:
                step(i)
    return out


def _each(fn, *lists):
    out = []
    for args in zip(*lists):
        out.append(fn(*args))
        yield
    return out


def _rwkv_chunk(zr, zl, prev_r, prev_l, prm, n_seq, **state_fns):
    front = yield from _rwkv_front(zr, zl, prev_r, prev_l, prm, n_seq)
    return (yield from _rwkv_back(*front, prm, n_seq, **state_fns))


def _rwkv_front(zr, zl, prev_r, prev_l, prm, n_seq):
    (mur_ref, mul_ref, w0_ref, wup_ref, a0_ref, aup_ref, gup_ref,
     kk_ref, ka_ref, rk_ref, lnw_ref, lnb_ref, seg_ref) = prm
    C = RWKV_CHUNK
    W = RWKV_W
    R = zr.shape[0]
    TS = C // n_seq
    row = lax.broadcasted_iota(jnp.int32, (R, 1), 0)
    first = (row % TS) == 0

    def mix(z, prev, mu_ref):
        z_prev = jnp.where(first, prev, pltpu.roll(z, 1, 0))
        return z + (z_prev - z) * mu_ref[...]

    xs_r = mix(zr[:, 0:W], prev_r[:, 0:W], mur_ref.at[:, 0:W])
    yield
    xs_k = mix(zr[:, W:2 * W], prev_r[:, W:2 * W], mur_ref.at[:, W:2 * W])
    yield
    xs_v = mix(zr[:, 2 * W:3 * W], prev_r[:, 2 * W:3 * W], mur_ref.at[:, 2 * W:3 * W])
    yield
    xl = mix(zl, prev_l, mul_ref)
    r, k, v = xs_r, xs_k, xs_v
    x_wa = xl[:, 0:LANES]
    x_g = xl[:, LANES:LORA_PAD]

    y_w = w0_ref[...] + _dot(jnp.tanh(x_wa), wup_ref[...])
    yield
    w_log = -(jnp.maximum(-y_w, 0.0) + jnp.log(1.0 + jnp.exp(-jnp.abs(y_w)))) - 0.5
    lw = -jnp.exp(w_log)
    yield
    a_rate = jax.nn.sigmoid(a0_ref[...] + _dot(x_wa, aup_ref[...]))
    yield
    g = _dot(jax.nn.sigmoid(x_g), gup_ref[...])
    yield

    kk = k * kk_ref[...]
    k = k * (1.0 + (a_rate - 1.0) * ka_ref[...])
    yield
    sums = _segsum(jnp.concatenate([kk * kk, r * k * rk_ref[...]], axis=0), seg_ref)
    yield
    kk = kk * lax.rsqrt(jnp.maximum(sums[0:R], 1e-24))
    bonus_rk = sums[R:2 * R]
    a = -kk
    b = kk * a_rate
    yield

    ti = lax.broadcasted_iota(jnp.int32, (2 * R, R), 0)
    tj = lax.broadcasted_iota(jnp.int32, (2 * R, R), 1)
    same = ((ti % R) // TS) == (tj // TS)
    sel = (same & ((ti >= R) | (tj <= ti))).astype(BF16)
    sums_lw = sum(jnp.dot(sel, t, preferred_element_type=F32) for t in _bf16_terms(lw, 3))
    yield
    cum = sums_lw[0:R]
    tot = sums_lw[R:2 * R]
    e_neg = jnp.exp(-cum)
    yield
    a_t = (a * jnp.exp(cum - lw)).astype(BF16)
    yield
    b_t = (b * e_neg).astype(BF16)
    yield
    k_t = (k * e_neg).astype(BF16)
    yield
    r_t = r * jnp.exp(cum)
    yield
    e_tail = jnp.exp(tot - cum)
    yield
    b_g = (b * e_tail).astype(BF16)
    yield
    k_g = (k * e_tail).astype(BF16)
    dec = jnp.exp(tot)
    yield
    return (a_t, b_t, k_t, b_g, k_g), (r_t, v, dec, g, bonus_rk)


def _rwkv_back(mm_operands, f32_arrays, prm, n_seq, read_state=None, update_state=None, fold_state=None):
    a_t, b_t, k_t, b_g, k_g = mm_operands
    r_t, v, dec, g, bonus_rk = f32_arrays
    lnw_ref, lnb_ref, seg_ref = prm[10:13]
    C = RWKV_CHUNK
    R = r_t.shape[0]
    TS = C // n_seq

    lane = lax.broadcasted_iota(jnp.int32, (1, LANES), 1)
    lo = lane < RWKV_N

    def split(x):
        return jnp.concatenate([jnp.where(lo, x, 0.0), jnp.where(lo, 0.0, x)], axis=0)

    row4 = lax.broadcasted_iota(jnp.int32, (C, 4 * C), 0)
    col4 = lax.broadcasted_iota(jnp.int32, (C, 4 * C), 1) % C
    same4 = (row4 // TS) == (col4 // TS)
    strict = same4 & (col4 < row4)
    incl = same4 & (col4 <= row4)
    row2 = lax.broadcasted_iota(jnp.int32, (C, 2 * C), 0)
    col2 = lax.broadcasted_iota(jnp.int32, (C, 2 * C), 1)
    eye2 = ((col2 % C) == row2).astype(F32)
    left = col2 < C

    tiles = [(slice(i * C, (i + 1) * C), slice(p * LANES, (p + 1) * LANES))
             for i in range(R // C) for p in range(RWKV_PAIRS)]
    v2s = [split(v[rs, sl]) for rs, sl in tiles]
    ars = [jnp.concatenate([a_t[rs, sl], r_t[rs, sl]], axis=0) for rs, sl in tiles]
    if fold_state is None:
        wys = read_state(ars)
    gms = yield from _each(
        lambda ar, t: _dot_nt(ar, jnp.concatenate([split(b_t[t[0], t[1]]), split(k_t[t[0], t[1]])], axis=0)),
        ars, tiles)
    gas = [jnp.where(strict, gm[0:C], 0.0) for gm in gms]
    grs = [jnp.where(incl, gm[C:2 * C], 0.0) for gm in gms]

    def blockdiag(x):
        return jnp.concatenate([jnp.where(left, x, 0.0), jnp.where(left, 0.0, x)], axis=0)

    ts = [eye2 + ga[:, 0:2 * C] for ga in gas]
    n_sq = TS.bit_length() - 2
    if n_sq >= 1:
        pws = yield from _each(lambda ga: _dot(ga[:, 0:2 * C], blockdiag(ga[:, 0:2 * C])), gas)
        for _ in range(n_sq - 1):
            tps = yield from _each(lambda t, pw: _dot(jnp.concatenate([t, pw], axis=0), blockdiag(pw)), ts, pws)
            ts = [t + tp[0:C] for t, tp in zip(ts, tps)]
            pws = [tp[C:2 * C] for tp in tps]
        ts = yield from _each(lambda t, pw: t + _dot(t, blockdiag(pw)), ts, pws)
    wis = yield from _each(lambda ga, v2: _dot(ga[:, 2 * C:4 * C], v2), gas, v2s)
    bks = [jnp.concatenate([b_g[rs, sl], k_g[rs, sl]], axis=0) for rs, sl in tiles]
    if fold_state is None:
        us = [_dot(t, split(wi + wy[0:C])) for t, wi, wy in zip(ts, wis, wys)]
        ys = [wy[C:2 * C] + _dot(gr, jnp.concatenate([split(u), v2], axis=0))
              for wy, gr, u, v2 in zip(wys, grs, us, v2s)]
        update_state([jnp.concatenate([u, v[rs, sl]], axis=0) for u, (rs, sl) in zip(us, tiles)], bks, dec)
    else:
        zc = jnp.zeros((C, LANES), F32)
        aus = yield from _each(
            lambda t, wi, tl: _dot(t, jnp.concatenate([split(a_t[tl[0], tl[1]]), split(wi)], axis=1)),
            ts, wis, tiles)
        rys = yield from _each(
            lambda gr, au, v2: _dot(gr, jnp.concatenate(
                [jnp.concatenate([split(au[:, 0:LANES]), split(au[:, LANES:2 * LANES])], axis=1),
                 jnp.concatenate([jnp.concatenate([zc, zc], axis=0), v2], axis=1)], axis=0)),
            grs, aus, v2s)
        mns = yield from _each(
            lambda bk, au, tl: _dot_tn(bk, jnp.concatenate(
                [au, jnp.concatenate([zc, v[tl[0], tl[1]]], axis=1)], axis=0)),
            bks, aus, tiles)
        ys = fold_state([r_t[rs, sl] + ry[:, 0:LANES] for ry, (rs, sl) in zip(rys, tiles)],
                        [ry[:, LANES:2 * LANES] for ry in rys], mns, dec)

    yield
    y = jnp.concatenate([jnp.concatenate(ys[i * RWKV_PAIRS:(i + 1) * RWKV_PAIRS], axis=1)
                         for i in range(R // C)], axis=0)
    mean = _segsum(y, seg_ref) * (1.0 / RWKV_N)
    d = y - mean
    var = _segsum(d * d, seg_ref) * (1.0 / RWKV_N)
    yn = d * lax.rsqrt(var + GN_EPS_RWKV) * lnw_ref[...] + lnb_ref[...]
    return ((yn + bonus_rk * v) * g).astype(BF16)


def _rwkv_prompt_kernel(*refs, batch):
    zr_refs = refs[:batch]
    zl_refs = refs[batch:2 * batch]
    prm = refs[2 * batch:2 * batch + 13]
    o_ref, s_out_ref, prevr_ref, prevl_ref, h_ref = refs[2 * batch + 13:]
    C = RWKV_CHUNK
    c = pl.program_id(0)

    @pl.when(c == 0)
    def _():
        prevr_ref[...] = jnp.zeros_like(prevr_ref)
        prevl_ref[...] = jnp.zeros_like(prevl_ref)
        h_ref[...] = jnp.zeros_like(h_ref)

    ki = lax.broadcasted_iota(jnp.int32, (LANES, LANES), 0)
    vi = lax.broadcasted_iota(jnp.int32, (LANES, LANES), 1)
    diag_blocks = (ki < RWKV_N) == (vi < RWKV_N)

    def group(i0):
        seqs = range(i0, i0 + RWKV_PROMPT_GROUP)
        tiles = [(i, p) for i in seqs for p in range(RWKV_PAIRS)]
        zr = jnp.concatenate([zr_refs[i][...] for i in seqs], axis=0)
        zl = jnp.concatenate([zl_refs[i][...] for i in seqs], axis=0)
        rows = lambda ref: jnp.concatenate(
            [jnp.broadcast_to(ref[i:i + 1, :], (C, ref.shape[1])) for i in seqs], axis=0)

        def fold_state(r_hats, y_inds, mns, dec):
            dec_cols = [jnp.broadcast_to(dec[(i - i0) * C:(i - i0) * C + 1, p * LANES:(p + 1) * LANES],
                                         (LANES, LANES)).T for i, p in tiles]
            hs = [h_ref[i, p] for i, p in tiles]
            res = [_dot(jnp.concatenate([rh, jnp.where(diag_blocks, mn[:, 0:LANES], 0.0)], axis=0), h)
                   for rh, mn, h in zip(r_hats, mns, hs)]
            for (i, p), h, dc, rs, mn in zip(tiles, hs, dec_cols, res, mns):
                h_ref[i, p] = h * dc + rs[C:C + LANES] + jnp.where(diag_blocks, mn[:, LANES:2 * LANES], 0.0)
            return [rs[0:C] + yi for rs, yi in zip(res, y_inds)]

        y = yield from _rwkv_chunk(zr, zl, rows(prevr_ref), rows(prevl_ref), prm, 1, fold_state=fold_state)
        for n, i in enumerate(seqs):
            o_ref[i] = y[n * C:(n + 1) * C]
            prevr_ref[i:i + 1, :] = zr[(n + 1) * C - 1:(n + 1) * C, :]
            prevl_ref[i:i + 1, :] = zl[(n + 1) * C - 1:(n + 1) * C, :]

    _run_staged([group(i0) for i0 in range(0, batch, RWKV_PROMPT_GROUP)], RWKV_PROMPT_LEAD)

    @pl.when(c == pl.num_programs(0) - 1)
    def _():
        for i in range(batch):
            for p in range(RWKV_PAIRS):
                s_out_ref[i, p] = h_ref[i, p].T


def _rwkv_sample_kernel(zr_ref, zl_ref, shr_ref, shl_ref, s0_ref, *rest, n_seq):
    prm = rest[:13]
    o_ref, s_out_ref = rest[13:]
    C = RWKV_CHUNK
    N = RWKV_N
    TS = C // n_seq
    tokseq = (lax.broadcasted_iota(jnp.int32, (1, LANES), 1) % C) // TS

    def pick(x, j):
        return jnp.where(tokseq == j, x, 0.0)

    def stacked(h):
        return s0_ref[:, h].reshape(n_seq * N, N)

    def read_state(ars):
        res = [[_dot_nt(stacked(2 * p + e), ar[:, e * N:(e + 1) * N]) for e in range(2)]
               for p, ar in enumerate(ars)]
        out = []
        for rp in res:
            halves = []
            for re in rp:
                acc = pick(re[0:N], 0)
                for j in range(1, n_seq):
                    acc = acc + pick(re[j * N:(j + 1) * N], j)
                halves.append(acc)
            out.append(jnp.concatenate(halves, axis=0).T)
        return out

    def update_state(uvs, bks, dec):
        uvts = [uv.T for uv in uvs]
        upds = [[_dot(jnp.concatenate([pick(uvt[e * N:(e + 1) * N], j) for j in range(n_seq)], axis=0),
                      bk[:, e * N:(e + 1) * N]) for e in range(2)]
                for uvt, bk in zip(uvts, bks)]
        for p, up in enumerate(upds):
            for e, upd in enumerate(up):
                h = 2 * p + e
                dec_h = jnp.concatenate(
                    [jnp.broadcast_to(dec[j * TS:j * TS + 1, h * N:(h + 1) * N], (N, N)) for j in range(n_seq)],
                    axis=0)
                s_out_ref[:, h] = (stacked(h) * dec_h + upd).reshape(n_seq, N, N)

    (o_ref[...],) = _run_staged([_rwkv_chunk(zr_ref[...], zl_ref[...], shr_ref[...], shl_ref[...], prm, n_seq,
                                             read_state=read_state, update_state=update_state)], 0)


def _rwkv_param_specs():
    W = RWKV_W
    shapes = [(1, 3 * W), (1, LORA_PAD), (1, W), (LANES, W), (1, W), (LANES, W), (LORA_PAD - LANES, W),
              (1, W), (1, W), (1, W), (1, W), (1, W), (LANES, LANES)]
    return [pl.BlockSpec(s, lambda *_: (0, 0)) for s in shapes]


def _rwkv_prompt(z, zl, params, batch, seq):
    C = RWKV_CHUNK
    W = RWKV_W
    nc = seq // C
    zr_spec = lambda i: pl.BlockSpec((C, 3 * W), lambda c: (i * nc + c, ZS // (3 * W)))
    zl_spec = lambda i: pl.BlockSpec((C, LORA_PAD), lambda c: (i * nc + c, 0))
    return pl.pallas_call(
        functools.partial(_rwkv_prompt_kernel, batch=batch),
        grid=(nc,),
        in_specs=[zr_spec(i) for i in range(batch)] + [zl_spec(i) for i in range(batch)]
        + _rwkv_param_specs(),
        out_specs=[
            pl.BlockSpec((batch, C, W), lambda c: (0, c, 0)),
            pl.BlockSpec((batch, RWKV_PAIRS, LANES, LANES), lambda c: (0, 0, 0, 0)),
        ],
        out_shape=[
            jax.ShapeDtypeStruct((batch, seq, W), BF16),
            jax.ShapeDtypeStruct((batch, RWKV_PAIRS, LANES, LANES), F32),
        ],
        scratch_shapes=[pltpu.VMEM((batch, 3 * W), F32), pltpu.VMEM((batch, LORA_PAD), F32),
                        pltpu.VMEM((batch, RWKV_PAIRS, LANES, LANES), F32)],
        compiler_params=_cparams(("arbitrary",)),
        name="rwkv_prompt",
    )(*([z] * batch), *([zl] * batch), *params)


def _rwkv_sample(z, zl, sh_r, sh_l, s0, params, row0, batch, seq):
    C = RWKV_CHUNK
    W = RWKV_W
    n_seq = C // seq
    rb0 = row0 // C
    return pl.pallas_call(
        functools.partial(_rwkv_sample_kernel, n_seq=n_seq),
        grid=(batch // n_seq,),
        in_specs=[
            pl.BlockSpec((C, 3 * W), lambda i: (rb0 + i, ZS // (3 * W))),
            pl.BlockSpec((C, LORA_PAD), lambda i: (rb0 + i, 0)),
            pl.BlockSpec((C, 3 * W), lambda i: (i, 0)),
            pl.BlockSpec((C, LORA_PAD), lambda i: (i, 0)),
            pl.BlockSpec((n_seq, RWKV_HEADS, RWKV_N, RWKV_N), lambda i: (i, 0, 0, 0)),
        ] + _rwkv_param_specs(),
        out_specs=[
            pl.BlockSpec((C, W), lambda i: (i, 0)),
            pl.BlockSpec((n_seq, RWKV_HEADS, RWKV_N, RWKV_N), lambda i: (i, 0, 0, 0)),
        ],
        out_shape=[
            jax.ShapeDtypeStruct((batch * seq, W), BF16),
            jax.ShapeDtypeStruct((batch, RWKV_HEADS, RWKV_N, RWKV_N), F32),
        ],
        compiler_params=_cparams(("parallel",)),
        name="rwkv_sample",
    )(z, zl, sh_r, sh_l, s0, *params)


def _pair_unblock(s):
    b = s.shape[0]
    lo = s[:, :, :RWKV_N, :RWKV_N]
    hi = s[:, :, RWKV_N:, RWKV_N:]
    return jnp.stack([lo, hi], axis=2).reshape(b, RWKV_HEADS, RWKV_N, RWKV_N)


def _mix_out_kernel(oret_a_ref, oret_b_ref, orwkv_a_ref, orwkv_b_ref, xa_ref, xb_ref, gr_ref, gw_ref,
                    wr_ref, ww_ref, wo_ref, gn_ref, o_ref, hn_ref, *, tiles_a):
    first = pl.program_id(0) < tiles_a
    o_ret = jnp.where(first, oret_a_ref[...], oret_b_ref[...])
    o_rwkv = jnp.where(first, orwkv_a_ref[...], orwkv_b_ref[...])
    a = jnp.dot(o_ret, wr_ref[...], preferred_element_type=F32)
    b = jnp.dot(o_rwkv, ww_ref[...], preferred_element_type=F32)
    merged = jax.nn.sigmoid(gr_ref[...]) * a + jax.nn.sigmoid(gw_ref[...]) * b
    x2 = (jnp.where(first, xa_ref[...], xb_ref[...])
          + jnp.dot(merged.astype(BF16), wo_ref[...], preferred_element_type=F32))
    o_ref[...] = x2
    hn_ref[...] = _rms(x2, gn_ref[...], NORM_EPS).astype(BF16)


def _mix_out(o_rets, o_rwkvs, xs, zg, w_ret_o, w_rwkv_o, w_out, g_next):
    n = xs[0].shape[0] + xs[1].shape[0]
    tm = TM_MIX
    tiles_a = o_rets[0].shape[0] // tm
    resident = lambda shape: pl.BlockSpec(shape, lambda i: (0, 0), pipeline_mode=pl.Buffered(1))
    first = lambda i: (jnp.minimum(i, tiles_a - 1), 0)
    second = lambda i: (jnp.maximum(i - tiles_a, 0), 0)
    return pl.pallas_call(
        functools.partial(_mix_out_kernel, tiles_a=tiles_a),
        grid=(n // tm,),
        in_specs=[
            pl.BlockSpec((tm, RET_V), first),
            pl.BlockSpec((tm, RET_V), second),
            pl.BlockSpec((tm, RWKV_W), first),
            pl.BlockSpec((tm, RWKV_W), second),
            pl.BlockSpec((tm, D_MODEL), first),
            pl.BlockSpec((tm, D_MODEL), second),
            pl.BlockSpec((tm, D_MODEL), lambda i: (i, ZGR // D_MODEL)),
            pl.BlockSpec((tm, D_MODEL), lambda i: (i, ZGW // D_MODEL)),
            resident((RET_V, D_MODEL)),
            resident((RWKV_W, D_MODEL)),
            resident((D_MODEL, D_MODEL)),
            pl.BlockSpec((1, D_MODEL), lambda i: (0, 0)),
        ],
        out_specs=[pl.BlockSpec((tm, D_MODEL), lambda i: (i, 0)), pl.BlockSpec((tm, D_MODEL), lambda i: (i, 0))],
        out_shape=[jax.ShapeDtypeStruct((n, D_MODEL), F32), jax.ShapeDtypeStruct((n, D_MODEL), BF16)],
        compiler_params=_cparams(("arbitrary",)),
        name="mix_out",
    )(*o_rets, *o_rwkvs, *xs, zg, zg, w_ret_o, w_rwkv_o, w_out, g_next)


def kernel(x_prompt, x_sample, state_ret, state_rwkv, state_shift, norm_ffn1, ffn1_w_gu, ffn1_w_down, norm_mix, w_in, w_ret_o, rwkv_mu, rwkv_w0, rwkv_w_up, rwkv_a0, rwkv_a_up, rwkv_g_up, rwkv_k_k, rwkv_k_a, rwkv_r_k, rwkv_ln_w, rwkv_ln_b, w_rwkv_o, w_out, norm_ffn2, ffn2_w_gu, ffn2_w_down, norm_final):
    bp, tp, d = x_prompt.shape
    bs, ts, _ = x_sample.shape
    n_p = bp * tp
    n_s = bs * ts
    W = RWKV_W
    row = lambda a: a.reshape(1, -1)
    assert d == D_MODEL and n_p % TM == 0 and n_s == TM and n_p % TM_MIX == 0
    assert tp % RET_CHUNK == 0 and tp % RWKV_CHUNK == 0 and bp % RWKV_PROMPT_GROUP == 0 and bp % RET_PROMPT_SEQS == 0
    assert RWKV_CHUNK % ts == 0 and n_s % RWKV_CHUNK == 0 and RET_SAMPLE_ROWS % ts == 0 and n_s % RET_SAMPLE_ROWS == 0

    x1_s, h_s, *w_ffn1 = _ffn(x_sample.reshape(n_s, d), 0, n_s, row(norm_ffn1[0]),
                              (ffn1_w_gu[0], ffn1_w_down[0]), row(norm_mix[0]), final_norm=False, emit_h=True)
    x1_p, h_p = _ffn(x_prompt.reshape(n_p, d), 0, n_p, row(norm_ffn1[0]), w_ffn1, row(norm_mix[0]),
                     final_norm=False, emit_h=True)

    w_in_t = jnp.swapaxes(w_in[0], 0, 1)
    z = _proj((h_p, h_s), w_in_t, 0, TN_PROJ_MAIN, Z_MAIN_W)
    zl = _proj((h_p, h_s), w_in_t, Z_MAIN_W, LORA_PAD, LORA_PAD)
    zg = _proj((h_p, h_s), w_in_t[Z_MAIN_W + LORA_W:].astype(BF16), 0, TN_PROJ_GATES, 2 * D_MODEL)

    lg = jnp.log(1.0 - 2.0 ** (-5.0 - jnp.arange(RET_HEADS, dtype=F32)))
    cos_p, sin_p = _rope_tables(jnp.arange(tp, dtype=F32))
    cos_s, sin_s = _rope_tables(PAST_LEN + jnp.arange(ts, dtype=F32))
    rep = RET_SAMPLE_ROWS // ts
    oret_p, ret_p = _ret_prompt(z, lg, cos_p, sin_p, bp, tp)
    oret_s, ret_s = _ret_sample(z, lg, jnp.tile(cos_s, (rep, 1)), jnp.tile(sin_s, (rep, 1)),
                                state_ret[0], n_p, bs, ts)

    mu = rwkv_mu[0]
    pad_l = lambda a: jnp.pad(a, [(0, 0)] * (a.ndim - 1) + [(0, LORA_PAD - LORA_W)])
    seg = (jnp.arange(LANES)[:, None] // RWKV_N == jnp.arange(LANES)[None, :] // RWKV_N).astype(BF16)
    zero64 = jnp.zeros((DECAY_LORA, W), F32)
    params = (
        row(mu[:3 * W]), pad_l(row(mu[3 * W:])),
        row(rwkv_w0[0]), jnp.concatenate([rwkv_w_up[0], zero64], axis=0).astype(BF16),
        row(rwkv_a0[0]), jnp.concatenate([zero64, rwkv_a_up[0]], axis=0).astype(BF16),
        jnp.pad(rwkv_g_up[0], ((0, LORA_PAD - LANES - GATE_LORA), (0, 0))).astype(BF16),
        row(rwkv_k_k[0]), row(rwkv_k_a[0]), row(rwkv_r_k[0]), row(rwkv_ln_w[0]), row(rwkv_ln_b[0]),
        seg,
    )
    orw_p, rwkv_p = _rwkv_prompt(z, zl, params, bp, tp)
    sh_s = state_shift[0]
    first_rows = lambda a: jnp.pad(a[:, None, :], ((0, 0), (0, ts - 1), (0, 0))).reshape(n_s, -1)
    orw_s, rwkv_s = _rwkv_sample(z, zl, first_rows(sh_s[:, :3 * W]), first_rows(pad_l(sh_s[:, 3 * W:])),
                                 state_rwkv[0], params, n_p, bs, ts)

    x2, h_ffn2 = _mix_out((oret_p.reshape(n_p, RET_V), oret_s), (orw_p.reshape(n_p, W), orw_s), (x1_p, x1_s), zg,
                          w_ret_o[0].astype(BF16), w_rwkv_o[0].astype(BF16), w_out[0].astype(BF16),
                          row(norm_ffn2[0]))
    y_s, *w_ffn2 = _ffn(x2, n_p, n_s, row(norm_ffn2[0]), (ffn2_w_gu[0], ffn2_w_down[0]), row(norm_final),
                        final_norm=True, emit_h=False, h=h_ffn2)
    (y_p,) = _ffn(x2, 0, n_p, row(norm_ffn2[0]), w_ffn2, row(norm_final),
                  final_norm=True, emit_h=False, h=h_ffn2)

    def last_rows(firsts, t):
        rows = [f + t - 1 for f in firsts]
        zr_last = jnp.concatenate([z[r:r + 1, ZS:ZS + 3 * W] for r in rows], axis=0)
        zl_last = jnp.concatenate([zl[r:r + 1, :LORA_W] for r in rows], axis=0)
        return jnp.concatenate([zr_last, zl_last], axis=-1)[None]

    shift_s = jnp.concatenate([lax.slice(z, (n_p + ts - 1, ZS), (n_p + n_s, ZS + 3 * W), (ts, 1)),
                               lax.slice(zl, (n_p + ts - 1, 0), (n_p + n_s, LORA_W), (ts, 1))], axis=-1)[None]
    return (y_p.reshape(bp, tp, d), y_s.reshape(bs, ts, d),
            ret_p[None], _pair_unblock(rwkv_p)[None], last_rows(range(0, n_p, tp), tp),
            ret_s[None], rwkv_s[None], shift_s)
```

```python
import functools

import jax
import jax.numpy as jnp
from jax import lax
from jax.experimental import pallas as pl
from jax.experimental.pallas import tpu as pltpu

F32 = jnp.float32
BF16 = jnp.bfloat16

D_MODEL = 2048
PAST_LEN = 16384
RET_HEADS = 4
RET_DK = 256
RET_DV = 512
RET_CHUNK = 128
ROPE_BASE = 10000.0
RET_QK = RET_HEADS * RET_DK
RET_V = RET_HEADS * RET_DV
RWKV_HEADS = 16
RWKV_N = 64
RWKV_W = RWKV_HEADS * RWKV_N
DECAY_LORA = 64
AAA_LORA = 64
GATE_LORA = 160
LORA_W = DECAY_LORA + AAA_LORA + GATE_LORA
LORA_PAD = 384
D_FF = 5632
NORM_EPS = 1e-6
GN_EPS_RET = 1e-6
GN_EPS_RWKV = 64e-5

ZQ, ZK, ZV, ZG, ZS = 0, 1024, 2048, 4096, 6144
Z_MAIN_W = 9216
ZGR, ZGW = 0, 2048

LANES = 128
SUBLANES = 8
MXU_TILE = 256
RWKV_PAIRS = RWKV_W // LANES
RWKV_CHUNK = 64
RWKV_PROMPT_GROUP = 2
RWKV_PROMPT_LEAD = 21

V7X_VMEM_BYTES = 64 * 1024 * 1024
VMEM_LIMIT = V7X_VMEM_BYTES - 8 * 1024 * 1024

TM = 512
TF = 512
TF_CAST = 256
TN_PROJ = 1024
TN_PROJ_MAIN = 1536
TM_MIX = 256


def _rms(x, g, eps):
    return x * lax.rsqrt(jnp.mean(x * x, axis=-1, keepdims=True) + eps) * g


def _dot(a, b):
    return jnp.dot(a.astype(BF16), b.astype(BF16), preferred_element_type=F32)


def _dot_nt(a, b):
    return lax.dot_general(a.astype(BF16), b.astype(BF16), (((1,), (1,)), ((), ())),
                           preferred_element_type=F32)


def _dot_tn(a, b):
    return lax.dot_general(a.astype(BF16), b.astype(BF16), (((0,), (0,)), ((), ())),
                           preferred_element_type=F32)


def _bf16_terms(x, n):
    terms = []
    for _ in range(n):
        t = x.astype(BF16)
        terms.append(t)
        x = x - t.astype(F32)
    return terms


def _cparams(sem):
    return pltpu.CompilerParams(dimension_semantics=sem, vmem_limit_bytes=VMEM_LIMIT)


def _ffn_kernel(*refs, final_norm, emit_h, h_in, cast_w):
    if h_in:
        hin_ref, refs = refs[0], refs[1:]
    x_ref, g_ref, wg_ref, wu_ref, wd_ref, gf_ref = refs[:6]
    outs = list(refs[6:-2])
    h_ref, acc_ref = refs[-2:]
    o_ref = outs.pop(0)
    hn_ref = outs.pop(0) if emit_h else None
    j = pl.program_id(1)

    @pl.when(j == 0)
    def _():
        if not h_in:
            h_ref[...] = _rms(x_ref[...], g_ref[...], NORM_EPS).astype(BF16)
        acc_ref[...] = jnp.zeros_like(acc_ref)

    if cast_w:
        wg, wu, wd = wg_ref[...].astype(BF16), wu_ref[...].astype(BF16), wd_ref[...].astype(BF16)
        outs[0][0], outs[1][0], outs[2][...] = wg, wu, wd
    else:
        wg, wu, wd = wg_ref[0], wu_ref[0], wd_ref[...]
    h = hin_ref[...] if h_in else h_ref[...]
    tf = wg.shape[1]
    half = tf // 2 if tf // 2 >= MXU_TILE else tf
    cols = tuple(slice(c, c + half) for c in range(0, tf, half))
    gus = [(jnp.dot(h, wg[:, c], preferred_element_type=F32), jnp.dot(h, wu[:, c], preferred_element_type=F32))
           for c in cols]
    downs = [jnp.dot(((g * jax.nn.sigmoid(g)) * u).astype(BF16), wd[c, :], preferred_element_type=F32)
             for (g, u), c in zip(gus, cols)]
    acc_ref[...] += sum(downs[1:], downs[0])

    @pl.when(j == pl.num_programs(1) - 1)
    def _():
        y = x_ref[...] + 0.5 * acc_ref[...]
        if final_norm:
            y = _rms(y, gf_ref[...], NORM_EPS)
        if emit_h:
            hn_ref[...] = _rms(y, gf_ref[...], NORM_EPS).astype(BF16)
        o_ref[...] = y


def _ffn(x, row0, rows, g, weights, g_after, final_norm, emit_h, h=None):
    cast_w = len(weights) == 2
    tf = TF_CAST if cast_w else TF
    nj = D_FF // tf
    t0 = row0 // TM
    tile = pl.BlockSpec((TM, D_MODEL), lambda i, j: (t0 + i, 0))
    out_tile = pl.BlockSpec((TM, D_MODEL), lambda i, j: (i, 0))
    h_in = h is not None
    if cast_w:
        w_specs = [pl.BlockSpec((D_MODEL, tf), lambda i, j: (0, j)),
                   pl.BlockSpec((D_MODEL, tf), lambda i, j: (0, j + nj)),
                   pl.BlockSpec((tf, D_MODEL), lambda i, j: (j, 0))]
        w_args = (weights[0], weights[0], weights[1])
    else:
        w_specs = [pl.BlockSpec((1, D_MODEL, tf), lambda i, j: (j, 0, 0)),
                   pl.BlockSpec((1, D_MODEL, tf), lambda i, j: (j, 0, 0)),
                   pl.BlockSpec((tf, D_MODEL), lambda i, j: (j, 0))]
        w_args = weights
    out_specs = [out_tile] + ([out_tile] if emit_h else [])
    out_shape = [jax.ShapeDtypeStruct((rows, D_MODEL), F32)] + (
        [jax.ShapeDtypeStruct((rows, D_MODEL), BF16)] if emit_h else [])
    if cast_w:
        per = TF // tf
        tile_major = pl.BlockSpec((1, D_MODEL, tf), lambda i, j: (j // per, 0, j % per))
        out_specs += [tile_major, tile_major, pl.BlockSpec((tf, D_MODEL), lambda i, j: (j, 0))]
        out_shape += [jax.ShapeDtypeStruct((D_FF // TF, D_MODEL, TF), BF16)] * 2 + [
            jax.ShapeDtypeStruct((D_FF, D_MODEL), BF16)]
        assert rows == TM, "each weight tile is written exactly once"
    return pl.pallas_call(
        functools.partial(_ffn_kernel, final_norm=final_norm, emit_h=emit_h, h_in=h_in, cast_w=cast_w),
        grid=(rows // TM, nj),
        in_specs=([tile] if h_in else []) + [tile, pl.BlockSpec((1, D_MODEL), lambda i, j: (0, 0))] + w_specs
        + [pl.BlockSpec((1, D_MODEL), lambda i, j: (0, 0))],
        out_specs=out_specs,
        out_shape=out_shape,
        scratch_shapes=[pltpu.VMEM((TM, D_MODEL), BF16), pltpu.VMEM((TM, D_MODEL), F32)],
        compiler_params=_cparams(("parallel", "arbitrary")),
        name="ffn_cast" if cast_w else "ffn",
    )(*((h,) if h_in else ()), x, g, *w_args, g_after)


def _proj_kernel(ha_ref, hb_ref, wt_ref, o_ref, w_ref, *, tiles_a):
    i = pl.program_id(1)

    @pl.when(i == 0)
    def _():
        w_ref[...] = wt_ref[...].astype(BF16)

    o_ref[...] = _dot_nt(jnp.where(i < tiles_a, ha_ref[...], hb_ref[...]), w_ref[...])


def _proj(hs, w_t, col0, tn, nout):
    tiles_a = hs[0].shape[0] // TM
    n = hs[0].shape[0] + hs[1].shape[0]
    return pl.pallas_call(
        functools.partial(_proj_kernel, tiles_a=tiles_a),
        grid=(nout // tn, n // TM),
        in_specs=[
            pl.BlockSpec((TM, D_MODEL), lambda j, i: (jnp.minimum(i, tiles_a - 1), 0)),
            pl.BlockSpec((TM, D_MODEL), lambda j, i: (jnp.maximum(i - tiles_a, 0), 0)),
            pl.BlockSpec((pl.Element(tn), pl.Element(D_MODEL)),
                         lambda j, i: (pl.multiple_of(col0 + j * tn, SUBLANES), 0)),
        ],
        out_specs=pl.BlockSpec((TM, tn), lambda j, i: (i, j)),
        out_shape=jax.ShapeDtypeStruct((n, nout), F32),
        scratch_shapes=[pltpu.VMEM((tn, D_MODEL), BF16)],
        compiler_params=_cparams(("parallel", "arbitrary")),
        name="proj",
    )(*hs, w_t)


def _pair_swap(x):
    n = x.shape[-1]
    lane = lax.broadcasted_iota(jnp.int32, x.shape, x.ndim - 1)
    prev = pltpu.roll(x, 1, x.ndim - 1)
    nxt = pltpu.roll(x, n - 1, x.ndim - 1)
    return jnp.where((lane & 1) == 1, prev, nxt)


def _rotary(x, cos, sin_signed):
    return x * cos + _pair_swap(x) * sin_signed


def _ret_finish(o, zg):
    o = o * lax.rsqrt(jnp.mean(o * o, axis=-1, keepdims=True) + GN_EPS_RET)
    return o * (zg * jax.nn.sigmoid(zg))


RET_PROMPT_SEQS = 2


def _ret_prompt_kernel(lg_ref, *refs):
    nb = RET_PROMPT_SEQS
    q_refs, k_refs, v_refs, g_refs = (refs[i * nb:(i + 1) * nb] for i in range(4))
    cos_ref, sin_ref, o_ref, s_out_ref, s_ref = refs[4 * nb:]
    c = pl.program_id(1)
    C = RET_CHUNK

    @pl.when(c == 0)
    def _():
        s_ref[...] = jnp.zeros_like(s_ref)

    cos = cos_ref[...]
    sin = sin_ref[...]
    ti = lax.broadcasted_iota(jnp.int32, (C, C), 0)
    tj = lax.broadcasted_iota(jnp.int32, (C, C), 1)
    rel = (ti - tj).astype(F32)
    idx = lax.broadcasted_iota(jnp.int32, (C, 1), 0).astype(F32)
    H = range(RET_HEADS)
    lgs = [lg_ref[h] for h in H]
    finals = []

    def sequence(b):
        qs = [_rotary(q_refs[b][:, h * RET_DK:(h + 1) * RET_DK], cos, sin) for h in H]
        ks = [_rotary(k_refs[b][:, h * RET_DK:(h + 1) * RET_DK], cos, sin) * (RET_DK ** -0.5) for h in H]
        yield
        vs = [v_refs[b][:, h * RET_DV:(h + 1) * RET_DV] for h in H]
        ss = [s_ref[b, h] for h in H]
        scores = [_dot_nt(q, k) * jnp.where(rel >= 0, jnp.exp(lg * jnp.maximum(rel, 0.0)), 0.0)
                  for q, k, lg in zip(qs, ks, lgs)]
        crosses = [_dot(q, s) * jnp.exp(lg * (idx + 1.0)) for q, s, lg in zip(qs, ss, lgs)]
        inners = [_dot(sc, v) for sc, v in zip(scores, vs)]
        s_news = [jnp.exp(lg * C) * s + _dot_tn(k * jnp.exp(lg * (C - 1.0 - idx)), v)
                  for lg, s, k, v in zip(lgs, ss, ks, vs)]
        yield
        for h in H:
            s_ref[b, h] = s_news[h]
            o_ref[b, :, h * RET_DV:(h + 1) * RET_DV] = _ret_finish(
                inners[h] + crosses[h], g_refs[b][:, h * RET_DV:(h + 1) * RET_DV]).astype(BF16)
        finals.append((b, s_news))

    _run_staged([sequence(b) for b in range(nb)], 1)

    @pl.when(c == pl.num_programs(1) - 1)
    def _():
        for b, s_news in finals:
            for h in H:
                s_out_ref[b, h] = s_news[h]


def _ret_prompt(z, lg, cos, sin, batch, seq):
    nc = seq // RET_CHUNK
    C = RET_CHUNK
    nb = RET_PROMPT_SEQS
    cols = lambda width, col0: [pl.BlockSpec((C, width), lambda g, c, b=b: ((g * nb + b) * nc + c, col0 // width))
                                for b in range(nb)]
    return pl.pallas_call(
        _ret_prompt_kernel,
        grid=(batch // nb, nc),
        in_specs=[pl.BlockSpec(memory_space=pltpu.SMEM)]
        + cols(RET_QK, ZQ) + cols(RET_QK, ZK) + cols(RET_V, ZV) + cols(RET_V, ZG)
        + [pl.BlockSpec((C, RET_DK), lambda g, c: (c, 0)), pl.BlockSpec((C, RET_DK), lambda g, c: (c, 0))],
        out_specs=[
            pl.BlockSpec((nb, C, RET_V), lambda g, c: (g, c, 0)),
            pl.BlockSpec((nb, RET_HEADS, RET_DK, RET_DV), lambda g, c: (g, 0, 0, 0)),
        ],
        out_shape=[
            jax.ShapeDtypeStruct((batch, seq, RET_V), BF16),
            jax.ShapeDtypeStruct((batch, RET_HEADS, RET_DK, RET_DV), F32),
        ],
        scratch_shapes=[pltpu.VMEM((nb, RET_HEADS, RET_DK, RET_DV), F32)],
        compiler_params=_cparams(("parallel", "arbitrary")),
        name="ret_prompt",
    )(lg, *([z] * (4 * nb)), cos, sin)


RET_SAMPLE_ROWS = 16


def _ret_sample_kernel(lg_ref, q_ref, k_ref, v_ref, g_ref, cos_ref, sin_ref, s0_ref,
                       o_ref, s_out_ref, *, seq):
    R = RET_SAMPLE_ROWS
    nb = R // seq
    cos = cos_ref[...]
    sin = sin_ref[...]
    ri = lax.broadcasted_iota(jnp.int32, (R, R), 0)
    rj = lax.broadcasted_iota(jnp.int32, (R, R), 1)
    rel = ((ri % seq) - (rj % seq)).astype(F32)
    ok = ((ri // seq) == (rj // seq)) & (rel >= 0)
    row = lax.broadcasted_iota(jnp.int32, (R, 1), 0)
    t = (row % seq).astype(F32)

    for h in range(RET_HEADS):
        lg = lg_ref[h]
        q = _rotary(q_ref[:, h * RET_DK:(h + 1) * RET_DK], cos, sin)
        k = _rotary(k_ref[:, h * RET_DK:(h + 1) * RET_DK], cos, sin) * (RET_DK ** -0.5)
        v = v_ref[:, h * RET_DV:(h + 1) * RET_DV]
        mask = jnp.where(ok, jnp.exp(lg * jnp.maximum(rel, 0.0)), 0.0)
        q_dec = jnp.exp(lg * (t + 1.0))
        kd = k * jnp.exp(lg * (seq - 1.0 - t))
        inner = _dot(_dot_nt(q, k) * mask, v)
        cross = jnp.zeros_like(inner)
        for b in range(nb):
            s = s0_ref[b, h]
            mine = (row // seq) == b
            cross = jnp.where(mine, _dot(q, s), cross)
            s_out_ref[b, h] = jnp.exp(lg * seq) * s + _dot_tn(jnp.where(mine, kd, 0.0), v)
        o_ref[:, h * RET_DV:(h + 1) * RET_DV] = _ret_finish(
            inner + cross * q_dec, g_ref[:, h * RET_DV:(h + 1) * RET_DV]).astype(BF16)


def _ret_sample(z, lg, cos, sin, s0, row0, batch, seq):
    R = RET_SAMPLE_ROWS
    nb = R // seq
    rb0 = row0 // R
    return pl.pallas_call(
        functools.partial(_ret_sample_kernel, seq=seq),
        grid=(batch // nb,),
        in_specs=[
            pl.BlockSpec(memory_space=pltpu.SMEM),
            pl.BlockSpec((R, RET_QK), lambda i: (rb0 + i, ZQ // RET_QK)),
            pl.BlockSpec((R, RET_QK), lambda i: (rb0 + i, ZK // RET_QK)),
            pl.BlockSpec((R, RET_V), lambda i: (rb0 + i, ZV // RET_V)),
            pl.BlockSpec((R, RET_V), lambda i: (rb0 + i, ZG // RET_V)),
            pl.BlockSpec((R, RET_DK), lambda i: (0, 0)),
            pl.BlockSpec((R, RET_DK), lambda i: (0, 0)),
            pl.BlockSpec((nb, RET_HEADS, RET_DK, RET_DV), lambda i: (i, 0, 0, 0)),
        ],
        out_specs=[
            pl.BlockSpec((R, RET_V), lambda i: (i, 0)),
            pl.BlockSpec((nb, RET_HEADS, RET_DK, RET_DV), lambda i: (i, 0, 0, 0)),
        ],
        out_shape=[
            jax.ShapeDtypeStruct((batch * seq, RET_V), BF16),
            jax.ShapeDtypeStruct((batch, RET_HEADS, RET_DK, RET_DV), F32),
        ],
        compiler_params=_cparams(("parallel",)),
        name="ret_sample",
    )(lg, z, z, z, z, cos, sin, s0)


def _rope_tables(pos):
    half = RET_DK // 2
    inv = 1.0 / (ROPE_BASE ** jnp.linspace(0.0, 1.0, half, dtype=F32))
    ang = pos[:, None] * inv[None, :]
    cos = jnp.repeat(jnp.cos(ang), 2, axis=-1)
    sin = jnp.sin(ang)
    sin_signed = jnp.stack([-sin, sin], axis=-1).reshape(pos.shape[0], RET_DK)
    return cos, sin_signed


def _segsum(x, seg_ref):
    rows = x.shape[0]
    n = rows * RWKV_PAIRS
    xs = jnp.concatenate([x[:, p * LANES:(p + 1) * LANES] for p in range(RWKV_PAIRS)], axis=0)
    ss = jnp.dot(jnp.concatenate(_bf16_terms(xs, 2), axis=0), seg_ref[...], preferred_element_type=F32)
    ss = ss[0:n] + ss[n:2 * n]
    return jnp.concatenate([ss[p * rows:(p + 1) * rows] for p in range(RWKV_PAIRS)], axis=1)


def _run_staged(gens, lead):
    out = [None] * len(gens)
    live = [True] * len(gens)

    def step(i):
        try:
            next(gens[i])
        except StopIteration as stop:
            out[i] = stop.value
            live[i] = False

    for _ in range(lead):
        if live[0]:
            step(0)
    while any(live):
        for i in range(len(gens)):
            if live[i]:
                step(i)
    return out


def _each(fn, *lists):
    out = []
    for args in zip(*lists):
        out.append(fn(*args))
        yield
    return out


def _rwkv_chunk(zr, zl, prev_r, prev_l, prm, n_seq, **state_fns):
    front = yield from _rwkv_front(zr, zl, prev_r, prev_l, prm, n_seq)
    return (yield from _rwkv_back(*front, prm, n_seq, **state_fns))


def _rwkv_front(zr, zl, prev_r, prev_l, prm, n_seq):
    (mur_ref, mul_ref, w0_ref, wup_ref, a0_ref, aup_ref, gup_ref,
     kk_ref, ka_ref, rk_ref, lnw_ref, lnb_ref, seg_ref) = prm
    C = RWKV_CHUNK
    W = RWKV_W
    R = zr.shape[0]
    TS = C // n_seq
    row = lax.broadcasted_iota(jnp.int32, (R, 1), 0)
    first = (row % TS) == 0

    def mix(z, prev, mu_ref):
        z_prev = jnp.where(first, prev, pltpu.roll(z, 1, 0))
        return z + (z_prev - z) * mu_ref[...]

    xs_r = mix(zr[:, 0:W], prev_r[:, 0:W], mur_ref.at[:, 0:W])
    yield
    xs_k = mix(zr[:, W:2 * W], prev_r[:, W:2 * W], mur_ref.at[:, W:2 * W])
    yield
    xs_v = mix(zr[:, 2 * W:3 * W], prev_r[:, 2 * W:3 * W], mur_ref.at[:, 2 * W:3 * W])
    yield
    xl = mix(zl, prev_l, mul_ref)
    r, k, v = xs_r, xs_k, xs_v
    x_wa = xl[:, 0:LANES]
    x_g = xl[:, LANES:LORA_PAD]

    y_w = w0_ref[...] + _dot(jnp.tanh(x_wa), wup_ref[...])
    yield
    w_log = -(jnp.maximum(-y_w, 0.0) + jnp.log(1.0 + jnp.exp(-jnp.abs(y_w)))) - 0.5
    lw = -jnp.exp(w_log)
    yield
    a_rate = jax.nn.sigmoid(a0_ref[...] + _dot(x_wa, aup_ref[...]))
    yield
    g = _dot(jax.nn.sigmoid(x_g), gup_ref[...])
    yield

    kk = k * kk_ref[...]
    k = k * (1.0 + (a_rate - 1.0) * ka_ref[...])
    yield
    sums = _segsum(jnp.concatenate([kk * kk, r * k * rk_ref[...]], axis=0), seg_ref)
    yield
    kk = kk * lax.rsqrt(jnp.maximum(sums[0:R], 1e-24))
    bonus_rk = sums[R:2 * R]
    a = -kk
    b = kk * a_rate
    yield

    rows_sel = R if n_seq == 1 else 2 * R
    ti = lax.broadcasted_iota(jnp.int32, (rows_sel, R), 0)
    tj = lax.broadcasted_iota(jnp.int32, (rows_sel, R), 1)
    same = ((ti % R) // TS) == (tj // TS)
    sel = (same & ((ti >= R) | (tj <= ti))).astype(BF16)
    sums_lw = sum(jnp.dot(sel, t, preferred_element_type=F32) for t in _bf16_terms(lw, 3))
    yield
    cum = sums_lw[0:R]
    if n_seq == 1:
        tot = jnp.concatenate([jnp.broadcast_to(cum[i + C - 1:i + C], (C, W)) for i in range(0, R, C)], axis=0)
    else:
        tot = sums_lw[R:2 * R]
    e_neg = jnp.exp(-cum)
    yield
    a_t = (a * jnp.exp(cum - lw)).astype(BF16)
    yield
    b_t = (b * e_neg).astype(BF16)
    yield
    k_t = (k * e_neg).astype(BF16)
    yield
    r_t = r * jnp.exp(cum)
    yield
    e_tail = jnp.exp(tot - cum)
    yield
    b_g = (b * e_tail).astype(BF16)
    yield
    k_g = (k * e_tail).astype(BF16)
    dec = jnp.exp(tot)
    yield
    return (a_t, b_t, k_t, b_g, k_g), (r_t, v, dec, g, bonus_rk)


def _rwkv_back(mm_operands, f32_arrays, prm, n_seq, read_state=None, update_state=None, fold_state=None):
    a_t, b_t, k_t, b_g, k_g = mm_operands
    r_t, v, dec, g, bonus_rk = f32_arrays
    lnw_ref, lnb_ref, seg_ref = prm[10:13]
    C = RWKV_CHUNK
    R = r_t.shape[0]
    TS = C // n_seq

    lane = lax.broadcasted_iota(jnp.int32, (1, LANES), 1)
    lo = lane < RWKV_N

    def split(x):
        return jnp.concatenate([jnp.where(lo, x, 0.0), jnp.where(lo, 0.0, x)], axis=0)

    row4 = lax.broadcasted_iota(jnp.int32, (C, 4 * C), 0)
    col4 = lax.broadcasted_iota(jnp.int32, (C, 4 * C), 1) % C
    same4 = (row4 // TS) == (col4 // TS)
    strict = same4 & (col4 < row4)
    incl = same4 & (col4 <= row4)
    row2 = lax.broadcasted_iota(jnp.int32, (C, 2 * C), 0)
    col2 = lax.broadcasted_iota(jnp.int32, (C, 2 * C), 1)
    eye2 = ((col2 % C) == row2).astype(F32)
    left = col2 < C

    tiles = [(slice(i * C, (i + 1) * C), slice(p * LANES, (p + 1) * LANES))
             for i in range(R // C) for p in range(RWKV_PAIRS)]
    v2s = [split(v[rs, sl]) for rs, sl in tiles]
    ars = [jnp.concatenate([a_t[rs, sl], r_t[rs, sl]], axis=0) for rs, sl in tiles]
    if fold_state is None:
        wys = read_state(ars)
    gms = yield from _each(
        lambda ar, t: _dot_nt(ar, jnp.concatenate([split(b_t[t[0], t[1]]), split(k_t[t[0], t[1]])], axis=0)),
        ars, tiles)
    gas = [jnp.where(strict, gm[0:C], 0.0) for gm in gms]
    grs = [jnp.where(incl, gm[C:2 * C], 0.0) for gm in gms]

    def blockdiag(x):
        return jnp.concatenate([jnp.where(left, x, 0.0), jnp.where(left, 0.0, x)], axis=0)

    ts = [eye2 + ga[:, 0:2 * C] for ga in gas]
    n_sq = TS.bit_length() - 2
    if n_sq >= 1:
        pws = yield from _each(lambda ga: _dot(ga[:, 0:2 * C], blockdiag(ga[:, 0:2 * C])), gas)
        for _ in range(n_sq - 1):
            tps = yield from _each(lambda t, pw: _dot(jnp.concatenate([t, pw], axis=0), blockdiag(pw)), ts, pws)
            ts = [t + tp[0:C] for t, tp in zip(ts, tps)]
            pws = [tp[C:2 * C] for tp in tps]
        ts = yield from _each(lambda t, pw: t + _dot(t, blockdiag(pw)), ts, pws)
    wis = yield from _each(lambda ga, v2: _dot(ga[:, 2 * C:4 * C], v2), gas, v2s)
    bks = [jnp.concatenate([b_g[rs, sl], k_g[rs, sl]], axis=0) for rs, sl in tiles]
    if fold_state is None:
        us = [_dot(t, split(wi + wy[0:C])) for t, wi, wy in zip(ts, wis, wys)]
        ys = [wy[C:2 * C] + _dot(gr, jnp.concatenate([split(u), v2], axis=0))
              for wy, gr, u, v2 in zip(wys, grs, us, v2s)]
        update_state([jnp.concatenate([u, v[rs, sl]], axis=0) for u, (rs, sl) in zip(us, tiles)], bks, dec)
    else:
        zc = jnp.zeros((C, LANES), F32)
        aus = yield from _each(
            lambda t, wi, tl: _dot(t, jnp.concatenate([split(a_t[tl[0], tl[1]]), split(wi)], axis=1)),
            ts, wis, tiles)
        rys = yield from _each(
            lambda gr, au, v2: _dot(gr, jnp.concatenate(
                [jnp.concatenate([split(au[:, 0:LANES]), split(au[:, LANES:2 * LANES])], axis=1),
                 jnp.concatenate([jnp.concatenate([zc, zc], axis=0), v2], axis=1)], axis=0)),
            grs, aus, v2s)
        mns = yield from _each(
            lambda bk, au, tl: _dot_tn(bk, jnp.concatenate(
                [au, jnp.concatenate([zc, v[tl[0], tl[1]]], axis=1)], axis=0)),
            bks, aus, tiles)
        ys = fold_state([r_t[rs, sl] + ry[:, 0:LANES] for ry, (rs, sl) in zip(rys, tiles)],
                        [ry[:, LANES:2 * LANES] for ry in rys], mns, dec)

    yield
    y = jnp.concatenate([jnp.concatenate(ys[i * RWKV_PAIRS:(i + 1) * RWKV_PAIRS], axis=1)
                         for i in range(R // C)], axis=0)
    mean = _segsum(y, seg_ref) * (1.0 / RWKV_N)
    d = y - mean
    var = _segsum(d * d, seg_ref) * (1.0 / RWKV_N)
    yn = d * lax.rsqrt(var + GN_EPS_RWKV) * lnw_ref[...] + lnb_ref[...]
    return ((yn + bonus_rk * v) * g).astype(BF16)


def _rwkv_prompt_kernel(*refs, batch):
    zr_refs = refs[:batch]
    zl_refs = refs[batch:2 * batch]
    prm = refs[2 * batch:2 * batch + 13]
    o_ref, s_out_ref, prevr_ref, prevl_ref, h_ref = refs[2 * batch + 13:]
    C = RWKV_CHUNK
    c = pl.program_id(0)

    @pl.when(c == 0)
    def _():
        prevr_ref[...] = jnp.zeros_like(prevr_ref)
        prevl_ref[...] = jnp.zeros_like(prevl_ref)
        h_ref[...] = jnp.zeros_like(h_ref)

    ki = lax.broadcasted_iota(jnp.int32, (LANES, LANES), 0)
    vi = lax.broadcasted_iota(jnp.int32, (LANES, LANES), 1)
    diag_blocks = (ki < RWKV_N) == (vi < RWKV_N)

    def group(i0):
        seqs = range(i0, i0 + RWKV_PROMPT_GROUP)
        tiles = [(i, p) for i in seqs for p in range(RWKV_PAIRS)]
        zr = jnp.concatenate([zr_refs[i][...] for i in seqs], axis=0)
        zl = jnp.concatenate([zl_refs[i][...] for i in seqs], axis=0)
        rows = lambda ref: jnp.concatenate(
            [jnp.broadcast_to(ref[i:i + 1, :], (C, ref.shape[1])) for i in seqs], axis=0)

        def fold_state(r_hats, y_inds, mns, dec):
            dec_cols = [jnp.broadcast_to(dec[(i - i0) * C:(i - i0) * C + 1, p * LANES:(p + 1) * LANES],
                                         (LANES, LANES)).T for i, p in tiles]
            hs = [h_ref[i, p] for i, p in tiles]
            res = [_dot(jnp.concatenate([rh, jnp.where(diag_blocks, mn[:, 0:LANES], 0.0)], axis=0), h)
                   for rh, mn, h in zip(r_hats, mns, hs)]
            for (i, p), h, dc, rs, mn in zip(tiles, hs, dec_cols, res, mns):
                h_ref[i, p] = h * dc + rs[C:C + LANES] + jnp.where(diag_blocks, mn[:, LANES:2 * LANES], 0.0)
            return [rs[0:C] + yi for rs, yi in zip(res, y_inds)]

        y = yield from _rwkv_chunk(zr, zl, rows(prevr_ref), rows(prevl_ref), prm, 1, fold_state=fold_state)
        for n, i in enumerate(seqs):
            o_ref[i] = y[n * C:(n + 1) * C]
            prevr_ref[i:i + 1, :] = zr[(n + 1) * C - 1:(n + 1) * C, :]
            prevl_ref[i:i + 1, :] = zl[(n + 1) * C - 1:(n + 1) * C, :]

    _run_staged([group(i0) for i0 in range(0, batch, RWKV_PROMPT_GROUP)], RWKV_PROMPT_LEAD)

    @pl.when(c == pl.num_programs(0) - 1)
    def _():
        for i in range(batch):
            for p in range(RWKV_PAIRS):
                s_out_ref[i, p] = h_ref[i, p].T


def _rwkv_sample_kernel(zr_ref, zl_ref, shr_ref, shl_ref, s0_ref, *rest, n_seq):
    prm = rest[:13]
    o_ref, s_out_ref = rest[13:]
    C = RWKV_CHUNK
    N = RWKV_N
    TS = C // n_seq
    tokseq = (lax.broadcasted_iota(jnp.int32, (1, LANES), 1) % C) // TS

    def pick(x, j):
        return jnp.where(tokseq == j, x, 0.0)

    def stacked(h):
        return s0_ref[:, h].reshape(n_seq * N, N)

    def read_state(ars):
        res = [[_dot_nt(stacked(2 * p + e), ar[:, e * N:(e + 1) * N]) for e in range(2)]
               for p, ar in enumerate(ars)]
        out = []
        for rp in res:
            halves = []
            for re in rp:
                acc = pick(re[0:N], 0)
                for j in range(1, n_seq):
                    acc = acc + pick(re[j * N:(j + 1) * N], j)
                halves.append(acc)
            out.append(jnp.concatenate(halves, axis=0).T)
        return out

    def update_state(uvs, bks, dec):
        uvts = [uv.T for uv in uvs]
        upds = [[_dot(jnp.concatenate([pick(uvt[e * N:(e + 1) * N], j) for j in range(n_seq)], axis=0),
                      bk[:, e * N:(e + 1) * N]) for e in range(2)]
                for uvt, bk in zip(uvts, bks)]
        for p, up in enumerate(upds):
            for e, upd in enumerate(up):
                h = 2 * p + e
                dec_h = jnp.concatenate(
                    [jnp.broadcast_to(dec[j * TS:j * TS + 1, h * N:(h + 1) * N], (N, N)) for j in range(n_seq)],
                    axis=0)
                s_out_ref[:, h] = (stacked(h) * dec_h + upd).reshape(n_seq, N, N)

    (o_ref[...],) = _run_staged([_rwkv_chunk(zr_ref[...], zl_ref[...], shr_ref[...], shl_ref[...], prm, n_seq,
                                             read_state=read_state, update_state=update_state)], 0)


def _rwkv_param_specs():
    W = RWKV_W
    shapes = [(1, 3 * W), (1, LORA_PAD), (1, W), (LANES, W), (1, W), (LANES, W), (LORA_PAD - LANES, W),
              (1, W), (1, W), (1, W), (1, W), (1, W), (LANES, LANES)]
    return [pl.BlockSpec(s, lambda *_: (0, 0)) for s in shapes]


def _rwkv_prompt(z, zl, params, batch, seq):
    C = RWKV_CHUNK
    W = RWKV_W
    nc = seq // C
    zr_spec = lambda i: pl.BlockSpec((C, 3 * W), lambda c: (i * nc + c, ZS // (3 * W)))
    zl_spec = lambda i: pl.BlockSpec((C, LORA_PAD), lambda c: (i * nc + c, 0))
    return pl.pallas_call(
        functools.partial(_rwkv_prompt_kernel, batch=batch),
        grid=(nc,),
        in_specs=[zr_spec(i) for i in range(batch)] + [zl_spec(i) for i in range(batch)]
        + _rwkv_param_specs(),
        out_specs=[
            pl.BlockSpec((batch, C, W), lambda c: (0, c, 0)),
            pl.BlockSpec((batch, RWKV_PAIRS, LANES, LANES), lambda c: (0, 0, 0, 0)),
        ],
        out_shape=[
            jax.ShapeDtypeStruct((batch, seq, W), BF16),
            jax.ShapeDtypeStruct((batch, RWKV_PAIRS, LANES, LANES), F32),
        ],
        scratch_shapes=[pltpu.VMEM((batch, 3 * W), F32), pltpu.VMEM((batch, LORA_PAD), F32),
                        pltpu.VMEM((batch, RWKV_PAIRS, LANES, LANES), F32)],
        compiler_params=_cparams(("arbitrary",)),
        name="rwkv_prompt",
    )(*([z] * batch), *([zl] * batch), *params)


def _rwkv_sample(z, zl, sh_r, sh_l, s0, params, row0, batch, seq):
    C = RWKV_CHUNK
    W = RWKV_W
    n_seq = C // seq
    rb0 = row0 // C
    return pl.pallas_call(
        functools.partial(_rwkv_sample_kernel, n_seq=n_seq),
        grid=(batch // n_seq,),
        in_specs=[
            pl.BlockSpec((C, 3 * W), lambda i: (rb0 + i, ZS // (3 * W))),
            pl.BlockSpec((C, LORA_PAD), lambda i: (rb0 + i, 0)),
            pl.BlockSpec((C, 3 * W), lambda i: (i, 0)),
            pl.BlockSpec((C, LORA_PAD), lambda i: (i, 0)),
            pl.BlockSpec((n_seq, RWKV_HEADS, RWKV_N, RWKV_N), lambda i: (i, 0, 0, 0)),
        ] + _rwkv_param_specs(),
        out_specs=[
            pl.BlockSpec((C, W), lambda i: (i, 0)),
            pl.BlockSpec((n_seq, RWKV_HEADS, RWKV_N, RWKV_N), lambda i: (i, 0, 0, 0)),
        ],
        out_shape=[
            jax.ShapeDtypeStruct((batch * seq, W), BF16),
            jax.ShapeDtypeStruct((batch, RWKV_HEADS, RWKV_N, RWKV_N), F32),
        ],
        compiler_params=_cparams(("parallel",)),
        name="rwkv_sample",
    )(z, zl, sh_r, sh_l, s0, *params)


def _pair_unblock(s):
    b = s.shape[0]
    lo = s[:, :, :RWKV_N, :RWKV_N]
    hi = s[:, :, RWKV_N:, RWKV_N:]
    return jnp.stack([lo, hi], axis=2).reshape(b, RWKV_HEADS, RWKV_N, RWKV_N)


def _mix_out_kernel(oret_a_ref, oret_b_ref, orwkv_a_ref, orwkv_b_ref, xa_ref, xb_ref, gr_ref, gw_ref,
                    wr_ref, ww_ref, wo_ref, gn_ref, o_ref, hn_ref, *, tiles_a):
    first = pl.program_id(0) < tiles_a
    o_ret = jnp.where(first, oret_a_ref[...], oret_b_ref[...])
    o_rwkv = jnp.where(first, orwkv_a_ref[...], orwkv_b_ref[...])
    a = jnp.dot(o_ret, wr_ref[...], preferred_element_type=F32)
    b = jnp.dot(o_rwkv, ww_ref[...], preferred_element_type=F32)
    merged = jax.nn.sigmoid(gr_ref[...]) * a + jax.nn.sigmoid(gw_ref[...]) * b
    x2 = (jnp.where(first, xa_ref[...], xb_ref[...])
          + jnp.dot(merged.astype(BF16), wo_ref[...], preferred_element_type=F32))
    o_ref[...] = x2
    hn_ref[...] = _rms(x2, gn_ref[...], NORM_EPS).astype(BF16)


def _mix_out(o_rets, o_rwkvs, xs, zg, w_ret_o, w_rwkv_o, w_out, g_next):
    n = xs[0].shape[0] + xs[1].shape[0]
    tm = TM_MIX
    tiles_a = o_rets[0].shape[0] // tm
    resident = lambda shape: pl.BlockSpec(shape, lambda i: (0, 0), pipeline_mode=pl.Buffered(1))
    first = lambda i: (jnp.minimum(i, tiles_a - 1), 0)
    second = lambda i: (jnp.maximum(i - tiles_a, 0), 0)
    return pl.pallas_call(
        functools.partial(_mix_out_kernel, tiles_a=tiles_a),
        grid=(n // tm,),
        in_specs=[
            pl.BlockSpec((tm, RET_V), first),
            pl.BlockSpec((tm, RET_V), second),
            pl.BlockSpec((tm, RWKV_W), first),
            pl.BlockSpec((tm, RWKV_W), second),
            pl.BlockSpec((tm, D_MODEL), first),
            pl.BlockSpec((tm, D_MODEL), second),
            pl.BlockSpec((tm, D_MODEL), lambda i: (i, ZGR // D_MODEL)),
            pl.BlockSpec((tm, D_MODEL), lambda i: (i, ZGW // D_MODEL)),
            resident((RET_V, D_MODEL)),
            resident((RWKV_W, D_MODEL)),
            resident((D_MODEL, D_MODEL)),
            pl.BlockSpec((1, D_MODEL), lambda i: (0, 0)),
        ],
        out_specs=[pl.BlockSpec((tm, D_MODEL), lambda i: (i, 0)), pl.BlockSpec((tm, D_MODEL), lambda i: (i, 0))],
        out_shape=[jax.ShapeDtypeStruct((n, D_MODEL), F32), jax.ShapeDtypeStruct((n, D_MODEL), BF16)],
        compiler_params=_cparams(("arbitrary",)),
        name="mix_out",
    )(*o_rets, *o_rwkvs, *xs, zg, zg, w_ret_o, w_rwkv_o, w_out, g_next)


def kernel(x_prompt, x_sample, state_ret, state_rwkv, state_shift, norm_ffn1, ffn1_w_gu, ffn1_w_down, norm_mix, w_in, w_ret_o, rwkv_mu, rwkv_w0, rwkv_w_up, rwkv_a0, rwkv_a_up, rwkv_g_up, rwkv_k_k, rwkv_k_a, rwkv_r_k, rwkv_ln_w, rwkv_ln_b, w_rwkv_o, w_out, norm_ffn2, ffn2_w_gu, ffn2_w_down, norm_final):
    bp, tp, d = x_prompt.shape
    bs, ts, _ = x_sample.shape
    n_p = bp * tp
    n_s = bs * ts
    W = RWKV_W
    row = lambda a: a.reshape(1, -1)
    assert d == D_MODEL and n_p % TM == 0 and n_s == TM and n_p % TM_MIX == 0
    assert tp % RET_CHUNK == 0 and tp % RWKV_CHUNK == 0 and bp % RWKV_PROMPT_GROUP == 0 and bp % RET_PROMPT_SEQS == 0
    assert RWKV_CHUNK % ts == 0 and n_s % RWKV_CHUNK == 0 and RET_SAMPLE_ROWS % ts == 0 and n_s % RET_SAMPLE_ROWS == 0

    x1_s, h_s, *w_ffn1 = _ffn(x_sample.reshape(n_s, d), 0, n_s, row(norm_ffn1[0]),
                              (ffn1_w_gu[0], ffn1_w_down[0]), row(norm_mix[0]), final_norm=False, emit_h=True)
    x1_p, h_p = _ffn(x_prompt.reshape(n_p, d), 0, n_p, row(norm_ffn1[0]), w_ffn1, row(norm_mix[0]),
                     final_norm=False, emit_h=True)

    w_in_t = jnp.swapaxes(w_in[0], 0, 1)
    z = _proj((h_p, h_s), w_in_t, 0, TN_PROJ_MAIN, Z_MAIN_W)
    zl = _proj((h_p, h_s), w_in_t, Z_MAIN_W, LORA_PAD, LORA_PAD)
    zg = _proj((h_p, h_s), w_in_t, Z_MAIN_W + LORA_W, TN_PROJ, 2 * D_MODEL)

    lg = jnp.log(1.0 - 2.0 ** (-5.0 - jnp.arange(RET_HEADS, dtype=F32)))
    cos_p, sin_p = _rope_tables(jnp.arange(tp, dtype=F32))
    cos_s, sin_s = _rope_tables(PAST_LEN + jnp.arange(ts, dtype=F32))
    rep = RET_SAMPLE_ROWS // ts
    oret_p, ret_p = _ret_prompt(z, lg, cos_p, sin_p, bp, tp)
    oret_s, ret_s = _ret_sample(z, lg, jnp.tile(cos_s, (rep, 1)), jnp.tile(sin_s, (rep, 1)),
                                state_ret[0], n_p, bs, ts)

    mu = rwkv_mu[0]
    pad_l = lambda a: jnp.pad(a, [(0, 0)] * (a.ndim - 1) + [(0, LORA_PAD - LORA_W)])
    seg = (jnp.arange(LANES)[:, None] // RWKV_N == jnp.arange(LANES)[None, :] // RWKV_N).astype(BF16)
    zero64 = jnp.zeros((DECAY_LORA, W), F32)
    params = (
        row(mu[:3 * W]), pad_l(row(mu[3 * W:])),
        row(rwkv_w0[0]), jnp.concatenate([rwkv_w_up[0], zero64], axis=0).astype(BF16),
        row(rwkv_a0[0]), jnp.concatenate([zero64, rwkv_a_up[0]], axis=0).astype(BF16),
        jnp.pad(rwkv_g_up[0], ((0, LORA_PAD - LANES - GATE_LORA), (0, 0))).astype(BF16),
        row(rwkv_k_k[0]), row(rwkv_k_a[0]), row(rwkv_r_k[0]), row(rwkv_ln_w[0]), row(rwkv_ln_b[0]),
        seg,
    )
    orw_p, rwkv_p = _rwkv_prompt(z, zl, params, bp, tp)
    sh_s = state_shift[0]
    first_rows = lambda a: jnp.pad(a[:, None, :], ((0, 0), (0, ts - 1), (0, 0))).reshape(n_s, -1)
    orw_s, rwkv_s = _rwkv_sample(z, zl, first_rows(sh_s[:, :3 * W]), first_rows(pad_l(sh_s[:, 3 * W:])),
                                 state_rwkv[0], params, n_p, bs, ts)

    x2, h_ffn2 = _mix_out((oret_p.reshape(n_p, RET_V), oret_s), (orw_p.reshape(n_p, W), orw_s), (x1_p, x1_s), zg,
                          w_ret_o[0].astype(BF16), w_rwkv_o[0].astype(BF16), w_out[0].astype(BF16),
                          row(norm_ffn2[0]))
    y_s, *w_ffn2 = _ffn(x2, n_p, n_s, row(norm_ffn2[0]), (ffn2_w_gu[0], ffn2_w_down[0]), row(norm_final),
                        final_norm=True, emit_h=False, h=h_ffn2)
    (y_p,) = _ffn(x2, 0, n_p, row(norm_ffn2[0]), w_ffn2, row(norm_final),
                  final_norm=True, emit_h=False, h=h_ffn2)

    def last_rows(firsts, t):
        rows = [f + t - 1 for f in firsts]
        zr_last = jnp.concatenate([z[r:r + 1, ZS:ZS + 3 * W] for r in rows], axis=0)
        zl_last = jnp.concatenate([zl[r:r + 1, :LORA_W] for r in rows], axis=0)
        return jnp.concatenate([zr_last, zl_last], axis=-1)[None]

    shift_s = jnp.concatenate([lax.slice(z, (n_p + ts - 1, ZS), (n_p + n_s, ZS + 3 * W), (ts, 1)),
                               lax.slice(zl, (n_p + ts - 1, 0), (n_p + n_s, LORA_W), (ts, 1))], axis=-1)[None]
    return (y_p.reshape(bp, tp, d), y_s.reshape(bs, ts, d),
            ret_p[None], _pair_unblock(rwkv_p)[None], last_rows(range(0, n_p, tp), tp),
            ret_s[None], rwkv_s[None], shift_s)
```

```python
import functools

import jax
import jax.numpy as jnp
from jax import lax
from jax.experimental import pallas as pl
from jax.experimental.pallas import tpu as pltpu

F32 = jnp.float32
BF16 = jnp.bfloat16

D_MODEL = 2048
PAST_LEN = 16384
RET_HEADS = 4
RET_DK = 256
RET_DV = 512
RET_CHUNK = 128
ROPE_BASE = 10000.0
RET_QK = RET_HEADS * RET_DK
RET_V = RET_HEADS * RET_DV
RWKV_HEADS = 16
RWKV_N = 64
RWKV_W = RWKV_HEADS * RWKV_N
DECAY_LORA = 64
AAA_LORA = 64
GATE_LORA = 160
LORA_W = DECAY_LORA + AAA_LORA + GATE_LORA
LORA_PAD = 384
D_FF = 5632
NORM_EPS = 1e-6
GN_EPS_RET = 1e-6
GN_EPS_RWKV = 64e-5

ZQ, ZK, ZV, ZG, ZS = 0, 1024, 2048, 4096, 6144
Z_MAIN_W = 9216
ZGR, ZGW = 0, 2048

LANES = 128
SUBLANES = 8
MXU_TILE = 256
RWKV_PAIRS = RWKV_W // LANES
RWKV_CHUNK = 64
RWKV_PROMPT_GROUP = 2
RWKV_PROMPT_LEAD = 21

V7X_VMEM_BYTES = 64 * 1024 * 1024
VMEM_LIMIT = V7X_VMEM_BYTES - 8 * 1024 * 1024

TM = 512
TF = 512
TF_CAST = 256
TN_PROJ = 1024
TN_PROJ_MAIN = 1536
TM_MIX = 256


def _rms(x, g, eps):
    return x * lax.rsqrt(jnp.mean(x * x, axis=-1, keepdims=True) + eps) * g


def _dot(a, b):
    return jnp.dot(a.astype(BF16), b.astype(BF16), preferred_element_type=F32)


def _dot_nt(a, b):
    return lax.dot_general(a.astype(BF16), b.astype(BF16), (((1,), (1,)), ((), ())),
                           preferred_element_type=F32)


def _dot_tn(a, b):
    return lax.dot_general(a.astype(BF16), b.astype(BF16), (((0,), (0,)), ((), ())),
                           preferred_element_type=F32)


def _bf16_terms(x, n):
    terms = []
    for _ in range(n):
        t = x.astype(BF16)
        terms.append(t)
        x = x - t.astype(F32)
    return terms


def _cparams(sem):
    return pltpu.CompilerParams(dimension_semantics=sem, vmem_limit_bytes=VMEM_LIMIT)


def _ffn_kernel(*refs, final_norm, emit_h, h_in, cast_w):
    if h_in:
        hin_ref, refs = refs[0], refs[1:]
    x_ref, g_ref, wg_ref, wu_ref, wd_ref, gf_ref = refs[:6]
    refs = refs[6:]
    if emit_h:
        wl_ref, refs = refs[0], refs[1:]
    outs = list(refs[:-2])
    h_ref, acc_ref = refs[-2:]
    o_ref = outs.pop(0)
    hn_ref, zl_ref = (outs.pop(0), outs.pop(0)) if emit_h else (None, None)
    j = pl.program_id(1)

    @pl.when(j == 0)
    def _():
        if not h_in:
            h_ref[...] = _rms(x_ref[...], g_ref[...], NORM_EPS).astype(BF16)
        acc_ref[...] = jnp.zeros_like(acc_ref)

    if cast_w:
        wg, wu, wd = wg_ref[...].astype(BF16), wu_ref[...].astype(BF16), wd_ref[...].astype(BF16)
        outs[0][0], outs[1][0], outs[2][...] = wg, wu, wd
    else:
        wg, wu, wd = wg_ref[0], wu_ref[0], wd_ref[...]
    h = hin_ref[...] if h_in else h_ref[...]
    tf = wg.shape[1]
    half = tf // 2 if tf // 2 >= MXU_TILE else tf
    cols = tuple(slice(c, c + half) for c in range(0, tf, half))
    gus = [(jnp.dot(h, wg[:, c], preferred_element_type=F32), jnp.dot(h, wu[:, c], preferred_element_type=F32))
           for c in cols]
    downs = [jnp.dot(((g * jax.nn.sigmoid(g)) * u).astype(BF16), wd[c, :], preferred_element_type=F32)
             for (g, u), c in zip(gus, cols)]
    acc_ref[...] += sum(downs[1:], downs[0])

    @pl.when(j == pl.num_programs(1) - 1)
    def _():
        y = x_ref[...] + 0.5 * acc_ref[...]
        if final_norm:
            y = _rms(y, gf_ref[...], NORM_EPS)
        if emit_h:
            hn = _rms(y, gf_ref[...], NORM_EPS).astype(BF16)
            hn_ref[...] = hn
            zl_ref[...] = _dot_nt(hn, wl_ref[...])
        o_ref[...] = y


def _ffn(x, row0, rows, g, weights, g_after, final_norm, h=None, w_next_t=None):
    cast_w = len(weights) == 2
    emit_h = w_next_t is not None
    tf = TF_CAST if cast_w else TF
    nj = D_FF // tf
    t0 = row0 // TM
    tile = pl.BlockSpec((TM, D_MODEL), lambda i, j: (t0 + i, 0))
    out_tile = pl.BlockSpec((TM, D_MODEL), lambda i, j: (i, 0))
    h_in = h is not None
    if cast_w:
        w_specs = [pl.BlockSpec((D_MODEL, tf), lambda i, j: (0, j)),
                   pl.BlockSpec((D_MODEL, tf), lambda i, j: (0, j + nj)),
                   pl.BlockSpec((tf, D_MODEL), lambda i, j: (j, 0))]
        w_args = (weights[0], weights[0], weights[1])
    else:
        w_specs = [pl.BlockSpec((1, D_MODEL, tf), lambda i, j: (j, 0, 0)),
                   pl.BlockSpec((1, D_MODEL, tf), lambda i, j: (j, 0, 0)),
                   pl.BlockSpec((tf, D_MODEL), lambda i, j: (j, 0))]
        w_args = weights
    out_specs = [out_tile]
    out_shape = [jax.ShapeDtypeStruct((rows, D_MODEL), F32)]
    next_args, next_specs = (), []
    if emit_h:
        ncol = w_next_t.shape[0]
        out_specs += [out_tile, pl.BlockSpec((TM, ncol), lambda i, j: (i, 0))]
        out_shape += [jax.ShapeDtypeStruct((rows, D_MODEL), BF16), jax.ShapeDtypeStruct((rows, ncol), F32)]
        next_args, next_specs = (w_next_t,), [pl.BlockSpec((ncol, D_MODEL), lambda i, j: (0, 0))]
    if cast_w:
        per = TF // tf
        tile_major = pl.BlockSpec((1, D_MODEL, tf), lambda i, j: (j // per, 0, j % per))
        out_specs += [tile_major, tile_major, pl.BlockSpec((tf, D_MODEL), lambda i, j: (j, 0))]
        out_shape += [jax.ShapeDtypeStruct((D_FF // TF, D_MODEL, TF), BF16)] * 2 + [
            jax.ShapeDtypeStruct((D_FF, D_MODEL), BF16)]
        assert rows == TM, "each weight tile is written exactly once"
    return pl.pallas_call(
        functools.partial(_ffn_kernel, final_norm=final_norm, emit_h=emit_h, h_in=h_in, cast_w=cast_w),
        grid=(rows // TM, nj),
        in_specs=([tile] if h_in else []) + [tile, pl.BlockSpec((1, D_MODEL), lambda i, j: (0, 0))] + w_specs
        + [pl.BlockSpec((1, D_MODEL), lambda i, j: (0, 0))] + next_specs,
        out_specs=out_specs,
        out_shape=out_shape,
        scratch_shapes=[pltpu.VMEM((TM, D_MODEL), BF16), pltpu.VMEM((TM, D_MODEL), F32)],
        compiler_params=_cparams(("parallel", "arbitrary")),
        name="ffn_cast" if cast_w else "ffn",
    )(*((h,) if h_in else ()), x, g, *w_args, g_after, *next_args)


def _proj_kernel(ha_ref, hb_ref, wt_ref, o_ref, w_ref, *, tiles_a):
    i = pl.program_id(1)

    @pl.when(i == 0)
    def _():
        w_ref[...] = wt_ref[...].astype(BF16)

    o_ref[...] = _dot_nt(jnp.where(i < tiles_a, ha_ref[...], hb_ref[...]), w_ref[...])


def _proj(hs, w_t, col0, tn, nout):
    tiles_a = hs[0].shape[0] // TM
    n = hs[0].shape[0] + hs[1].shape[0]
    return pl.pallas_call(
        functools.partial(_proj_kernel, tiles_a=tiles_a),
        grid=(nout // tn, n // TM),
        in_specs=[
            pl.BlockSpec((TM, D_MODEL), lambda j, i: (jnp.minimum(i, tiles_a - 1), 0)),
            pl.BlockSpec((TM, D_MODEL), lambda j, i: (jnp.maximum(i - tiles_a, 0), 0)),
            pl.BlockSpec((pl.Element(tn), pl.Element(D_MODEL)),
                         lambda j, i: (pl.multiple_of(col0 + j * tn, SUBLANES), 0)),
        ],
        out_specs=pl.BlockSpec((TM, tn), lambda j, i: (i, j)),
        out_shape=jax.ShapeDtypeStruct((n, nout), F32),
        scratch_shapes=[pltpu.VMEM((tn, D_MODEL), BF16)],
        compiler_params=_cparams(("parallel", "arbitrary")),
        name="proj",
    )(*hs, w_t)


def _pair_swap(x):
    n = x.shape[-1]
    lane = lax.broadcasted_iota(jnp.int32, x.shape, x.ndim - 1)
    prev = pltpu.roll(x, 1, x.ndim - 1)
    nxt = pltpu.roll(x, n - 1, x.ndim - 1)
    return jnp.where((lane & 1) == 1, prev, nxt)


def _rotary(x, cos, sin_signed):
    return x * cos + _pair_swap(x) * sin_signed


def _ret_finish(o, zg):
    o = o * lax.rsqrt(jnp.mean(o * o, axis=-1, keepdims=True) + GN_EPS_RET)
    return o * (zg * jax.nn.sigmoid(zg))


RET_PROMPT_SEQS = 2


def _ret_prompt_kernel(lg_ref, *refs):
    nb = RET_PROMPT_SEQS
    q_refs, k_refs, v_refs, g_refs = (refs[i * nb:(i + 1) * nb] for i in range(4))
    cos_ref, sin_ref, o_ref, s_out_ref, s_ref = refs[4 * nb:]
    c = pl.program_id(1)
    C = RET_CHUNK

    @pl.when(c == 0)
    def _():
        s_ref[...] = jnp.zeros_like(s_ref)

    cos = cos_ref[...]
    sin = sin_ref[...]
    ti = lax.broadcasted_iota(jnp.int32, (C, C), 0)
    tj = lax.broadcasted_iota(jnp.int32, (C, C), 1)
    rel = (ti - tj).astype(F32)
    idx = lax.broadcasted_iota(jnp.int32, (C, 1), 0).astype(F32)
    H = range(RET_HEADS)
    lgs = [lg_ref[h] for h in H]
    finals = []

    def sequence(b):
        qs = [_rotary(q_refs[b][:, h * RET_DK:(h + 1) * RET_DK], cos, sin) for h in H]
        ks = [_rotary(k_refs[b][:, h * RET_DK:(h + 1) * RET_DK], cos, sin) * (RET_DK ** -0.5) for h in H]
        yield
        vs = [v_refs[b][:, h * RET_DV:(h + 1) * RET_DV] for h in H]
        ss = [s_ref[b, h] for h in H]
        scores = [_dot_nt(q, k) * jnp.where(rel >= 0, jnp.exp(lg * jnp.maximum(rel, 0.0)), 0.0)
                  for q, k, lg in zip(qs, ks, lgs)]
        crosses = [_dot(q, s) * jnp.exp(lg * (idx + 1.0)) for q, s, lg in zip(qs, ss, lgs)]
        inners = [_dot(sc, v) for sc, v in zip(scores, vs)]
        s_news = [jnp.exp(lg * C) * s + _dot_tn(k * jnp.exp(lg * (C - 1.0 - idx)), v)
                  for lg, s, k, v in zip(lgs, ss, ks, vs)]
        yield
        for h in H:
            s_ref[b, h] = s_news[h]
            o_ref[b, :, h * RET_DV:(h + 1) * RET_DV] = _ret_finish(
                inners[h] + crosses[h], g_refs[b][:, h * RET_DV:(h + 1) * RET_DV]).astype(BF16)
        finals.append((b, s_news))

    _run_staged([sequence(b) for b in range(nb)], 1)

    @pl.when(c == pl.num_programs(1) - 1)
    def _():
        for b, s_news in finals:
            for h in H:
                s_out_ref[b, h] = s_news[h]


def _ret_prompt(z, lg, cos, sin, batch, seq):
    nc = seq // RET_CHUNK
    C = RET_CHUNK
    nb = RET_PROMPT_SEQS
    cols = lambda width, col0: [pl.BlockSpec((C, width), lambda g, c, b=b: ((g * nb + b) * nc + c, col0 // width))
                                for b in range(nb)]
    return pl.pallas_call(
        _ret_prompt_kernel,
        grid=(batch // nb, nc),
        in_specs=[pl.BlockSpec(memory_space=pltpu.SMEM)]
        + cols(RET_QK, ZQ) + cols(RET_QK, ZK) + cols(RET_V, ZV) + cols(RET_V, ZG)
        + [pl.BlockSpec((C, RET_DK), lambda g, c: (c, 0)), pl.BlockSpec((C, RET_DK), lambda g, c: (c, 0))],
        out_specs=[
            pl.BlockSpec((nb, C, RET_V), lambda g, c: (g, c, 0)),
            pl.BlockSpec((nb, RET_HEADS, RET_DK, RET_DV), lambda g, c: (g, 0, 0, 0)),
        ],
        out_shape=[
            jax.ShapeDtypeStruct((batch, seq, RET_V), BF16),
            jax.ShapeDtypeStruct((batch, RET_HEADS, RET_DK, RET_DV), F32),
        ],
        scratch_shapes=[pltpu.VMEM((nb, RET_HEADS, RET_DK, RET_DV), F32)],
        compiler_params=_cparams(("parallel", "arbitrary")),
        name="ret_prompt",
    )(lg, *([z] * (4 * nb)), cos, sin)


RET_SAMPLE_ROWS = 16


def _ret_sample_kernel(lg_ref, q_ref, k_ref, v_ref, g_ref, cos_ref, sin_ref, s0_ref,
                       o_ref, s_out_ref, *, seq):
    R = RET_SAMPLE_ROWS
    nb = R // seq
    cos = cos_ref[...]
    sin = sin_ref[...]
    ri = lax.broadcasted_iota(jnp.int32, (R, R), 0)
    rj = lax.broadcasted_iota(jnp.int32, (R, R), 1)
    rel = ((ri % seq) - (rj % seq)).astype(F32)
    ok = ((ri // seq) == (rj // seq)) & (rel >= 0)
    row = lax.broadcasted_iota(jnp.int32, (R, 1), 0)
    t = (row % seq).astype(F32)

    for h in range(RET_HEADS):
        lg = lg_ref[h]
        q = _rotary(q_ref[:, h * RET_DK:(h + 1) * RET_DK], cos, sin)
        k = _rotary(k_ref[:, h * RET_DK:(h + 1) * RET_DK], cos, sin) * (RET_DK ** -0.5)
        v = v_ref[:, h * RET_DV:(h + 1) * RET_DV]
        mask = jnp.where(ok, jnp.exp(lg * jnp.maximum(rel, 0.0)), 0.0)
        q_dec = jnp.exp(lg * (t + 1.0))
        kd = k * jnp.exp(lg * (seq - 1.0 - t))
        inner = _dot(_dot_nt(q, k) * mask, v)
        cross = jnp.zeros_like(inner)
        for b in range(nb):
            s = s0_ref[b, h]
            mine = (row // seq) == b
            cross = jnp.where(mine, _dot(q, s), cross)
            s_out_ref[b, h] = jnp.exp(lg * seq) * s + _dot_tn(jnp.where(mine, kd, 0.0), v)
        o_ref[:, h * RET_DV:(h + 1) * RET_DV] = _ret_finish(
            inner + cross * q_dec, g_ref[:, h * RET_DV:(h + 1) * RET_DV]).astype(BF16)


def _ret_sample(z, lg, cos, sin, s0, row0, batch, seq):
    R = RET_SAMPLE_ROWS
    nb = R // seq
    rb0 = row0 // R
    return pl.pallas_call(
        functools.partial(_ret_sample_kernel, seq=seq),
        grid=(batch // nb,),
        in_specs=[
            pl.BlockSpec(memory_space=pltpu.SMEM),
            pl.BlockSpec((R, RET_QK), lambda i: (rb0 + i, ZQ // RET_QK)),
            pl.BlockSpec((R, RET_QK), lambda i: (rb0 + i, ZK // RET_QK)),
            pl.BlockSpec((R, RET_V), lambda i: (rb0 + i, ZV // RET_V)),
            pl.BlockSpec((R, RET_V), lambda i: (rb0 + i, ZG // RET_V)),
            pl.BlockSpec((R, RET_DK), lambda i: (0, 0)),
            pl.BlockSpec((R, RET_DK), lambda i: (0, 0)),
            pl.BlockSpec((nb, RET_HEADS, RET_DK, RET_DV), lambda i: (i, 0, 0, 0)),
        ],
        out_specs=[
            pl.BlockSpec((R, RET_V), lambda i: (i, 0)),
            pl.BlockSpec((nb, RET_HEADS, RET_DK, RET_DV), lambda i: (i, 0, 0, 0)),
        ],
        out_shape=[
            jax.ShapeDtypeStruct((batch * seq, RET_V), BF16),
            jax.ShapeDtypeStruct((batch, RET_HEADS, RET_DK, RET_DV), F32),
        ],
        compiler_params=_cparams(("parallel",)),
        name="ret_sample",
    )(lg, z, z, z, z, cos, sin, s0)


def _rope_tables(pos):
    half = RET_DK // 2
    inv = 1.0 / (ROPE_BASE ** jnp.linspace(0.0, 1.0, half, dtype=F32))
    ang = pos[:, None] * inv[None, :]
    cos = jnp.repeat(jnp.cos(ang), 2, axis=-1)
    sin = jnp.sin(ang)
    sin_signed = jnp.stack([-sin, sin], axis=-1).reshape(pos.shape[0], RET_DK)
    return cos, sin_signed


def _segsum(x, seg_ref):
    rows = x.shape[0]
    n = rows * RWKV_PAIRS
    xs = jnp.concatenate([x[:, p * LANES:(p + 1) * LANES] for p in range(RWKV_PAIRS)], axis=0)
    ss = jnp.dot(jnp.concatenate(_bf16_terms(xs, 2), axis=0), seg_ref[...], preferred_element_type=F32)
    ss = ss[0:n] + ss[n:2 * n]
    return jnp.concatenate([ss[p * rows:(p + 1) * rows] for p in range(RWKV_PAIRS)], axis=1)


def _run_staged(gens, lead):
    out = [None] * len(gens)
    live = [True] * len(gens)

    def step(i):
        try:
            next(gens[i])
        except StopIteration as stop:
            out[i] = stop.value
            live[i] = False

    for _ in range(lead):
        if live[0]:
            step(0)
    while any(live):
        for i in range(len(gens)):
            if live[i]:
                step(i)
    return out


def _each(fn, *lists):
    out = []
    for args in zip(*lists):
        out.append(fn(*args))
        yield
    return out


def _rwkv_chunk(zr, zl, prev_r, prev_l, prm, n_seq, **state_fns):
    front = yield from _rwkv_front(zr, zl, prev_r, prev_l, prm, n_seq)
    return (yield from _rwkv_back(*front, prm, n_seq, **state_fns))


def _rwkv_front(zr, zl, prev_r, prev_l, prm, n_seq):
    (mur_ref, mul_ref, w0_ref, wup_ref, a0_ref, aup_ref, gup_ref,
     kk_ref, ka_ref, rk_ref, lnw_ref, lnb_ref, seg_ref) = prm
    C = RWKV_CHUNK
    W = RWKV_W
    R = zr.shape[0]
    TS = C // n_seq
    row = lax.broadcasted_iota(jnp.int32, (R, 1), 0)
    first = (row % TS) == 0

    def mix(z, prev, mu_ref):
        z_prev = jnp.where(first, prev, pltpu.roll(z, 1, 0))
        return z + (z_prev - z) * mu_ref[...]

    xs_r = mix(zr[:, 0:W], prev_r[:, 0:W], mur_ref.at[:, 0:W])
    yield
    xs_k = mix(zr[:, W:2 * W], prev_r[:, W:2 * W], mur_ref.at[:, W:2 * W])
    yield
    xs_v = mix(zr[:, 2 * W:3 * W], prev_r[:, 2 * W:3 * W], mur_ref.at[:, 2 * W:3 * W])
    yield
    xl = mix(zl, prev_l, mul_ref)
    r, k, v = xs_r, xs_k, xs_v
    x_wa = xl[:, 0:LANES]
    x_g = xl[:, LANES:LORA_PAD]

    y_w = w0_ref[...] + _dot(jnp.tanh(x_wa), wup_ref[...])
    yield
    w_log = -(jnp.maximum(-y_w, 0.0) + jnp.log(1.0 + jnp.exp(-jnp.abs(y_w)))) - 0.5
    lw = -jnp.exp(w_log)
    yield
    a_rate = jax.nn.sigmoid(a0_ref[...] + _dot(x_wa, aup_ref[...]))
    yield
    g = _dot(jax.nn.sigmoid(x_g), gup_ref[...])
    yield

    kk = k * kk_ref[...]
    k = k * (1.0 + (a_rate - 1.0) * ka_ref[...])
    yield
    sums = _segsum(jnp.concatenate([kk * kk, r * k * rk_ref[...]], axis=0), seg_ref)
    yield
    kk = kk * lax.rsqrt(jnp.maximum(sums[0:R], 1e-24))
    bonus_rk = sums[R:2 * R]
    a = -kk
    b = kk * a_rate
    yield

    rows_sel = R if n_seq == 1 else 2 * R
    ti = lax.broadcasted_iota(jnp.int32, (rows_sel, R), 0)
    tj = lax.broadcasted_iota(jnp.int32, (rows_sel, R), 1)
    same = ((ti % R) // TS) == (tj // TS)
    sel = (same & ((ti >= R) | (tj <= ti))).astype(BF16)
    sums_lw = sum(jnp.dot(sel, t, preferred_element_type=F32) for t in _bf16_terms(lw, 3))
    yield
    cum = sums_lw[0:R]
    if n_seq == 1:
        tot = jnp.concatenate([jnp.broadcast_to(cum[i + C - 1:i + C], (C, W)) for i in range(0, R, C)], axis=0)
    else:
        tot = sums_lw[R:2 * R]
    e_neg = jnp.exp(-cum)
    yield
    a_t = (a * jnp.exp(cum - lw)).astype(BF16)
    yield
    b_t = (b * e_neg).astype(BF16)
    yield
    k_t = (k * e_neg).astype(BF16)
    yield
    r_t = r * jnp.exp(cum)
    yield
    e_tail = jnp.exp(tot - cum)
    yield
    b_g = (b * e_tail).astype(BF16)
    yield
    k_g = (k * e_tail).astype(BF16)
    dec = jnp.exp(tot)
    yield
    return (a_t, b_t, k_t, b_g, k_g), (r_t, v, dec, g, bonus_rk)


def _rwkv_back(mm_operands, f32_arrays, prm, n_seq, read_state=None, update_state=None, fold_state=None):
    a_t, b_t, k_t, b_g, k_g = mm_operands
    r_t, v, dec, g, bonus_rk = f32_arrays
    lnw_ref, lnb_ref, seg_ref = prm[10:13]
    C = RWKV_CHUNK
    R = r_t.shape[0]
    TS = C // n_seq

    lane = lax.broadcasted_iota(jnp.int32, (1, LANES), 1)
    lo = lane < RWKV_N

    def split(x):
        return jnp.concatenate([jnp.where(lo, x, 0.0), jnp.where(lo, 0.0, x)], axis=0)

    row4 = lax.broadcasted_iota(jnp.int32, (C, 4 * C), 0)
    col4 = lax.broadcasted_iota(jnp.int32, (C, 4 * C), 1) % C
    same4 = (row4 // TS) == (col4 // TS)
    strict = same4 & (col4 < row4)
    incl = same4 & (col4 <= row4)
    row2 = lax.broadcasted_iota(jnp.int32, (C, 2 * C), 0)
    col2 = lax.broadcasted_iota(jnp.int32, (C, 2 * C), 1)
    eye2 = ((col2 % C) == row2).astype(F32)
    left = col2 < C

    tiles = [(slice(i * C, (i + 1) * C), slice(p * LANES, (p + 1) * LANES))
             for i in range(R // C) for p in range(RWKV_PAIRS)]
    v2s = [split(v[rs, sl]) for rs, sl in tiles]
    ars = [jnp.concatenate([a_t[rs, sl], r_t[rs, sl]], axis=0) for rs, sl in tiles]
    if fold_state is None:
        wys = read_state(ars)
    gms = yield from _each(
        lambda ar, t: _dot_nt(ar, jnp.concatenate([split(b_t[t[0], t[1]]), split(k_t[t[0], t[1]])], axis=0)),
        ars, tiles)
    gas = [jnp.where(strict, gm[0:C], 0.0) for gm in gms]
    grs = [jnp.where(incl, gm[C:2 * C], 0.0) for gm in gms]

    def blockdiag(x):
        return jnp.concatenate([jnp.where(left, x, 0.0), jnp.where(left, 0.0, x)], axis=0)

    ts = [eye2 + ga[:, 0:2 * C] for ga in gas]
    n_sq = TS.bit_length() - 2
    if n_sq >= 1:
        pws = yield from _each(lambda ga: _dot(ga[:, 0:2 * C], blockdiag(ga[:, 0:2 * C])), gas)
        for _ in range(n_sq - 1):
            tps = yield from _each(lambda t, pw: _dot(jnp.concatenate([t, pw], axis=0), blockdiag(pw)), ts, pws)
            ts = [t + tp[0:C] for t, tp in zip(ts, tps)]
            pws = [tp[C:2 * C] for tp in tps]
        ts = yield from _each(lambda t, pw: t + _dot(t, blockdiag(pw)), ts, pws)
    wis = yield from _each(lambda ga, v2: _dot(ga[:, 2 * C:4 * C], v2), gas, v2s)
    bks = [jnp.concatenate([b_g[rs, sl], k_g[rs, sl]], axis=0) for rs, sl in tiles]
    if fold_state is None:
        us = [_dot(t, split(wi + wy[0:C])) for t, wi, wy in zip(ts, wis, wys)]
        ys = [wy[C:2 * C] + _dot(gr, jnp.concatenate([split(u), v2], axis=0))
              for wy, gr, u, v2 in zip(wys, grs, us, v2s)]
        update_state([jnp.concatenate([u, v[rs, sl]], axis=0) for u, (rs, sl) in zip(us, tiles)], bks, dec)
    else:
        zc = jnp.zeros((C, LANES), F32)
        aus = yield from _each(
            lambda t, wi, tl: _dot(t, jnp.concatenate([split(a_t[tl[0], tl[1]]), split(wi)], axis=1)),
            ts, wis, tiles)
        rys = yield from _each(
            lambda gr, au, v2: _dot(gr, jnp.concatenate(
                [jnp.concatenate([split(au[:, 0:LANES]), split(au[:, LANES:2 * LANES])], axis=1),
                 jnp.concatenate([jnp.concatenate([zc, zc], axis=0), v2], axis=1)], axis=0)),
            grs, aus, v2s)
        mns = yield from _each(
            lambda bk, au, tl: _dot_tn(bk, jnp.concatenate(
                [au, jnp.concatenate([zc, v[tl[0], tl[1]]], axis=1)], axis=0)),
            bks, aus, tiles)
        ys = fold_state([r_t[rs, sl] + ry[:, 0:LANES] for ry, (rs, sl) in zip(rys, tiles)],
                        [ry[:, LANES:2 * LANES] for ry in rys], mns, dec)

    yield
    y = jnp.concatenate([jnp.concatenate(ys[i * RWKV_PAIRS:(i + 1) * RWKV_PAIRS], axis=1)
                         for i in range(R // C)], axis=0)
    mean = _segsum(y, seg_ref) * (1.0 / RWKV_N)
    d = y - mean
    var = _segsum(d * d, seg_ref) * (1.0 / RWKV_N)
    yn = d * lax.rsqrt(var + GN_EPS_RWKV) * lnw_ref[...] + lnb_ref[...]
    return ((yn + bonus_rk * v) * g).astype(BF16)


def _rwkv_prompt_kernel(*refs, batch):
    zr_refs = refs[:batch]
    zl_refs = refs[batch:2 * batch]
    prm = refs[2 * batch:2 * batch + 13]
    o_ref, s_out_ref, prevr_ref, prevl_ref, h_ref = refs[2 * batch + 13:]
    C = RWKV_CHUNK
    c = pl.program_id(0)

    @pl.when(c == 0)
    def _():
        prevr_ref[...] = jnp.zeros_like(prevr_ref)
        prevl_ref[...] = jnp.zeros_like(prevl_ref)
        h_ref[...] = jnp.zeros_like(h_ref)

    ki = lax.broadcasted_iota(jnp.int32, (LANES, LANES), 0)
    vi = lax.broadcasted_iota(jnp.int32, (LANES, LANES), 1)
    diag_blocks = (ki < RWKV_N) == (vi < RWKV_N)

    def group(i0):
        seqs = range(i0, i0 + RWKV_PROMPT_GROUP)
        tiles = [(i, p) for i in seqs for p in range(RWKV_PAIRS)]
        zr = jnp.concatenate([zr_refs[i][...] for i in seqs], axis=0)
        zl = jnp.concatenate([zl_refs[i][...] for i in seqs], axis=0)
        rows = lambda ref: jnp.concatenate(
            [jnp.broadcast_to(ref[i:i + 1, :], (C, ref.shape[1])) for i in seqs], axis=0)

        def fold_state(r_hats, y_inds, mns, dec):
            dec_cols = [jnp.broadcast_to(dec[(i - i0) * C:(i - i0) * C + 1, p * LANES:(p + 1) * LANES],
                                         (LANES, LANES)).T for i, p in tiles]
            hs = [h_ref[i, p] for i, p in tiles]
            res = [_dot(jnp.concatenate([rh, jnp.where(diag_blocks, mn[:, 0:LANES], 0.0)], axis=0), h)
                   for rh, mn, h in zip(r_hats, mns, hs)]
            for (i, p), h, dc, rs, mn in zip(tiles, hs, dec_cols, res, mns):
                h_ref[i, p] = h * dc + rs[C:C + LANES] + jnp.where(diag_blocks, mn[:, LANES:2 * LANES], 0.0)
            return [rs[0:C] + yi for rs, yi in zip(res, y_inds)]

        y = yield from _rwkv_chunk(zr, zl, rows(prevr_ref), rows(prevl_ref), prm, 1, fold_state=fold_state)
        for n, i in enumerate(seqs):
            o_ref[i] = y[n * C:(n + 1) * C]
            prevr_ref[i:i + 1, :] = zr[(n + 1) * C - 1:(n + 1) * C, :]
            prevl_ref[i:i + 1, :] = zl[(n + 1) * C - 1:(n + 1) * C, :]

    _run_staged([group(i0) for i0 in range(0, batch, RWKV_PROMPT_GROUP)], RWKV_PROMPT_LEAD)

    @pl.when(c == pl.num_programs(0) - 1)
    def _():
        for i in range(batch):
            for p in range(RWKV_PAIRS):
                s_out_ref[i, p] = h_ref[i, p].T


def _rwkv_sample_kernel(zr_ref, zl_ref, shr_ref, shl_ref, s0_ref, *rest, n_seq):
    prm = rest[:13]
    o_ref, s_out_ref = rest[13:]
    C = RWKV_CHUNK
    N = RWKV_N
    TS = C // n_seq
    tokseq = (lax.broadcasted_iota(jnp.int32, (1, LANES), 1) % C) // TS

    def pick(x, j):
        return jnp.where(tokseq == j, x, 0.0)

    def stacked(h):
        return s0_ref[:, h].reshape(n_seq * N, N)

    def read_state(ars):
        res = [[_dot_nt(stacked(2 * p + e), ar[:, e * N:(e + 1) * N]) for e in range(2)]
               for p, ar in enumerate(ars)]
        out = []
        for rp in res:
            halves = []
            for re in rp:
                acc = pick(re[0:N], 0)
                for j in range(1, n_seq):
                    acc = acc + pick(re[j * N:(j + 1) * N], j)
                halves.append(acc)
            out.append(jnp.concatenate(halves, axis=0).T)
        return out

    def update_state(uvs, bks, dec):
        uvts = [uv.T for uv in uvs]
        upds = [[_dot(jnp.concatenate([pick(uvt[e * N:(e + 1) * N], j) for j in range(n_seq)], axis=0),
                      bk[:, e * N:(e + 1) * N]) for e in range(2)]
                for uvt, bk in zip(uvts, bks)]
        for p, up in enumerate(upds):
            for e, upd in enumerate(up):
                h = 2 * p + e
                dec_h = jnp.concatenate(
                    [jnp.broadcast_to(dec[j * TS:j * TS + 1, h * N:(h + 1) * N], (N, N)) for j in range(n_seq)],
                    axis=0)
                s_out_ref[:, h] = (stacked(h) * dec_h + upd).reshape(n_seq, N, N)

    (o_ref[...],) = _run_staged([_rwkv_chunk(zr_ref[...], zl_ref[...], shr_ref[...], shl_ref[...], prm, n_seq,
                                             read_state=read_state, update_state=update_state)], 0)


def _rwkv_param_specs():
    W = RWKV_W
    shapes = [(1, 3 * W), (1, LORA_PAD), (1, W), (LANES, W), (1, W), (LANES, W), (LORA_PAD - LANES, W),
              (1, W), (1, W), (1, W), (1, W), (1, W), (LANES, LANES)]
    return [pl.BlockSpec(s, lambda *_: (0, 0)) for s in shapes]


def _rwkv_prompt(z, zl, params, batch, seq):
    C = RWKV_CHUNK
    W = RWKV_W
    nc = seq // C
    zr_spec = lambda i: pl.BlockSpec((C, 3 * W), lambda c: (i * nc + c, ZS // (3 * W)))
    zl_spec = lambda i: pl.BlockSpec((C, LORA_PAD), lambda c: (i * nc + c, 0))
    return pl.pallas_call(
        functools.partial(_rwkv_prompt_kernel, batch=batch),
        grid=(nc,),
        in_specs=[zr_spec(i) for i in range(batch)] + [zl_spec(i) for i in range(batch)]
        + _rwkv_param_specs(),
        out_specs=[
            pl.BlockSpec((batch, C, W), lambda c: (0, c, 0)),
            pl.BlockSpec((batch, RWKV_PAIRS, LANES, LANES), lambda c: (0, 0, 0, 0)),
        ],
        out_shape=[
            jax.ShapeDtypeStruct((batch, seq, W), BF16),
            jax.ShapeDtypeStruct((batch, RWKV_PAIRS, LANES, LANES), F32),
        ],
        scratch_shapes=[pltpu.VMEM((batch, 3 * W), F32), pltpu.VMEM((batch, LORA_PAD), F32),
                        pltpu.VMEM((batch, RWKV_PAIRS, LANES, LANES), F32)],
        compiler_params=_cparams(("arbitrary",)),
        name="rwkv_prompt",
    )(*([z] * batch), *([zl] * batch), *params)


def _rwkv_sample(z, zl, sh_r, sh_l, s0, params, row0, batch, seq):
    C = RWKV_CHUNK
    W = RWKV_W
    n_seq = C // seq
    rb0 = row0 // C
    return pl.pallas_call(
        functools.partial(_rwkv_sample_kernel, n_seq=n_seq),
        grid=(batch // n_seq,),
        in_specs=[
            pl.BlockSpec((C, 3 * W), lambda i: (rb0 + i, ZS // (3 * W))),
            pl.BlockSpec((C, LORA_PAD), lambda i: (i, 0)),
            pl.BlockSpec((C, 3 * W), lambda i: (i, 0)),
            pl.BlockSpec((C, LORA_PAD), lambda i: (i, 0)),
            pl.BlockSpec((n_seq, RWKV_HEADS, RWKV_N, RWKV_N), lambda i: (i, 0, 0, 0)),
        ] + _rwkv_param_specs(),
        out_specs=[
            pl.BlockSpec((C, W), lambda i: (i, 0)),
            pl.BlockSpec((n_seq, RWKV_HEADS, RWKV_N, RWKV_N), lambda i: (i, 0, 0, 0)),
        ],
        out_shape=[
            jax.ShapeDtypeStruct((batch * seq, W), BF16),
            jax.ShapeDtypeStruct((batch, RWKV_HEADS, RWKV_N, RWKV_N), F32),
        ],
        compiler_params=_cparams(("parallel",)),
        name="rwkv_sample",
    )(z, zl, sh_r, sh_l, s0, *params)


def _pair_unblock(s):
    b = s.shape[0]
    lo = s[:, :, :RWKV_N, :RWKV_N]
    hi = s[:, :, RWKV_N:, RWKV_N:]
    return jnp.stack([lo, hi], axis=2).reshape(b, RWKV_HEADS, RWKV_N, RWKV_N)


def _mix_out_kernel(oret_a_ref, oret_b_ref, orwkv_a_ref, orwkv_b_ref, xa_ref, xb_ref, gr_ref, gw_ref,
                    wr_ref, ww_ref, wo_ref, gn_ref, o_ref, hn_ref, *, tiles_a):
    first = pl.program_id(0) < tiles_a
    o_ret = jnp.where(first, oret_a_ref[...], oret_b_ref[...])
    o_rwkv = jnp.where(first, orwkv_a_ref[...], orwkv_b_ref[...])
    a = jnp.dot(o_ret, wr_ref[...], preferred_element_type=F32)
    b = jnp.dot(o_rwkv, ww_ref[...], preferred_element_type=F32)
    merged = jax.nn.sigmoid(gr_ref[...]) * a + jax.nn.sigmoid(gw_ref[...]) * b
    x2 = (jnp.where(first, xa_ref[...], xb_ref[...])
          + jnp.dot(merged.astype(BF16), wo_ref[...], preferred_element_type=F32))
    o_ref[...] = x2
    hn_ref[...] = _rms(x2, gn_ref[...], NORM_EPS).astype(BF16)


def _mix_out(o_rets, o_rwkvs, xs, zg, w_ret_o, w_rwkv_o, w_out, g_next):
    n = xs[0].shape[0] + xs[1].shape[0]
    tm = TM_MIX
    tiles_a = o_rets[0].shape[0] // tm
    resident = lambda shape: pl.BlockSpec(shape, lambda i: (0, 0), pipeline_mode=pl.Buffered(1))
    first = lambda i: (jnp.minimum(i, tiles_a - 1), 0)
    second = lambda i: (jnp.maximum(i - tiles_a, 0), 0)
    return pl.pallas_call(
        functools.partial(_mix_out_kernel, tiles_a=tiles_a),
        grid=(n // tm,),
        in_specs=[
            pl.BlockSpec((tm, RET_V), first),
            pl.BlockSpec((tm, RET_V), second),
            pl.BlockSpec((tm, RWKV_W), first),
            pl.BlockSpec((tm, RWKV_W), second),
            pl.BlockSpec((tm, D_MODEL), first),
            pl.BlockSpec((tm, D_MODEL), second),
            pl.BlockSpec((tm, D_MODEL), lambda i: (i, ZGR // D_MODEL)),
            pl.BlockSpec((tm, D_MODEL), lambda i: (i, ZGW // D_MODEL)),
            resident((RET_V, D_MODEL)),
            resident((RWKV_W, D_MODEL)),
            resident((D_MODEL, D_MODEL)),
            pl.BlockSpec((1, D_MODEL), lambda i: (0, 0)),
        ],
        out_specs=[pl.BlockSpec((tm, D_MODEL), lambda i: (i, 0)), pl.BlockSpec((tm, D_MODEL), lambda i: (i, 0))],
        out_shape=[jax.ShapeDtypeStruct((n, D_MODEL), F32), jax.ShapeDtypeStruct((n, D_MODEL), BF16)],
        compiler_params=_cparams(("arbitrary",)),
        name="mix_out",
    )(*o_rets, *o_rwkvs, *xs, zg, zg, w_ret_o, w_rwkv_o, w_out, g_next)


def kernel(x_prompt, x_sample, state_ret, state_rwkv, state_shift, norm_ffn1, ffn1_w_gu, ffn1_w_down, norm_mix, w_in, w_ret_o, rwkv_mu, rwkv_w0, rwkv_w_up, rwkv_a0, rwkv_a_up, rwkv_g_up, rwkv_k_k, rwkv_k_a, rwkv_r_k, rwkv_ln_w, rwkv_ln_b, w_rwkv_o, w_out, norm_ffn2, ffn2_w_gu, ffn2_w_down, norm_final):
    bp, tp, d = x_prompt.shape
    bs, ts, _ = x_sample.shape
    n_p = bp * tp
    n_s = bs * ts
    W = RWKV_W
    row = lambda a: a.reshape(1, -1)
    assert d == D_MODEL and n_p % TM == 0 and n_s == TM and n_p % TM_MIX == 0
    assert tp % RET_CHUNK == 0 and tp % RWKV_CHUNK == 0 and bp % RWKV_PROMPT_GROUP == 0 and bp % RET_PROMPT_SEQS == 0
    assert RWKV_CHUNK % ts == 0 and n_s % RWKV_CHUNK == 0 and RET_SAMPLE_ROWS % ts == 0 and n_s % RET_SAMPLE_ROWS == 0

    w_in_t = jnp.swapaxes(w_in[0], 0, 1)
    w_lora_t = w_in_t[Z_MAIN_W:Z_MAIN_W + LORA_PAD].astype(BF16)
    x1_s, h_s, zl_s, *w_ffn1 = _ffn(x_sample.reshape(n_s, d), 0, n_s, row(norm_ffn1[0]),
                                    (ffn1_w_gu[0], ffn1_w_down[0]), row(norm_mix[0]), final_norm=False,
                                    w_next_t=w_lora_t)
    x1_p, h_p, zl_p = _ffn(x_prompt.reshape(n_p, d), 0, n_p, row(norm_ffn1[0]), w_ffn1, row(norm_mix[0]),
                           final_norm=False, w_next_t=w_lora_t)

    z = _proj((h_p, h_s), w_in_t, 0, TN_PROJ_MAIN, Z_MAIN_W)
    zg = _proj((h_p, h_s), w_in_t, Z_MAIN_W + LORA_W, TN_PROJ, 2 * D_MODEL)

    lg = jnp.log(1.0 - 2.0 ** (-5.0 - jnp.arange(RET_HEADS, dtype=F32)))
    cos_p, sin_p = _rope_tables(jnp.arange(tp, dtype=F32))
    cos_s, sin_s = _rope_tables(PAST_LEN + jnp.arange(ts, dtype=F32))
    rep = RET_SAMPLE_ROWS // ts
    oret_p, ret_p = _ret_prompt(z, lg, cos_p, sin_p, bp, tp)
    oret_s, ret_s = _ret_sample(z, lg, jnp.tile(cos_s, (rep, 1)), jnp.tile(sin_s, (rep, 1)),
                                state_ret[0], n_p, bs, ts)

    mu = rwkv_mu[0]
    pad_l = lambda a: jnp.pad(a, [(0, 0)] * (a.ndim - 1) + [(0, LORA_PAD - LORA_W)])
    seg = (jnp.arange(LANES)[:, None] // RWKV_N == jnp.arange(LANES)[None, :] // RWKV_N).astype(BF16)
    zero64 = jnp.zeros((DECAY_LORA, W), F32)
    params = (
        row(mu[:3 * W]), pad_l(row(mu[3 * W:])),
        row(rwkv_w0[0]), jnp.concatenate([rwkv_w_up[0], zero64], axis=0).astype(BF16),
        row(rwkv_a0[0]), jnp.concatenate([zero64, rwkv_a_up[0]], axis=0).astype(BF16),
        jnp.pad(rwkv_g_up[0], ((0, LORA_PAD - LANES - GATE_LORA), (0, 0))).astype(BF16),
        row(rwkv_k_k[0]), row(rwkv_k_a[0]), row(rwkv_r_k[0]), row(rwkv_ln_w[0]), row(rwkv_ln_b[0]),
        seg,
    )
    orw_p, rwkv_p = _rwkv_prompt(z, zl_p, params, bp, tp)
    sh_s = state_shift[0]
    first_rows = lambda a: jnp.pad(a[:, None, :], ((0, 0), (0, ts - 1), (0, 0))).reshape(n_s, -1)
    orw_s, rwkv_s = _rwkv_sample(z, zl_s, first_rows(sh_s[:, :3 * W]), first_rows(pad_l(sh_s[:, 3 * W:])),
                                 state_rwkv[0], params, n_p, bs, ts)

    x2, h_ffn2 = _mix_out((oret_p.reshape(n_p, RET_V), oret_s), (orw_p.reshape(n_p, W), orw_s), (x1_p, x1_s), zg,
                          w_ret_o[0].astype(BF16), w_rwkv_o[0].astype(BF16), w_out[0].astype(BF16),
                          row(norm_ffn2[0]))
    y_s, *w_ffn2 = _ffn(x2, n_p, n_s, row(norm_ffn2[0]), (ffn2_w_gu[0], ffn2_w_down[0]), row(norm_final),
                        final_norm=True, h=h_ffn2)
    (y_p,) = _ffn(x2, 0, n_p, row(norm_ffn2[0]), w_ffn2, row(norm_final), final_norm=True, h=h_ffn2)

    def last_rows(firsts, t):
        rows = [f + t - 1 for f in firsts]
        zr_last = jnp.concatenate([z[r:r + 1, ZS:ZS + 3 * W] for r in rows], axis=0)
        zl_last = jnp.concatenate([zl_p[r:r + 1, :LORA_W] for r in rows], axis=0)
        return jnp.concatenate([zr_last, zl_last], axis=-1)[None]

    shift_s = jnp.concatenate([lax.slice(z, (n_p + ts - 1, ZS), (n_p + n_s, ZS + 3 * W), (ts, 1)),
                               lax.slice(zl_s, (ts - 1, 0), (n_s, LORA_W), (ts, 1))], axis=-1)[None]
    return (y_p.reshape(bp, tp, d), y_s.reshape(bs, ts, d),
            ret_p[None], _pair_unblock(rwkv_p)[None], last_rows(range(0, n_p, tp), tp),
            ret_s[None], rwkv_s[None], shift_s)
```

```python
import functools

import jax
import jax.numpy as jnp
from jax import lax
from jax.experimental import pallas as pl
from jax.experimental.pallas import tpu as pltpu

F32 = jnp.float32
BF16 = jnp.bfloat16

D_MODEL = 2048
PAST_LEN = 16384
RET_HEADS = 4
RET_DK = 256
RET_DV = 512
RET_CHUNK = 128
ROPE_BASE = 10000.0
RET_QK = RET_HEADS * RET_DK
RET_V = RET_HEADS * RET_DV
RWKV_HEADS = 16
RWKV_N = 64
RWKV_W = RWKV_HEADS * RWKV_N
DECAY_LORA = 64
AAA_LORA = 64
GATE_LORA = 160
LORA_W = DECAY_LORA + AAA_LORA + GATE_LORA
LORA_PAD = 384
D_FF = 5632
NORM_EPS = 1e-6
GN_EPS_RET = 1e-6
GN_EPS_RWKV = 64e-5

ZQ, ZK, ZV, ZG, ZS = 0, 1024, 2048, 4096, 6144
Z_MAIN_W = 9216
ZGR, ZGW = 0, 2048

LANES = 128
SUBLANES = 8
MXU_TILE = 256
RWKV_PAIRS = RWKV_W // LANES
RWKV_CHUNK = 64
RWKV_PROMPT_GROUP = 2
RWKV_PROMPT_LEAD = 21

V7X_VMEM_BYTES = 64 * 1024 * 1024
VMEM_LIMIT = V7X_VMEM_BYTES - 8 * 1024 * 1024

TM = 512
TF = 512
TF_CAST = 256
TN_PROJ = 1024
TN_PROJ_MAIN = 1536
TM_MIX = 256


def _rms(x, g, eps):
    return x * lax.rsqrt(jnp.mean(x * x, axis=-1, keepdims=True) + eps) * g


def _dot(a, b):
    return jnp.dot(a.astype(BF16), b.astype(BF16), preferred_element_type=F32)


def _dot_nt(a, b):
    return lax.dot_general(a.astype(BF16), b.astype(BF16), (((1,), (1,)), ((), ())),
                           preferred_element_type=F32)


def _dot_tn(a, b):
    return lax.dot_general(a.astype(BF16), b.astype(BF16), (((0,), (0,)), ((), ())),
                           preferred_element_type=F32)


def _bf16_terms(x, n):
    terms = []
    for _ in range(n):
        t = x.astype(BF16)
        terms.append(t)
        x = x - t.astype(F32)
    return terms


def _cparams(sem):
    return pltpu.CompilerParams(dimension_semantics=sem, vmem_limit_bytes=VMEM_LIMIT)


def _ffn_kernel(*refs, final_norm, emit_h, h_in, cast_w):
    if h_in:
        hin_ref, refs = refs[0], refs[1:]
    x_ref, g_ref, wg_ref, wu_ref, wd_ref, gf_ref = refs[:6]
    refs = refs[6:]
    if emit_h:
        wl_ref, refs = refs[0], refs[1:]
    outs = list(refs[:-2])
    h_ref, acc_ref = refs[-2:]
    o_ref = outs.pop(0)
    hn_ref, zl_ref = (outs.pop(0), outs.pop(0)) if emit_h else (None, None)
    j = pl.program_id(1)

    @pl.when(j == 0)
    def _():
        if not h_in:
            h_ref[...] = _rms(x_ref[...], g_ref[...], NORM_EPS).astype(BF16)
        acc_ref[...] = jnp.zeros_like(acc_ref)

    if cast_w:
        wg, wu, wd = wg_ref[...].astype(BF16), wu_ref[...].astype(BF16), wd_ref[...].astype(BF16)
        outs[0][0], outs[1][0], outs[2][...] = wg, wu, wd
    else:
        wg, wu, wd = wg_ref[0], wu_ref[0], wd_ref[...]
    h = hin_ref[...] if h_in else h_ref[...]
    tf = wg.shape[1]
    half = tf // 2 if tf // 2 >= MXU_TILE else tf
    cols = tuple(slice(c, c + half) for c in range(0, tf, half))
    gus = [(jnp.dot(h, wg[:, c], preferred_element_type=F32), jnp.dot(h, wu[:, c], preferred_element_type=F32))
           for c in cols]
    downs = [jnp.dot(((g * jax.nn.sigmoid(g)) * u).astype(BF16), wd[c, :], preferred_element_type=F32)
             for (g, u), c in zip(gus, cols)]
    acc_ref[...] += sum(downs[1:], downs[0])

    @pl.when(j == pl.num_programs(1) - 1)
    def _():
        y = x_ref[...] + 0.5 * acc_ref[...]
        if final_norm:
            y = _rms(y, gf_ref[...], NORM_EPS)
        if emit_h:
            hn = _rms(y, gf_ref[...], NORM_EPS).astype(BF16)
            hn_ref[...] = hn
            zl_ref[...] = _dot_nt(hn, wl_ref[...])
        o_ref[...] = y


def _ffn(x, row0, rows, g, weights, g_after, final_norm, h=None, w_next_t=None):
    cast_w = len(weights) == 2
    emit_h = w_next_t is not None
    tf = TF_CAST if cast_w else TF
    nj = D_FF // tf
    t0 = row0 // TM
    tile = pl.BlockSpec((TM, D_MODEL), lambda i, j: (t0 + i, 0))
    out_tile = pl.BlockSpec((TM, D_MODEL), lambda i, j: (i, 0))
    h_in = h is not None
    if cast_w:
        w_specs = [pl.BlockSpec((D_MODEL, tf), lambda i, j: (0, j)),
                   pl.BlockSpec((D_MODEL, tf), lambda i, j: (0, j + nj)),
                   pl.BlockSpec((tf, D_MODEL), lambda i, j: (j, 0))]
        w_args = (weights[0], weights[0], weights[1])
    else:
        w_specs = [pl.BlockSpec((1, D_MODEL, tf), lambda i, j: (j, 0, 0)),
                   pl.BlockSpec((1, D_MODEL, tf), lambda i, j: (j, 0, 0)),
                   pl.BlockSpec((tf, D_MODEL), lambda i, j: (j, 0))]
        w_args = weights
    out_specs = [out_tile]
    out_shape = [jax.ShapeDtypeStruct((rows, D_MODEL), F32)]
    next_args, next_specs = (), []
    if emit_h:
        ncol = w_next_t.shape[0]
        out_specs += [out_tile, pl.BlockSpec((TM, ncol), lambda i, j: (i, 0))]
        out_shape += [jax.ShapeDtypeStruct((rows, D_MODEL), BF16), jax.ShapeDtypeStruct((rows, ncol), F32)]
        next_args, next_specs = (w_next_t,), [pl.BlockSpec((ncol, D_MODEL), lambda i, j: (0, 0))]
    if cast_w:
        per = TF // tf
        tile_major = pl.BlockSpec((1, D_MODEL, tf), lambda i, j: (j // per, 0, j % per))
        out_specs += [tile_major, tile_major, pl.BlockSpec((tf, D_MODEL), lambda i, j: (j, 0))]
        out_shape += [jax.ShapeDtypeStruct((D_FF // TF, D_MODEL, TF), BF16)] * 2 + [
            jax.ShapeDtypeStruct((D_FF, D_MODEL), BF16)]
        assert rows == TM, "each weight tile is written exactly once"
    return pl.pallas_call(
        functools.partial(_ffn_kernel, final_norm=final_norm, emit_h=emit_h, h_in=h_in, cast_w=cast_w),
        grid=(rows // TM, nj),
        in_specs=([tile] if h_in else []) + [tile, pl.BlockSpec((1, D_MODEL), lambda i, j: (0, 0))] + w_specs
        + [pl.BlockSpec((1, D_MODEL), lambda i, j: (0, 0))] + next_specs,
        out_specs=out_specs,
        out_shape=out_shape,
        scratch_shapes=[pltpu.VMEM((TM, D_MODEL), BF16), pltpu.VMEM((TM, D_MODEL), F32)],
        compiler_params=_cparams(("parallel", "arbitrary")),
        name="ffn_cast" if cast_w else "ffn",
    )(*((h,) if h_in else ()), x, g, *w_args, g_after, *next_args)


def _proj_kernel(ha_ref, hb_ref, wt_hbm, o_ref, w_ref, wbuf_ref, sem_ref, *, tiles_a, col0, tn):
    j = pl.program_id(0)
    i = pl.program_id(1)
    nj = pl.num_programs(0)

    def tile_copy(jj, slot):
        rows = pl.ds(pl.multiple_of(col0 + jj * tn, SUBLANES), tn)
        return pltpu.make_async_copy(wt_hbm.at[rows, :], wbuf_ref.at[slot], sem_ref.at[slot])

    @pl.when(i == 0)
    def _():
        slot = j % 2

        @pl.when(j == 0)
        def _():
            tile_copy(0, 0).start()

        tile_copy(j, slot).wait()

        @pl.when(j + 1 < nj)
        def _():
            tile_copy(j + 1, 1 - slot).start()

        w_ref[...] = wbuf_ref[slot].astype(BF16)

    o_ref[...] = _dot_nt(jnp.where(i < tiles_a, ha_ref[...], hb_ref[...]), w_ref[...])


def _proj(hs, w_t, col0, tn, nout):
    tiles_a = hs[0].shape[0] // TM
    n = hs[0].shape[0] + hs[1].shape[0]
    assert col0 % SUBLANES == 0 and tn % SUBLANES == 0 and nout % tn == 0
    return pl.pallas_call(
        functools.partial(_proj_kernel, tiles_a=tiles_a, col0=col0, tn=tn),
        grid=(nout // tn, n // TM),
        in_specs=[
            pl.BlockSpec((TM, D_MODEL), lambda j, i: (jnp.minimum(i, tiles_a - 1), 0)),
            pl.BlockSpec((TM, D_MODEL), lambda j, i: (jnp.maximum(i - tiles_a, 0), 0)),
            pl.BlockSpec(memory_space=pl.ANY),
        ],
        out_specs=pl.BlockSpec((TM, tn), lambda j, i: (i, j)),
        out_shape=jax.ShapeDtypeStruct((n, nout), F32),
        scratch_shapes=[pltpu.VMEM((tn, D_MODEL), BF16), pltpu.VMEM((2, tn, D_MODEL), F32),
                        pltpu.SemaphoreType.DMA((2,))],
        compiler_params=_cparams(("arbitrary", "arbitrary")),
        name="proj",
    )(*hs, w_t)


def _pair_swap(x):
    n = x.shape[-1]
    lane = lax.broadcasted_iota(jnp.int32, x.shape, x.ndim - 1)
    prev = pltpu.roll(x, 1, x.ndim - 1)
    nxt = pltpu.roll(x, n - 1, x.ndim - 1)
    return jnp.where((lane & 1) == 1, prev, nxt)


def _rotary(x, cos, sin_signed):
    return x * cos + _pair_swap(x) * sin_signed


def _ret_finish(o, zg):
    o = o * lax.rsqrt(jnp.mean(o * o, axis=-1, keepdims=True) + GN_EPS_RET)
    return o * (zg * jax.nn.sigmoid(zg))


RET_PROMPT_SEQS = 2


def _ret_prompt_kernel(lg_ref, *refs):
    nb = RET_PROMPT_SEQS
    q_refs, k_refs, v_refs, g_refs = (refs[i * nb:(i + 1) * nb] for i in range(4))
    cos_ref, sin_ref, o_ref, s_out_ref, s_ref = refs[4 * nb:]
    c = pl.program_id(1)
    C = RET_CHUNK

    @pl.when(c == 0)
    def _():
        s_ref[...] = jnp.zeros_like(s_ref)

    cos = cos_ref[...]
    sin = sin_ref[...]
    ti = lax.broadcasted_iota(jnp.int32, (C, C), 0)
    tj = lax.broadcasted_iota(jnp.int32, (C, C), 1)
    rel = (ti - tj).astype(F32)
    idx = lax.broadcasted_iota(jnp.int32, (C, 1), 0).astype(F32)
    H = range(RET_HEADS)
    lgs = [lg_ref[h] for h in H]
    finals = []

    def sequence(b):
        qs = [_rotary(q_refs[b][:, h * RET_DK:(h + 1) * RET_DK], cos, sin) for h in H]
        ks = [_rotary(k_refs[b][:, h * RET_DK:(h + 1) * RET_DK], cos, sin) * (RET_DK ** -0.5) for h in H]
        yield
        vs = [v_refs[b][:, h * RET_DV:(h + 1) * RET_DV] for h in H]
        ss = [s_ref[b, h] for h in H]
        scores = [_dot_nt(q, k) * jnp.where(rel >= 0, jnp.exp(lg * jnp.maximum(rel, 0.0)), 0.0)
                  for q, k, lg in zip(qs, ks, lgs)]
        crosses = [_dot(q, s) * jnp.exp(lg * (idx + 1.0)) for q, s, lg in zip(qs, ss, lgs)]
        inners = [_dot(sc, v) for sc, v in zip(scores, vs)]
        s_news = [jnp.exp(lg * C) * s + _dot_tn(k * jnp.exp(lg * (C - 1.0 - idx)), v)
                  for lg, s, k, v in zip(lgs, ss, ks, vs)]
        yield
        for h in H:
            s_ref[b, h] = s_news[h]
            o_ref[b, :, h * RET_DV:(h + 1) * RET_DV] = _ret_finish(
                inners[h] + crosses[h], g_refs[b][:, h * RET_DV:(h + 1) * RET_DV]).astype(BF16)
        finals.append((b, s_news))

    _run_staged([sequence(b) for b in range(nb)], 1)

    @pl.when(c == pl.num_programs(1) - 1)
    def _():
        for b, s_news in finals:
            for h in H:
                s_out_ref[b, h] = s_news[h]


def _ret_prompt(z, lg, cos, sin, batch, seq):
    nc = seq // RET_CHUNK
    C = RET_CHUNK
    nb = RET_PROMPT_SEQS
    cols = lambda width, col0: [pl.BlockSpec((C, width), lambda g, c, b=b: ((g * nb + b) * nc + c, col0 // width))
                                for b in range(nb)]
    return pl.pallas_call(
        _ret_prompt_kernel,
        grid=(batch // nb, nc),
        in_specs=[pl.BlockSpec(memory_space=pltpu.SMEM)]
        + cols(RET_QK, ZQ) + cols(RET_QK, ZK) + cols(RET_V, ZV) + cols(RET_V, ZG)
        + [pl.BlockSpec((C, RET_DK), lambda g, c: (c, 0)), pl.BlockSpec((C, RET_DK), lambda g, c: (c, 0))],
        out_specs=[
            pl.BlockSpec((nb, C, RET_V), lambda g, c: (g, c, 0)),
            pl.BlockSpec((nb, RET_HEADS, RET_DK, RET_DV), lambda g, c: (g, 0, 0, 0)),
        ],
        out_shape=[
            jax.ShapeDtypeStruct((batch, seq, RET_V), BF16),
            jax.ShapeDtypeStruct((batch, RET_HEADS, RET_DK, RET_DV), F32),
        ],
        scratch_shapes=[pltpu.VMEM((nb, RET_HEADS, RET_DK, RET_DV), F32)],
        compiler_params=_cparams(("parallel", "arbitrary")),
        name="ret_prompt",
    )(lg, *([z] * (4 * nb)), cos, sin)


RET_SAMPLE_ROWS = 16


def _ret_sample_kernel(lg_ref, q_ref, k_ref, v_ref, g_ref, cos_ref, sin_ref, s0_ref,
                       o_ref, s_out_ref, *, seq):
    R = RET_SAMPLE_ROWS
    nb = R // seq
    cos = cos_ref[...]
    sin = sin_ref[...]
    ri = lax.broadcasted_iota(jnp.int32, (R, R), 0)
    rj = lax.broadcasted_iota(jnp.int32, (R, R), 1)
    rel = ((ri % seq) - (rj % seq)).astype(F32)
    ok = ((ri // seq) == (rj // seq)) & (rel >= 0)
    row = lax.broadcasted_iota(jnp.int32, (R, 1), 0)
    t = (row % seq).astype(F32)

    for h in range(RET_HEADS):
        lg = lg_ref[h]
        q = _rotary(q_ref[:, h * RET_DK:(h + 1) * RET_DK], cos, sin)
        k = _rotary(k_ref[:, h * RET_DK:(h + 1) * RET_DK], cos, sin) * (RET_DK ** -0.5)
        v = v_ref[:, h * RET_DV:(h + 1) * RET_DV]
        mask = jnp.where(ok, jnp.exp(lg * jnp.maximum(rel, 0.0)), 0.0)
        q_dec = jnp.exp(lg * (t + 1.0))
        kd = k * jnp.exp(lg * (seq - 1.0 - t))
        inner = _dot(_dot_nt(q, k) * mask, v)
        cross = jnp.zeros_like(inner)
        for b in range(nb):
            s = s0_ref[b, h]
            mine = (row // seq) == b
            cross = jnp.where(mine, _dot(q, s), cross)
            s_out_ref[b, h] = jnp.exp(lg * seq) * s + _dot_tn(jnp.where(mine, kd, 0.0), v)
        o_ref[:, h * RET_DV:(h + 1) * RET_DV] = _ret_finish(
            inner + cross * q_dec, g_ref[:, h * RET_DV:(h + 1) * RET_DV]).astype(BF16)


def _ret_sample(z, lg, cos, sin, s0, row0, batch, seq):
    R = RET_SAMPLE_ROWS
    nb = R // seq
    rb0 = row0 // R
    return pl.pallas_call(
        functools.partial(_ret_sample_kernel, seq=seq),
        grid=(batch // nb,),
        in_specs=[
            pl.BlockSpec(memory_space=pltpu.SMEM),
            pl.BlockSpec((R, RET_QK), lambda i: (rb0 + i, ZQ // RET_QK)),
            pl.BlockSpec((R, RET_QK), lambda i: (rb0 + i, ZK // RET_QK)),
            pl.BlockSpec((R, RET_V), lambda i: (rb0 + i, ZV // RET_V)),
            pl.BlockSpec((R, RET_V), lambda i: (rb0 + i, ZG // RET_V)),
            pl.BlockSpec((R, RET_DK), lambda i: (0, 0)),
            pl.BlockSpec((R, RET_DK), lambda i: (0, 0)),
            pl.BlockSpec((nb, RET_HEADS, RET_DK, RET_DV), lambda i: (i, 0, 0, 0)),
        ],
        out_specs=[
            pl.BlockSpec((R, RET_V), lambda i: (i, 0)),
            pl.BlockSpec((nb, RET_HEADS, RET_DK, RET_DV), lambda i: (i, 0, 0, 0)),
        ],
        out_shape=[
            jax.ShapeDtypeStruct((batch * seq, RET_V), BF16),
            jax.ShapeDtypeStruct((batch, RET_HEADS, RET_DK, RET_DV), F32),
        ],
        compiler_params=_cparams(("parallel",)),
        name="ret_sample",
    )(lg, z, z, z, z, cos, sin, s0)


def _rope_tables(pos):
    half = RET_DK // 2
    inv = 1.0 / (ROPE_BASE ** jnp.linspace(0.0, 1.0, half, dtype=F32))
    ang = pos[:, None] * inv[None, :]
    cos = jnp.repeat(jnp.cos(ang), 2, axis=-1)
    sin = jnp.sin(ang)
    sin_signed = jnp.stack([-sin, sin], axis=-1).reshape(pos.shape[0], RET_DK)
    return cos, sin_signed


def _segsum(x, seg_ref):
    rows = x.shape[0]
    n = rows * RWKV_PAIRS
    xs = jnp.concatenate([x[:, p * LANES:(p + 1) * LANES] for p in range(RWKV_PAIRS)], axis=0)
    ss = jnp.dot(jnp.concatenate(_bf16_terms(xs, 2), axis=0), seg_ref[...], preferred_element_type=F32)
    ss = ss[0:n] + ss[n:2 * n]
    return jnp.concatenate([ss[p * rows:(p + 1) * rows] for p in range(RWKV_PAIRS)], axis=1)


def _run_staged(gens, lead):
    out = [None] * len(gens)
    live = [True] * len(gens)

    def step(i):
        try:
            next(gens[i])
        except StopIteration as stop:
            out[i] = stop.value
            live[i] = False

    for _ in range(lead):
        if live[0]:
            step(0)
    while any(live):
        for i in range(len(gens)):
            if live[i]:
                step(i)
    return out


def _each(fn, *lists):
    out = []
    for args in zip(*lists):
        out.append(fn(*args))
        yield
    return out


def _rwkv_chunk(zr, zl, prev_r, prev_l, prm, n_seq, **state_fns):
    front = yield from _rwkv_front(zr, zl, prev_r, prev_l, prm, n_seq)
    return (yield from _rwkv_back(*front, prm, n_seq, **state_fns))


def _rwkv_front(zr, zl, prev_r, prev_l, prm, n_seq):
    (mur_ref, mul_ref, w0_ref, wup_ref, a0_ref, aup_ref, gup_ref,
     kk_ref, ka_ref, rk_ref, lnw_ref, lnb_ref, seg_ref) = prm
    C = RWKV_CHUNK
    W = RWKV_W
    R = zr.shape[0]
    TS = C // n_seq
    row = lax.broadcasted_iota(jnp.int32, (R, 1), 0)
    first = (row % TS) == 0

    def mix(z, prev, mu_ref):
        z_prev = jnp.where(first, prev, pltpu.roll(z, 1, 0))
        return z + (z_prev - z) * mu_ref[...]

    xs_r = mix(zr[:, 0:W], prev_r[:, 0:W], mur_ref.at[:, 0:W])
    yield
    xs_k = mix(zr[:, W:2 * W], prev_r[:, W:2 * W], mur_ref.at[:, W:2 * W])
    yield
    xs_v = mix(zr[:, 2 * W:3 * W], prev_r[:, 2 * W:3 * W], mur_ref.at[:, 2 * W:3 * W])
    yield
    xl = mix(zl, prev_l, mul_ref)
    r, k, v = xs_r, xs_k, xs_v
    x_wa = xl[:, 0:LANES]
    x_g = xl[:, LANES:LORA_PAD]

    y_w = w0_ref[...] + _dot(jnp.tanh(x_wa), wup_ref[...])
    yield
    w_log = -(jnp.maximum(-y_w, 0.0) + jnp.log(1.0 + jnp.exp(-jnp.abs(y_w)))) - 0.5
    lw = -jnp.exp(w_log)
    yield
    a_rate = jax.nn.sigmoid(a0_ref[...] + _dot(x_wa, aup_ref[...]))
    yield
    g = _dot(jax.nn.sigmoid(x_g), gup_ref[...])
    yield

    kk = k * kk_ref[...]
    k = k * (1.0 + (a_rate - 1.0) * ka_ref[...])
    yield
    sums = _segsum(jnp.concatenate([kk * kk, r * k * rk_ref[...]], axis=0), seg_ref)
    yield
    kk = kk * lax.rsqrt(jnp.maximum(sums[0:R], 1e-24))
    bonus_rk = sums[R:2 * R]
    a = -kk
    b = kk * a_rate
    yield

    rows_sel = R if n_seq == 1 else 2 * R
    ti = lax.broadcasted_iota(jnp.int32, (rows_sel, R), 0)
    tj = lax.broadcasted_iota(jnp.int32, (rows_sel, R), 1)
    same = ((ti % R) // TS) == (tj // TS)
    sel = (same & ((ti >= R) | (tj <= ti))).astype(BF16)
    sums_lw = sum(jnp.dot(sel, t, preferred_element_type=F32) for t in _bf16_terms(lw, 3))
    yield
    cum = sums_lw[0:R]
    if n_seq == 1:
        tot = jnp.concatenate([jnp.broadcast_to(cum[i + C - 1:i + C], (C, W)) for i in range(0, R, C)], axis=0)
    else:
        tot = sums_lw[R:2 * R]
    e_neg = jnp.exp(-cum)
    yield
    a_t = (a * jnp.exp(cum - lw)).astype(BF16)
    yield
    b_t = (b * e_neg).astype(BF16)
    yield
    k_t = (k * e_neg).astype(BF16)
    yield
    r_t = r * jnp.exp(cum)
    yield
    e_tail = jnp.exp(tot - cum)
    yield
    b_g = (b * e_tail).astype(BF16)
    yield
    k_g = (k * e_tail).astype(BF16)
    dec = jnp.exp(tot)
    yield
    return (a_t, b_t, k_t, b_g, k_g), (r_t, v, dec, g, bonus_rk)


def _rwkv_back(mm_operands, f32_arrays, prm, n_seq, read_state=None, update_state=None, fold_state=None):
    a_t, b_t, k_t, b_g, k_g = mm_operands
    r_t, v, dec, g, bonus_rk = f32_arrays
    lnw_ref, lnb_ref, seg_ref = prm[10:13]
    C = RWKV_CHUNK
    R = r_t.shape[0]
    TS = C // n_seq

    lane = lax.broadcasted_iota(jnp.int32, (1, LANES), 1)
    lo = lane < RWKV_N

    def split(x):
        return jnp.concatenate([jnp.where(lo, x, 0.0), jnp.where(lo, 0.0, x)], axis=0)

    row4 = lax.broadcasted_iota(jnp.int32, (C, 4 * C), 0)
    col4 = lax.broadcasted_iota(jnp.int32, (C, 4 * C), 1) % C
    same4 = (row4 // TS) == (col4 // TS)
    strict = same4 & (col4 < row4)
    incl = same4 & (col4 <= row4)
    row2 = lax.broadcasted_iota(jnp.int32, (C, 2 * C), 0)
    col2 = lax.broadcasted_iota(jnp.int32, (C, 2 * C), 1)
    eye2 = ((col2 % C) == row2).astype(F32)
    left = col2 < C

    tiles = [(slice(i * C, (i + 1) * C), slice(p * LANES, (p + 1) * LANES))
             for i in range(R // C) for p in range(RWKV_PAIRS)]
    v2s = [split(v[rs, sl]) for rs, sl in tiles]
    ars = [jnp.concatenate([a_t[rs, sl], r_t[rs, sl]], axis=0) for rs, sl in tiles]
    if fold_state is None:
        wys = read_state(ars)
    gms = yield from _each(
        lambda ar, t: _dot_nt(ar, jnp.concatenate([split(b_t[t[0], t[1]]), split(k_t[t[0], t[1]])], axis=0)),
        ars, tiles)
    gas = [jnp.where(strict, gm[0:C], 0.0) for gm in gms]
    grs = [jnp.where(incl, gm[C:2 * C], 0.0) for gm in gms]

    def blockdiag(x):
        return jnp.concatenate([jnp.where(left, x, 0.0), jnp.where(left, 0.0, x)], axis=0)

    ts = [eye2 + ga[:, 0:2 * C] for ga in gas]
    n_sq = TS.bit_length() - 2
    if n_sq >= 1:
        pws = yield from _each(lambda ga: _dot(ga[:, 0:2 * C], blockdiag(ga[:, 0:2 * C])), gas)
        for _ in range(n_sq - 1):
            tps = yield from _each(lambda t, pw: _dot(jnp.concatenate([t, pw], axis=0), blockdiag(pw)), ts, pws)
            ts = [t + tp[0:C] for t, tp in zip(ts, tps)]
            pws = [tp[C:2 * C] for tp in tps]
        ts = yield from _each(lambda t, pw: t + _dot(t, blockdiag(pw)), ts, pws)
    wis = yield from _each(lambda ga, v2: _dot(ga[:, 2 * C:4 * C], v2), gas, v2s)
    bks = [jnp.concatenate([b_g[rs, sl], k_g[rs, sl]], axis=0) for rs, sl in tiles]
    if fold_state is None:
        us = [_dot(t, split(wi + wy[0:C])) for t, wi, wy in zip(ts, wis, wys)]
        ys = [wy[C:2 * C] + _dot(gr, jnp.concatenate([split(u), v2], axis=0))
              for wy, gr, u, v2 in zip(wys, grs, us, v2s)]
        update_state([jnp.concatenate([u, v[rs, sl]], axis=0) for u, (rs, sl) in zip(us, tiles)], bks, dec)
    else:
        zc = jnp.zeros((C, LANES), F32)
        aus = yield from _each(
            lambda t, wi, tl: _dot(t, jnp.concatenate([split(a_t[tl[0], tl[1]]), split(wi)], axis=1)),
            ts, wis, tiles)
        rys = yield from _each(
            lambda gr, au, v2: _dot(gr, jnp.concatenate(
                [jnp.concatenate([split(au[:, 0:LANES]), split(au[:, LANES:2 * LANES])], axis=1),
                 jnp.concatenate([jnp.concatenate([zc, zc], axis=0), v2], axis=1)], axis=0)),
            grs, aus, v2s)
        mns = yield from _each(
            lambda bk, au, tl: _dot_tn(bk, jnp.concatenate(
                [au, jnp.concatenate([zc, v[tl[0], tl[1]]], axis=1)], axis=0)),
            bks, aus, tiles)
        ys = fold_state([r_t[rs, sl] + ry[:, 0:LANES] for ry, (rs, sl) in zip(rys, tiles)],
                        [ry[:, LANES:2 * LANES] for ry in rys], mns, dec)

    yield
    y = jnp.concatenate([jnp.concatenate(ys[i * RWKV_PAIRS:(i + 1) * RWKV_PAIRS], axis=1)
                         for i in range(R // C)], axis=0)
    mean = _segsum(y, seg_ref) * (1.0 / RWKV_N)
    d = y - mean
    var = _segsum(d * d, seg_ref) * (1.0 / RWKV_N)
    yn = d * lax.rsqrt(var + GN_EPS_RWKV) * lnw_ref[...] + lnb_ref[...]
    return ((yn + bonus_rk * v) * g).astype(BF16)


def _rwkv_prompt_kernel(*refs, batch):
    zr_refs = refs[:batch]
    zl_refs = refs[batch:2 * batch]
    prm = refs[2 * batch:2 * batch + 13]
    o_ref, s_out_ref, prevr_ref, prevl_ref, h_ref = refs[2 * batch + 13:]
    C = RWKV_CHUNK
    c = pl.program_id(0)

    @pl.when(c == 0)
    def _():
        prevr_ref[...] = jnp.zeros_like(prevr_ref)
        prevl_ref[...] = jnp.zeros_like(prevl_ref)
        h_ref[...] = jnp.zeros_like(h_ref)

    ki = lax.broadcasted_iota(jnp.int32, (LANES, LANES), 0)
    vi = lax.broadcasted_iota(jnp.int32, (LANES, LANES), 1)
    diag_blocks = (ki < RWKV_N) == (vi < RWKV_N)

    def group(i0):
        seqs = range(i0, i0 + RWKV_PROMPT_GROUP)
        tiles = [(i, p) for i in seqs for p in range(RWKV_PAIRS)]
        zr = jnp.concatenate([zr_refs[i][...] for i in seqs], axis=0)
        zl = jnp.concatenate([zl_refs[i][...] for i in seqs], axis=0)
        rows = lambda ref: jnp.concatenate(
            [jnp.broadcast_to(ref[i:i + 1, :], (C, ref.shape[1])) for i in seqs], axis=0)

        def fold_state(r_hats, y_inds, mns, dec):
            dec_cols = [jnp.broadcast_to(dec[(i - i0) * C:(i - i0) * C + 1, p * LANES:(p + 1) * LANES],
                                         (LANES, LANES)).T for i, p in tiles]
            hs = [h_ref[i, p] for i, p in tiles]
            res = [_dot(jnp.concatenate([rh, jnp.where(diag_blocks, mn[:, 0:LANES], 0.0)], axis=0), h)
                   for rh, mn, h in zip(r_hats, mns, hs)]
            for (i, p), h, dc, rs, mn in zip(tiles, hs, dec_cols, res, mns):
                h_ref[i, p] = h * dc + rs[C:C + LANES] + jnp.where(diag_blocks, mn[:, LANES:2 * LANES], 0.0)
            return [rs[0:C] + yi for rs, yi in zip(res, y_inds)]

        y = yield from _rwkv_chunk(zr, zl, rows(prevr_ref), rows(prevl_ref), prm, 1, fold_state=fold_state)
        for n, i in enumerate(seqs):
            o_ref[i] = y[n * C:(n + 1) * C]
            prevr_ref[i:i + 1, :] = zr[(n + 1) * C - 1:(n + 1) * C, :]
            prevl_ref[i:i + 1, :] = zl[(n + 1) * C - 1:(n + 1) * C, :]

    _run_staged([group(i0) for i0 in range(0, batch, RWKV_PROMPT_GROUP)], RWKV_PROMPT_LEAD)

    @pl.when(c == pl.num_programs(0) - 1)
    def _():
        for i in range(batch):
            for p in range(RWKV_PAIRS):
                s_out_ref[i, p] = h_ref[i, p].T


def _rwkv_sample_kernel(zr_ref, zl_ref, shr_ref, shl_ref, s0_ref, *rest, n_seq):
    prm = rest[:13]
    o_ref, s_out_ref = rest[13:]
    C = RWKV_CHUNK
    N = RWKV_N
    TS = C // n_seq
    tokseq = (lax.broadcasted_iota(jnp.int32, (1, LANES), 1) % C) // TS

    def pick(x, j):
        return jnp.where(tokseq == j, x, 0.0)

    def stacked(h):
        return s0_ref[:, h].reshape(n_seq * N, N)

    def read_state(ars):
        res = [[_dot_nt(stacked(2 * p + e), ar[:, e * N:(e + 1) * N]) for e in range(2)]
               for p, ar in enumerate(ars)]
        out = []
        for rp in res:
            halves = []
            for re in rp:
                acc = pick(re[0:N], 0)
                for j in range(1, n_seq):
                    acc = acc + pick(re[j * N:(j + 1) * N], j)
                halves.append(acc)
            out.append(jnp.concatenate(halves, axis=0).T)
        return out

    def update_state(uvs, bks, dec):
        uvts = [uv.T for uv in uvs]
        upds = [[_dot(jnp.concatenate([pick(uvt[e * N:(e + 1) * N], j) for j in range(n_seq)], axis=0),
                      bk[:, e * N:(e + 1) * N]) for e in range(2)]
                for uvt, bk in zip(uvts, bks)]
        for p, up in enumerate(upds):
            for e, upd in enumerate(up):
                h = 2 * p + e
                dec_h = jnp.concatenate(
                    [jnp.broadcast_to(dec[j * TS:j * TS + 1, h * N:(h + 1) * N], (N, N)) for j in range(n_seq)],
                    axis=0)
                s_out_ref[:, h] = (stacked(h) * dec_h + upd).reshape(n_seq, N, N)

    (o_ref[...],) = _run_staged([_rwkv_chunk(zr_ref[...], zl_ref[...], shr_ref[...], shl_ref[...], prm, n_seq,
                                             read_state=read_state, update_state=update_state)], 0)


def _rwkv_param_specs():
    W = RWKV_W
    shapes = [(1, 3 * W), (1, LORA_PAD), (1, W), (LANES, W), (1, W), (LANES, W), (LORA_PAD - LANES, W),
              (1, W), (1, W), (1, W), (1, W), (1, W), (LANES, LANES)]
    return [pl.BlockSpec(s, lambda *_: (0, 0)) for s in shapes]


def _rwkv_prompt(z, zl, params, batch, seq):
    C = RWKV_CHUNK
    W = RWKV_W
    nc = seq // C
    zr_spec = lambda i: pl.BlockSpec((C, 3 * W), lambda c: (i * nc + c, ZS // (3 * W)))
    zl_spec = lambda i: pl.BlockSpec((C, LORA_PAD), lambda c: (i * nc + c, 0))
    return pl.pallas_call(
        functools.partial(_rwkv_prompt_kernel, batch=batch),
        grid=(nc,),
        in_specs=[zr_spec(i) for i in range(batch)] + [zl_spec(i) for i in range(batch)]
        + _rwkv_param_specs(),
        out_specs=[
            pl.BlockSpec((batch, C, W), lambda c: (0, c, 0)),
            pl.BlockSpec((batch, RWKV_PAIRS, LANES, LANES), lambda c: (0, 0, 0, 0)),
        ],
        out_shape=[
            jax.ShapeDtypeStruct((batch, seq, W), BF16),
            jax.ShapeDtypeStruct((batch, RWKV_PAIRS, LANES, LANES), F32),
        ],
        scratch_shapes=[pltpu.VMEM((batch, 3 * W), F32), pltpu.VMEM((batch, LORA_PAD), F32),
                        pltpu.VMEM((batch, RWKV_PAIRS, LANES, LANES), F32)],
        compiler_params=_cparams(("arbitrary",)),
        name="rwkv_prompt",
    )(*([z] * batch), *([zl] * batch), *params)


def _rwkv_sample(z, zl, sh_r, sh_l, s0, params, row0, batch, seq):
    C = RWKV_CHUNK
    W = RWKV_W
    n_seq = C // seq
    rb0 = row0 // C
    return pl.pallas_call(
        functools.partial(_rwkv_sample_kernel, n_seq=n_seq),
        grid=(batch // n_seq,),
        in_specs=[
            pl.BlockSpec((C, 3 * W), lambda i: (rb0 + i, ZS // (3 * W))),
            pl.BlockSpec((C, LORA_PAD), lambda i: (i, 0)),
            pl.BlockSpec((C, 3 * W), lambda i: (i, 0)),
            pl.BlockSpec((C, LORA_PAD), lambda i: (i, 0)),
            pl.BlockSpec((n_seq, RWKV_HEADS, RWKV_N, RWKV_N), lambda i: (i, 0, 0, 0)),
        ] + _rwkv_param_specs(),
        out_specs=[
            pl.BlockSpec((C, W), lambda i: (i, 0)),
            pl.BlockSpec((n_seq, RWKV_HEADS, RWKV_N, RWKV_N), lambda i: (i, 0, 0, 0)),
        ],
        out_shape=[
            jax.ShapeDtypeStruct((batch * seq, W), BF16),
            jax.ShapeDtypeStruct((batch, RWKV_HEADS, RWKV_N, RWKV_N), F32),
        ],
        compiler_params=_cparams(("parallel",)),
        name="rwkv_sample",
    )(z, zl, sh_r, sh_l, s0, *params)


def _pair_unblock(s):
    b = s.shape[0]
    lo = s[:, :, :RWKV_N, :RWKV_N]
    hi = s[:, :, RWKV_N:, RWKV_N:]
    return jnp.stack([lo, hi], axis=2).reshape(b, RWKV_HEADS, RWKV_N, RWKV_N)


def _mix_out_kernel(oret_a_ref, oret_b_ref, orwkv_a_ref, orwkv_b_ref, xa_ref, xb_ref, gr_ref, gw_ref,
                    wr_ref, ww_ref, wo_ref, gn_ref, o_ref, hn_ref, *, tiles_a):
    first = pl.program_id(0) < tiles_a
    o_ret = jnp.where(first, oret_a_ref[...], oret_b_ref[...])
    o_rwkv = jnp.where(first, orwkv_a_ref[...], orwkv_b_ref[...])
    a = jnp.dot(o_ret, wr_ref[...], preferred_element_type=F32)
    b = jnp.dot(o_rwkv, ww_ref[...], preferred_element_type=F32)
    merged = jax.nn.sigmoid(gr_ref[...]) * a + jax.nn.sigmoid(gw_ref[...]) * b
    x2 = (jnp.where(first, xa_ref[...], xb_ref[...])
          + jnp.dot(merged.astype(BF16), wo_ref[...], preferred_element_type=F32))
    o_ref[...] = x2
    hn_ref[...] = _rms(x2, gn_ref[...], NORM_EPS).astype(BF16)


def _mix_out(o_rets, o_rwkvs, xs, zg, w_ret_o, w_rwkv_o, w_out, g_next):
    n = xs[0].shape[0] + xs[1].shape[0]
    tm = TM_MIX
    tiles_a = o_rets[0].shape[0] // tm
    resident = lambda shape: pl.BlockSpec(shape, lambda i: (0, 0), pipeline_mode=pl.Buffered(1))
    first = lambda i: (jnp.minimum(i, tiles_a - 1), 0)
    second = lambda i: (jnp.maximum(i - tiles_a, 0), 0)
    return pl.pallas_call(
        functools.partial(_mix_out_kernel, tiles_a=tiles_a),
        grid=(n // tm,),
        in_specs=[
            pl.BlockSpec((tm, RET_V), first),
            pl.BlockSpec((tm, RET_V), second),
            pl.BlockSpec((tm, RWKV_W), first),
            pl.BlockSpec((tm, RWKV_W), second),
            pl.BlockSpec((tm, D_MODEL), first),
            pl.BlockSpec((tm, D_MODEL), second),
            pl.BlockSpec((tm, D_MODEL), lambda i: (i, ZGR // D_MODEL)),
            pl.BlockSpec((tm, D_MODEL), lambda i: (i, ZGW // D_MODEL)),
            resident((RET_V, D_MODEL)),
            resident((RWKV_W, D_MODEL)),
            resident((D_MODEL, D_MODEL)),
            pl.BlockSpec((1, D_MODEL), lambda i: (0, 0)),
        ],
        out_specs=[pl.BlockSpec((tm, D_MODEL), lambda i: (i, 0)), pl.BlockSpec((tm, D_MODEL), lambda i: (i, 0))],
        out_shape=[jax.ShapeDtypeStruct((n, D_MODEL), F32), jax.ShapeDtypeStruct((n, D_MODEL), BF16)],
        compiler_params=_cparams(("arbitrary",)),
        name="mix_out",
    )(*o_rets, *o_rwkvs, *xs, zg, zg, w_ret_o, w_rwkv_o, w_out, g_next)


def kernel(x_prompt, x_sample, state_ret, state_rwkv, state_shift, norm_ffn1, ffn1_w_gu, ffn1_w_down, norm_mix, w_in, w_ret_o, rwkv_mu, rwkv_w0, rwkv_w_up, rwkv_a0, rwkv_a_up, rwkv_g_up, rwkv_k_k, rwkv_k_a, rwkv_r_k, rwkv_ln_w, rwkv_ln_b, w_rwkv_o, w_out, norm_ffn2, ffn2_w_gu, ffn2_w_down, norm_final):
    bp, tp, d = x_prompt.shape
    bs, ts, _ = x_sample.shape
    n_p = bp * tp
    n_s = bs * ts
    W = RWKV_W
    row = lambda a: a.reshape(1, -1)
    assert d == D_MODEL and n_p % TM == 0 and n_s == TM and n_p % TM_MIX == 0
    assert tp % RET_CHUNK == 0 and tp % RWKV_CHUNK == 0 and bp % RWKV_PROMPT_GROUP == 0 and bp % RET_PROMPT_SEQS == 0
    assert RWKV_CHUNK % ts == 0 and n_s % RWKV_CHUNK == 0 and RET_SAMPLE_ROWS % ts == 0 and n_s % RET_SAMPLE_ROWS == 0

    w_in_t = jnp.swapaxes(w_in[0], 0, 1)
    w_lora_t = w_in_t[Z_MAIN_W:Z_MAIN_W + LORA_PAD].astype(BF16)
    x1_s, h_s, zl_s, *w_ffn1 = _ffn(x_sample.reshape(n_s, d), 0, n_s, row(norm_ffn1[0]),
                                    (ffn1_w_gu[0], ffn1_w_down[0]), row(norm_mix[0]), final_norm=False,
                                    w_next_t=w_lora_t)
    x1_p, h_p, zl_p = _ffn(x_prompt.reshape(n_p, d), 0, n_p, row(norm_ffn1[0]), w_ffn1, row(norm_mix[0]),
                           final_norm=False, w_next_t=w_lora_t)

    z = _proj((h_p, h_s), w_in_t, 0, TN_PROJ_MAIN, Z_MAIN_W)
    zg = _proj((h_p, h_s), w_in_t, Z_MAIN_W + LORA_W, TN_PROJ, 2 * D_MODEL)

    lg = jnp.log(1.0 - 2.0 ** (-5.0 - jnp.arange(RET_HEADS, dtype=F32)))
    cos_p, sin_p = _rope_tables(jnp.arange(tp, dtype=F32))
    cos_s, sin_s = _rope_tables(PAST_LEN + jnp.arange(ts, dtype=F32))
    rep = RET_SAMPLE_ROWS // ts
    oret_p, ret_p = _ret_prompt(z, lg, cos_p, sin_p, bp, tp)
    oret_s, ret_s = _ret_sample(z, lg, jnp.tile(cos_s, (rep, 1)), jnp.tile(sin_s, (rep, 1)),
                                state_ret[0], n_p, bs, ts)

    mu = rwkv_mu[0]
    pad_l = lambda a: jnp.pad(a, [(0, 0)] * (a.ndim - 1) + [(0, LORA_PAD - LORA_W)])
    seg = (jnp.arange(LANES)[:, None] // RWKV_N == jnp.arange(LANES)[None, :] // RWKV_N).astype(BF16)
    zero64 = jnp.zeros((DECAY_LORA, W), F32)
    params = (
        row(mu[:3 * W]), pad_l(row(mu[3 * W:])),
        row(rwkv_w0[0]), jnp.concatenate([rwkv_w_up[0], zero64], axis=0).astype(BF16),
        row(rwkv_a0[0]), jnp.concatenate([zero64, rwkv_a_up[0]], axis=0).astype(BF16),
        jnp.pad(rwkv_g_up[0], ((0, LORA_PAD - LANES - GATE_LORA), (0, 0))).astype(BF16),
        row(rwkv_k_k[0]), row(rwkv_k_a[0]), row(rwkv_r_k[0]), row(rwkv_ln_w[0]), row(rwkv_ln_b[0]),
        seg,
    )
    orw_p, rwkv_p = _rwkv_prompt(z, zl_p, params, bp, tp)
    sh_s = state_shift[0]
    first_rows = lambda a: jnp.pad(a[:, None, :], ((0, 0), (0, ts - 1), (0, 0))).reshape(n_s, -1)
    orw_s, rwkv_s = _rwkv_sample(z, zl_s, first_rows(sh_s[:, :3 * W]), first_rows(pad_l(sh_s[:, 3 * W:])),
                                 state_rwkv[0], params, n_p, bs, ts)

    x2, h_ffn2 = _mix_out((oret_p.reshape(n_p, RET_V), oret_s), (orw_p.reshape(n_p, W), orw_s), (x1_p, x1_s), zg,
                          w_ret_o[0].astype(BF16), w_rwkv_o[0].astype(BF16), w_out[0].astype(BF16),
                          row(norm_ffn2[0]))
    y_s, *w_ffn2 = _ffn(x2, n_p, n_s, row(norm_ffn2[0]), (ffn2_w_gu[0], ffn2_w_down[0]), row(norm_final),
                        final_norm=True, h=h_ffn2)
    (y_p,) = _ffn(x2, 0, n_p, row(norm_ffn2[0]), w_ffn2, row(norm_final), final_norm=True, h=h_ffn2)

    def last_rows(firsts, t):
        rows = [f + t - 1 for f in firsts]
        zr_last = jnp.concatenate([z[r:r + 1, ZS:ZS + 3 * W] for r in rows], axis=0)
        zl_last = jnp.concatenate([zl_p[r:r + 1, :LORA_W] for r in rows], axis=0)
        return jnp.concatenate([zr_last, zl_last], axis=-1)[None]

    shift_s = jnp.concatenate([lax.slice(z, (n_p + ts - 1, ZS), (n_p + n_s, ZS + 3 * W), (ts, 1)),
                               lax.slice(zl_s, (ts - 1, 0), (n_s, LORA_W), (ts, 1))], axis=-1)[None]
    return (y_p.reshape(bp, tp, d), y_s.reshape(bs, ts, d),
            ret_p[None], _pair_unblock(rwkv_p)[None], last_rows(range(0, n_p, tp), tp),
            ret_s[None], rwkv_s[None], shift_s)
```

```python
import functools

import jax
import jax.numpy as jnp
from jax import lax
from jax.experimental import pallas as pl
from jax.experimental.pallas import tpu as pltpu

F32 = jnp.float32
BF16 = jnp.bfloat16

D_MODEL = 2048
PAST_LEN = 16384
RET_HEADS = 4
RET_DK = 256
RET_DV = 512
RET_CHUNK = 128
ROPE_BASE = 10000.0
RET_QK = RET_HEADS * RET_DK
RET_V = RET_HEADS * RET_DV
RWKV_HEADS = 16
RWKV_N = 64
RWKV_W = RWKV_HEADS * RWKV_N
DECAY_LORA = 64
AAA_LORA = 64
GATE_LORA = 160
LORA_W = DECAY_LORA + AAA_LORA + GATE_LORA
LORA_PAD = 384
D_FF = 5632
NORM_EPS = 1e-6
GN_EPS_RET = 1e-6
GN_EPS_RWKV = 64e-5

ZQ, ZK, ZV, ZG, ZS = 0, 1024, 2048, 4096, 6144
Z_MAIN_W = 9216
ZGR, ZGW = 0, 2048

LANES = 128
SUBLANES = 8
MXU_TILE = 256
RWKV_PAIRS = RWKV_W // LANES
RWKV_CHUNK = 64
RWKV_PROMPT_GROUP = 2
RWKV_PROMPT_LEAD = 21

V7X_VMEM_BYTES = 64 * 1024 * 1024
VMEM_LIMIT = V7X_VMEM_BYTES - 8 * 1024 * 1024

TM = 512
TF = 512
TF_CAST = 256
TN_PROJ = 2048
TN_PROJ_MAIN = 2304
TM_MIX = 256


def _rms(x, g, eps):
    return x * lax.rsqrt(jnp.mean(x * x, axis=-1, keepdims=True) + eps) * g


def _dot(a, b):
    return jnp.dot(a.astype(BF16), b.astype(BF16), preferred_element_type=F32)


def _dot_nt(a, b):
    return lax.dot_general(a.astype(BF16), b.astype(BF16), (((1,), (1,)), ((), ())),
                           preferred_element_type=F32)


def _dot_tn(a, b):
    return lax.dot_general(a.astype(BF16), b.astype(BF16), (((0,), (0,)), ((), ())),
                           preferred_element_type=F32)


def _bf16_terms(x, n):
    terms = []
    for _ in range(n):
        t = x.astype(BF16)
        terms.append(t)
        x = x - t.astype(F32)
    return terms


def _cparams(sem):
    return pltpu.CompilerParams(dimension_semantics=sem, vmem_limit_bytes=VMEM_LIMIT)


def _ffn_kernel(*refs, final_norm, emit_h, h_in, cast_w):
    if h_in:
        hin_ref, refs = refs[0], refs[1:]
    x_ref, g_ref, wg_ref, wu_ref, wd_ref, gf_ref = refs[:6]
    refs = refs[6:]
    if emit_h:
        wl_ref, refs = refs[0], refs[1:]
    outs = list(refs[:-2])
    h_ref, acc_ref = refs[-2:]
    o_ref = outs.pop(0)
    hn_ref, zl_ref = (outs.pop(0), outs.pop(0)) if emit_h else (None, None)
    j = pl.program_id(1)

    @pl.when(j == 0)
    def _():
        if not h_in:
            h_ref[...] = _rms(x_ref[...], g_ref[...], NORM_EPS).astype(BF16)
        acc_ref[...] = jnp.zeros_like(acc_ref)

    if cast_w:
        wg, wu, wd = wg_ref[...].astype(BF16), wu_ref[...].astype(BF16), wd_ref[...].astype(BF16)
        outs[0][0], outs[1][0], outs[2][...] = wg, wu, wd
    else:
        wg, wu, wd = wg_ref[0], wu_ref[0], wd_ref[...]
    h = hin_ref[...] if h_in else h_ref[...]
    tf = wg.shape[1]
    half = tf // 2 if tf // 2 >= MXU_TILE else tf
    cols = tuple(slice(c, c + half) for c in range(0, tf, half))
    gus = [(jnp.dot(h, wg[:, c], preferred_element_type=F32), jnp.dot(h, wu[:, c], preferred_element_type=F32))
           for c in cols]
    downs = [jnp.dot(((g * jax.nn.sigmoid(g)) * u).astype(BF16), wd[c, :], preferred_element_type=F32)
             for (g, u), c in zip(gus, cols)]
    acc_ref[...] += sum(downs[1:], downs[0])

    @pl.when(j == pl.num_programs(1) - 1)
    def _():
        y = x_ref[...] + 0.5 * acc_ref[...]
        if final_norm:
            y = _rms(y, gf_ref[...], NORM_EPS)
        if emit_h:
            hn = _rms(y, gf_ref[...], NORM_EPS).astype(BF16)
            hn_ref[...] = hn
            zl_ref[...] = _dot_nt(hn, wl_ref[...])
        o_ref[...] = y


def _ffn(x, row0, rows, g, weights, g_after, final_norm, h=None, w_next_t=None):
    cast_w = len(weights) == 2
    emit_h = w_next_t is not None
    tf = TF_CAST if cast_w else TF
    nj = D_FF // tf
    t0 = row0 // TM
    tile = pl.BlockSpec((TM, D_MODEL), lambda i, j: (t0 + i, 0))
    out_tile = pl.BlockSpec((TM, D_MODEL), lambda i, j: (i, 0))
    h_in = h is not None
    if cast_w:
        w_specs = [pl.BlockSpec((D_MODEL, tf), lambda i, j: (0, j)),
                   pl.BlockSpec((D_MODEL, tf), lambda i, j: (0, j + nj)),
                   pl.BlockSpec((tf, D_MODEL), lambda i, j: (j, 0))]
        w_args = (weights[0], weights[0], weights[1])
    else:
        w_specs = [pl.BlockSpec((1, D_MODEL, tf), lambda i, j: (j, 0, 0)),
                   pl.BlockSpec((1, D_MODEL, tf), lambda i, j: (j, 0, 0)),
                   pl.BlockSpec((tf, D_MODEL), lambda i, j: (j, 0))]
        w_args = weights
    out_specs = [out_tile]
    out_shape = [jax.ShapeDtypeStruct((rows, D_MODEL), F32)]
    next_args, next_specs = (), []
    if emit_h:
        ncol = w_next_t.shape[0]
        out_specs += [out_tile, pl.BlockSpec((TM, ncol), lambda i, j: (i, 0))]
        out_shape += [jax.ShapeDtypeStruct((rows, D_MODEL), BF16), jax.ShapeDtypeStruct((rows, ncol), F32)]
        next_args, next_specs = (w_next_t,), [pl.BlockSpec((ncol, D_MODEL), lambda i, j: (0, 0))]
    if cast_w:
        per = TF // tf
        tile_major = pl.BlockSpec((1, D_MODEL, tf), lambda i, j: (j // per, 0, j % per))
        out_specs += [tile_major, tile_major, pl.BlockSpec((tf, D_MODEL), lambda i, j: (j, 0))]
        out_shape += [jax.ShapeDtypeStruct((D_FF // TF, D_MODEL, TF), BF16)] * 2 + [
            jax.ShapeDtypeStruct((D_FF, D_MODEL), BF16)]
        assert rows == TM, "each weight tile is written exactly once"
    return pl.pallas_call(
        functools.partial(_ffn_kernel, final_norm=final_norm, emit_h=emit_h, h_in=h_in, cast_w=cast_w),
        grid=(rows // TM, nj),
        in_specs=([tile] if h_in else []) + [tile, pl.BlockSpec((1, D_MODEL), lambda i, j: (0, 0))] + w_specs
        + [pl.BlockSpec((1, D_MODEL), lambda i, j: (0, 0))] + next_specs,
        out_specs=out_specs,
        out_shape=out_shape,
        scratch_shapes=[pltpu.VMEM((TM, D_MODEL), BF16), pltpu.VMEM((TM, D_MODEL), F32)],
        compiler_params=_cparams(("parallel", "arbitrary")),
        name="ffn_cast" if cast_w else "ffn",
    )(*((h,) if h_in else ()), x, g, *w_args, g_after, *next_args)


def _proj_kernel(ha_ref, hb_ref, wt_hbm, o_ref, w_ref, wbuf_ref, sem_ref, *, tiles_a, col0, tn):
    j = pl.program_id(0)
    i = pl.program_id(1)
    nj = pl.num_programs(0)

    def tile_copy(jj):
        rows = pl.ds(pl.multiple_of(col0 + jj * tn, SUBLANES), tn)
        return pltpu.make_async_copy(wt_hbm.at[rows, :], wbuf_ref, sem_ref.at[0])

    @pl.when(i == 0)
    def _():
        @pl.when(j == 0)
        def _():
            tile_copy(0).start()

        tile_copy(j).wait()
        w_ref[...] = wbuf_ref[...].astype(BF16)

        @pl.when(j + 1 < nj)
        def _():
            tile_copy(j + 1).start()

    o_ref[...] = _dot_nt(jnp.where(i < tiles_a, ha_ref[...], hb_ref[...]), w_ref[...])


def _proj(hs, w_t, col0, tn, nout):
    tiles_a = hs[0].shape[0] // TM
    n = hs[0].shape[0] + hs[1].shape[0]
    assert col0 % SUBLANES == 0 and tn % SUBLANES == 0 and nout % tn == 0
    return pl.pallas_call(
        functools.partial(_proj_kernel, tiles_a=tiles_a, col0=col0, tn=tn),
        grid=(nout // tn, n // TM),
        in_specs=[
            pl.BlockSpec((TM, D_MODEL), lambda j, i: (jnp.minimum(i, tiles_a - 1), 0)),
            pl.BlockSpec((TM, D_MODEL), lambda j, i: (jnp.maximum(i - tiles_a, 0), 0)),
            pl.BlockSpec(memory_space=pl.ANY),
        ],
        out_specs=pl.BlockSpec((TM, tn), lambda j, i: (i, j)),
        out_shape=jax.ShapeDtypeStruct((n, nout), F32),
        scratch_shapes=[pltpu.VMEM((tn, D_MODEL), BF16), pltpu.VMEM((tn, D_MODEL), F32),
                        pltpu.SemaphoreType.DMA((1,))],
        compiler_params=_cparams(("arbitrary", "arbitrary")),
        name="proj",
    )(*hs, w_t)


def _pair_swap(x):
    n = x.shape[-1]
    lane = lax.broadcasted_iota(jnp.int32, x.shape, x.ndim - 1)
    prev = pltpu.roll(x, 1, x.ndim - 1)
    nxt = pltpu.roll(x, n - 1, x.ndim - 1)
    return jnp.where((lane & 1) == 1, prev, nxt)


def _rotary(x, cos, sin_signed):
    return x * cos + _pair_swap(x) * sin_signed


def _ret_finish(o, zg):
    o = o * lax.rsqrt(jnp.mean(o * o, axis=-1, keepdims=True) + GN_EPS_RET)
    return o * (zg * jax.nn.sigmoid(zg))


RET_PROMPT_SEQS = 2


def _ret_prompt_kernel(lg_ref, *refs):
    nb = RET_PROMPT_SEQS
    q_refs, k_refs, v_refs, g_refs = (refs[i * nb:(i + 1) * nb] for i in range(4))
    cos_ref, sin_ref, o_ref, s_out_ref, s_ref = refs[4 * nb:]
    c = pl.program_id(1)
    C = RET_CHUNK

    @pl.when(c == 0)
    def _():
        s_ref[...] = jnp.zeros_like(s_ref)

    cos = cos_ref[...]
    sin = sin_ref[...]
    ti = lax.broadcasted_iota(jnp.int32, (C, C), 0)
    tj = lax.broadcasted_iota(jnp.int32, (C, C), 1)
    rel = (ti - tj).astype(F32)
    idx = lax.broadcasted_iota(jnp.int32, (C, 1), 0).astype(F32)
    H = range(RET_HEADS)
    lgs = [lg_ref[h] for h in H]
    finals = []

    def sequence(b):
        qs = [_rotary(q_refs[b][:, h * RET_DK:(h + 1) * RET_DK], cos, sin) for h in H]
        ks = [_rotary(k_refs[b][:, h * RET_DK:(h + 1) * RET_DK], cos, sin) * (RET_DK ** -0.5) for h in H]
        yield
        vs = [v_refs[b][:, h * RET_DV:(h + 1) * RET_DV] for h in H]
        ss = [s_ref[b, h] for h in H]
        scores = [_dot_nt(q, k) * jnp.where(rel >= 0, jnp.exp(lg * jnp.maximum(rel, 0.0)), 0.0)
                  for q, k, lg in zip(qs, ks, lgs)]
        crosses = [_dot(q, s) * jnp.exp(lg * (idx + 1.0)) for q, s, lg in zip(qs, ss, lgs)]
        inners = [_dot(sc, v) for sc, v in zip(scores, vs)]
        s_news = [jnp.exp(lg * C) * s + _dot_tn(k * jnp.exp(lg * (C - 1.0 - idx)), v)
                  for lg, s, k, v in zip(lgs, ss, ks, vs)]
        yield
        for h in H:
            s_ref[b, h] = s_news[h]
            o_ref[b, :, h * RET_DV:(h + 1) * RET_DV] = _ret_finish(
                inners[h] + crosses[h], g_refs[b][:, h * RET_DV:(h + 1) * RET_DV]).astype(BF16)
        finals.append((b, s_news))

    _run_staged([sequence(b) for b in range(nb)], 1)

    @pl.when(c == pl.num_programs(1) - 1)
    def _():
        for b, s_news in finals:
            for h in H:
                s_out_ref[b, h] = s_news[h]


def _ret_prompt(z, lg, cos, sin, batch, seq):
    nc = seq // RET_CHUNK
    C = RET_CHUNK
    nb = RET_PROMPT_SEQS
    cols = lambda width, col0: [pl.BlockSpec((C, width), lambda g, c, b=b: ((g * nb + b) * nc + c, col0 // width))
                                for b in range(nb)]
    return pl.pallas_call(
        _ret_prompt_kernel,
        grid=(batch // nb, nc),
        in_specs=[pl.BlockSpec(memory_space=pltpu.SMEM)]
        + cols(RET_QK, ZQ) + cols(RET_QK, ZK) + cols(RET_V, ZV) + cols(RET_V, ZG)
        + [pl.BlockSpec((C, RET_DK), lambda g, c: (c, 0)), pl.BlockSpec((C, RET_DK), lambda g, c: (c, 0))],
        out_specs=[
            pl.BlockSpec((nb, C, RET_V), lambda g, c: (g, c, 0)),
            pl.BlockSpec((nb, RET_HEADS, RET_DK, RET_DV), lambda g, c: (g, 0, 0, 0)),
        ],
        out_shape=[
            jax.ShapeDtypeStruct((batch, seq, RET_V), BF16),
            jax.ShapeDtypeStruct((batch, RET_HEADS, RET_DK, RET_DV), F32),
        ],
        scratch_shapes=[pltpu.VMEM((nb, RET_HEADS, RET_DK, RET_DV), F32)],
        compiler_params=_cparams(("parallel", "arbitrary")),
        name="ret_prompt",
    )(lg, *([z] * (4 * nb)), cos, sin)


RET_SAMPLE_ROWS = 16


def _ret_sample_kernel(lg_ref, q_ref, k_ref, v_ref, g_ref, cos_ref, sin_ref, s0_ref,
                       o_ref, s_out_ref, *, seq):
    R = RET_SAMPLE_ROWS
    nb = R // seq
    cos = cos_ref[...]
    sin = sin_ref[...]
    ri = lax.broadcasted_iota(jnp.int32, (R, R), 0)
    rj = lax.broadcasted_iota(jnp.int32, (R, R), 1)
    rel = ((ri % seq) - (rj % seq)).astype(F32)
    ok = ((ri // seq) == (rj // seq)) & (rel >= 0)
    row = lax.broadcasted_iota(jnp.int32, (R, 1), 0)
    t = (row % seq).astype(F32)

    for h in range(RET_HEADS):
        lg = lg_ref[h]
        q = _rotary(q_ref[:, h * RET_DK:(h + 1) * RET_DK], cos, sin)
        k = _rotary(k_ref[:, h * RET_DK:(h + 1) * RET_DK], cos, sin) * (RET_DK ** -0.5)
        v = v_ref[:, h * RET_DV:(h + 1) * RET_DV]
        mask = jnp.where(ok, jnp.exp(lg * jnp.maximum(rel, 0.0)), 0.0)
        q_dec = jnp.exp(lg * (t + 1.0))
        kd = k * jnp.exp(lg * (seq - 1.0 - t))
        inner = _dot(_dot_nt(q, k) * mask, v)
        cross = jnp.zeros_like(inner)
        for b in range(nb):
            s = s0_ref[b, h]
            mine = (row // seq) == b
            cross = jnp.where(mine, _dot(q, s), cross)
            s_out_ref[b, h] = jnp.exp(lg * seq) * s + _dot_tn(jnp.where(mine, kd, 0.0), v)
        o_ref[:, h * RET_DV:(h + 1) * RET_DV] = _ret_finish(
            inner + cross * q_dec, g_ref[:, h * RET_DV:(h + 1) * RET_DV]).astype(BF16)


def _ret_sample(z, lg, cos, sin, s0, row0, batch, seq):
    R = RET_SAMPLE_ROWS
    nb = R // seq
    rb0 = row0 // R
    return pl.pallas_call(
        functools.partial(_ret_sample_kernel, seq=seq),
        grid=(batch // nb,),
        in_specs=[
            pl.BlockSpec(memory_space=pltpu.SMEM),
            pl.BlockSpec((R, RET_QK), lambda i: (rb0 + i, ZQ // RET_QK)),
            pl.BlockSpec((R, RET_QK), lambda i: (rb0 + i, ZK // RET_QK)),
            pl.BlockSpec((R, RET_V), lambda i: (rb0 + i, ZV // RET_V)),
            pl.BlockSpec((R, RET_V), lambda i: (rb0 + i, ZG // RET_V)),
            pl.BlockSpec((R, RET_DK), lambda i: (0, 0)),
            pl.BlockSpec((R, RET_DK), lambda i: (0, 0)),
            pl.BlockSpec((nb, RET_HEADS, RET_DK, RET_DV), lambda i: (i, 0, 0, 0)),
        ],
        out_specs=[
            pl.BlockSpec((R, RET_V), lambda i: (i, 0)),
            pl.BlockSpec((nb, RET_HEADS, RET_DK, RET_DV), lambda i: (i, 0, 0, 0)),
        ],
        out_shape=[
            jax.ShapeDtypeStruct((batch * seq, RET_V), BF16),
            jax.ShapeDtypeStruct((batch, RET_HEADS, RET_DK, RET_DV), F32),
        ],
        compiler_params=_cparams(("parallel",)),
        name="ret_sample",
    )(lg, z, z, z, z, cos, sin, s0)


def _rope_tables(pos):
    half = RET_DK // 2
    inv = 1.0 / (ROPE_BASE ** jnp.linspace(0.0, 1.0, half, dtype=F32))
    ang = pos[:, None] * inv[None, :]
    cos = jnp.repeat(jnp.cos(ang), 2, axis=-1)
    sin = jnp.sin(ang)
    sin_signed = jnp.stack([-sin, sin], axis=-1).reshape(pos.shape[0], RET_DK)
    return cos, sin_signed


def _segsum(x, seg_ref):
    rows = x.shape[0]
    n = rows * RWKV_PAIRS
    xs = jnp.concatenate([x[:, p * LANES:(p + 1) * LANES] for p in range(RWKV_PAIRS)], axis=0)
    ss = jnp.dot(jnp.concatenate(_bf16_terms(xs, 2), axis=0), seg_ref[...], preferred_element_type=F32)
    ss = ss[0:n] + ss[n:2 * n]
    return jnp.concatenate([ss[p * rows:(p + 1) * rows] for p in range(RWKV_PAIRS)], axis=1)


def _run_staged(gens, lead):
    out = [None] * len(gens)
    live = [True] * len(gens)

    def step(i):
        try:
            next(gens[i])
        except StopIteration as stop:
            out[i] = stop.value
            live[i] = False

    for _ in range(lead):
        if live[0]:
            step(0)
    while any(live):
        for i in range(len(gens)):
            if live[i]:
                step(i)
    return out


def _each(fn, *lists):
    out = []
    for args in zip(*lists):
        out.append(fn(*args))
        yield
    return out


def _rwkv_chunk(zr, zl, prev_r, prev_l, prm, n_seq, **state_fns):
    front = yield from _rwkv_front(zr, zl, prev_r, prev_l, prm, n_seq)
    return (yield from _rwkv_back(*front, prm, n_seq, **state_fns))


def _rwkv_front(zr, zl, prev_r, prev_l, prm, n_seq):
    (mur_ref, mul_ref, w0_ref, wup_ref, a0_ref, aup_ref, gup_ref,
     kk_ref, ka_ref, rk_ref, lnw_ref, lnb_ref, seg_ref) = prm
    C = RWKV_CHUNK
    W = RWKV_W
    R = zr.shape[0]
    TS = C // n_seq
    row = lax.broadcasted_iota(jnp.int32, (R, 1), 0)
    first = (row % TS) == 0

    def mix(z, prev, mu_ref):
        z_prev = jnp.where(first, prev, pltpu.roll(z, 1, 0))
        return z + (z_prev - z) * mu_ref[...]

    xs_r = mix(zr[:, 0:W], prev_r[:, 0:W], mur_ref.at[:, 0:W])
    yield
    xs_k = mix(zr[:, W:2 * W], prev_r[:, W:2 * W], mur_ref.at[:, W:2 * W])
    yield
    xs_v = mix(zr[:, 2 * W:3 * W], prev_r[:, 2 * W:3 * W], mur_ref.at[:, 2 * W:3 * W])
    yield
    xl = mix(zl, prev_l, mul_ref)
    r, k, v = xs_r, xs_k, xs_v
    x_wa = xl[:, 0:LANES]
    x_g = xl[:, LANES:LORA_PAD]

    y_w = w0_ref[...] + _dot(jnp.tanh(x_wa), wup_ref[...])
    yield
    w_log = -(jnp.maximum(-y_w, 0.0) + jnp.log(1.0 + jnp.exp(-jnp.abs(y_w)))) - 0.5
    lw = -jnp.exp(w_log)
    yield
    a_rate = jax.nn.sigmoid(a0_ref[...] + _dot(x_wa, aup_ref[...]))
    yield
    g = _dot(jax.nn.sigmoid(x_g), gup_ref[...])
    yield

    kk = k * kk_ref[...]
    k = k * (1.0 + (a_rate - 1.0) * ka_ref[...])
    yield
    sums = _segsum(jnp.concatenate([kk * kk, r * k * rk_ref[...]], axis=0), seg_ref)
    yield
    kk = kk * lax.rsqrt(jnp.maximum(sums[0:R], 1e-24))
    bonus_rk = sums[R:2 * R]
    a = -kk
    b = kk * a_rate
    yield

    rows_sel = R if n_seq == 1 else 2 * R
    ti = lax.broadcasted_iota(jnp.int32, (rows_sel, R), 0)
    tj = lax.broadcasted_iota(jnp.int32, (rows_sel, R), 1)
    same = ((ti % R) // TS) == (tj // TS)
    sel = (same & ((ti >= R) | (tj <= ti))).astype(BF16)
    sums_lw = sum(jnp.dot(sel, t, preferred_element_type=F32) for t in _bf16_terms(lw, 3))
    yield
    cum = sums_lw[0:R]
    if n_seq == 1:
        tot = jnp.concatenate([jnp.broadcast_to(cum[i + C - 1:i + C], (C, W)) for i in range(0, R, C)], axis=0)
    else:
        tot = sums_lw[R:2 * R]
    e_neg = jnp.exp(-cum)
    yield
    a_t = (a * jnp.exp(cum - lw)).astype(BF16)
    yield
    b_t = (b * e_neg).astype(BF16)
    yield
    k_t = (k * e_neg).astype(BF16)
    yield
    r_t = r * jnp.exp(cum)
    yield
    e_tail = jnp.exp(tot - cum)
    yield
    b_g = (b * e_tail).astype(BF16)
    yield
    k_g = (k * e_tail).astype(BF16)
    dec = jnp.exp(tot)
    yield
    return (a_t, b_t, k_t, b_g, k_g), (r_t, v, dec, g, bonus_rk)


def _rwkv_back(mm_operands, f32_arrays, prm, n_seq, read_state=None, update_state=None, fold_state=None):
    a_t, b_t, k_t, b_g, k_g = mm_operands
    r_t, v, dec, g, bonus_rk = f32_arrays
    lnw_ref, lnb_ref, seg_ref = prm[10:13]
    C = RWKV_CHUNK
    R = r_t.shape[0]
    TS = C // n_seq

    lane = lax.broadcasted_iota(jnp.int32, (1, LANES), 1)
    lo = lane < RWKV_N

    def split(x):
        return jnp.concatenate([jnp.where(lo, x, 0.0), jnp.where(lo, 0.0, x)], axis=0)

    row4 = lax.broadcasted_iota(jnp.int32, (C, 4 * C), 0)
    col4 = lax.broadcasted_iota(jnp.int32, (C, 4 * C), 1) % C
    same4 = (row4 // TS) == (col4 // TS)
    strict = same4 & (col4 < row4)
    incl = same4 & (col4 <= row4)
    row2 = lax.broadcasted_iota(jnp.int32, (C, 2 * C), 0)
    col2 = lax.broadcasted_iota(jnp.int32, (C, 2 * C), 1)
    eye2 = ((col2 % C) == row2).astype(F32)
    left = col2 < C

    tiles = [(slice(i * C, (i + 1) * C), slice(p * LANES, (p + 1) * LANES))
             for i in range(R // C) for p in range(RWKV_PAIRS)]
    v2s = [split(v[rs, sl]) for rs, sl in tiles]
    ars = [jnp.concatenate([a_t[rs, sl], r_t[rs, sl]], axis=0) for rs, sl in tiles]
    if fold_state is None:
        wys = read_state(ars)
    gms = yield from _each(
        lambda ar, t: _dot_nt(ar, jnp.concatenate([split(b_t[t[0], t[1]]), split(k_t[t[0], t[1]])], axis=0)),
        ars, tiles)
    gas = [jnp.where(strict, gm[0:C], 0.0) for gm in gms]
    grs = [jnp.where(incl, gm[C:2 * C], 0.0) for gm in gms]

    def blockdiag(x):
        return jnp.concatenate([jnp.where(left, x, 0.0), jnp.where(left, 0.0, x)], axis=0)

    ts = [eye2 + ga[:, 0:2 * C] for ga in gas]
    n_sq = TS.bit_length() - 2
    if n_sq >= 1:
        pws = yield from _each(lambda ga: _dot(ga[:, 0:2 * C], blockdiag(ga[:, 0:2 * C])), gas)
        for _ in range(n_sq - 1):
            tps = yield from _each(lambda t, pw: _dot(jnp.concatenate([t, pw], axis=0), blockdiag(pw)), ts, pws)
            ts = [t + tp[0:C] for t, tp in zip(ts, tps)]
            pws = [tp[C:2 * C] for tp in tps]
        ts = yield from _each(lambda t, pw: t + _dot(t, blockdiag(pw)), ts, pws)
    wis = yield from _each(lambda ga, v2: _dot(ga[:, 2 * C:4 * C], v2), gas, v2s)
    bks = [jnp.concatenate([b_g[rs, sl], k_g[rs, sl]], axis=0) for rs, sl in tiles]
    if fold_state is None:
        us = [_dot(t, split(wi + wy[0:C])) for t, wi, wy in zip(ts, wis, wys)]
        ys = [wy[C:2 * C] + _dot(gr, jnp.concatenate([split(u), v2], axis=0))
              for wy, gr, u, v2 in zip(wys, grs, us, v2s)]
        update_state([jnp.concatenate([u, v[rs, sl]], axis=0) for u, (rs, sl) in zip(us, tiles)], bks, dec)
    else:
        zc = jnp.zeros((C, LANES), F32)
        aus = yield from _each(
            lambda t, wi, tl: _dot(t, jnp.concatenate([split(a_t[tl[0], tl[1]]), split(wi)], axis=1)),
            ts, wis, tiles)
        rys = yield from _each(
            lambda gr, au, v2: _dot(gr, jnp.concatenate(
                [jnp.concatenate([split(au[:, 0:LANES]), split(au[:, LANES:2 * LANES])], axis=1),
                 jnp.concatenate([jnp.concatenate([zc, zc], axis=0), v2], axis=1)], axis=0)),
            grs, aus, v2s)
        mns = yield from _each(
            lambda bk, au, tl: _dot_tn(bk, jnp.concatenate(
                [au, jnp.concatenate([zc, v[tl[0], tl[1]]], axis=1)], axis=0)),
            bks, aus, tiles)
        ys = fold_state([r_t[rs, sl] + ry[:, 0:LANES] for ry, (rs, sl) in zip(rys, tiles)],
                        [ry[:, LANES:2 * LANES] for ry in rys], mns, dec)

    yield
    y = jnp.concatenate([jnp.concatenate(ys[i * RWKV_PAIRS:(i + 1) * RWKV_PAIRS], axis=1)
                         for i in range(R // C)], axis=0)
    mean = _segsum(y, seg_ref) * (1.0 / RWKV_N)
    d = y - mean
    var = _segsum(d * d, seg_ref) * (1.0 / RWKV_N)
    yn = d * lax.rsqrt(var + GN_EPS_RWKV) * lnw_ref[...] + lnb_ref[...]
    return ((yn + bonus_rk * v) * g).astype(BF16)


def _rwkv_prompt_kernel(*refs, batch):
    zr_refs = refs[:batch]
    zl_refs = refs[batch:2 * batch]
    prm = refs[2 * batch:2 * batch + 13]
    o_ref, s_out_ref, prevr_ref, prevl_ref, h_ref = refs[2 * batch + 13:]
    C = RWKV_CHUNK
    c = pl.program_id(0)

    @pl.when(c == 0)
    def _():
        prevr_ref[...] = jnp.zeros_like(prevr_ref)
        prevl_ref[...] = jnp.zeros_like(prevl_ref)
        h_ref[...] = jnp.zeros_like(h_ref)

    ki = lax.broadcasted_iota(jnp.int32, (LANES, LANES), 0)
    vi = lax.broadcasted_iota(jnp.int32, (LANES, LANES), 1)
    diag_blocks = (ki < RWKV_N) == (vi < RWKV_N)

    def group(i0):
        seqs = range(i0, i0 + RWKV_PROMPT_GROUP)
        tiles = [(i, p) for i in seqs for p in range(RWKV_PAIRS)]
        zr = jnp.concatenate([zr_refs[i][...] for i in seqs], axis=0)
        zl = jnp.concatenate([zl_refs[i][...] for i in seqs], axis=0)
        rows = lambda ref: jnp.concatenate(
            [jnp.broadcast_to(ref[i:i + 1, :], (C, ref.shape[1])) for i in seqs], axis=0)

        def fold_state(r_hats, y_inds, mns, dec):
            dec_cols = [jnp.broadcast_to(dec[(i - i0) * C:(i - i0) * C + 1, p * LANES:(p + 1) * LANES],
                                         (LANES, LANES)).T for i, p in tiles]
            hs = [h_ref[i, p] for i, p in tiles]
            res = [_dot(jnp.concatenate([rh, jnp.where(diag_blocks, mn[:, 0:LANES], 0.0)], axis=0), h)
                   for rh, mn, h in zip(r_hats, mns, hs)]
            for (i, p), h, dc, rs, mn in zip(tiles, hs, dec_cols, res, mns):
                h_ref[i, p] = h * dc + rs[C:C + LANES] + jnp.where(diag_blocks, mn[:, LANES:2 * LANES], 0.0)
            return [rs[0:C] + yi for rs, yi in zip(res, y_inds)]

        y = yield from _rwkv_chunk(zr, zl, rows(prevr_ref), rows(prevl_ref), prm, 1, fold_state=fold_state)
        for n, i in enumerate(seqs):
            o_ref[i] = y[n * C:(n + 1) * C]
            prevr_ref[i:i + 1, :] = zr[(n + 1) * C - 1:(n + 1) * C, :]
            prevl_ref[i:i + 1, :] = zl[(n + 1) * C - 1:(n + 1) * C, :]

    _run_staged([group(i0) for i0 in range(0, batch, RWKV_PROMPT_GROUP)], RWKV_PROMPT_LEAD)

    @pl.when(c == pl.num_programs(0) - 1)
    def _():
        for i in range(batch):
            for p in range(RWKV_PAIRS):
                s_out_ref[i, p] = h_ref[i, p].T


def _rwkv_sample_kernel(zr_ref, zl_ref, shr_ref, shl_ref, s0_ref, *rest, n_seq):
    prm = rest[:13]
    o_ref, s_out_ref = rest[13:]
    C = RWKV_CHUNK
    N = RWKV_N
    TS = C // n_seq
    tokseq = (lax.broadcasted_iota(jnp.int32, (1, LANES), 1) % C) // TS

    def pick(x, j):
        return jnp.where(tokseq == j, x, 0.0)

    def stacked(h):
        return s0_ref[:, h].reshape(n_seq * N, N)

    def read_state(ars):
        res = [[_dot_nt(stacked(2 * p + e), ar[:, e * N:(e + 1) * N]) for e in range(2)]
               for p, ar in enumerate(ars)]
        out = []
        for rp in res:
            halves = []
            for re in rp:
                acc = pick(re[0:N], 0)
                for j in range(1, n_seq):
                    acc = acc + pick(re[j * N:(j + 1) * N], j)
                halves.append(acc)
            out.append(jnp.concatenate(halves, axis=0).T)
        return out

    def update_state(uvs, bks, dec):
        uvts = [uv.T for uv in uvs]
        upds = [[_dot(jnp.concatenate([pick(uvt[e * N:(e + 1) * N], j) for j in range(n_seq)], axis=0),
                      bk[:, e * N:(e + 1) * N]) for e in range(2)]
                for uvt, bk in zip(uvts, bks)]
        for p, up in enumerate(upds):
            for e, upd in enumerate(up):
                h = 2 * p + e
                dec_h = jnp.concatenate(
                    [jnp.broadcast_to(dec[j * TS:j * TS + 1, h * N:(h + 1) * N], (N, N)) for j in range(n_seq)],
                    axis=0)
                s_out_ref[:, h] = (stacked(h) * dec_h + upd).reshape(n_seq, N, N)

    (o_ref[...],) = _run_staged([_rwkv_chunk(zr_ref[...], zl_ref[...], shr_ref[...], shl_ref[...], prm, n_seq,
                                             read_state=read_state, update_state=update_state)], 0)


def _rwkv_param_specs():
    W = RWKV_W
    shapes = [(1, 3 * W), (1, LORA_PAD), (1, W), (LANES, W), (1, W), (LANES, W), (LORA_PAD - LANES, W),
              (1, W), (1, W), (1, W), (1, W), (1, W), (LANES, LANES)]
    return [pl.BlockSpec(s, lambda *_: (0, 0)) for s in shapes]


def _rwkv_prompt(z, zl, params, batch, seq):
    C = RWKV_CHUNK
    W = RWKV_W
    nc = seq // C
    zr_spec = lambda i: pl.BlockSpec((C, 3 * W), lambda c: (i * nc + c, ZS // (3 * W)))
    zl_spec = lambda i: pl.BlockSpec((C, LORA_PAD), lambda c: (i * nc + c, 0))
    return pl.pallas_call(
        functools.partial(_rwkv_prompt_kernel, batch=batch),
        grid=(nc,),
        in_specs=[zr_spec(i) for i in range(batch)] + [zl_spec(i) for i in range(batch)]
        + _rwkv_param_specs(),
        out_specs=[
            pl.BlockSpec((batch, C, W), lambda c: (0, c, 0)),
            pl.BlockSpec((batch, RWKV_PAIRS, LANES, LANES), lambda c: (0, 0, 0, 0)),
        ],
        out_shape=[
            jax.ShapeDtypeStruct((batch, seq, W), BF16),
            jax.ShapeDtypeStruct((batch, RWKV_PAIRS, LANES, LANES), F32),
        ],
        scratch_shapes=[pltpu.VMEM((batch, 3 * W), F32), pltpu.VMEM((batch, LORA_PAD), F32),
                        pltpu.VMEM((batch, RWKV_PAIRS, LANES, LANES), F32)],
        compiler_params=_cparams(("arbitrary",)),
        name="rwkv_prompt",
    )(*([z] * batch), *([zl] * batch), *params)


def _rwkv_sample(z, zl, sh_r, sh_l, s0, params, row0, batch, seq):
    C = RWKV_CHUNK
    W = RWKV_W
    n_seq = C // seq
    rb0 = row0 // C
    return pl.pallas_call(
        functools.partial(_rwkv_sample_kernel, n_seq=n_seq),
        grid=(batch // n_seq,),
        in_specs=[
            pl.BlockSpec((C, 3 * W), lambda i: (rb0 + i, ZS // (3 * W))),
            pl.BlockSpec((C, LORA_PAD), lambda i: (i, 0)),
            pl.BlockSpec((C, 3 * W), lambda i: (i, 0)),
            pl.BlockSpec((C, LORA_PAD), lambda i: (i, 0)),
            pl.BlockSpec((n_seq, RWKV_HEADS, RWKV_N, RWKV_N), lambda i: (i, 0, 0, 0)),
        ] + _rwkv_param_specs(),
        out_specs=[
            pl.BlockSpec((C, W), lambda i: (i, 0)),
            pl.BlockSpec((n_seq, RWKV_HEADS, RWKV_N, RWKV_N), lambda i: (i, 0, 0, 0)),
        ],
        out_shape=[
            jax.ShapeDtypeStruct((batch * seq, W), BF16),
            jax.ShapeDtypeStruct((batch, RWKV_HEADS, RWKV_N, RWKV_N), F32),
        ],
        compiler_params=_cparams(("parallel",)),
        name="rwkv_sample",
    )(z, zl, sh_r, sh_l, s0, *params)


def _pair_unblock(s):
    b = s.shape[0]
    lo = s[:, :, :RWKV_N, :RWKV_N]
    hi = s[:, :, RWKV_N:, RWKV_N:]
    return jnp.stack([lo, hi], axis=2).reshape(b, RWKV_HEADS, RWKV_N, RWKV_N)


def _mix_out_kernel(oret_a_ref, oret_b_ref, orwkv_a_ref, orwkv_b_ref, xa_ref, xb_ref, gr_ref, gw_ref,
                    wr_ref, ww_ref, wo_ref, gn_ref, o_ref, hn_ref, *, tiles_a):
    first = pl.program_id(0) < tiles_a
    o_ret = jnp.where(first, oret_a_ref[...], oret_b_ref[...])
    o_rwkv = jnp.where(first, orwkv_a_ref[...], orwkv_b_ref[...])
    a = jnp.dot(o_ret, wr_ref[...], preferred_element_type=F32)
    b = jnp.dot(o_rwkv, ww_ref[...], preferred_element_type=F32)
    merged = jax.nn.sigmoid(gr_ref[...]) * a + jax.nn.sigmoid(gw_ref[...]) * b
    x2 = (jnp.where(first, xa_ref[...], xb_ref[...])
          + jnp.dot(merged.astype(BF16), wo_ref[...], preferred_element_type=F32))
    o_ref[...] = x2
    hn_ref[...] = _rms(x2, gn_ref[...], NORM_EPS).astype(BF16)


def _mix_out(o_rets, o_rwkvs, xs, zg, w_ret_o, w_rwkv_o, w_out, g_next):
    n = xs[0].shape[0] + xs[1].shape[0]
    tm = TM_MIX
    tiles_a = o_rets[0].shape[0] // tm
    resident = lambda shape: pl.BlockSpec(shape, lambda i: (0, 0), pipeline_mode=pl.Buffered(1))
    first = lambda i: (jnp.minimum(i, tiles_a - 1), 0)
    second = lambda i: (jnp.maximum(i - tiles_a, 0), 0)
    return pl.pallas_call(
        functools.partial(_mix_out_kernel, tiles_a=tiles_a),
        grid=(n // tm,),
        in_specs=[
            pl.BlockSpec((tm, RET_V), first),
            pl.BlockSpec((tm, RET_V), second),
            pl.BlockSpec((tm, RWKV_W), first),
            pl.BlockSpec((tm, RWKV_W), second),
            pl.BlockSpec((tm, D_MODEL), first),
            pl.BlockSpec((tm, D_MODEL), second),
            pl.BlockSpec((tm, D_MODEL), lambda i: (i, ZGR // D_MODEL)),
            pl.BlockSpec((tm, D_MODEL), lambda i: (i, ZGW // D_MODEL)),
            resident((RET_V, D_MODEL)),
            resident((RWKV_W, D_MODEL)),
            resident((D_MODEL, D_MODEL)),
            pl.BlockSpec((1, D_MODEL), lambda i: (0, 0)),
        ],
        out_specs=[pl.BlockSpec((tm, D_MODEL), lambda i: (i, 0)), pl.BlockSpec((tm, D_MODEL), lambda i: (i, 0))],
        out_shape=[jax.ShapeDtypeStruct((n, D_MODEL), F32), jax.ShapeDtypeStruct((n, D_MODEL), BF16)],
        compiler_params=_cparams(("arbitrary",)),
        name="mix_out",
    )(*o_rets, *o_rwkvs, *xs, zg, zg, w_ret_o, w_rwkv_o, w_out, g_next)


def kernel(x_prompt, x_sample, state_ret, state_rwkv, state_shift, norm_ffn1, ffn1_w_gu, ffn1_w_down, norm_mix, w_in, w_ret_o, rwkv_mu, rwkv_w0, rwkv_w_up, rwkv_a0, rwkv_a_up, rwkv_g_up, rwkv_k_k, rwkv_k_a, rwkv_r_k, rwkv_ln_w, rwkv_ln_b, w_rwkv_o, w_out, norm_ffn2, ffn2_w_gu, ffn2_w_down, norm_final):
    bp, tp, d = x_prompt.shape
    bs, ts, _ = x_sample.shape
    n_p = bp * tp
    n_s = bs * ts
    W = RWKV_W
    row = lambda a: a.reshape(1, -1)
    assert d == D_MODEL and n_p % TM == 0 and n_s == TM and n_p % TM_MIX == 0
    assert tp % RET_CHUNK == 0 and tp % RWKV_CHUNK == 0 and bp % RWKV_PROMPT_GROUP == 0 and bp % RET_PROMPT_SEQS == 0
    assert RWKV_CHUNK % ts == 0 and n_s % RWKV_CHUNK == 0 and RET_SAMPLE_ROWS % ts == 0 and n_s % RET_SAMPLE_ROWS == 0

    w_in_t = jnp.swapaxes(w_in[0], 0, 1)
    w_lora_t = w_in_t[Z_MAIN_W:Z_MAIN_W + LORA_PAD].astype(BF16)
    x1_s, h_s, zl_s, *w_ffn1 = _ffn(x_sample.reshape(n_s, d), 0, n_s, row(norm_ffn1[0]),
                                    (ffn1_w_gu[0], ffn1_w_down[0]), row(norm_mix[0]), final_norm=False,
                                    w_next_t=w_lora_t)
    x1_p, h_p, zl_p = _ffn(x_prompt.reshape(n_p, d), 0, n_p, row(norm_ffn1[0]), w_ffn1, row(norm_mix[0]),
                           final_norm=False, w_next_t=w_lora_t)

    z = _proj((h_p, h_s), w_in_t, 0, TN_PROJ_MAIN, Z_MAIN_W)
    zg = _proj((h_p, h_s), w_in_t, Z_MAIN_W + LORA_W, TN_PROJ, 2 * D_MODEL)

    lg = jnp.log(1.0 - 2.0 ** (-5.0 - jnp.arange(RET_HEADS, dtype=F32)))
    cos_p, sin_p = _rope_tables(jnp.arange(tp, dtype=F32))
    cos_s, sin_s = _rope_tables(PAST_LEN + jnp.arange(ts, dtype=F32))
    rep = RET_SAMPLE_ROWS // ts
    oret_p, ret_p = _ret_prompt(z, lg, cos_p, sin_p, bp, tp)
    oret_s, ret_s = _ret_sample(z, lg, jnp.tile(cos_s, (rep, 1)), jnp.tile(sin_s, (rep, 1)),
                                state_ret[0], n_p, bs, ts)

    mu = rwkv_mu[0]
    pad_l = lambda a: jnp.pad(a, [(0, 0)] * (a.ndim - 1) + [(0, LORA_PAD - LORA_W)])
    seg = (jnp.arange(LANES)[:, None] // RWKV_N == jnp.arange(LANES)[None, :] // RWKV_N).astype(BF16)
    zero64 = jnp.zeros((DECAY_LORA, W), F32)
    params = (
        row(mu[:3 * W]), pad_l(row(mu[3 * W:])),
        row(rwkv_w0[0]), jnp.concatenate([rwkv_w_up[0], zero64], axis=0).astype(BF16),
        row(rwkv_a0[0]), jnp.concatenate([zero64, rwkv_a_up[0]], axis=0).astype(BF16),
        jnp.pad(rwkv_g_up[0], ((0, LORA_PAD - LANES - GATE_LORA), (0, 0))).astype(BF16),
        row(rwkv_k_k[0]), row(rwkv_k_a[0]), row(rwkv_r_k[0]), row(rwkv_ln_w[0]), row(rwkv_ln_b[0]),
        seg,
    )
    orw_p, rwkv_p = _rwkv_prompt(z, zl_p, params, bp, tp)
    sh_s = state_shift[0]
    first_rows = lambda a: jnp.pad(a[:, None, :], ((0, 0), (0, ts - 1), (0, 0))).reshape(n_s, -1)
    orw_s, rwkv_s = _rwkv_sample(z, zl_s, first_rows(sh_s[:, :3 * W]), first_rows(pad_l(sh_s[:, 3 * W:])),
                                 state_rwkv[0], params, n_p, bs, ts)

    x2, h_ffn2 = _mix_out((oret_p.reshape(n_p, RET_V), oret_s), (orw_p.reshape(n_p, W), orw_s), (x1_p, x1_s), zg,
                          w_ret_o[0].astype(BF16), w_rwkv_o[0].astype(BF16), w_out[0].astype(BF16),
                          row(norm_ffn2[0]))
    y_s, *w_ffn2 = _ffn(x2, n_p, n_s, row(norm_ffn2[0]), (ffn2_w_gu[0], ffn2_w_down[0]), row(norm_final),
                        final_norm=True, h=h_ffn2)
    (y_p,) = _ffn(x2, 0, n_p, row(norm_ffn2[0]), w_ffn2, row(norm_final), final_norm=True, h=h_ffn2)

    def last_rows(firsts, t):
        rows = [f + t - 1 for f in firsts]
        zr_last = jnp.concatenate([z[r:r + 1, ZS:ZS + 3 * W] for r in rows], axis=0)
        zl_last = jnp.concatenate([zl_p[r:r + 1, :LORA_W] for r in rows], axis=0)
        return jnp.concatenate([zr_last, zl_last], axis=-1)[None]

    shift_s = jnp.concatenate([lax.slice(z, (n_p + ts - 1, ZS), (n_p + n_s, ZS + 3 * W), (ts, 1)),
                               lax.slice(zl_s, (ts - 1, 0), (n_s, LORA_W), (ts, 1))], axis=-1)[None]
    return (y_p.reshape(bp, tp, d), y_s.reshape(bs, ts, d),
            ret_p[None], _pair_unblock(rwkv_p)[None], last_rows(range(0, n_p, tp), tp),
            ret_s[None], rwkv_s[None], shift_s)
```
